```python
import math
import jax, jax.numpy as jnp
from jax import lax
import numpy as np

D_MODEL = 1024
BATCH = 8
SEQ = 2048
DEPTH = 1

D_MIX = D_MODEL
SSD_HEADS = 8
SSD_HEAD_DIM = 64
D_SSD = SSD_HEADS * SSD_HEAD_DIM
SSD_GROUPS = 2
SSD_HEADS_PER_GROUP = SSD_HEADS // SSD_GROUPS
D_STATE = 128
CONV_WIDTH = 4
SSD_CHUNK = 128
D_CONV = D_SSD + 2 * SSD_GROUPS * D_STATE
ATTN_Q_HEADS = 8
ATTN_KV_HEADS = 2
ATTN_Q_PER_KV = ATTN_Q_HEADS // ATTN_KV_HEADS
ATTN_HEAD_DIM = 64
D_ATTN = ATTN_Q_HEADS * ATTN_HEAD_DIM
D_KV = ATTN_KV_HEADS * ATTN_HEAD_DIM
WINDOW = 128
ATTN_BLOCK = 128
REL_BUCKETS = 32
REL_MAX_DIST = 128
D_PROJ = D_SSD + D_CONV + SSD_HEADS + D_ATTN + 2 * D_KV
N_EXPERTS = 32
TOP_K = 4
D_EXPERT = D_MODEL
SWIGLU_LIMIT = 7.0
SWIGLU_ALPHA = 1.702
MOE_BLOCK = 128
RMS_EPS = 1e-5

kernel_name = "hymba_ssd_swa_sink_moe_block"


def rmsnorm(x, g):
    xf = x.astype(jnp.float32)
    y = xf * lax.rsqrt(jnp.mean(xf * xf, axis=-1, keepdims=True) + RMS_EPS)
    return (y * g.astype(jnp.float32)).astype(x.dtype)


def causal_depthwise_conv(u, w, b):
    k_width, chans = w.shape
    out = lax.conv_general_dilated(
        u, w[:, None, :].astype(u.dtype), window_strides=(1,),
        padding=[(k_width - 1, 0)], dimension_numbers=("NWC", "WIO", "NWC"),
        feature_group_count=chans)
    return out + b.astype(u.dtype)


def ssd_group(z, xbc, dt_raw, conv_w, conv_b, dt_bias, a_log, d_skip, norm_g):
    bsz, seq = z.shape[0], z.shape[1]
    nc = seq // SSD_CHUNK
    G, R, P, N, L = SSD_GROUPS, SSD_HEADS_PER_GROUP, SSD_HEAD_DIM, D_STATE, SSD_CHUNK
    xbc = jax.nn.silu(causal_depthwise_conv(xbc, conv_w, conv_b))
    xs = xbc[..., :D_SSD]
    bm = xbc[..., D_SSD:D_SSD + G * N].reshape(bsz, nc, L, G, N)
    cm = xbc[..., D_SSD + G * N:].reshape(bsz, nc, L, G, N)
    x = xs.reshape(bsz, nc, L, G, R, P)
    dt = jax.nn.softplus(dt_raw.astype(jnp.float32) + dt_bias.astype(jnp.float32))
    a = -jnp.exp(a_log.astype(jnp.float32))
    dt_c = dt.reshape(bsz, nc, L, G, R)
    xdt = x * dt_c[..., None]
    a_dt = jnp.transpose(dt_c * a.reshape(G, R), (0, 3, 4, 1, 2))
    a_cum = jnp.cumsum(a_dt, axis=-1)
    seg = a_cum[..., :, None] - a_cum[..., None, :]
    causal = jnp.tril(jnp.ones((L, L), dtype=bool))
    decay = jnp.exp(jnp.where(causal, seg, -jnp.inf))
    cb = jnp.einsum("bclgn,bcsgn->bgcls", cm, bm)
    y_diag = jnp.einsum("bgrcls,bcsgrp->bclgrp", cb[:, :, None] * decay, xdt)
    decay_to_end = jnp.exp(a_cum[..., -1:] - a_cum)
    states = jnp.einsum("bclgn,bgrcl,bclgrp->bcgrpn", bm, decay_to_end, xdt)
    chunk_decay = jnp.exp(a_cum[..., -1])

    def step(h, inp):
        st, dec = inp
        return h * dec[..., None, None] + st, h

    h0 = jnp.zeros(states.shape[:1] + states.shape[2:], states.dtype)
    _, prev = lax.scan(step, h0, (jnp.moveaxis(states, 1, 0), jnp.moveaxis(chunk_decay, 3, 0)))
    prev = jnp.moveaxis(prev, 0, 1)
    y_off = jnp.einsum("bclgn,bcgrpn,bgrcl->bclgrp", cm, prev, jnp.exp(a_cum))
    y = y_diag + y_off + x * d_skip.reshape(G, R)[:, :, None]
    y = y.reshape(bsz, seq, D_SSD).astype(z.dtype)
    return rmsnorm(y * jax.nn.silu(z), norm_g)


def t5_bucket(dist):
    max_exact = REL_BUCKETS // 2
    is_small = dist < max_exact
    d = jnp.maximum(dist, 1).astype(jnp.float32)
    large = max_exact + (jnp.log(d / max_exact) / math.log(REL_MAX_DIST / max_exact)
                         * (REL_BUCKETS - max_exact)).astype(jnp.int32)
    large = jnp.minimum(large, REL_BUCKETS - 1)
    return jnp.where(is_small, dist, large)


def swa_group(q, k, v, sinks, rel_bias):
    bsz, seq = q.shape[0], q.shape[1]
    L = ATTN_BLOCK
    nb = seq // L
    HK, G, Dh = ATTN_KV_HEADS, ATTN_Q_PER_KV, ATTN_HEAD_DIM
    qb = q.reshape(bsz, nb, L, HK, G, Dh)
    kb = k.reshape(bsz, nb, L, HK, Dh)
    vb = v.reshape(bsz, nb, L, HK, Dh)
    kk = jnp.concatenate([jnp.concatenate([jnp.zeros_like(kb[:, :1]), kb[:, :-1]], axis=1), kb], axis=2)
    vv = jnp.concatenate([jnp.concatenate([jnp.zeros_like(vb[:, :1]), vb[:, :-1]], axis=1), vb], axis=2)
    s = jnp.einsum("bnqhgd,bnkhd->bnhgqk", qb, kk).astype(jnp.float32) * (1.0 / math.sqrt(Dh))
    dist = L + jnp.arange(L)[:, None] - jnp.arange(2 * L)[None, :]
    in_window = (dist >= 0) & (dist < WINDOW)
    bias = rel_bias.astype(jnp.float32)[t5_bucket(jnp.clip(dist, 0, REL_MAX_DIST))]
    bias = jnp.transpose(bias.reshape(L, 2 * L, HK, G), (2, 3, 0, 1))
    prev_ok = (jnp.arange(nb)[:, None] > 0) | (jnp.arange(2 * L)[None, :] >= L)
    mask = in_window[None] & prev_ok[:, None, :]
    s = jnp.where(mask[None, :, None, None], s + bias[None, None], -jnp.inf)
    sink = sinks.astype(jnp.float32).reshape(HK, G)[None, None, :, :, None, None]
    m = jnp.maximum(jnp.max(s, axis=-1, keepdims=True), sink)
    p = jnp.exp(s - m)
    p = p / (jnp.sum(p, axis=-1, keepdims=True) + jnp.exp(sink - m))
    o = jnp.einsum("bnhgqk,bnkhd->bnqhgd", p.astype(vv.dtype), vv)
    return o.reshape(bsz, seq, D_ATTN)


def clamped_swiglu(hid):
    x_glu = jnp.minimum(hid[..., :D_EXPERT], SWIGLU_LIMIT)
    x_lin = jnp.clip(hid[..., D_EXPERT:], -SWIGLU_LIMIT, SWIGLU_LIMIT)
    return x_glu * jax.nn.sigmoid(SWIGLU_ALPHA * x_glu) * (x_lin + 1.0)


def moe_ffn(u, w_router, b_router, w1, b1, w2, b2):
    bsz, seq, d = u.shape
    T = bsz * seq
    xt = u.reshape(T, d)
    logits = (xt @ w_router + b_router).astype(jnp.float32)
    top_val, top_idx = lax.top_k(logits, TOP_K)
    gate = jax.nn.softmax(top_val, axis=-1)
    e_flat = top_idx.reshape(-1).astype(jnp.int32)
    tok_flat = jnp.repeat(jnp.arange(T, dtype=jnp.int32), TOP_K)
    g_flat = gate.reshape(-1)
    order = jnp.argsort(e_flat)
    e_sorted = e_flat[order]
    counts = jnp.bincount(e_flat, length=N_EXPERTS)
    starts = jnp.cumsum(counts) - counts
    padded = ((counts + MOE_BLOCK - 1) // MOE_BLOCK) * MOE_BLOCK
    padded_ends = jnp.cumsum(padded)
    padded_starts = padded_ends - padded
    n_assign = T * TOP_K
    dest = padded_starts[e_sorted] + jnp.arange(n_assign) - starts[e_sorted]
    P = n_assign + N_EXPERTS * MOE_BLOCK
    row_tok = jnp.full((P,), T, dtype=jnp.int32).at[dest].set(tok_flat[order])
    row_gate = jnp.zeros((P,), jnp.float32).at[dest].set(g_flat[order])
    nblk = P // MOE_BLOCK
    blk_expert = jnp.minimum(
        jnp.searchsorted(padded_ends, jnp.arange(nblk) * MOE_BLOCK, side="right"), N_EXPERTS - 1)
    x_pad = jnp.concatenate([xt, jnp.zeros((1, d), xt.dtype)], axis=0)
    xs = x_pad[row_tok].reshape(nblk, MOE_BLOCK, d)

    def expert_block(args):
        xb, e = args
        hid = clamped_swiglu(xb @ w1[e] + b1[e])
        return hid @ w2[e] + b2[e]

    ys = lax.map(expert_block, (xs, blk_expert)).reshape(P, d)
    ys = ys * row_gate[:, None].astype(ys.dtype)
    out = jax.ops.segment_sum(ys, row_tok, num_segments=T + 1)[:T]
    return out.reshape(bsz, seq, d)


def setup_inputs(seed: int = 0) -> dict:
    key = jax.random.key(seed)
    ks = jax.random.split(key, 24)
    f32 = jnp.float32
    nrm = lambda k, shape, scale: jax.random.normal(k, shape, f32) * scale
    gain = lambda k, shape: 1.0 + 0.05 * jax.random.normal(k, shape, f32)
    dt0 = jnp.exp(jax.random.uniform(ks[5], (DEPTH, SSD_HEADS), f32, math.log(1e-3), math.log(1e-1)))
    return {
        "x": nrm(ks[0], (BATCH, SEQ, D_MODEL), 1.0),
        "norm_mix_g": gain(ks[1], (DEPTH, D_MODEL)),
        "w_in": nrm(ks[2], (DEPTH, D_MODEL, D_PROJ), D_MODEL ** -0.5),
        "conv_w": nrm(ks[3], (DEPTH, CONV_WIDTH, D_CONV), CONV_WIDTH ** -0.5),
        "conv_b": nrm(ks[4], (DEPTH, D_CONV), 0.02),
        "dt_bias": jnp.log(jnp.expm1(dt0)),
        "a_log": jnp.log(jax.random.uniform(ks[6], (DEPTH, SSD_HEADS), f32, 1.0, 16.0)),
        "d_skip": gain(ks[7], (DEPTH, SSD_HEADS)),
        "ssd_norm_g": gain(ks[8], (DEPTH, D_SSD)),
        "attn_sinks": nrm(ks[9], (DEPTH, ATTN_Q_HEADS), 0.5),
        "rel_bias": nrm(ks[10], (REL_BUCKETS, ATTN_Q_HEADS), 0.5),
        "attn_norm_g": gain(ks[11], (DEPTH, D_ATTN)),
        "w_out": nrm(ks[12], (DEPTH, D_MIX, D_MODEL), D_MIX ** -0.5),
        "norm_ffn_g": gain(ks[13], (DEPTH, D_MODEL)),
        "w_router": nrm(ks[14], (DEPTH, D_MODEL, N_EXPERTS), D_MODEL ** -0.5),
        "b_router": nrm(ks[15], (DEPTH, N_EXPERTS), 0.01),
        "w1": nrm(ks[16], (DEPTH, N_EXPERTS, D_MODEL, 2 * D_EXPERT), D_MODEL ** -0.5),
        "b1": nrm(ks[17], (DEPTH, N_EXPERTS, 2 * D_EXPERT), 0.02),
        "w2": nrm(ks[18], (DEPTH, N_EXPERTS, D_EXPERT, D_MODEL), D_EXPERT ** -0.5),
        "b2": nrm(ks[19], (DEPTH, N_EXPERTS, D_MODEL), 0.02),
        "norm_final_g": gain(ks[20], (D_MODEL,)),
    }


def reference(x, norm_mix_g, w_in, conv_w, conv_b, dt_bias, a_log, d_skip, ssd_norm_g,
              attn_sinks, rel_bias, attn_norm_g, w_out, norm_ffn_g, w_router, b_router,
              w1, b1, w2, b2, norm_final_g):
    h = x
    o1 = D_SSD
    o2 = o1 + D_CONV
    o3 = o2 + SSD_HEADS
    o4 = o3 + D_ATTN
    o5 = o4 + D_KV
    for layer in range(DEPTH):
        u = rmsnorm(h, norm_mix_g[layer])
        proj = u @ w_in[layer]
        z, xbc, dt_raw = proj[..., :o1], proj[..., o1:o2], proj[..., o2:o3]
        q, k, v = proj[..., o3:o4], proj[..., o4:o5], proj[..., o5:]
        y_ssd = ssd_group(z, xbc, dt_raw, conv_w[layer], conv_b[layer], dt_bias[layer],
                          a_log[layer], d_skip[layer], ssd_norm_g[layer])
        y_attn = rmsnorm(swa_group(q, k, v, attn_sinks[layer], rel_bias), attn_norm_g[layer])
        h = h + jnp.concatenate([y_ssd, y_attn], axis=-1) @ w_out[layer]
        h = h + moe_ffn(rmsnorm(h, norm_ffn_g[layer]), w_router[layer], b_router[layer],
                        w1[layer], b1[layer], w2[layer], b2[layer])
    return rmsnorm(h, norm_final_g)
```

```python
import functools
import math

import numpy as np
import jax
import jax.numpy as jnp
from jax import lax
from jax.experimental import pallas as pl
from jax.experimental.pallas import tpu as pltpu

F32 = jnp.float32
BF16 = jnp.bfloat16
HIGHEST = lax.Precision.HIGHEST

D_MODEL = 1024
SSD_HEADS = 8
SSD_HEAD_DIM = 64
D_SSD = SSD_HEADS * SSD_HEAD_DIM
SSD_GROUPS = 2
SSD_HEADS_PER_GROUP = SSD_HEADS // SSD_GROUPS
D_STATE = 128
CONV_WIDTH = 4
CHUNK = 128
D_CONV = D_SSD + 2 * SSD_GROUPS * D_STATE
ATTN_Q_HEADS = 8
ATTN_KV_HEADS = 2
ATTN_Q_PER_KV = ATTN_Q_HEADS // ATTN_KV_HEADS
ATTN_HEAD_DIM = 64
D_ATTN = ATTN_Q_HEADS * ATTN_HEAD_DIM
D_KV = ATTN_KV_HEADS * ATTN_HEAD_DIM
WINDOW = 128
REL_BUCKETS = 32
REL_MAX_DIST = 128
N_EXPERTS = 32
TOP_K = 4
D_EXPERT = D_MODEL
SWIGLU_LIMIT = 7.0
SWIGLU_ALPHA = 1.702
RMS_EPS = 1e-5

LANES = 128
SUBLANES = 8
NEG = -1e30
VMEM_LIMIT = 56 * 1024 * 1024

TM_PROJ = 512
TM_ROUTE = 512
TM_MOVE = 256
BM_EXPERT = 256

D_PROJ_PACKED = D_SSD + D_CONV + D_ATTN + 2 * D_KV + LANES


def _params(n_axes):
    return pltpu.CompilerParams(dimension_semantics=("arbitrary",) * n_axes,
                                vmem_limit_bytes=VMEM_LIMIT)


def _rms(x, g):
    return x * lax.rsqrt(jnp.mean(x * x, axis=-1, keepdims=True) + RMS_EPS) * g


def _silu(x):
    return x / (1.0 + jnp.exp(-x))


def _dot(a, b, **kw):
    return jnp.dot(a, b, preferred_element_type=F32, **kw)


def _t5_bucket_table():
    dist = CHUNK + np.arange(CHUNK)[:, None] - np.arange(2 * CHUNK)[None, :]
    in_window = (dist >= 0) & (dist < WINDOW)
    d = np.clip(dist, 0, REL_MAX_DIST)
    max_exact = REL_BUCKETS // 2
    large = max_exact + (np.log(np.maximum(d, 1).astype(np.float32) / max_exact)
                         / math.log(REL_MAX_DIST / max_exact)
                         * (REL_BUCKETS - max_exact)).astype(np.int32)
    large = np.minimum(large, REL_BUCKETS - 1)
    bucket = np.where(d < max_exact, d, large)
    return np.where(in_window, bucket, -1).astype(np.int32)


def _bias_kernel(rb_ref, bucket_ref, o_ref):
    bucket = bucket_ref[...]
    for h in range(ATTN_Q_HEADS):
        acc = jnp.full(bucket.shape, NEG, F32)
        for b in range(REL_BUCKETS):
            acc = jnp.where(bucket == b, rb_ref[b, h], acc)
        o_ref[h] = acc


def _bias_table(rel_bias):
    bucket = jnp.asarray(_t5_bucket_table())
    return pl.pallas_call(
        _bias_kernel,
        out_shape=jax.ShapeDtypeStruct((ATTN_Q_HEADS, CHUNK, 2 * CHUNK), F32),
        in_specs=[pl.BlockSpec(memory_space=pltpu.SMEM),
                  pl.BlockSpec(memory_space=pltpu.VMEM)],
        out_specs=pl.BlockSpec(memory_space=pltpu.VMEM),
        name="bias_table",
    )(rel_bias, bucket)


def _inproj_kernel(x_ref, g_ref, w_ref, z_ref, xbc_ref, q_ref, kv_ref, dt_ref):
    u = _rms(x_ref[...], g_ref[...]).astype(BF16)
    proj = _dot(u, w_ref[...])
    o = 0
    for ref in (z_ref, xbc_ref, q_ref, kv_ref, dt_ref):
        w = ref.shape[1]
        ref[...] = proj[:, o:o + w]
        o += w


def _inproj(x2, g, w_packed):
    t = x2.shape[0]
    widths = (D_SSD, D_CONV, D_ATTN, 2 * D_KV, LANES)
    return pl.pallas_call(
        _inproj_kernel,
        grid=(t // TM_PROJ,),
        in_specs=[pl.BlockSpec((TM_PROJ, D_MODEL), lambda i: (i, 0)),
                  pl.BlockSpec((1, D_MODEL), lambda i: (0, 0)),
                  pl.BlockSpec((D_MODEL, D_PROJ_PACKED), lambda i: (0, 0))],
        out_specs=[pl.BlockSpec((TM_PROJ, w), lambda i: (i, 0)) for w in widths],
        out_shape=[jax.ShapeDtypeStruct((t, w), F32) for w in widths],
        compiler_params=_params(1),
        name="inproj",
    )(x2, g, w_packed)


def _ssd_kernel(z_ref, xbc_ref, dt_ref, cw_ref, cb_ref, dtb_ref, alog_ref, dskip_ref, g_ref,
                expand_ref, o_ref, state_ref, xpad_ref):
    L = CHUNK
    G, R, P, N = SSD_GROUPS, SSD_HEADS_PER_GROUP, SSD_HEAD_DIM, D_STATE
    GW = R * P

    @pl.when(pl.program_id(1) == 0)
    def _():
        state_ref[...] = jnp.zeros(state_ref.shape, F32)
        xpad_ref[0:SUBLANES, :] = jnp.zeros((SUBLANES, D_CONV), F32)

    xpad_ref[SUBLANES:SUBLANES + L, :] = xbc_ref[...]
    acc = jnp.broadcast_to(cb_ref[...], (L, D_CONV))
    for k in range(CONV_WIDTH):
        acc = acc + cw_ref[k:k + 1, :] * xpad_ref[pl.ds(SUBLANES - (CONV_WIDTH - 1) + k, L), :]
    xpad_ref[0:SUBLANES, :] = xpad_ref[L:L + SUBLANES, :]
    xbc = _silu(acc)
    xs = xbc[:, :D_SSD]
    bm = xbc[:, D_SSD:D_SSD + G * N]
    cm = xbc[:, D_SSD + G * N:]

    dtr = dt_ref[...] + dtb_ref[...]
    dt = jnp.maximum(dtr, 0.0) + jnp.log(1.0 + jnp.exp(-jnp.abs(dtr)))
    a_dt = dt * (-jnp.exp(alog_ref[...]))
    ri = lax.broadcasted_iota(jnp.int32, (L, L), 0)
    ci = lax.broadcasted_iota(jnp.int32, (L, L), 1)
    causal = ci <= ri
    a_cum = _dot(causal.astype(F32), a_dt, precision=HIGHEST)
    a_cum_t = _dot(a_dt.T, (ri <= ci).astype(F32), precision=HIGHEST)
    a_last = a_cum[L - 1:L, :]
    stack = jnp.concatenate(
        [dt, jnp.exp(a_cum), jnp.exp(a_last - a_cum),
         jnp.broadcast_to(jnp.exp(a_last), (SUBLANES, LANES))], axis=0)
    ex = _dot(stack, expand_ref[...], precision=HIGHEST)
    dt_x, ea_x, dte_x, cd_x = ex[0:L], ex[L:2 * L], ex[2 * L:3 * L], ex[3 * L:3 * L + 1]
    xdt = xs * dt_x

    ys = []
    for g in range(G):
        bm_g = bm[:, g * N:(g + 1) * N]
        cm_g = cm[:, g * N:(g + 1) * N].astype(BF16)
        cb = lax.dot_general(cm_g, bm_g.astype(BF16), (((1,), (1,)), ((), ())),
                             preferred_element_type=F32)
        xdt_g = xdt[:, g * GW:(g + 1) * GW]
        yd = []
        for r in range(R):
            h = g * R + r
            seg = a_cum[:, h:h + 1] - a_cum_t[h:h + 1, :]
            dec = jnp.exp(jnp.where(causal, seg, NEG))
            yd.append(_dot((cb * dec).astype(BF16), xdt_g[:, r * P:(r + 1) * P].astype(BF16)))
        y_diag = jnp.concatenate(yd, axis=1)
        st = state_ref[g]
        y_off = _dot(cm_g, st.astype(BF16)) * ea_x[:, g * GW:(g + 1) * GW]
        new = _dot(bm_g.T.astype(BF16), (xdt_g * dte_x[:, g * GW:(g + 1) * GW]).astype(BF16))
        state_ref[g] = st * cd_x[:, g * GW:(g + 1) * GW] + new
        ys.append(y_diag + y_off + xs[:, g * GW:(g + 1) * GW] * dskip_ref[:, g * GW:(g + 1) * GW])
    y = jnp.concatenate(ys, axis=1)
    o_ref[...] = _rms(y * _silu(z_ref[...]), g_ref[...])


def _ssd(z, xbc, dt, conv_w, conv_b, dt_bias, a_log, d_skip_x, norm_g, bsz, nc):
    t = z.shape[0]
    expand = np.zeros((LANES, D_SSD), np.float32)
    for h in range(SSD_HEADS):
        expand[h, h * SSD_HEAD_DIM:(h + 1) * SSD_HEAD_DIM] = 1.0
    row = lambda b, c: (b * nc + c, 0)
    fixed = lambda b, c: (0, 0)
    return pl.pallas_call(
        _ssd_kernel,
        grid=(bsz, nc),
        in_specs=[pl.BlockSpec((CHUNK, D_SSD), row),
                  pl.BlockSpec((CHUNK, D_CONV), row),
                  pl.BlockSpec((CHUNK, LANES), row),
                  pl.BlockSpec((CONV_WIDTH, D_CONV), fixed),
                  pl.BlockSpec((1, D_CONV), fixed),
                  pl.BlockSpec((1, LANES), fixed),
                  pl.BlockSpec((1, LANES), fixed),
                  pl.BlockSpec((1, D_SSD), fixed),
                  pl.BlockSpec((1, D_SSD), fixed),
                  pl.BlockSpec((LANES, D_SSD), fixed)],
        out_specs=pl.BlockSpec((CHUNK, D_SSD), row),
        out_shape=jax.ShapeDtypeStruct((t, D_SSD), F32),
        scratch_shapes=[pltpu.VMEM((SSD_GROUPS, D_STATE, SSD_HEADS_PER_GROUP * SSD_HEAD_DIM), F32),
                        pltpu.VMEM((CHUNK + SUBLANES, D_CONV), F32)],
        compiler_params=_params(2),
        name="ssd",
    )(z, xbc, dt, conv_w, conv_b, dt_bias, a_log, d_skip_x, norm_g, jnp.asarray(expand))


def _swa_kernel(sink_ref, q_ref, kv_ref, kvp_ref, bias_ref, g_ref, o_ref):
    L, Dh = CHUNK, ATTN_HEAD_DIM
    q = q_ref[...] * (1.0 / math.sqrt(Dh))
    kv = kv_ref[...]
    kvp = kvp_ref[...]
    col = lax.broadcasted_iota(jnp.int32, (L, 2 * L), 1)
    first_col = jnp.where(pl.program_id(1) > 0, 0, L)
    outs = []
    for hk in range(ATTN_KV_HEADS):
        ks = slice(hk * Dh, (hk + 1) * Dh)
        vs = slice(D_KV + hk * Dh, D_KV + (hk + 1) * Dh)
        kc = jnp.concatenate([kvp[:, ks], kv[:, ks]], axis=0).astype(BF16)
        vc = jnp.concatenate([kvp[:, vs], kv[:, vs]], axis=0).astype(BF16)
        for g in range(ATTN_Q_PER_KV):
            h = hk * ATTN_Q_PER_KV + g
            qh = q[:, h * Dh:(h + 1) * Dh].astype(BF16)
            s = lax.dot_general(qh, kc, (((1,), (1,)), ((), ())), preferred_element_type=F32)
            s = jnp.where(col >= first_col, s + bias_ref[h], NEG)
            sink = sink_ref[h]
            m = jnp.maximum(jnp.max(s, axis=-1, keepdims=True), sink)
            p = jnp.exp(s - m)
            denom = jnp.sum(p, axis=-1, keepdims=True) + jnp.exp(sink - m)
            outs.append(_dot(p.astype(BF16), vc) / denom)
    o_ref[...] = _rms(jnp.concatenate(outs, axis=1), g_ref[...])


def _swa(q, kv, sinks, bias, norm_g, bsz, nb):
    t = q.shape[0]
    row = lambda b, n: (b * nb + n, 0)
    prev = lambda b, n: (b * nb + jnp.maximum(n - 1, 0), 0)
    return pl.pallas_call(
        _swa_kernel,
        grid=(bsz, nb),
        in_specs=[pl.BlockSpec(memory_space=pltpu.SMEM),
                  pl.BlockSpec((CHUNK, D_ATTN), row),
                  pl.BlockSpec((CHUNK, 2 * D_KV), row),
                  pl.BlockSpec((CHUNK, 2 * D_KV), prev),
                  pl.BlockSpec((ATTN_Q_HEADS, CHUNK, 2 * CHUNK), lambda b, n: (0, 0, 0)),
                  pl.BlockSpec((1, D_ATTN), lambda b, n: (0, 0))],
        out_specs=pl.BlockSpec((CHUNK, D_ATTN), row),
        out_shape=jax.ShapeDtypeStruct((t, D_ATTN), F32),
        compiler_params=_params(2),
        name="swa",
    )(sinks, q, kv, kv, bias, norm_g)


def _route_kernel(x_ref, ys_ref, ya_ref, wo1_ref, wo2_ref, g_ref, wr_ref, br_ref,
                  h_ref, u_ref, slab_i_ref, slab_f_ref, cnt_ref, run_ref):
    tm = x_ref.shape[0]

    @pl.when(pl.program_id(0) == 0)
    def _():
        run_ref[...] = jnp.zeros(run_ref.shape, F32)

    h = (x_ref[...] + _dot(ys_ref[...].astype(BF16), wo1_ref[...])
         + _dot(ya_ref[...].astype(BF16), wo2_ref[...]))
    h_ref[...] = h
    u = _rms(h, g_ref[...])
    u_ref[...] = u
    logits = _dot(u, wr_ref[...], precision=HIGHEST) + br_ref[...]

    lane = lax.broadcasted_iota(jnp.int32, (tm, LANES), 1)
    lane_f = lane.astype(F32)
    vals, idxs = [], []
    cur = logits
    for _ in range(TOP_K):
        m = jnp.max(cur, axis=-1, keepdims=True)
        ix = jnp.min(jnp.where(cur == m, lane_f, float(LANES)), axis=-1, keepdims=True)
        vals.append(m)
        idxs.append(ix)
        cur = jnp.where(lane_f == ix, 2.0 * NEG, cur)
    es = [jnp.exp(v - vals[0]) for v in vals]
    den = es[0] + es[1] + es[2] + es[3]

    onehot = jnp.zeros((tm, LANES), F32)
    for ix in idxs:
        onehot = onehot + (lane_f == ix).astype(F32)
    ri = lax.broadcasted_iota(jnp.int32, (tm, tm), 0)
    ci = lax.broadcasted_iota(jnp.int32, (tm, tm), 1)
    before = _dot((ci < ri).astype(BF16), onehot.astype(BF16)) + run_ref[...]
    run_ref[...] = run_ref[...] + jnp.sum(onehot, axis=0, keepdims=True)
    cnt_ref[...] = jnp.broadcast_to(run_ref[...], cnt_ref.shape)

    slab_i = jnp.zeros((tm, LANES), jnp.int32)
    slab_f = jnp.zeros((tm, LANES), F32)
    for k in range(TOP_K):
        rank = jnp.sum(jnp.where(lane_f == idxs[k], before, 0.0), axis=-1, keepdims=True)
        slab_i = jnp.where(lane == k, idxs[k].astype(jnp.int32), slab_i)
        slab_i = jnp.where(lane == TOP_K + k, rank.astype(jnp.int32), slab_i)
        slab_f = jnp.where(lane == k, es[k] / den, slab_f)
    slab_i_ref[...] = slab_i
    slab_f_ref[...] = slab_f


def _route(x2, y_ssd, y_attn, wo1, wo2, g, wr, br):
    t = x2.shape[0]
    tm = TM_ROUTE
    row = lambda i: (i, 0)
    fixed = lambda i: (0, 0)
    return pl.pallas_call(
        _route_kernel,
        grid=(t // tm,),
        in_specs=[pl.BlockSpec((tm, D_MODEL), row),
                  pl.BlockSpec((tm, D_SSD), row),
                  pl.BlockSpec((tm, D_ATTN), row),
                  pl.BlockSpec((D_SSD, D_MODEL), fixed),
                  pl.BlockSpec((D_ATTN, D_MODEL), fixed),
                  pl.BlockSpec((1, D_MODEL), fixed),
                  pl.BlockSpec((D_MODEL, LANES), fixed),
                  pl.BlockSpec((1, LANES), fixed)],
        out_specs=[pl.BlockSpec((tm, D_MODEL), row),
                   pl.BlockSpec((tm, D_MODEL), row),
                   pl.BlockSpec((tm, LANES), row),
                   pl.BlockSpec((tm, LANES), row),
                   pl.BlockSpec((SUBLANES, LANES), fixed)],
        out_shape=[jax.ShapeDtypeStruct((t, D_MODEL), F32),
                   jax.ShapeDtypeStruct((t, D_MODEL), F32),
                   jax.ShapeDtypeStruct((t, LANES), jnp.int32),
                   jax.ShapeDtypeStruct((t, LANES), F32),
                   jax.ShapeDtypeStruct((SUBLANES, LANES), F32)],
        scratch_shapes=[pltpu.VMEM((1, LANES), F32)],
        compiler_params=_params(1),
        name="route",
    )(x2, y_ssd, y_attn, wo1, wo2, g, wr, br)


def _row_copy(src, src_row, dst, dst_row, sem):
    return pltpu.make_async_copy(src.at[pl.ds(src_row, 1), :], dst.at[pl.ds(dst_row, 1), :], sem)


def _dispatch_kernel(dest_ref, zflag_ref, u_ref, xs_ref, zero_ref, sem, zsem):
    tm = u_ref.shape[0]
    base = pl.program_id(0) * (tm * TOP_K)
    bm = zero_ref.shape[0]

    @pl.when(pl.program_id(0) == 0)
    def _():
        zero_ref[...] = jnp.zeros(zero_ref.shape, F32)

        def fill(b):
            return pltpu.make_async_copy(zero_ref, xs_ref.at[pl.ds(b * bm, bm), :], zsem)

        def zstart(b, carry):
            @pl.when(zflag_ref[b] != 0)
            def _():
                fill(b).start()
            return carry

        def zwait(b, carry):
            @pl.when(zflag_ref[b] != 0)
            def _():
                fill(b).wait()
            return carry

        lax.fori_loop(0, zflag_ref.shape[0], zstart, 0)
        lax.fori_loop(0, zflag_ref.shape[0], zwait, 0)

    def issue(r, carry):
        for k in range(TOP_K):
            _row_copy(u_ref, r, xs_ref, dest_ref[base + r * TOP_K + k], sem).start()
        return carry

    lax.fori_loop(0, tm, issue, 0)

    def drain(r, carry):
        for k in range(TOP_K):
            _row_copy(u_ref, r, xs_ref, dest_ref[base + r * TOP_K + k], sem).wait()
        return carry

    lax.fori_loop(0, tm, drain, 0)


def _dispatch(dest_flat, zero_flag, u):
    t = u.shape[0]
    tm = TM_MOVE
    n_rows = zero_flag.shape[0] * BM_EXPERT
    return pl.pallas_call(
        _dispatch_kernel,
        grid_spec=pltpu.PrefetchScalarGridSpec(
            num_scalar_prefetch=2,
            grid=(t // tm,),
            in_specs=[pl.BlockSpec((tm, D_MODEL), lambda i, d, zf: (i, 0))],
            out_specs=pl.BlockSpec(memory_space=pl.ANY),
            scratch_shapes=[pltpu.VMEM((BM_EXPERT, D_MODEL), F32),
                            pltpu.SemaphoreType.DMA,
                            pltpu.SemaphoreType.DMA]),
        out_shape=jax.ShapeDtypeStruct((n_rows, D_MODEL), F32),
        compiler_params=_params(1),
        name="dispatch",
    )(dest_flat, zero_flag, u)


def _expert_kernel(be_ref, nu_ref, xs_ref, w1_ref, b1_ref, w2_ref, b2_ref, ys_ref, w1b_ref, w2b_ref):
    i = pl.program_id(0)

    @pl.when(i >= nu_ref[0])
    def _():
        ys_ref[...] = jnp.zeros(ys_ref.shape, F32)

    @pl.when(i < nu_ref[0])
    def _():
        e = be_ref[i]
        prev_e = be_ref[jnp.maximum(i - 1, 0)]

        @pl.when((i == 0) | (prev_e != e))
        def _():
            w1b_ref[...] = w1_ref[0].astype(BF16)
            w2b_ref[...] = w2_ref[0].astype(BF16)

        hid = _dot(xs_ref[...].astype(BF16), w1b_ref[...]) + b1_ref[0]
        x_glu = jnp.minimum(hid[:, :D_EXPERT], SWIGLU_LIMIT)
        x_lin = jnp.clip(hid[:, D_EXPERT:], -SWIGLU_LIMIT, SWIGLU_LIMIT)
        act = x_glu / (1.0 + jnp.exp(-SWIGLU_ALPHA * x_glu)) * (x_lin + 1.0)
        ys_ref[...] = _dot(act.astype(BF16), w2b_ref[...]) + b2_ref[0]


def _experts(blk_expert, n_used, xs, w1, b1, w2, b2):
    n_rows = xs.shape[0]
    bm = BM_EXPERT
    nblk = n_rows // bm
    row = lambda i, be, nu: (jnp.minimum(i, nu[0] - 1), 0)
    wsel = lambda i, be, nu: (be[jnp.minimum(i, nu[0] - 1)], 0, 0)
    return pl.pallas_call(
        _expert_kernel,
        grid_spec=pltpu.PrefetchScalarGridSpec(
            num_scalar_prefetch=2,
            grid=(nblk,),
            in_specs=[pl.BlockSpec((bm, D_MODEL), row),
                      pl.BlockSpec((1, D_MODEL, 2 * D_EXPERT), wsel),
                      pl.BlockSpec((1, 1, 2 * D_EXPERT), wsel),
                      pl.BlockSpec((1, D_EXPERT, D_MODEL), wsel),
                      pl.BlockSpec((1, 1, D_MODEL), wsel)],
            out_specs=pl.BlockSpec((bm, D_MODEL), lambda i, be, nu: (i, 0)),
            scratch_shapes=[pltpu.VMEM((D_MODEL, 2 * D_EXPERT), BF16),
                            pltpu.VMEM((D_EXPERT, D_MODEL), BF16)]),
        out_shape=jax.ShapeDtypeStruct((n_rows, D_MODEL), F32),
        compiler_params=_params(1),
        name="experts",
    )(blk_expert, n_used, xs, w1, b1, w2, b2)


def _combine_kernel(dest_ref, ys_ref, h_ref, gate_ref, g_ref, o_ref, buf_ref, sem):
    tm = h_ref.shape[0]
    base = pl.program_id(0) * (tm * TOP_K)

    def issue(r, carry):
        for k in range(TOP_K):
            _row_copy(ys_ref, dest_ref[base + r * TOP_K + k], buf_ref.at[k], r, sem).start()
        return carry

    lax.fori_loop(0, tm, issue, 0)

    def drain(r, carry):
        for k in range(TOP_K):
            _row_copy(ys_ref, dest_ref[base + r * TOP_K + k], buf_ref.at[k], r, sem).wait()
        return carry

    lax.fori_loop(0, tm, drain, 0)

    acc = h_ref[...]
    gate = gate_ref[...]
    for k in range(TOP_K):
        acc = acc + gate[:, k:k + 1] * buf_ref[k]
    o_ref[...] = _rms(acc, g_ref[...])


def _combine(dest_flat, ys, h, gates, g):
    t = h.shape[0]
    tm = TM_MOVE
    return pl.pallas_call(
        _combine_kernel,
        grid_spec=pltpu.PrefetchScalarGridSpec(
            num_scalar_prefetch=1,
            grid=(t // tm,),
            in_specs=[pl.BlockSpec(memory_space=pl.ANY),
                      pl.BlockSpec((tm, D_MODEL), lambda i, d: (i, 0)),
                      pl.BlockSpec((tm, LANES), lambda i, d: (i, 0)),
                      pl.BlockSpec((1, D_MODEL), lambda i, d: (0, 0))],
            out_specs=pl.BlockSpec((tm, D_MODEL), lambda i, d: (i, 0)),
            scratch_shapes=[pltpu.VMEM((TOP_K, tm, D_MODEL), F32),
                            pltpu.SemaphoreType.DMA]),
        out_shape=jax.ShapeDtypeStruct((t, D_MODEL), F32),
        compiler_params=_params(1),
        name="combine",
    )(dest_flat, ys, h, gates, g)


def _pad_lanes(v, fill=0.0):
    return jnp.pad(v.reshape(1, -1), ((0, 0), (0, LANES - v.shape[-1])), constant_values=fill)


def kernel(x, norm_mix_g, w_in, conv_w, conv_b, dt_bias, a_log, d_skip, ssd_norm_g, attn_sinks,
           rel_bias, attn_norm_g, w_out, norm_ffn_g, w_router, b_router, w1, b1, w2, b2,
           norm_final_g):
    bsz, seq, d = x.shape
    t = bsz * seq
    nc = seq // CHUNK
    depth = w_in.shape[0]
    o2 = D_SSD + D_CONV
    o3 = o2 + SSD_HEADS
    bias = _bias_table(rel_bias)

    assert depth == 1, "the final norm is fused into the combine kernel: single layer only"
    h = x.reshape(t, d)
    for layer in range(depth):
        w = w_in[layer]
        w_packed = jnp.concatenate(
            [w[:, :o2], w[:, o3:], w[:, o2:o3], jnp.zeros((d, LANES - SSD_HEADS), w.dtype)],
            axis=1).astype(BF16)
        z, xbc, q, kv, dt = _inproj(h, norm_mix_g[layer].reshape(1, d), w_packed)
        y_ssd = _ssd(z, xbc, dt, conv_w[layer], conv_b[layer].reshape(1, -1),
                     _pad_lanes(dt_bias[layer]), _pad_lanes(a_log[layer]),
                     jnp.repeat(d_skip[layer], SSD_HEAD_DIM).reshape(1, -1),
                     ssd_norm_g[layer].reshape(1, -1), bsz, nc)
        y_attn = _swa(q, kv, attn_sinks[layer], bias, attn_norm_g[layer].reshape(1, -1), bsz, nc)

        wo = w_out[layer].astype(BF16)
        wr = jnp.pad(w_router[layer], ((0, 0), (0, LANES - N_EXPERTS)))
        br = _pad_lanes(b_router[layer], NEG)
        h_mid, u, slab_i, gates, cnt = _route(
            h, y_ssd, y_attn, wo[:D_SSD], wo[D_SSD:], norm_ffn_g[layer].reshape(1, d), wr, br)

        bm = BM_EXPERT
        nblk = (t * TOP_K) // bm + N_EXPERTS
        counts = cnt[0, :N_EXPERTS].astype(jnp.int32)
        padded = ((counts + bm - 1) // bm) * bm
        pend = jnp.cumsum(padded)
        pstart = pend - padded
        dest = (pstart[slab_i[:, :TOP_K]] + slab_i[:, TOP_K:2 * TOP_K]).reshape(-1)
        n_used = (pend[-1:] // bm).astype(jnp.int32)
        blk_expert = jnp.minimum(
            jnp.searchsorted(pend, jnp.arange(nblk, dtype=jnp.int32) * bm, side="right"),
            N_EXPERTS - 1).astype(jnp.int32)

        blk = jnp.arange(nblk, dtype=jnp.int32)
        next_expert = jnp.concatenate([blk_expert[1:], blk_expert[-1:]])
        zero_flag = ((blk >= n_used[0] - 1) | (next_expert != blk_expert)).astype(jnp.int32)

        xs = _dispatch(dest, zero_flag, u)
        ys = _experts(blk_expert, n_used, xs, w1[layer], b1[layer].reshape(N_EXPERTS, 1, -1),
                      w2[layer], b2[layer].reshape(N_EXPERTS, 1, -1))
        h = _combine(dest, ys, h_mid, gates, norm_final_g.reshape(1, d))
    return h.reshape(bsz, seq, d)
```

```python
import functools
import math

import numpy as np
import jax
import jax.numpy as jnp
from jax import lax
from jax.experimental import pallas as pl
from jax.experimental.pallas import tpu as pltpu

F32 = jnp.float32
BF16 = jnp.bfloat16
HIGHEST = lax.Precision.HIGHEST

D_MODEL = 1024
SSD_HEADS = 8
SSD_HEAD_DIM = 64
D_SSD = SSD_HEADS * SSD_HEAD_DIM
SSD_GROUPS = 2
SSD_HEADS_PER_GROUP = SSD_HEADS // SSD_GROUPS
D_STATE = 128
CONV_WIDTH = 4
CHUNK = 128
D_CONV = D_SSD + 2 * SSD_GROUPS * D_STATE
ATTN_Q_HEADS = 8
ATTN_KV_HEADS = 2
ATTN_Q_PER_KV = ATTN_Q_HEADS // ATTN_KV_HEADS
ATTN_HEAD_DIM = 64
D_ATTN = ATTN_Q_HEADS * ATTN_HEAD_DIM
D_KV = ATTN_KV_HEADS * ATTN_HEAD_DIM
WINDOW = 128
REL_BUCKETS = 32
REL_MAX_DIST = 128
N_EXPERTS = 32
TOP_K = 4
D_EXPERT = D_MODEL
SWIGLU_LIMIT = 7.0
SWIGLU_ALPHA = 1.702
RMS_EPS = 1e-5

LANES = 128
SUBLANES = 8
NEG = -1e30
VMEM_LIMIT = 56 * 1024 * 1024

TM_PROJ = 512
TM_ROUTE = 512
BM_EXPERT = 256
R_SORTED = TM_ROUTE * TOP_K + N_EXPERTS * SUBLANES

D_PROJ_PACKED = D_SSD + D_CONV + D_ATTN + 2 * D_KV + LANES


def _params(n_axes):
    return pltpu.CompilerParams(dimension_semantics=("arbitrary",) * n_axes,
                                vmem_limit_bytes=VMEM_LIMIT)


def _rms(x, g):
    return x * lax.rsqrt(jnp.mean(x * x, axis=-1, keepdims=True) + RMS_EPS) * g


def _silu(x):
    return x / (1.0 + jnp.exp(-x))


def _dot(a, b, **kw):
    return jnp.dot(a, b, preferred_element_type=F32, **kw)


def _t5_bucket_table():
    dist = CHUNK + np.arange(CHUNK)[:, None] - np.arange(2 * CHUNK)[None, :]
    in_window = (dist >= 0) & (dist < WINDOW)
    d = np.clip(dist, 0, REL_MAX_DIST)
    max_exact = REL_BUCKETS // 2
    large = max_exact + (np.log(np.maximum(d, 1).astype(np.float32) / max_exact)
                         / math.log(REL_MAX_DIST / max_exact)
                         * (REL_BUCKETS - max_exact)).astype(np.int32)
    large = np.minimum(large, REL_BUCKETS - 1)
    bucket = np.where(d < max_exact, d, large)
    return np.where(in_window, bucket, -1).astype(np.int32)


def _bias_kernel(rb_ref, bucket_ref, o_ref):
    bucket = bucket_ref[...]
    for h in range(ATTN_Q_HEADS):
        acc = jnp.full(bucket.shape, NEG, F32)
        for b in range(REL_BUCKETS):
            acc = jnp.where(bucket == b, rb_ref[b, h], acc)
        o_ref[h] = acc


def _bias_table(rel_bias):
    bucket = jnp.asarray(_t5_bucket_table())
    return pl.pallas_call(
        _bias_kernel,
        out_shape=jax.ShapeDtypeStruct((ATTN_Q_HEADS, CHUNK, 2 * CHUNK), F32),
        in_specs=[pl.BlockSpec(memory_space=pltpu.SMEM),
                  pl.BlockSpec(memory_space=pltpu.VMEM)],
        out_specs=pl.BlockSpec(memory_space=pltpu.VMEM),
        name="bias_table",
    )(rel_bias, bucket)


def _inproj_kernel(x_ref, g_ref, w_ref, z_ref, xbc_ref, q_ref, kv_ref, dt_ref):
    u = _rms(x_ref[...], g_ref[...]).astype(BF16)
    proj = _dot(u, w_ref[...])
    o = 0
    for ref in (z_ref, xbc_ref, q_ref, kv_ref, dt_ref):
        w = ref.shape[1]
        ref[...] = proj[:, o:o + w]
        o += w


def _inproj(x2, g, w_packed):
    t = x2.shape[0]
    widths = (D_SSD, D_CONV, D_ATTN, 2 * D_KV, LANES)
    return pl.pallas_call(
        _inproj_kernel,
        grid=(t // TM_PROJ,),
        in_specs=[pl.BlockSpec((TM_PROJ, D_MODEL), lambda i: (i, 0)),
                  pl.BlockSpec((1, D_MODEL), lambda i: (0, 0)),
                  pl.BlockSpec((D_MODEL, D_PROJ_PACKED), lambda i: (0, 0))],
        out_specs=[pl.BlockSpec((TM_PROJ, w), lambda i: (i, 0)) for w in widths],
        out_shape=[jax.ShapeDtypeStruct((t, w), F32) for w in widths],
        compiler_params=_params(1),
        name="inproj",
    )(x2, g, w_packed)


def _ssd_kernel(z_ref, xbc_ref, dt_ref, cw_ref, cb_ref, dtb_ref, alog_ref, dskip_ref, g_ref,
                expand_ref, o_ref, state_ref, xpad_ref):
    L = CHUNK
    G, R, P, N = SSD_GROUPS, SSD_HEADS_PER_GROUP, SSD_HEAD_DIM, D_STATE
    GW = R * P

    @pl.when(pl.program_id(1) == 0)
    def _():
        state_ref[...] = jnp.zeros(state_ref.shape, F32)
        xpad_ref[0:SUBLANES, :] = jnp.zeros((SUBLANES, D_CONV), F32)

    xpad_ref[SUBLANES:SUBLANES + L, :] = xbc_ref[...]
    acc = jnp.broadcast_to(cb_ref[...], (L, D_CONV))
    for k in range(CONV_WIDTH):
        acc = acc + cw_ref[k:k + 1, :] * xpad_ref[pl.ds(SUBLANES - (CONV_WIDTH - 1) + k, L), :]
    xpad_ref[0:SUBLANES, :] = xpad_ref[L:L + SUBLANES, :]
    xbc = _silu(acc)
    xs = xbc[:, :D_SSD]
    bm = xbc[:, D_SSD:D_SSD + G * N]
    cm = xbc[:, D_SSD + G * N:]

    dtr = dt_ref[...] + dtb_ref[...]
    dt = jnp.maximum(dtr, 0.0) + jnp.log(1.0 + jnp.exp(-jnp.abs(dtr)))
    a_dt = dt * (-jnp.exp(alog_ref[...]))
    ri = lax.broadcasted_iota(jnp.int32, (L, L), 0)
    ci = lax.broadcasted_iota(jnp.int32, (L, L), 1)
    causal = ci <= ri
    a_cum = _dot(causal.astype(F32), a_dt, precision=HIGHEST)
    a_cum_t = _dot(a_dt.T, (ri <= ci).astype(F32), precision=HIGHEST)
    a_last = a_cum[L - 1:L, :]
    stack = jnp.concatenate(
        [dt, jnp.exp(a_cum), jnp.exp(a_last - a_cum),
         jnp.broadcast_to(jnp.exp(a_last), (SUBLANES, LANES))], axis=0)
    ex = _dot(stack, expand_ref[...], precision=HIGHEST)
    dt_x, ea_x, dte_x, cd_x = ex[0:L], ex[L:2 * L], ex[2 * L:3 * L], ex[3 * L:3 * L + 1]
    xdt = xs * dt_x

    ys = []
    for g in range(G):
        bm_g = bm[:, g * N:(g + 1) * N]
        cm_g = cm[:, g * N:(g + 1) * N].astype(BF16)
        cb = lax.dot_general(cm_g, bm_g.astype(BF16), (((1,), (1,)), ((), ())),
                             preferred_element_type=F32)
        xdt_g = xdt[:, g * GW:(g + 1) * GW]
        yd = []
        for r in range(R):
            h = g * R + r
            seg = a_cum[:, h:h + 1] - a_cum_t[h:h + 1, :]
            dec = jnp.exp(jnp.where(causal, seg, NEG))
            yd.append(_dot((cb * dec).astype(BF16), xdt_g[:, r * P:(r + 1) * P].astype(BF16)))
        y_diag = jnp.concatenate(yd, axis=1)
        st = state_ref[g]
        y_off = _dot(cm_g, st.astype(BF16)) * ea_x[:, g * GW:(g + 1) * GW]
        new = _dot(bm_g.T.astype(BF16), (xdt_g * dte_x[:, g * GW:(g + 1) * GW]).astype(BF16))
        state_ref[g] = st * cd_x[:, g * GW:(g + 1) * GW] + new
        ys.append(y_diag + y_off + xs[:, g * GW:(g + 1) * GW] * dskip_ref[:, g * GW:(g + 1) * GW])
    y = jnp.concatenate(ys, axis=1)
    o_ref[...] = _rms(y * _silu(z_ref[...]), g_ref[...])


def _ssd(z, xbc, dt, conv_w, conv_b, dt_bias, a_log, d_skip_x, norm_g, bsz, nc):
    t = z.shape[0]
    expand = np.zeros((LANES, D_SSD), np.float32)
    for h in range(SSD_HEADS):
        expand[h, h * SSD_HEAD_DIM:(h + 1) * SSD_HEAD_DIM] = 1.0
    row = lambda b, c: (b * nc + c, 0)
    fixed = lambda b, c: (0, 0)
    return pl.pallas_call(
        _ssd_kernel,
        grid=(bsz, nc),
        in_specs=[pl.BlockSpec((CHUNK, D_SSD), row),
                  pl.BlockSpec((CHUNK, D_CONV), row),
                  pl.BlockSpec((CHUNK, LANES), row),
                  pl.BlockSpec((CONV_WIDTH, D_CONV), fixed),
                  pl.BlockSpec((1, D_CONV), fixed),
                  pl.BlockSpec((1, LANES), fixed),
                  pl.BlockSpec((1, LANES), fixed),
                  pl.BlockSpec((1, D_SSD), fixed),
                  pl.BlockSpec((1, D_SSD), fixed),
                  pl.BlockSpec((LANES, D_SSD), fixed)],
        out_specs=pl.BlockSpec((CHUNK, D_SSD), row),
        out_shape=jax.ShapeDtypeStruct((t, D_SSD), F32),
        scratch_shapes=[pltpu.VMEM((SSD_GROUPS, D_STATE, SSD_HEADS_PER_GROUP * SSD_HEAD_DIM), F32),
                        pltpu.VMEM((CHUNK + SUBLANES, D_CONV), F32)],
        compiler_params=_params(2),
        name="ssd",
    )(z, xbc, dt, conv_w, conv_b, dt_bias, a_log, d_skip_x, norm_g, jnp.asarray(expand))


def _swa_kernel(sink_ref, q_ref, kv_ref, kvp_ref, bias_ref, g_ref, o_ref):
    L, Dh = CHUNK, ATTN_HEAD_DIM
    q = q_ref[...] * (1.0 / math.sqrt(Dh))
    kv = kv_ref[...]
    kvp = kvp_ref[...]
    col = lax.broadcasted_iota(jnp.int32, (L, 2 * L), 1)
    first_col = jnp.where(pl.program_id(1) > 0, 0, L)
    outs = []
    for hk in range(ATTN_KV_HEADS):
        ks = slice(hk * Dh, (hk + 1) * Dh)
        vs = slice(D_KV + hk * Dh, D_KV + (hk + 1) * Dh)
        kc = jnp.concatenate([kvp[:, ks], kv[:, ks]], axis=0).astype(BF16)
        vc = jnp.concatenate([kvp[:, vs], kv[:, vs]], axis=0).astype(BF16)
        for g in range(ATTN_Q_PER_KV):
            h = hk * ATTN_Q_PER_KV + g
            qh = q[:, h * Dh:(h + 1) * Dh].astype(BF16)
            s = lax.dot_general(qh, kc, (((1,), (1,)), ((), ())), preferred_element_type=F32)
            s = jnp.where(col >= first_col, s + bias_ref[h], NEG)
            sink = sink_ref[h]
            m = jnp.maximum(jnp.max(s, axis=-1, keepdims=True), sink)
            p = jnp.exp(s - m)
            denom = jnp.sum(p, axis=-1, keepdims=True) + jnp.exp(sink - m)
            outs.append(_dot(p.astype(BF16), vc) / denom)
    o_ref[...] = _rms(jnp.concatenate(outs, axis=1), g_ref[...])


def _swa(q, kv, sinks, bias, norm_g, bsz, nb):
    t = q.shape[0]
    row = lambda b, n: (b * nb + n, 0)
    prev = lambda b, n: (b * nb + jnp.maximum(n - 1, 0), 0)
    return pl.pallas_call(
        _swa_kernel,
        grid=(bsz, nb),
        in_specs=[pl.BlockSpec(memory_space=pltpu.SMEM),
                  pl.BlockSpec((CHUNK, D_ATTN), row),
                  pl.BlockSpec((CHUNK, 2 * D_KV), row),
                  pl.BlockSpec((CHUNK, 2 * D_KV), prev),
                  pl.BlockSpec((ATTN_Q_HEADS, CHUNK, 2 * CHUNK), lambda b, n: (0, 0, 0)),
                  pl.BlockSpec((1, D_ATTN), lambda b, n: (0, 0))],
        out_specs=pl.BlockSpec((CHUNK, D_ATTN), row),
        out_shape=jax.ShapeDtypeStruct((t, D_ATTN), F32),
        compiler_params=_params(2),
        name="swa",
    )(sinks, q, kv, kv, bias, norm_g)


def _route_kernel(x_ref, ys_ref, ya_ref, wo1_ref, wo2_ref, g_ref, wr_ref, br_ref,
                  h_ref, u_ref, slab_ref, post_ref, tab_ref, run_ref):
    tm = x_ref.shape[0]

    @pl.when(pl.program_id(0) == 0)
    def _():
        run_ref[...] = jnp.zeros(run_ref.shape, F32)

    h = (x_ref[...] + _dot(ys_ref[...].astype(BF16), wo1_ref[...])
         + _dot(ya_ref[...].astype(BF16), wo2_ref[...]))
    h_ref[...] = h
    u = _rms(h, g_ref[...])
    u_ref[...] = u.astype(BF16)
    logits = _dot(u, wr_ref[...], precision=HIGHEST) + br_ref[...]

    lane = lax.broadcasted_iota(jnp.int32, (tm, LANES), 1)
    lane_f = lane.astype(F32)
    vals, idxs = [], []
    cur = logits
    for _ in range(TOP_K):
        m = jnp.max(cur, axis=-1, keepdims=True)
        ix = jnp.min(jnp.where(cur == m, lane_f, float(LANES)), axis=-1, keepdims=True)
        vals.append(m)
        idxs.append(ix)
        cur = jnp.where(lane_f == ix, 2.0 * NEG, cur)
    es = [jnp.exp(v - vals[0]) for v in vals]
    den = es[0] + es[1] + es[2] + es[3]

    onehot = jnp.zeros((tm, LANES), F32)
    for ix in idxs:
        onehot = onehot + (lane_f == ix).astype(F32)
    ri = lax.broadcasted_iota(jnp.int32, (tm, tm), 0)
    ci = lax.broadcasted_iota(jnp.int32, (tm, tm), 1)
    before = _dot((ci < ri).astype(BF16), onehot.astype(BF16))
    cnt = jnp.sum(onehot, axis=0, keepdims=True)
    cnt8 = jnp.floor((cnt + (SUBLANES - 1)) * (1.0 / SUBLANES)) * SUBLANES
    el = lax.broadcasted_iota(jnp.int32, (LANES, LANES), 0)
    ec = lax.broadcasted_iota(jnp.int32, (LANES, LANES), 1)
    seg_start = _dot(jnp.broadcast_to(cnt8, (SUBLANES, LANES)), (el < ec).astype(F32),
                     precision=HIGHEST)[0:1, :]
    where_to = before + seg_start

    slab = jnp.zeros((tm, LANES), F32)
    for k in range(TOP_K):
        pos = jnp.sum(jnp.where(lane_f == idxs[k], where_to, 0.0), axis=-1, keepdims=True)
        slab = jnp.where(lane == k, pos, slab)
        slab = jnp.where(lane == TOP_K + k, es[k] / den, slab)
    slab_ref[...] = slab
    post_ref[...] = slab.T[0:SUBLANES, :].astype(jnp.int32)

    row = lax.broadcasted_iota(jnp.int32, (SUBLANES, LANES), 0)
    tab = jnp.where(row == 0, cnt8, jnp.where(row == 1, run_ref[...], jnp.where(row == 2, seg_start, 0.0)))
    tab_ref[...] = tab.astype(jnp.int32)
    run_ref[...] = run_ref[...] + cnt8


def _route(x2, y_ssd, y_attn, wo1, wo2, g, wr, br):
    t = x2.shape[0]
    tm = TM_ROUTE
    row = lambda i: (i, 0)
    fixed = lambda i: (0, 0)
    return pl.pallas_call(
        _route_kernel,
        grid=(t // tm,),
        in_specs=[pl.BlockSpec((tm, D_MODEL), row),
                  pl.BlockSpec((tm, D_SSD), row),
                  pl.BlockSpec((tm, D_ATTN), row),
                  pl.BlockSpec((D_SSD, D_MODEL), fixed),
                  pl.BlockSpec((D_ATTN, D_MODEL), fixed),
                  pl.BlockSpec((1, D_MODEL), fixed),
                  pl.BlockSpec((D_MODEL, LANES), fixed),
                  pl.BlockSpec((1, LANES), fixed)],
        out_specs=[pl.BlockSpec((tm, D_MODEL), row),
                   pl.BlockSpec((tm, D_MODEL), row),
                   pl.BlockSpec((tm, LANES), row),
                   pl.BlockSpec((SUBLANES, tm), lambda i: (0, i)),
                   pl.BlockSpec((SUBLANES, LANES), row)],
        out_shape=[jax.ShapeDtypeStruct((t, D_MODEL), F32),
                   jax.ShapeDtypeStruct((t, D_MODEL), BF16),
                   jax.ShapeDtypeStruct((t, LANES), F32),
                   jax.ShapeDtypeStruct((SUBLANES, t), jnp.int32),
                   jax.ShapeDtypeStruct((t // tm * SUBLANES, LANES), jnp.int32)],
        scratch_shapes=[pltpu.VMEM((1, LANES), F32)],
        compiler_params=_params(1),
        name="route",
    )(x2, y_ssd, y_attn, wo1, wo2, g, wr, br)


def _for_each_segment_chunk(tile, cnt_ref, fn):
    def per_expert(e, carry):
        seg = tile * N_EXPERTS + e

        def per_chunk(j, c):
            fn(seg, j * SUBLANES)
            return c

        lax.fori_loop(0, cnt_ref[seg] // SUBLANES, per_chunk, 0)
        return carry

    lax.fori_loop(0, N_EXPERTS, per_expert, 0)


def _chunk(ref, row):
    return ref.at[pl.ds(pl.multiple_of(row, SUBLANES), SUBLANES), :]


def _dispatch_kernel(cnt_ref, segst_ref, segdst_ref, zflag_ref, u_ref, post_ref, xs_ref,
                     buf_ref, zero_ref, sems, zsem):
    tm = u_ref.shape[0]
    i = pl.program_id(0)
    slot = i % 2
    bm = zero_ref.shape[0]

    @pl.when(pl.program_id(0) == 0)
    def _():
        zero_ref[...] = jnp.zeros(zero_ref.shape, F32)

        def fill(b):
            return pltpu.make_async_copy(zero_ref, xs_ref.at[pl.ds(b * bm, bm), :], zsem)

        def zstart(b, carry):
            @pl.when(zflag_ref[b] != 0)
            def _():
                fill(b).start()
            return carry

        def zwait(b, carry):
            @pl.when(zflag_ref[b] != 0)
            def _():
                fill(b).wait()
            return carry

        lax.fori_loop(0, zflag_ref.shape[0], zstart, 0)
        lax.fori_loop(0, zflag_ref.shape[0], zwait, 0)

    pos = post_ref[...]
    j = lax.broadcasted_iota(jnp.int32, (R_SORTED, tm), 0)
    sel = jnp.zeros((R_SORTED, tm), F32)
    for k in range(TOP_K):
        sel = jnp.where(j == pos[k:k + 1, :], 1.0, sel)
    buf_ref[slot] = _dot(sel.astype(BF16), u_ref[...])

    def copies(tile, slot_, action):
        def fn(seg, off):
            action(pltpu.make_async_copy(_chunk(buf_ref.at[slot_], segst_ref[seg] + off),
                                         _chunk(xs_ref, segdst_ref[seg] + off), sems.at[slot_]))
        _for_each_segment_chunk(tile, cnt_ref, fn)

    copies(i, slot, lambda c: c.start())

    @pl.when(i > 0)
    def _():
        copies(i - 1, 1 - slot, lambda c: c.wait())

    @pl.when(i == pl.num_programs(0) - 1)
    def _():
        copies(i, slot, lambda c: c.wait())


def _dispatch(cnt8, segst, segdst, zero_flag, u, post):
    t = u.shape[0]
    tm = TM_ROUTE
    n_rows = zero_flag.shape[0] * BM_EXPERT
    return pl.pallas_call(
        _dispatch_kernel,
        grid_spec=pltpu.PrefetchScalarGridSpec(
            num_scalar_prefetch=4,
            grid=(t // tm,),
            in_specs=[pl.BlockSpec((tm, D_MODEL), lambda i, *_: (i, 0)),
                      pl.BlockSpec((SUBLANES, tm), lambda i, *_: (0, i))],
            out_specs=pl.BlockSpec(memory_space=pl.ANY),
            scratch_shapes=[pltpu.VMEM((2, R_SORTED, D_MODEL), F32),
                            pltpu.VMEM((BM_EXPERT, D_MODEL), F32),
                            pltpu.SemaphoreType.DMA((2,)),
                            pltpu.SemaphoreType.DMA]),
        out_shape=jax.ShapeDtypeStruct((n_rows, D_MODEL), F32),
        compiler_params=_params(1),
        name="dispatch",
    )(cnt8, segst, segdst, zero_flag, u, post)


def _expert_kernel(be_ref, nu_ref, xs_ref, w1_ref, b1_ref, w2_ref, b2_ref, ys_ref, w1b_ref, w2b_ref):
    i = pl.program_id(0)

    @pl.when(i >= nu_ref[0])
    def _():
        ys_ref[...] = jnp.zeros(ys_ref.shape, F32)

    @pl.when(i < nu_ref[0])
    def _():
        e = be_ref[i]
        prev_e = be_ref[jnp.maximum(i - 1, 0)]

        @pl.when((i == 0) | (prev_e != e))
        def _():
            w1b_ref[...] = w1_ref[0].astype(BF16)
            w2b_ref[...] = w2_ref[0].astype(BF16)

        hid = _dot(xs_ref[...].astype(BF16), w1b_ref[...]) + b1_ref[0]
        x_glu = jnp.minimum(hid[:, :D_EXPERT], SWIGLU_LIMIT)
        x_lin = jnp.clip(hid[:, D_EXPERT:], -SWIGLU_LIMIT, SWIGLU_LIMIT)
        act = x_glu / (1.0 + jnp.exp(-SWIGLU_ALPHA * x_glu)) * (x_lin + 1.0)
        ys_ref[...] = _dot(act.astype(BF16), w2b_ref[...]) + b2_ref[0]


def _experts(blk_expert, n_used, xs, w1, b1, w2, b2):
    n_rows = xs.shape[0]
    bm = BM_EXPERT
    nblk = n_rows // bm
    row = lambda i, be, nu: (jnp.minimum(i, nu[0] - 1), 0)
    wsel = lambda i, be, nu: (be[jnp.minimum(i, nu[0] - 1)], 0, 0)
    return pl.pallas_call(
        _expert_kernel,
        grid_spec=pltpu.PrefetchScalarGridSpec(
            num_scalar_prefetch=2,
            grid=(nblk,),
            in_specs=[pl.BlockSpec((bm, D_MODEL), row),
                      pl.BlockSpec((1, D_MODEL, 2 * D_EXPERT), wsel),
                      pl.BlockSpec((1, 1, 2 * D_EXPERT), wsel),
                      pl.BlockSpec((1, D_EXPERT, D_MODEL), wsel),
                      pl.BlockSpec((1, 1, D_MODEL), wsel)],
            out_specs=pl.BlockSpec((bm, D_MODEL), lambda i, be, nu: (i, 0)),
            scratch_shapes=[pltpu.VMEM((D_MODEL, 2 * D_EXPERT), BF16),
                            pltpu.VMEM((D_EXPERT, D_MODEL), BF16)]),
        out_shape=jax.ShapeDtypeStruct((n_rows, D_MODEL), F32),
        compiler_params=_params(1),
        name="experts",
    )(blk_expert, n_used, xs, w1, b1, w2, b2)


def _combine_kernel(cnt_ref, segst_ref, segdst_ref, ys_ref, h_ref, slab_ref, g_ref, o_ref,
                    buf_ref, sems):
    tm = h_ref.shape[0]
    i = pl.program_id(0)
    slot = i % 2

    def copies(tile, slot_, action):
        def fn(seg, off):
            action(pltpu.make_async_copy(_chunk(ys_ref, segdst_ref[seg] + off),
                                         _chunk(buf_ref.at[slot_], segst_ref[seg] + off),
                                         sems.at[slot_]))
        _for_each_segment_chunk(tile, cnt_ref, fn)

    @pl.when(i == 0)
    def _():
        buf_ref[...] = jnp.zeros(buf_ref.shape, F32)
        copies(0, 0, lambda c: c.start())

    @pl.when(i + 1 < pl.num_programs(0))
    def _():
        copies(i + 1, 1 - slot, lambda c: c.start())

    copies(i, slot, lambda c: c.wait())

    slab = slab_ref[...]
    lane = lax.broadcasted_iota(jnp.int32, (tm, R_SORTED), 1).astype(F32)
    wmat = jnp.zeros((tm, R_SORTED), F32)
    for k in range(TOP_K):
        wmat = jnp.where(lane == slab[:, k:k + 1], slab[:, TOP_K + k:TOP_K + k + 1], wmat)
    moe = _dot(wmat.astype(BF16), buf_ref[slot].astype(BF16))
    o_ref[...] = _rms(h_ref[...] + moe, g_ref[...])


def _combine(cnt8, segst, segdst, ys, h, slab, g):
    t = h.shape[0]
    tm = TM_ROUTE
    return pl.pallas_call(
        _combine_kernel,
        grid_spec=pltpu.PrefetchScalarGridSpec(
            num_scalar_prefetch=3,
            grid=(t // tm,),
            in_specs=[pl.BlockSpec(memory_space=pl.ANY),
                      pl.BlockSpec((tm, D_MODEL), lambda i, *_: (i, 0)),
                      pl.BlockSpec((tm, LANES), lambda i, *_: (i, 0)),
                      pl.BlockSpec((1, D_MODEL), lambda i, *_: (0, 0))],
            out_specs=pl.BlockSpec((tm, D_MODEL), lambda i, *_: (i, 0)),
            scratch_shapes=[pltpu.VMEM((2, R_SORTED, D_MODEL), F32),
                            pltpu.SemaphoreType.DMA((2,))]),
        out_shape=jax.ShapeDtypeStruct((t, D_MODEL), F32),
        compiler_params=_params(1),
        name="combine",
    )(cnt8, segst, segdst, ys, h, slab, g)


def _pad_lanes(v, fill=0.0):
    return jnp.pad(v.reshape(1, -1), ((0, 0), (0, LANES - v.shape[-1])), constant_values=fill)


def kernel(x, norm_mix_g, w_in, conv_w, conv_b, dt_bias, a_log, d_skip, ssd_norm_g, attn_sinks,
           rel_bias, attn_norm_g, w_out, norm_ffn_g, w_router, b_router, w1, b1, w2, b2,
           norm_final_g):
    bsz, seq, d = x.shape
    t = bsz * seq
    nc = seq // CHUNK
    depth = w_in.shape[0]
    o2 = D_SSD + D_CONV
    o3 = o2 + SSD_HEADS
    bias = _bias_table(rel_bias)

    assert depth == 1, "the final norm is fused into the combine kernel: single layer only"
    h = x.reshape(t, d)
    for layer in range(depth):
        w = w_in[layer]
        w_packed = jnp.concatenate(
            [w[:, :o2], w[:, o3:], w[:, o2:o3], jnp.zeros((d, LANES - SSD_HEADS), w.dtype)],
            axis=1).astype(BF16)
        z, xbc, q, kv, dt = _inproj(h, norm_mix_g[layer].reshape(1, d), w_packed)
        y_ssd = _ssd(z, xbc, dt, conv_w[layer], conv_b[layer].reshape(1, -1),
                     _pad_lanes(dt_bias[layer]), _pad_lanes(a_log[layer]),
                     jnp.repeat(d_skip[layer], SSD_HEAD_DIM).reshape(1, -1),
                     ssd_norm_g[layer].reshape(1, -1), bsz, nc)
        y_attn = _swa(q, kv, attn_sinks[layer], bias, attn_norm_g[layer].reshape(1, -1), bsz, nc)

        wo = w_out[layer].astype(BF16)
        wr = jnp.pad(w_router[layer], ((0, 0), (0, LANES - N_EXPERTS)))
        br = _pad_lanes(b_router[layer], NEG)
        h_mid, u, slab, post, tab = _route(
            h, y_ssd, y_attn, wo[:D_SSD], wo[D_SSD:], norm_ffn_g[layer].reshape(1, d), wr, br)

        bm = BM_EXPERT
        ntiles = t // TM_ROUTE
        nblk = (t * TOP_K + ntiles * N_EXPERTS * (SUBLANES - 1)) // bm + N_EXPERTS
        tab = tab.reshape(ntiles, SUBLANES, LANES)[:, :, :N_EXPERTS]
        cnt8, off8, segst = tab[:, 0], tab[:, 1], tab[:, 2]
        total8 = off8[-1] + cnt8[-1]
        padded = ((total8 + bm - 1) // bm) * bm
        pend = jnp.cumsum(padded)
        segdst = (pend - padded)[None, :] + off8
        n_used = pend[-1:] // bm
        blk = jnp.arange(nblk, dtype=jnp.int32)
        blk_expert = jnp.minimum(
            jnp.sum((blk[:, None] * bm >= pend[None, :]).astype(jnp.int32), axis=1), N_EXPERTS - 1)
        next_expert = jnp.concatenate([blk_expert[1:], blk_expert[-1:]])
        zero_flag = ((blk >= n_used[0] - 1) | (next_expert != blk_expert)).astype(jnp.int32)
        cnt8, segst, segdst = cnt8.reshape(-1), segst.reshape(-1), segdst.reshape(-1)

        xs = _dispatch(cnt8, segst, segdst, zero_flag, u, post)
        ys = _experts(blk_expert, n_used, xs, w1[layer], b1[layer].reshape(N_EXPERTS, 1, -1),
                      w2[layer], b2[layer].reshape(N_EXPERTS, 1, -1))
        h = _combine(cnt8, segst, segdst, ys, h_mid, slab, norm_final_g.reshape(1, d))
    return h.reshape(bsz, seq, d)
```

```python
import functools
import math

import numpy as np
import jax
import jax.numpy as jnp
from jax import lax
from jax.experimental import pallas as pl
from jax.experimental.pallas import tpu as pltpu

F32 = jnp.float32
BF16 = jnp.bfloat16
HIGHEST = lax.Precision.HIGHEST

D_MODEL = 1024
SSD_HEADS = 8
SSD_HEAD_DIM = 64
D_SSD = SSD_HEADS * SSD_HEAD_DIM
SSD_GROUPS = 2
SSD_HEADS_PER_GROUP = SSD_HEADS // SSD_GROUPS
D_STATE = 128
CONV_WIDTH = 4
CHUNK = 128
D_CONV = D_SSD + 2 * SSD_GROUPS * D_STATE
ATTN_Q_HEADS = 8
ATTN_KV_HEADS = 2
ATTN_Q_PER_KV = ATTN_Q_HEADS // ATTN_KV_HEADS
ATTN_HEAD_DIM = 64
D_ATTN = ATTN_Q_HEADS * ATTN_HEAD_DIM
D_KV = ATTN_KV_HEADS * ATTN_HEAD_DIM
WINDOW = 128
REL_BUCKETS = 32
REL_MAX_DIST = 128
N_EXPERTS = 32
TOP_K = 4
D_EXPERT = D_MODEL
SWIGLU_LIMIT = 7.0
SWIGLU_ALPHA = 1.702
RMS_EPS = 1e-5

LANES = 128
SUBLANES = 8
NEG = -1e30
VMEM_LIMIT = 56 * 1024 * 1024

TM_PROJ = 512
TM_ROUTE = 512
BM_EXPERT = 256
R_SORTED = TM_ROUTE * TOP_K + N_EXPERTS * SUBLANES

D_PROJ_PACKED = D_SSD + D_CONV + D_ATTN + 2 * D_KV + LANES


def _params(n_axes):
    return pltpu.CompilerParams(dimension_semantics=("arbitrary",) * n_axes,
                                vmem_limit_bytes=VMEM_LIMIT)


def _rms(x, g):
    return x * lax.rsqrt(jnp.mean(x * x, axis=-1, keepdims=True) + RMS_EPS) * g


def _silu(x):
    return x / (1.0 + jnp.exp(-x))


def _dot(a, b, **kw):
    return jnp.dot(a, b, preferred_element_type=F32, **kw)


def _t5_bucket_table():
    dist = CHUNK + np.arange(CHUNK)[:, None] - np.arange(2 * CHUNK)[None, :]
    in_window = (dist >= 0) & (dist < WINDOW)
    d = np.clip(dist, 0, REL_MAX_DIST)
    max_exact = REL_BUCKETS // 2
    large = max_exact + (np.log(np.maximum(d, 1).astype(np.float32) / max_exact)
                         / math.log(REL_MAX_DIST / max_exact)
                         * (REL_BUCKETS - max_exact)).astype(np.int32)
    large = np.minimum(large, REL_BUCKETS - 1)
    bucket = np.where(d < max_exact, d, large)
    return np.where(in_window, bucket, -1).astype(np.int32)


def _bias_kernel(rb_ref, bucket_ref, o_ref):
    bucket = bucket_ref[...]
    for h in range(ATTN_Q_HEADS):
        acc = jnp.full(bucket.shape, NEG, F32)
        for b in range(REL_BUCKETS):
            acc = jnp.where(bucket == b, rb_ref[b, h], acc)
        o_ref[h] = acc


def _bias_table(rel_bias):
    bucket = jnp.asarray(_t5_bucket_table())
    return pl.pallas_call(
        _bias_kernel,
        out_shape=jax.ShapeDtypeStruct((ATTN_Q_HEADS, CHUNK, 2 * CHUNK), F32),
        in_specs=[pl.BlockSpec(memory_space=pltpu.SMEM),
                  pl.BlockSpec(memory_space=pltpu.VMEM)],
        out_specs=pl.BlockSpec(memory_space=pltpu.VMEM),
        name="bias_table",
    )(rel_bias, bucket)


def _inproj_kernel(x_ref, g_ref, w_ref, z_ref, xbc_ref, q_ref, kv_ref, dt_ref):
    u = _rms(x_ref[...], g_ref[...]).astype(BF16)
    proj = _dot(u, w_ref[...])
    o = 0
    for ref in (z_ref, xbc_ref, q_ref, kv_ref, dt_ref):
        w = ref.shape[1]
        ref[...] = proj[:, o:o + w]
        o += w


def _inproj(x2, g, w_packed):
    t = x2.shape[0]
    widths = (D_SSD, D_CONV, D_ATTN, 2 * D_KV, LANES)
    return pl.pallas_call(
        _inproj_kernel,
        grid=(t // TM_PROJ,),
        in_specs=[pl.BlockSpec((TM_PROJ, D_MODEL), lambda i: (i, 0)),
                  pl.BlockSpec((1, D_MODEL), lambda i: (0, 0)),
                  pl.BlockSpec((D_MODEL, D_PROJ_PACKED), lambda i: (0, 0))],
        out_specs=[pl.BlockSpec((TM_PROJ, w), lambda i: (i, 0)) for w in widths],
        out_shape=[jax.ShapeDtypeStruct((t, w), F32) for w in widths],
        compiler_params=_params(1),
        name="inproj",
    )(x2, g, w_packed)


def _ssd_kernel(z_ref, xbc_ref, dt_ref, cw_ref, cb_ref, dtb_ref, alog_ref, dskip_ref, g_ref,
                expand_ref, o_ref, state_ref, xpad_ref):
    L = CHUNK
    G, R, P, N = SSD_GROUPS, SSD_HEADS_PER_GROUP, SSD_HEAD_DIM, D_STATE
    GW = R * P

    @pl.when(pl.program_id(1) == 0)
    def _():
        state_ref[...] = jnp.zeros(state_ref.shape, F32)
        xpad_ref[0:SUBLANES, :] = jnp.zeros((SUBLANES, D_CONV), F32)

    xpad_ref[SUBLANES:SUBLANES + L, :] = xbc_ref[...]
    acc = jnp.broadcast_to(cb_ref[...], (L, D_CONV))
    for k in range(CONV_WIDTH):
        acc = acc + cw_ref[k:k + 1, :] * xpad_ref[pl.ds(SUBLANES - (CONV_WIDTH - 1) + k, L), :]
    xpad_ref[0:SUBLANES, :] = xpad_ref[L:L + SUBLANES, :]
    xbc = _silu(acc)
    xs = xbc[:, :D_SSD]
    bm = xbc[:, D_SSD:D_SSD + G * N]
    cm = xbc[:, D_SSD + G * N:]

    dtr = dt_ref[...] + dtb_ref[...]
    dt = jnp.maximum(dtr, 0.0) + jnp.log(1.0 + jnp.exp(-jnp.abs(dtr)))
    a_dt = dt * (-jnp.exp(alog_ref[...]))
    ri = lax.broadcasted_iota(jnp.int32, (L, L), 0)
    ci = lax.broadcasted_iota(jnp.int32, (L, L), 1)
    causal = ci <= ri
    a_cum = _dot(causal.astype(F32), a_dt, precision=HIGHEST)
    a_cum_t = _dot(a_dt.T, (ri <= ci).astype(F32), precision=HIGHEST)
    a_last = a_cum[L - 1:L, :]
    stack = jnp.concatenate(
        [dt, jnp.exp(a_cum), jnp.exp(a_last - a_cum),
         jnp.broadcast_to(jnp.exp(a_last), (SUBLANES, LANES))], axis=0)
    ex = _dot(stack, expand_ref[...], precision=HIGHEST)
    dt_x, ea_x, dte_x, cd_x = ex[0:L], ex[L:2 * L], ex[2 * L:3 * L], ex[3 * L:3 * L + 1]
    xdt = xs * dt_x

    ys = []
    for g in range(G):
        bm_g = bm[:, g * N:(g + 1) * N]
        cm_g = cm[:, g * N:(g + 1) * N].astype(BF16)
        cb = lax.dot_general(cm_g, bm_g.astype(BF16), (((1,), (1,)), ((), ())),
                             preferred_element_type=F32)
        xdt_g = xdt[:, g * GW:(g + 1) * GW]
        yd = []
        for r in range(R):
            h = g * R + r
            seg = a_cum[:, h:h + 1] - a_cum_t[h:h + 1, :]
            dec = jnp.exp(jnp.where(causal, seg, NEG))
            yd.append(_dot((cb * dec).astype(BF16), xdt_g[:, r * P:(r + 1) * P].astype(BF16)))
        y_diag = jnp.concatenate(yd, axis=1)
        st = state_ref[g]
        y_off = _dot(cm_g, st.astype(BF16)) * ea_x[:, g * GW:(g + 1) * GW]
        new = _dot(bm_g.T.astype(BF16), (xdt_g * dte_x[:, g * GW:(g + 1) * GW]).astype(BF16))
        state_ref[g] = st * cd_x[:, g * GW:(g + 1) * GW] + new
        ys.append(y_diag + y_off + xs[:, g * GW:(g + 1) * GW] * dskip_ref[:, g * GW:(g + 1) * GW])
    y = jnp.concatenate(ys, axis=1)
    o_ref[...] = _rms(y * _silu(z_ref[...]), g_ref[...])


def _ssd(z, xbc, dt, conv_w, conv_b, dt_bias, a_log, d_skip_x, norm_g, bsz, nc):
    t = z.shape[0]
    expand = np.zeros((LANES, D_SSD), np.float32)
    for h in range(SSD_HEADS):
        expand[h, h * SSD_HEAD_DIM:(h + 1) * SSD_HEAD_DIM] = 1.0
    row = lambda b, c: (b * nc + c, 0)
    fixed = lambda b, c: (0, 0)
    return pl.pallas_call(
        _ssd_kernel,
        grid=(bsz, nc),
        in_specs=[pl.BlockSpec((CHUNK, D_SSD), row),
                  pl.BlockSpec((CHUNK, D_CONV), row),
                  pl.BlockSpec((CHUNK, LANES), row),
                  pl.BlockSpec((CONV_WIDTH, D_CONV), fixed),
                  pl.BlockSpec((1, D_CONV), fixed),
                  pl.BlockSpec((1, LANES), fixed),
                  pl.BlockSpec((1, LANES), fixed),
                  pl.BlockSpec((1, D_SSD), fixed),
                  pl.BlockSpec((1, D_SSD), fixed),
                  pl.BlockSpec((LANES, D_SSD), fixed)],
        out_specs=pl.BlockSpec((CHUNK, D_SSD), row),
        out_shape=jax.ShapeDtypeStruct((t, D_SSD), F32),
        scratch_shapes=[pltpu.VMEM((SSD_GROUPS, D_STATE, SSD_HEADS_PER_GROUP * SSD_HEAD_DIM), F32),
                        pltpu.VMEM((CHUNK + SUBLANES, D_CONV), F32)],
        compiler_params=_params(2),
        name="ssd",
    )(z, xbc, dt, conv_w, conv_b, dt_bias, a_log, d_skip_x, norm_g, jnp.asarray(expand))


def _swa_kernel(sink_ref, q_ref, kv_ref, kvp_ref, bias_ref, g_ref, o_ref):
    L, Dh = CHUNK, ATTN_HEAD_DIM
    q = q_ref[...] * (1.0 / math.sqrt(Dh))
    kv = kv_ref[...]
    kvp = kvp_ref[...]
    col = lax.broadcasted_iota(jnp.int32, (L, 2 * L), 1)
    first_col = jnp.where(pl.program_id(1) > 0, 0, L)
    outs = []
    for hk in range(ATTN_KV_HEADS):
        ks = slice(hk * Dh, (hk + 1) * Dh)
        vs = slice(D_KV + hk * Dh, D_KV + (hk + 1) * Dh)
        kc = jnp.concatenate([kvp[:, ks], kv[:, ks]], axis=0).astype(BF16)
        vc = jnp.concatenate([kvp[:, vs], kv[:, vs]], axis=0).astype(BF16)
        for g in range(ATTN_Q_PER_KV):
            h = hk * ATTN_Q_PER_KV + g
            qh = q[:, h * Dh:(h + 1) * Dh].astype(BF16)
            s = lax.dot_general(qh, kc, (((1,), (1,)), ((), ())), preferred_element_type=F32)
            s = jnp.where(col >= first_col, s + bias_ref[h], NEG)
            sink = sink_ref[h]
            m = jnp.maximum(jnp.max(s, axis=-1, keepdims=True), sink)
            p = jnp.exp(s - m)
            denom = jnp.sum(p, axis=-1, keepdims=True) + jnp.exp(sink - m)
            outs.append(_dot(p.astype(BF16), vc) / denom)
    o_ref[...] = _rms(jnp.concatenate(outs, axis=1), g_ref[...])


def _swa(q, kv, sinks, bias, norm_g, bsz, nb):
    t = q.shape[0]
    row = lambda b, n: (b * nb + n, 0)
    prev = lambda b, n: (b * nb + jnp.maximum(n - 1, 0), 0)
    return pl.pallas_call(
        _swa_kernel,
        grid=(bsz, nb),
        in_specs=[pl.BlockSpec(memory_space=pltpu.SMEM),
                  pl.BlockSpec((CHUNK, D_ATTN), row),
                  pl.BlockSpec((CHUNK, 2 * D_KV), row),
                  pl.BlockSpec((CHUNK, 2 * D_KV), prev),
                  pl.BlockSpec((ATTN_Q_HEADS, CHUNK, 2 * CHUNK), lambda b, n: (0, 0, 0)),
                  pl.BlockSpec((1, D_ATTN), lambda b, n: (0, 0))],
        out_specs=pl.BlockSpec((CHUNK, D_ATTN), row),
        out_shape=jax.ShapeDtypeStruct((t, D_ATTN), F32),
        compiler_params=_params(2),
        name="swa",
    )(sinks, q, kv, kv, bias, norm_g)


def _route_kernel(x_ref, ys_ref, ya_ref, wo1_ref, wo2_ref, g_ref, wr_ref, br_ref,
                  h_ref, u_ref, slab_ref, post_ref, tab_ref, run_ref):
    tm = x_ref.shape[0]

    @pl.when(pl.program_id(0) == 0)
    def _():
        run_ref[...] = jnp.zeros(run_ref.shape, F32)

    h = (x_ref[...] + _dot(ys_ref[...].astype(BF16), wo1_ref[...])
         + _dot(ya_ref[...].astype(BF16), wo2_ref[...]))
    h_ref[...] = h
    u = _rms(h, g_ref[...])
    ub = u.astype(BF16)
    u_ref[...] = ub
    E = N_EXPERTS
    logits = lax.dot_general(wr_ref[...], ub, (((1,), (1,)), ((), ())), preferred_element_type=F32)
    logits = logits + jnp.concatenate([br_ref[...]] * (tm // LANES), axis=1)

    eidx = lax.broadcasted_iota(jnp.int32, (E, tm), 0).astype(F32)
    vals, idxs = [], []
    cur = logits
    for _ in range(TOP_K):
        m = jnp.max(cur, axis=0, keepdims=True)
        ix = jnp.min(jnp.where(cur == m, eidx, float(E)), axis=0, keepdims=True)
        vals.append(m)
        idxs.append(ix)
        cur = jnp.where(eidx == ix, NEG, cur)
    es = [jnp.exp(v - vals[0]) for v in vals]
    den = es[0] + es[1] + es[2] + es[3]

    onehot = jnp.zeros((E, tm), F32)
    for ix in idxs:
        onehot = onehot + (eidx == ix).astype(F32)
    ri = lax.broadcasted_iota(jnp.int32, (tm, tm), 0)
    ci = lax.broadcasted_iota(jnp.int32, (tm, tm), 1)
    before = _dot(onehot.astype(BF16), (ri < ci).astype(BF16))
    cnt = jnp.sum(onehot, axis=1, keepdims=True)
    cnt8 = jnp.floor((cnt + (SUBLANES - 1)) * (1.0 / SUBLANES)) * SUBLANES
    el = lax.broadcasted_iota(jnp.int32, (E, E), 0)
    ec = lax.broadcasted_iota(jnp.int32, (E, E), 1)
    seg_start = _dot((ec < el).astype(F32), jnp.broadcast_to(cnt8, (E, LANES)),
                     precision=HIGHEST)[:, 0:1]
    where_to = before + seg_start

    rows = [jnp.sum(jnp.where(eidx == idxs[k], where_to, 0.0), axis=0, keepdims=True)
            for k in range(TOP_K)]
    rows += [es[k] / den for k in range(TOP_K)]
    stack = jnp.concatenate(rows, axis=0)
    post_ref[...] = stack.astype(jnp.int32)
    slab_ref[...] = jnp.concatenate([stack, jnp.zeros((LANES - 2 * TOP_K, tm), F32)], axis=0).T

    lane = lax.broadcasted_iota(jnp.int32, (E, LANES), 1)
    tab = jnp.where(lane == 0, cnt8, jnp.where(lane == 1, run_ref[...], jnp.where(lane == 2, seg_start, 0.0)))
    tab_ref[...] = tab.astype(jnp.int32)
    run_ref[...] = run_ref[...] + cnt8


def _route(x2, y_ssd, y_attn, wo1, wo2, g, wr, br):
    t = x2.shape[0]
    tm = TM_ROUTE
    row = lambda i: (i, 0)
    fixed = lambda i: (0, 0)
    return pl.pallas_call(
        _route_kernel,
        grid=(t // tm,),
        in_specs=[pl.BlockSpec((tm, D_MODEL), row),
                  pl.BlockSpec((tm, D_SSD), row),
                  pl.BlockSpec((tm, D_ATTN), row),
                  pl.BlockSpec((D_SSD, D_MODEL), fixed),
                  pl.BlockSpec((D_ATTN, D_MODEL), fixed),
                  pl.BlockSpec((1, D_MODEL), fixed),
                  pl.BlockSpec((N_EXPERTS, D_MODEL), fixed),
                  pl.BlockSpec((N_EXPERTS, LANES), fixed)],
        out_specs=[pl.BlockSpec((tm, D_MODEL), row),
                   pl.BlockSpec((tm, D_MODEL), row),
                   pl.BlockSpec((tm, LANES), row),
                   pl.BlockSpec((SUBLANES, tm), lambda i: (0, i)),
                   pl.BlockSpec((N_EXPERTS, LANES), row)],
        out_shape=[jax.ShapeDtypeStruct((t, D_MODEL), F32),
                   jax.ShapeDtypeStruct((t, D_MODEL), BF16),
                   jax.ShapeDtypeStruct((t, LANES), F32),
                   jax.ShapeDtypeStruct((SUBLANES, t), jnp.int32),
                   jax.ShapeDtypeStruct((t // tm * N_EXPERTS, LANES), jnp.int32)],
        scratch_shapes=[pltpu.VMEM((N_EXPERTS, LANES), F32)],
        compiler_params=_params(1),
        name="route",
    )(x2, y_ssd, y_attn, wo1, wo2, g, wr, br)


N_CHUNKS = R_SORTED // SUBLANES


def _for_each_chunk(tile, nch_ref, cdst_ref, fn):
    def body(c, carry):
        fn(c * SUBLANES, cdst_ref[tile * N_CHUNKS + c])
        return carry

    lax.fori_loop(0, nch_ref[tile], body, 0)


def _chunk(ref, row):
    return ref.at[pl.ds(pl.multiple_of(row, SUBLANES), SUBLANES), :]


def _dispatch_kernel(nch_ref, cdst_ref, zflag_ref, u_ref, post_ref, xs_ref,
                     buf_ref, zero_ref, sems, zsem):
    tm = u_ref.shape[0]
    i = pl.program_id(0)
    slot = i % 2
    bm = zero_ref.shape[0]

    @pl.when(pl.program_id(0) == 0)
    def _():
        zero_ref[...] = jnp.zeros(zero_ref.shape, F32)

        def fill(b):
            return pltpu.make_async_copy(zero_ref, xs_ref.at[pl.ds(b * bm, bm), :], zsem)

        def zstart(b, carry):
            @pl.when(zflag_ref[b] != 0)
            def _():
                fill(b).start()
            return carry

        def zwait(b, carry):
            @pl.when(zflag_ref[b] != 0)
            def _():
                fill(b).wait()
            return carry

        lax.fori_loop(0, zflag_ref.shape[0], zstart, 0)
        lax.fori_loop(0, zflag_ref.shape[0], zwait, 0)

    pos = post_ref[...]
    j = lax.broadcasted_iota(jnp.int32, (R_SORTED, tm), 0)
    sel = jnp.zeros((R_SORTED, tm), F32)
    for k in range(TOP_K):
        sel = jnp.where(j == pos[k:k + 1, :], 1.0, sel)
    buf_ref[slot] = _dot(sel.astype(BF16), u_ref[...])

    def copies(tile, slot_, action):
        def fn(src_row, dst_row):
            action(pltpu.make_async_copy(_chunk(buf_ref.at[slot_], src_row),
                                         _chunk(xs_ref, dst_row), sems.at[slot_]))
        _for_each_chunk(tile, nch_ref, cdst_ref, fn)

    copies(i, slot, lambda c: c.start())

    @pl.when(i > 0)
    def _():
        copies(i - 1, 1 - slot, lambda c: c.wait())

    @pl.when(i == pl.num_programs(0) - 1)
    def _():
        copies(i, slot, lambda c: c.wait())


def _dispatch(nch, cdst, zero_flag, u, post):
    t = u.shape[0]
    tm = TM_ROUTE
    n_rows = zero_flag.shape[0] * BM_EXPERT
    return pl.pallas_call(
        _dispatch_kernel,
        grid_spec=pltpu.PrefetchScalarGridSpec(
            num_scalar_prefetch=3,
            grid=(t // tm,),
            in_specs=[pl.BlockSpec((tm, D_MODEL), lambda i, *_: (i, 0)),
                      pl.BlockSpec((SUBLANES, tm), lambda i, *_: (0, i))],
            out_specs=pl.BlockSpec(memory_space=pl.ANY),
            scratch_shapes=[pltpu.VMEM((2, R_SORTED, D_MODEL), F32),
                            pltpu.VMEM((BM_EXPERT, D_MODEL), F32),
                            pltpu.SemaphoreType.DMA((2,)),
                            pltpu.SemaphoreType.DMA]),
        out_shape=jax.ShapeDtypeStruct((n_rows, D_MODEL), F32),
        compiler_params=_params(1),
        name="dispatch",
    )(nch, cdst, zero_flag, u, post)


def _expert_kernel(be_ref, nu_ref, first_ref, next_ref, xs_ref, w1_ref, b1_ref, w2_ref, b2_ref,
                   ys_ref, w1f_ref, w2f_ref, w1b_ref, w2b_ref, sems):
    i = pl.program_id(0)

    def fetch(e):
        return (pltpu.make_async_copy(w1_ref.at[e], w1f_ref, sems.at[0]),
                pltpu.make_async_copy(w2_ref.at[e], w2f_ref, sems.at[1]))

    @pl.when(i >= nu_ref[0])
    def _():
        ys_ref[...] = jnp.zeros(ys_ref.shape, F32)

    @pl.when(i < nu_ref[0])
    def _():
        e = be_ref[i]

        @pl.when(i == 0)
        def _():
            for c in fetch(e):
                c.start()

        @pl.when(first_ref[i] != 0)
        def _():
            for c in fetch(e):
                c.wait()
            w1b_ref[...] = w1f_ref[...].astype(BF16)
            w2b_ref[...] = w2f_ref[...].astype(BF16)

            @pl.when(next_ref[i] >= 0)
            def _():
                for c in fetch(next_ref[i]):
                    c.start()

        hid = _dot(xs_ref[...].astype(BF16), w1b_ref[...]) + b1_ref[0]
        x_glu = jnp.minimum(hid[:, :D_EXPERT], SWIGLU_LIMIT)
        x_lin = jnp.clip(hid[:, D_EXPERT:], -SWIGLU_LIMIT, SWIGLU_LIMIT)
        act = x_glu / (1.0 + jnp.exp(-SWIGLU_ALPHA * x_glu)) * (x_lin + 1.0)
        ys_ref[...] = _dot(act.astype(BF16), w2b_ref[...]) + b2_ref[0]


def _experts(blk_expert, n_used, first_flag, next_expert, xs, w1, b1, w2, b2):
    n_rows = xs.shape[0]
    bm = BM_EXPERT
    nblk = n_rows // bm
    last = lambda i, nu: jnp.maximum(jnp.minimum(i, nu[0] - 1), 0)
    row = lambda i, be, nu, *_: (last(i, nu), 0)
    wsel = lambda i, be, nu, *_: (be[last(i, nu)], 0, 0)
    return pl.pallas_call(
        _expert_kernel,
        grid_spec=pltpu.PrefetchScalarGridSpec(
            num_scalar_prefetch=4,
            grid=(nblk,),
            in_specs=[pl.BlockSpec((bm, D_MODEL), row),
                      pl.BlockSpec(memory_space=pl.ANY),
                      pl.BlockSpec((1, 1, 2 * D_EXPERT), wsel),
                      pl.BlockSpec(memory_space=pl.ANY),
                      pl.BlockSpec((1, 1, D_MODEL), wsel)],
            out_specs=pl.BlockSpec((bm, D_MODEL), lambda i, *_: (i, 0)),
            scratch_shapes=[pltpu.VMEM((D_MODEL, 2 * D_EXPERT), F32),
                            pltpu.VMEM((D_EXPERT, D_MODEL), F32),
                            pltpu.VMEM((D_MODEL, 2 * D_EXPERT), BF16),
                            pltpu.VMEM((D_EXPERT, D_MODEL), BF16),
                            pltpu.SemaphoreType.DMA((2,))]),
        out_shape=jax.ShapeDtypeStruct((n_rows, D_MODEL), F32),
        compiler_params=_params(1),
        name="experts",
    )(blk_expert, n_used, first_flag, next_expert, xs, w1, b1, w2, b2)


def _combine_kernel(nch_ref, cdst_ref, ys_ref, h_ref, slab_ref, g_ref, o_ref, buf_ref, sems):
    tm = h_ref.shape[0]
    i = pl.program_id(0)
    slot = i % 2

    def copies(tile, slot_, action):
        def fn(sorted_row, ys_row):
            action(pltpu.make_async_copy(_chunk(ys_ref, ys_row),
                                         _chunk(buf_ref.at[slot_], sorted_row), sems.at[slot_]))
        _for_each_chunk(tile, nch_ref, cdst_ref, fn)

    @pl.when(i == 0)
    def _():
        buf_ref[...] = jnp.zeros(buf_ref.shape, F32)
        copies(0, 0, lambda c: c.start())

    @pl.when(i + 1 < pl.num_programs(0))
    def _():
        copies(i + 1, 1 - slot, lambda c: c.start())

    copies(i, slot, lambda c: c.wait())

    slab = slab_ref[...]
    lane = lax.broadcasted_iota(jnp.int32, (tm, R_SORTED), 1).astype(F32)
    wmat = jnp.zeros((tm, R_SORTED), F32)
    for k in range(TOP_K):
        wmat = jnp.where(lane == slab[:, k:k + 1], slab[:, TOP_K + k:TOP_K + k + 1], wmat)
    moe = _dot(wmat.astype(BF16), buf_ref[slot].astype(BF16))
    o_ref[...] = _rms(h_ref[...] + moe, g_ref[...])


def _combine(nch, cdst, ys, h, slab, g):
    t = h.shape[0]
    tm = TM_ROUTE
    return pl.pallas_call(
        _combine_kernel,
        grid_spec=pltpu.PrefetchScalarGridSpec(
            num_scalar_prefetch=2,
            grid=(t // tm,),
            in_specs=[pl.BlockSpec(memory_space=pl.ANY),
                      pl.BlockSpec((tm, D_MODEL), lambda i, *_: (i, 0)),
                      pl.BlockSpec((tm, LANES), lambda i, *_: (i, 0)),
                      pl.BlockSpec((1, D_MODEL), lambda i, *_: (0, 0))],
            out_specs=pl.BlockSpec((tm, D_MODEL), lambda i, *_: (i, 0)),
            scratch_shapes=[pltpu.VMEM((2, R_SORTED, D_MODEL), F32),
                            pltpu.SemaphoreType.DMA((2,))]),
        out_shape=jax.ShapeDtypeStruct((t, D_MODEL), F32),
        compiler_params=_params(1),
        name="combine",
    )(nch, cdst, ys, h, slab, g)


def _pad_lanes(v, fill=0.0):
    return jnp.pad(v.reshape(1, -1), ((0, 0), (0, LANES - v.shape[-1])), constant_values=fill)


def kernel(x, norm_mix_g, w_in, conv_w, conv_b, dt_bias, a_log, d_skip, ssd_norm_g, attn_sinks,
           rel_bias, attn_norm_g, w_out, norm_ffn_g, w_router, b_router, w1, b1, w2, b2,
           norm_final_g):
    bsz, seq, d = x.shape
    t = bsz * seq
    nc = seq // CHUNK
    depth = w_in.shape[0]
    o2 = D_SSD + D_CONV
    o3 = o2 + SSD_HEADS
    bias = _bias_table(rel_bias)

    assert depth == 1, "the final norm is fused into the combine kernel: single layer only"
    h = x.reshape(t, d)
    for layer in range(depth):
        w = w_in[layer]
        w_packed = jnp.concatenate(
            [w[:, :o2], w[:, o3:], w[:, o2:o3], jnp.zeros((d, LANES - SSD_HEADS), w.dtype)],
            axis=1).astype(BF16)
        z, xbc, q, kv, dt = _inproj(h, norm_mix_g[layer].reshape(1, d), w_packed)
        y_ssd = _ssd(z, xbc, dt, conv_w[layer], conv_b[layer].reshape(1, -1),
                     _pad_lanes(dt_bias[layer]), _pad_lanes(a_log[layer]),
                     jnp.repeat(d_skip[layer], SSD_HEAD_DIM).reshape(1, -1),
                     ssd_norm_g[layer].reshape(1, -1), bsz, nc)
        y_attn = _swa(q, kv, attn_sinks[layer], bias, attn_norm_g[layer].reshape(1, -1), bsz, nc)

        wo = w_out[layer].astype(BF16)
        wr = w_router[layer].T.astype(BF16)
        br = jnp.broadcast_to(b_router[layer][:, None], (N_EXPERTS, LANES))
        h_mid, u, slab, post, tab = _route(
            h, y_ssd, y_attn, wo[:D_SSD], wo[D_SSD:], norm_ffn_g[layer].reshape(1, d), wr, br)

        bm = BM_EXPERT
        ntiles = t // TM_ROUTE
        nblk = (t * TOP_K + ntiles * N_EXPERTS * (SUBLANES - 1)) // bm + N_EXPERTS
        tab = tab.reshape(ntiles, N_EXPERTS, LANES)
        cnt8, off8, segst = tab[:, :, 0], tab[:, :, 1], tab[:, :, 2]
        total8 = off8[-1] + cnt8[-1]
        padded = ((total8 + bm - 1) // bm) * bm
        pend = jnp.cumsum(padded)
        segdst = (pend - padded)[None, :] + off8
        n_used = pend[-1:] // bm
        eids = jnp.arange(N_EXPERTS, dtype=jnp.int32)
        blk = jnp.arange(nblk, dtype=jnp.int32)
        blk_expert = jnp.minimum(
            jnp.sum((blk[:, None] * bm >= pend[None, :]).astype(jnp.int32), axis=1), N_EXPERTS - 1)
        after = jnp.concatenate([blk_expert[1:], blk_expert[-1:]])
        before = jnp.concatenate([blk_expert[:1], blk_expert[:-1]])
        zero_flag = ((blk >= n_used[0] - 1) | (after != blk_expert)).astype(jnp.int32)
        first_flag = ((blk == 0) | (before != blk_expert)).astype(jnp.int32)
        cand = jnp.where((eids[None, :] > eids[:, None]) & (padded[None, :] > 0), eids[None, :], N_EXPERTS)
        next_nonempty = jnp.min(cand, axis=1)
        next_nonempty = jnp.where(next_nonempty == N_EXPERTS, -1, next_nonempty)
        next_expert = jnp.sum(jnp.where(blk_expert[:, None] == eids[None, :], next_nonempty[None, :], 0), axis=1)
        chunk_row = jnp.arange(N_CHUNKS, dtype=jnp.int32) * SUBLANES
        seg_of_chunk = jnp.sum((chunk_row[None, :, None] >= (segst + cnt8)[:, None, :]).astype(jnp.int32), axis=2)
        shift = jnp.sum(jnp.where(seg_of_chunk[:, :, None] == eids[None, None, :],
                                  (segdst - segst)[:, None, :], 0), axis=2)
        cdst = (chunk_row[None, :] + shift).reshape(-1)
        nch = jnp.sum(cnt8, axis=1) // SUBLANES

        xs = _dispatch(nch, cdst, zero_flag, u, post)
        ys = _experts(blk_expert, n_used, first_flag, next_expert, xs, w1[layer],
                      b1[layer].reshape(N_EXPERTS, 1, -1), w2[layer], b2[layer].reshape(N_EXPERTS, 1, -1))
        h = _combine(nch, cdst, ys, h_mid, slab, norm_final_g.reshape(1, d))
    return h.reshape(bsz, seq, d)
```

```python
import functools
import math

import numpy as np
import jax
import jax.numpy as jnp
from jax import lax
from jax.experimental import pallas as pl
from jax.experimental.pallas import tpu as pltpu

F32 = jnp.float32
BF16 = jnp.bfloat16
HIGHEST = lax.Precision.HIGHEST

D_MODEL = 1024
SSD_HEADS = 8
SSD_HEAD_DIM = 64
D_SSD = SSD_HEADS * SSD_HEAD_DIM
SSD_GROUPS = 2
SSD_HEADS_PER_GROUP = SSD_HEADS // SSD_GROUPS
D_STATE = 128
CONV_WIDTH = 4
CHUNK = 128
D_CONV = D_SSD + 2 * SSD_GROUPS * D_STATE
ATTN_Q_HEADS = 8
ATTN_KV_HEADS = 2
ATTN_Q_PER_KV = ATTN_Q_HEADS // ATTN_KV_HEADS
ATTN_HEAD_DIM = 64
D_ATTN = ATTN_Q_HEADS * ATTN_HEAD_DIM
D_KV = ATTN_KV_HEADS * ATTN_HEAD_DIM
WINDOW = 128
REL_BUCKETS = 32
REL_MAX_DIST = 128
N_EXPERTS = 32
TOP_K = 4
D_EXPERT = D_MODEL
SWIGLU_LIMIT = 7.0
SWIGLU_ALPHA = 1.702
RMS_EPS = 1e-5

LANES = 128
SUBLANES = 8
NEG = -1e30
VMEM_LIMIT = 56 * 1024 * 1024

TM_PROJ = 512
TM_ROUTE = 512
BM_EXPERT = 256
SEQS_PER_STEP = 2
R_SORTED = TM_ROUTE * TOP_K + N_EXPERTS * SUBLANES

D_PROJ_PACKED = D_SSD + D_CONV + D_ATTN + 2 * D_KV + LANES


def _params(n_axes):
    return pltpu.CompilerParams(dimension_semantics=("arbitrary",) * n_axes,
                                vmem_limit_bytes=VMEM_LIMIT)


def _rms(x, g):
    return x * lax.rsqrt(jnp.mean(x * x, axis=-1, keepdims=True) + RMS_EPS) * g


def _silu(x):
    return x / (1.0 + jnp.exp(-x))


def _dot(a, b, **kw):
    return jnp.dot(a, b, preferred_element_type=F32, **kw)


def _t5_bucket_table():
    dist = CHUNK + np.arange(CHUNK)[:, None] - np.arange(2 * CHUNK)[None, :]
    in_window = (dist >= 0) & (dist < WINDOW)
    d = np.clip(dist, 0, REL_MAX_DIST)
    max_exact = REL_BUCKETS // 2
    large = max_exact + (np.log(np.maximum(d, 1).astype(np.float32) / max_exact)
                         / math.log(REL_MAX_DIST / max_exact)
                         * (REL_BUCKETS - max_exact)).astype(np.int32)
    large = np.minimum(large, REL_BUCKETS - 1)
    bucket = np.where(d < max_exact, d, large)
    return np.where(in_window, bucket, -1).astype(np.int32)


def _bias_kernel(rb_ref, bucket_ref, o_ref):
    bucket = bucket_ref[...]
    for h in range(ATTN_Q_HEADS):
        acc = jnp.full(bucket.shape, NEG, F32)
        for b in range(REL_BUCKETS):
            acc = jnp.where(bucket == b, rb_ref[b, h], acc)
        o_ref[h] = acc


def _bias_table(rel_bias):
    bucket = jnp.asarray(_t5_bucket_table())
    return pl.pallas_call(
        _bias_kernel,
        out_shape=jax.ShapeDtypeStruct((ATTN_Q_HEADS, CHUNK, 2 * CHUNK), F32),
        in_specs=[pl.BlockSpec(memory_space=pltpu.SMEM),
                  pl.BlockSpec(memory_space=pltpu.VMEM)],
        out_specs=pl.BlockSpec(memory_space=pltpu.VMEM),
        name="bias_table",
    )(rel_bias, bucket)


def _inproj_kernel(x_ref, g_ref, w_ref, z_ref, xbc_ref, q_ref, kv_ref, dt_ref):
    u = _rms(x_ref[...], g_ref[...]).astype(BF16)
    proj = _dot(u, w_ref[...])
    o = 0
    for ref in (z_ref, xbc_ref, q_ref, kv_ref, dt_ref):
        w = ref.shape[1]
        ref[...] = proj[:, o:o + w]
        o += w


def _inproj(x2, g, w_packed):
    t = x2.shape[0]
    widths = (D_SSD, D_CONV, D_ATTN, 2 * D_KV, LANES)
    return pl.pallas_call(
        _inproj_kernel,
        grid=(t // TM_PROJ,),
        in_specs=[pl.BlockSpec((TM_PROJ, D_MODEL), lambda i: (i, 0)),
                  pl.BlockSpec((1, D_MODEL), lambda i: (0, 0)),
                  pl.BlockSpec((D_MODEL, D_PROJ_PACKED), lambda i: (0, 0))],
        out_specs=[pl.BlockSpec((TM_PROJ, w), lambda i: (i, 0)) for w in widths],
        out_shape=[jax.ShapeDtypeStruct((t, w), F32) for w in widths],
        compiler_params=_params(1),
        name="inproj",
    )(x2, g, w_packed)


def _ssd_kernel(z_ref, xbc_ref, dt_ref, cw_ref, cb_ref, dtb_ref, alog_ref, dskip_ref, g_ref,
                expand_ref, o_ref, state_ref, xpad_ref):
    @pl.when(pl.program_id(1) == 0)
    def _():
        state_ref[...] = jnp.zeros(state_ref.shape, F32)
        xpad_ref[:, 0:SUBLANES, :] = jnp.zeros((xpad_ref.shape[0], SUBLANES, D_CONV), F32)

    for s in range(z_ref.shape[0]):
        _ssd_chunk(z_ref.at[s], xbc_ref.at[s], dt_ref.at[s], cw_ref, cb_ref, dtb_ref, alog_ref,
                   dskip_ref, g_ref, expand_ref, o_ref.at[s], state_ref.at[s], xpad_ref.at[s])


def _ssd_chunk(z_ref, xbc_ref, dt_ref, cw_ref, cb_ref, dtb_ref, alog_ref, dskip_ref, g_ref,
               expand_ref, o_ref, state_ref, xpad_ref):
    L = CHUNK
    G, R, P, N = SSD_GROUPS, SSD_HEADS_PER_GROUP, SSD_HEAD_DIM, D_STATE
    GW = R * P

    xpad_ref[SUBLANES:SUBLANES + L, :] = xbc_ref[...]
    acc = jnp.broadcast_to(cb_ref[...], (L, D_CONV))
    for k in range(CONV_WIDTH):
        acc = acc + cw_ref[k:k + 1, :] * xpad_ref[pl.ds(SUBLANES - (CONV_WIDTH - 1) + k, L), :]
    xpad_ref[0:SUBLANES, :] = xpad_ref[L:L + SUBLANES, :]
    xbc = _silu(acc)
    xs = xbc[:, :D_SSD]
    bm = xbc[:, D_SSD:D_SSD + G * N]
    cm = xbc[:, D_SSD + G * N:]

    dtr = dt_ref[...] + dtb_ref[...]
    dt = jnp.maximum(dtr, 0.0) + jnp.log(1.0 + jnp.exp(-jnp.abs(dtr)))
    a_dt = dt * (-jnp.exp(alog_ref[...]))
    ri = lax.broadcasted_iota(jnp.int32, (L, L), 0)
    ci = lax.broadcasted_iota(jnp.int32, (L, L), 1)
    causal = ci <= ri
    a_cum = _dot(causal.astype(F32), a_dt, precision=HIGHEST)
    a_cum_t = _dot(a_dt.T, (ri <= ci).astype(F32), precision=HIGHEST)
    a_last = a_cum[L - 1:L, :]
    stack = jnp.concatenate(
        [dt, jnp.exp(a_cum), jnp.exp(a_last - a_cum),
         jnp.broadcast_to(jnp.exp(a_last), (SUBLANES, LANES))], axis=0)
    ex = _dot(stack, expand_ref[...], precision=HIGHEST)
    dt_x, ea_x, dte_x, cd_x = ex[0:L], ex[L:2 * L], ex[2 * L:3 * L], ex[3 * L:3 * L + 1]
    xdt = xs * dt_x

    ys = []
    for g in range(G):
        bm_g = bm[:, g * N:(g + 1) * N]
        cm_g = cm[:, g * N:(g + 1) * N].astype(BF16)
        cb = lax.dot_general(cm_g, bm_g.astype(BF16), (((1,), (1,)), ((), ())),
                             preferred_element_type=F32)
        xdt_g = xdt[:, g * GW:(g + 1) * GW]
        yd = []
        for r in range(R):
            h = g * R + r
            seg = a_cum[:, h:h + 1] - a_cum_t[h:h + 1, :]
            dec = jnp.exp(jnp.where(causal, seg, NEG))
            yd.append(_dot((cb * dec).astype(BF16), xdt_g[:, r * P:(r + 1) * P].astype(BF16)))
        y_diag = jnp.concatenate(yd, axis=1)
        st = state_ref[g]
        y_off = _dot(cm_g, st.astype(BF16)) * ea_x[:, g * GW:(g + 1) * GW]
        new = _dot(bm_g.T.astype(BF16), (xdt_g * dte_x[:, g * GW:(g + 1) * GW]).astype(BF16))
        state_ref[g] = st * cd_x[:, g * GW:(g + 1) * GW] + new
        ys.append(y_diag + y_off + xs[:, g * GW:(g + 1) * GW] * dskip_ref[:, g * GW:(g + 1) * GW])
    y = jnp.concatenate(ys, axis=1)
    o_ref[...] = _rms(y * _silu(z_ref[...]), g_ref[...])


def _ssd(z, xbc, dt, conv_w, conv_b, dt_bias, a_log, d_skip_x, norm_g, bsz, nc):
    expand = np.zeros((LANES, D_SSD), np.float32)
    for h in range(SSD_HEADS):
        expand[h, h * SSD_HEAD_DIM:(h + 1) * SSD_HEAD_DIM] = 1.0
    ns = SEQS_PER_STEP
    row = lambda b, c: (b, c, 0)
    fixed = lambda b, c: (0, 0)
    return pl.pallas_call(
        _ssd_kernel,
        grid=(bsz // ns, nc),
        in_specs=[pl.BlockSpec((ns, CHUNK, D_SSD), row),
                  pl.BlockSpec((ns, CHUNK, D_CONV), row),
                  pl.BlockSpec((ns, CHUNK, LANES), row),
                  pl.BlockSpec((CONV_WIDTH, D_CONV), fixed),
                  pl.BlockSpec((1, D_CONV), fixed),
                  pl.BlockSpec((1, LANES), fixed),
                  pl.BlockSpec((1, LANES), fixed),
                  pl.BlockSpec((1, D_SSD), fixed),
                  pl.BlockSpec((1, D_SSD), fixed),
                  pl.BlockSpec((LANES, D_SSD), fixed)],
        out_specs=pl.BlockSpec((ns, CHUNK, D_SSD), row),
        out_shape=jax.ShapeDtypeStruct((bsz, nc * CHUNK, D_SSD), F32),
        scratch_shapes=[pltpu.VMEM((ns, SSD_GROUPS, D_STATE, SSD_HEADS_PER_GROUP * SSD_HEAD_DIM), F32),
                        pltpu.VMEM((ns, CHUNK + SUBLANES, D_CONV), F32)],
        compiler_params=_params(2),
        name="ssd",
    )(z, xbc, dt, conv_w, conv_b, dt_bias, a_log, d_skip_x, norm_g, jnp.asarray(expand))


def _swa_kernel(sink_ref, q_ref, kv_ref, kvp_ref, bias_ref, g_ref, o_ref):
    for s in range(q_ref.shape[0]):
        _swa_block(sink_ref, q_ref.at[s], kv_ref.at[s], kvp_ref.at[s], bias_ref, g_ref, o_ref.at[s])


def _swa_block(sink_ref, q_ref, kv_ref, kvp_ref, bias_ref, g_ref, o_ref):
    L, Dh = CHUNK, ATTN_HEAD_DIM
    q = q_ref[...] * (1.0 / math.sqrt(Dh))
    kv = kv_ref[...]
    kvp = kvp_ref[...]
    col = lax.broadcasted_iota(jnp.int32, (L, 2 * L), 1)
    first_col = jnp.where(pl.program_id(1) > 0, 0, L)
    outs = []
    for hk in range(ATTN_KV_HEADS):
        ks = slice(hk * Dh, (hk + 1) * Dh)
        vs = slice(D_KV + hk * Dh, D_KV + (hk + 1) * Dh)
        kc = jnp.concatenate([kvp[:, ks], kv[:, ks]], axis=0).astype(BF16)
        vc = jnp.concatenate([kvp[:, vs], kv[:, vs]], axis=0).astype(BF16)
        for g in range(ATTN_Q_PER_KV):
            h = hk * ATTN_Q_PER_KV + g
            qh = q[:, h * Dh:(h + 1) * Dh].astype(BF16)
            s = lax.dot_general(qh, kc, (((1,), (1,)), ((), ())), preferred_element_type=F32)
            s = jnp.where(col >= first_col, s + bias_ref[h], NEG)
            sink = sink_ref[h]
            m = jnp.maximum(jnp.max(s, axis=-1, keepdims=True), sink)
            p = jnp.exp(s - m)
            denom = jnp.sum(p, axis=-1, keepdims=True) + jnp.exp(sink - m)
            outs.append(_dot(p.astype(BF16), vc) / denom)
    o_ref[...] = _rms(jnp.concatenate(outs, axis=1), g_ref[...])


def _swa(q, kv, sinks, bias, norm_g, bsz, nb):
    ns = SEQS_PER_STEP
    row = lambda b, n: (b, n, 0)
    prev = lambda b, n: (b, jnp.maximum(n - 1, 0), 0)
    return pl.pallas_call(
        _swa_kernel,
        grid=(bsz // ns, nb),
        in_specs=[pl.BlockSpec(memory_space=pltpu.SMEM),
                  pl.BlockSpec((ns, CHUNK, D_ATTN), row),
                  pl.BlockSpec((ns, CHUNK, 2 * D_KV), row),
                  pl.BlockSpec((ns, CHUNK, 2 * D_KV), prev),
                  pl.BlockSpec((ATTN_Q_HEADS, CHUNK, 2 * CHUNK), lambda b, n: (0, 0, 0)),
                  pl.BlockSpec((1, D_ATTN), lambda b, n: (0, 0))],
        out_specs=pl.BlockSpec((ns, CHUNK, D_ATTN), row),
        out_shape=jax.ShapeDtypeStruct((bsz, nb * CHUNK, D_ATTN), F32),
        compiler_params=_params(2),
        name="swa",
    )(sinks, q, kv, kv, bias, norm_g)


def _route_kernel(x_ref, ys_ref, ya_ref, wo1_ref, wo2_ref, g_ref, wr_ref, br_ref,
                  h_ref, u_ref, slab_ref, post_ref, tab_ref, run_ref):
    tm = x_ref.shape[0]

    @pl.when(pl.program_id(0) == 0)
    def _():
        run_ref[...] = jnp.zeros(run_ref.shape, F32)

    h = (x_ref[...] + _dot(ys_ref[...].astype(BF16), wo1_ref[...])
         + _dot(ya_ref[...].astype(BF16), wo2_ref[...]))
    h_ref[...] = h
    u = _rms(h, g_ref[...])
    ub = u.astype(BF16)
    u_ref[...] = ub
    E = N_EXPERTS
    logits = lax.dot_general(wr_ref[...], ub, (((1,), (1,)), ((), ())), preferred_element_type=F32)
    logits = logits + jnp.concatenate([br_ref[...]] * (tm // LANES), axis=1)

    eidx = lax.broadcasted_iota(jnp.int32, (E, tm), 0).astype(F32)
    vals, idxs = [], []
    cur = logits
    for _ in range(TOP_K):
        m = jnp.max(cur, axis=0, keepdims=True)
        ix = jnp.min(jnp.where(cur == m, eidx, float(E)), axis=0, keepdims=True)
        vals.append(m)
        idxs.append(ix)
        cur = jnp.where(eidx == ix, NEG, cur)
    es = [jnp.exp(v - vals[0]) for v in vals]
    den = es[0] + es[1] + es[2] + es[3]

    onehot = jnp.zeros((E, tm), F32)
    for ix in idxs:
        onehot = onehot + (eidx == ix).astype(F32)
    ri = lax.broadcasted_iota(jnp.int32, (tm, tm), 0)
    ci = lax.broadcasted_iota(jnp.int32, (tm, tm), 1)
    before = _dot(onehot.astype(BF16), (ri < ci).astype(BF16))
    cnt = jnp.sum(onehot, axis=1, keepdims=True)
    cnt8 = jnp.floor((cnt + (SUBLANES - 1)) * (1.0 / SUBLANES)) * SUBLANES
    el = lax.broadcasted_iota(jnp.int32, (E, E), 0)
    ec = lax.broadcasted_iota(jnp.int32, (E, E), 1)
    seg_start = _dot((ec < el).astype(F32), jnp.broadcast_to(cnt8, (E, LANES)),
                     precision=HIGHEST)[:, 0:1]
    where_to = before + seg_start

    rows = [jnp.sum(jnp.where(eidx == idxs[k], where_to, 0.0), axis=0, keepdims=True)
            for k in range(TOP_K)]
    rows += [es[k] / den for k in range(TOP_K)]
    stack = jnp.concatenate(rows, axis=0)
    post_ref[...] = stack.astype(jnp.int32)
    slab_ref[...] = jnp.concatenate([stack, jnp.zeros((LANES - 2 * TOP_K, tm), F32)], axis=0).T

    lane = lax.broadcasted_iota(jnp.int32, (E, LANES), 1)
    tab = jnp.where(lane == 0, cnt8, jnp.where(lane == 1, run_ref[...], jnp.where(lane == 2, seg_start, 0.0)))
    tab_ref[...] = tab.astype(jnp.int32)
    run_ref[...] = run_ref[...] + cnt8


def _route(x2, y_ssd, y_attn, wo1, wo2, g, wr, br):
    t = x2.shape[0]
    tm = TM_ROUTE
    row = lambda i: (i, 0)
    fixed = lambda i: (0, 0)
    return pl.pallas_call(
        _route_kernel,
        grid=(t // tm,),
        in_specs=[pl.BlockSpec((tm, D_MODEL), row),
                  pl.BlockSpec((tm, D_SSD), row),
                  pl.BlockSpec((tm, D_ATTN), row),
                  pl.BlockSpec((D_SSD, D_MODEL), fixed),
                  pl.BlockSpec((D_ATTN, D_MODEL), fixed),
                  pl.BlockSpec((1, D_MODEL), fixed),
                  pl.BlockSpec((N_EXPERTS, D_MODEL), fixed),
                  pl.BlockSpec((N_EXPERTS, LANES), fixed)],
        out_specs=[pl.BlockSpec((tm, D_MODEL), row),
                   pl.BlockSpec((tm, D_MODEL), row),
                   pl.BlockSpec((tm, LANES), row),
                   pl.BlockSpec((SUBLANES, tm), lambda i: (0, i)),
                   pl.BlockSpec((N_EXPERTS, LANES), row)],
        out_shape=[jax.ShapeDtypeStruct((t, D_MODEL), F32),
                   jax.ShapeDtypeStruct((t, D_MODEL), BF16),
                   jax.ShapeDtypeStruct((t, LANES), F32),
                   jax.ShapeDtypeStruct((SUBLANES, t), jnp.int32),
                   jax.ShapeDtypeStruct((t // tm * N_EXPERTS, LANES), jnp.int32)],
        scratch_shapes=[pltpu.VMEM((N_EXPERTS, LANES), F32)],
        compiler_params=_params(1),
        name="route",
    )(x2, y_ssd, y_attn, wo1, wo2, g, wr, br)


SEG_SIZE_BITS = (TM_ROUTE // SUBLANES).bit_length()


def _for_each_piece(tile, cnt_ref, segst_ref, segdst_ref, fn):
    def per_expert(e, carry):
        seg = tile * N_EXPERTS + e
        n = cnt_ref[seg] // SUBLANES
        src0 = segst_ref[seg]
        dst0 = segdst_ref[seg]
        for b in reversed(range(SEG_SIZE_BITS)):
            off = ((n >> (b + 1)) << (b + 1)) * SUBLANES

            @pl.when(((n >> b) & 1) == 1)
            def _():
                fn(SUBLANES << b, src0 + off, dst0 + off)
        return carry

    lax.fori_loop(0, N_EXPERTS, per_expert, 0)


def _rows(ref, row, n):
    return ref.at[pl.ds(pl.multiple_of(row, SUBLANES), n), :]


def _dispatch_kernel(cnt_ref, segst_ref, segdst_ref, zflag_ref, u_ref, post_ref, xs_ref,
                     buf_ref, zero_ref, sems, zsem):
    tm = u_ref.shape[0]
    i = pl.program_id(0)
    slot = i % 2
    bm = zero_ref.shape[0]

    @pl.when(pl.program_id(0) == 0)
    def _():
        zero_ref[...] = jnp.zeros(zero_ref.shape, F32)

        def fill(b):
            return pltpu.make_async_copy(zero_ref, xs_ref.at[pl.ds(b * bm, bm), :], zsem)

        def zstart(b, carry):
            @pl.when(zflag_ref[b] != 0)
            def _():
                fill(b).start()
            return carry

        def zwait(b, carry):
            @pl.when(zflag_ref[b] != 0)
            def _():
                fill(b).wait()
            return carry

        lax.fori_loop(0, zflag_ref.shape[0], zstart, 0)
        lax.fori_loop(0, zflag_ref.shape[0], zwait, 0)

    pos = post_ref[...]
    j = lax.broadcasted_iota(jnp.int32, (R_SORTED, tm), 0)
    sel = jnp.zeros((R_SORTED, tm), F32)
    for k in range(TOP_K):
        sel = jnp.where(j == pos[k:k + 1, :], 1.0, sel)
    buf_ref[slot] = _dot(sel.astype(BF16), u_ref[...])

    def copies(tile, slot_, action):
        def fn(n, src_row, dst_row):
            action(pltpu.make_async_copy(_rows(buf_ref.at[slot_], src_row, n),
                                         _rows(xs_ref, dst_row, n), sems.at[slot_]))
        _for_each_piece(tile, cnt_ref, segst_ref, segdst_ref, fn)

    copies(i, slot, lambda c: c.start())

    @pl.when(i > 0)
    def _():
        copies(i - 1, 1 - slot, lambda c: c.wait())

    @pl.when(i == pl.num_programs(0) - 1)
    def _():
        copies(i, slot, lambda c: c.wait())


def _dispatch(cnt8, segst, segdst, zero_flag, u, post):
    t = u.shape[0]
    tm = TM_ROUTE
    n_rows = zero_flag.shape[0] * BM_EXPERT
    return pl.pallas_call(
        _dispatch_kernel,
        grid_spec=pltpu.PrefetchScalarGridSpec(
            num_scalar_prefetch=4,
            grid=(t // tm,),
            in_specs=[pl.BlockSpec((tm, D_MODEL), lambda i, *_: (i, 0)),
                      pl.BlockSpec((SUBLANES, tm), lambda i, *_: (0, i))],
            out_specs=pl.BlockSpec(memory_space=pl.ANY),
            scratch_shapes=[pltpu.VMEM((2, R_SORTED, D_MODEL), F32),
                            pltpu.VMEM((BM_EXPERT, D_MODEL), F32),
                            pltpu.SemaphoreType.DMA((2,)),
                            pltpu.SemaphoreType.DMA]),
        out_shape=jax.ShapeDtypeStruct((n_rows, D_MODEL), F32),
        compiler_params=_params(1),
        name="dispatch",
    )(cnt8, segst, segdst, zero_flag, u, post)


def _expert_kernel(be_ref, nu_ref, first_ref, next_ref, xs_ref, w1_ref, b1_ref, w2_ref, b2_ref,
                   ys_ref, w1f_ref, w2f_ref, w1b_ref, w2b_ref, sems):
    i = pl.program_id(0)

    def fetch(e):
        return (pltpu.make_async_copy(w1_ref.at[e], w1f_ref, sems.at[0]),
                pltpu.make_async_copy(w2_ref.at[e], w2f_ref, sems.at[1]))

    @pl.when(i >= nu_ref[0])
    def _():
        ys_ref[...] = jnp.zeros(ys_ref.shape, F32)

    @pl.when(i < nu_ref[0])
    def _():
        e = be_ref[i]

        @pl.when(i == 0)
        def _():
            for c in fetch(e):
                c.start()

        @pl.when(first_ref[i] != 0)
        def _():
            for c in fetch(e):
                c.wait()
            w1b_ref[...] = w1f_ref[...].astype(BF16)
            w2b_ref[...] = w2f_ref[...].astype(BF16)

            @pl.when(next_ref[i] >= 0)
            def _():
                for c in fetch(next_ref[i]):
                    c.start()

        hid = _dot(xs_ref[...].astype(BF16), w1b_ref[...]) + b1_ref[0]
        x_glu = jnp.minimum(hid[:, :D_EXPERT], SWIGLU_LIMIT)
        x_lin = jnp.clip(hid[:, D_EXPERT:], -SWIGLU_LIMIT, SWIGLU_LIMIT)
        act = x_glu / (1.0 + jnp.exp(-SWIGLU_ALPHA * x_glu)) * (x_lin + 1.0)
        ys_ref[...] = _dot(act.astype(BF16), w2b_ref[...]) + b2_ref[0]


def _experts(blk_expert, n_used, first_flag, next_expert, xs, w1, b1, w2, b2):
    n_rows = xs.shape[0]
    bm = BM_EXPERT
    nblk = n_rows // bm
    last = lambda i, nu: jnp.maximum(jnp.minimum(i, nu[0] - 1), 0)
    row = lambda i, be, nu, *_: (last(i, nu), 0)
    wsel = lambda i, be, nu, *_: (be[last(i, nu)], 0, 0)
    return pl.pallas_call(
        _expert_kernel,
        grid_spec=pltpu.PrefetchScalarGridSpec(
            num_scalar_prefetch=4,
            grid=(nblk,),
            in_specs=[pl.BlockSpec((bm, D_MODEL), row),
                      pl.BlockSpec(memory_space=pl.ANY),
                      pl.BlockSpec((1, 1, 2 * D_EXPERT), wsel),
                      pl.BlockSpec(memory_space=pl.ANY),
                      pl.BlockSpec((1, 1, D_MODEL), wsel)],
            out_specs=pl.BlockSpec((bm, D_MODEL), lambda i, *_: (i, 0)),
            scratch_shapes=[pltpu.VMEM((D_MODEL, 2 * D_EXPERT), F32),
                            pltpu.VMEM((D_EXPERT, D_MODEL), F32),
                            pltpu.VMEM((D_MODEL, 2 * D_EXPERT), BF16),
                            pltpu.VMEM((D_EXPERT, D_MODEL), BF16),
                            pltpu.SemaphoreType.DMA((2,))]),
        out_shape=jax.ShapeDtypeStruct((n_rows, D_MODEL), F32),
        compiler_params=_params(1),
        name="experts",
    )(blk_expert, n_used, first_flag, next_expert, xs, w1, b1, w2, b2)


def _combine_kernel(cnt_ref, segst_ref, segdst_ref, ys_ref, h_ref, slab_ref, g_ref, o_ref,
                    buf_ref, sems):
    tm = h_ref.shape[0]
    i = pl.program_id(0)
    slot = i % 2

    def copies(tile, slot_, action):
        def fn(n, sorted_row, ys_row):
            action(pltpu.make_async_copy(_rows(ys_ref, ys_row, n),
                                         _rows(buf_ref.at[slot_], sorted_row, n), sems.at[slot_]))
        _for_each_piece(tile, cnt_ref, segst_ref, segdst_ref, fn)

    @pl.when(i == 0)
    def _():
        buf_ref[...] = jnp.zeros(buf_ref.shape, F32)
        copies(0, 0, lambda c: c.start())

    @pl.when(i + 1 < pl.num_programs(0))
    def _():
        copies(i + 1, 1 - slot, lambda c: c.start())

    copies(i, slot, lambda c: c.wait())

    slab = slab_ref[...]
    lane = lax.broadcasted_iota(jnp.int32, (tm, R_SORTED), 1).astype(F32)
    wmat = jnp.zeros((tm, R_SORTED), F32)
    for k in range(TOP_K):
        wmat = jnp.where(lane == slab[:, k:k + 1], slab[:, TOP_K + k:TOP_K + k + 1], wmat)
    moe = _dot(wmat.astype(BF16), buf_ref[slot].astype(BF16))
    o_ref[...] = _rms(h_ref[...] + moe, g_ref[...])


def _combine(cnt8, segst, segdst, ys, h, slab, g):
    t = h.shape[0]
    tm = TM_ROUTE
    return pl.pallas_call(
        _combine_kernel,
        grid_spec=pltpu.PrefetchScalarGridSpec(
            num_scalar_prefetch=3,
            grid=(t // tm,),
            in_specs=[pl.BlockSpec(memory_space=pl.ANY),
                      pl.BlockSpec((tm, D_MODEL), lambda i, *_: (i, 0)),
                      pl.BlockSpec((tm, LANES), lambda i, *_: (i, 0)),
                      pl.BlockSpec((1, D_MODEL), lambda i, *_: (0, 0))],
            out_specs=pl.BlockSpec((tm, D_MODEL), lambda i, *_: (i, 0)),
            scratch_shapes=[pltpu.VMEM((2, R_SORTED, D_MODEL), F32),
                            pltpu.SemaphoreType.DMA((2,))]),
        out_shape=jax.ShapeDtypeStruct((t, D_MODEL), F32),
        compiler_params=_params(1),
        name="combine",
    )(cnt8, segst, segdst, ys, h, slab, g)


def _pad_lanes(v, fill=0.0):
    return jnp.pad(v.reshape(1, -1), ((0, 0), (0, LANES - v.shape[-1])), constant_values=fill)


def kernel(x, norm_mix_g, w_in, conv_w, conv_b, dt_bias, a_log, d_skip, ssd_norm_g, attn_sinks,
           rel_bias, attn_norm_g, w_out, norm_ffn_g, w_router, b_router, w1, b1, w2, b2,
           norm_final_g):
    bsz, seq, d = x.shape
    t = bsz * seq
    nc = seq // CHUNK
    depth = w_in.shape[0]
    o2 = D_SSD + D_CONV
    o3 = o2 + SSD_HEADS
    bias = _bias_table(rel_bias)

    assert depth == 1, "the final norm is fused into the combine kernel: single layer only"
    h = x.reshape(t, d)
    for layer in range(depth):
        w = w_in[layer]
        w_packed = jnp.concatenate(
            [w[:, :o2], w[:, o3:], w[:, o2:o3], jnp.zeros((d, LANES - SSD_HEADS), w.dtype)],
            axis=1).astype(BF16)
        z, xbc, q, kv, dt = [a.reshape(bsz, seq, -1)
                             for a in _inproj(h, norm_mix_g[layer].reshape(1, d), w_packed)]
        y_ssd = _ssd(z, xbc, dt, conv_w[layer], conv_b[layer].reshape(1, -1),
                     _pad_lanes(dt_bias[layer]), _pad_lanes(a_log[layer]),
                     jnp.repeat(d_skip[layer], SSD_HEAD_DIM).reshape(1, -1),
                     ssd_norm_g[layer].reshape(1, -1), bsz, nc).reshape(t, -1)
        y_attn = _swa(q, kv, attn_sinks[layer], bias, attn_norm_g[layer].reshape(1, -1),
                      bsz, nc).reshape(t, -1)

        wo = w_out[layer].astype(BF16)
        wr = w_router[layer].T.astype(BF16)
        br = jnp.broadcast_to(b_router[layer][:, None], (N_EXPERTS, LANES))
        h_mid, u, slab, post, tab = _route(
            h, y_ssd, y_attn, wo[:D_SSD], wo[D_SSD:], norm_ffn_g[layer].reshape(1, d), wr, br)

        bm = BM_EXPERT
        ntiles = t // TM_ROUTE
        nblk = (t * TOP_K + ntiles * N_EXPERTS * (SUBLANES - 1)) // bm + N_EXPERTS
        tab = tab.reshape(ntiles, N_EXPERTS, LANES)
        cnt8, off8, segst = tab[:, :, 0], tab[:, :, 1], tab[:, :, 2]
        total8 = off8[-1] + cnt8[-1]
        padded = ((total8 + bm - 1) // bm) * bm
        pend = jnp.cumsum(padded)
        segdst = (pend - padded)[None, :] + off8
        n_used = pend[-1:] // bm
        eids = jnp.arange(N_EXPERTS, dtype=jnp.int32)
        blk = jnp.arange(nblk, dtype=jnp.int32)
        blk_expert = jnp.minimum(
            jnp.sum((blk[:, None] * bm >= pend[None, :]).astype(jnp.int32), axis=1), N_EXPERTS - 1)
        after = jnp.concatenate([blk_expert[1:], blk_expert[-1:]])
        before = jnp.concatenate([blk_expert[:1], blk_expert[:-1]])
        zero_flag = ((blk >= n_used[0] - 1) | (after != blk_expert)).astype(jnp.int32)
        first_flag = ((blk == 0) | (before != blk_expert)).astype(jnp.int32)
        cand = jnp.where((eids[None, :] > eids[:, None]) & (padded[None, :] > 0), eids[None, :], N_EXPERTS)
        next_nonempty = jnp.min(cand, axis=1)
        next_nonempty = jnp.where(next_nonempty == N_EXPERTS, -1, next_nonempty)
        next_expert = jnp.sum(jnp.where(blk_expert[:, None] == eids[None, :], next_nonempty[None, :], 0), axis=1)
        cnt8, segst, segdst = cnt8.reshape(-1), segst.reshape(-1), segdst.reshape(-1)

        xs = _dispatch(cnt8, segst, segdst, zero_flag, u, post)
        ys = _experts(blk_expert, n_used, first_flag, next_expert, xs, w1[layer],
                      b1[layer].reshape(N_EXPERTS, 1, -1), w2[layer], b2[layer].reshape(N_EXPERTS, 1, -1))
        h = _combine(cnt8, segst, segdst, ys, h_mid, slab, norm_final_g.reshape(1, d))
    return h.reshape(bsz, seq, d)
```

```python
import functools
import math

import numpy as np
import jax
import jax.numpy as jnp
from jax import lax
from jax.experimental import pallas as pl
from jax.experimental.pallas import tpu as pltpu

F32 = jnp.float32
BF16 = jnp.bfloat16
HIGHEST = lax.Precision.HIGHEST

D_MODEL = 1024
SSD_HEADS = 8
SSD_HEAD_DIM = 64
D_SSD = SSD_HEADS * SSD_HEAD_DIM
SSD_GROUPS = 2
SSD_HEADS_PER_GROUP = SSD_HEADS // SSD_GROUPS
D_STATE = 128
CONV_WIDTH = 4
CHUNK = 128
D_CONV = D_SSD + 2 * SSD_GROUPS * D_STATE
ATTN_Q_HEADS = 8
ATTN_KV_HEADS = 2
ATTN_Q_PER_KV = ATTN_Q_HEADS // ATTN_KV_HEADS
ATTN_HEAD_DIM = 64
D_ATTN = ATTN_Q_HEADS * ATTN_HEAD_DIM
D_KV = ATTN_KV_HEADS * ATTN_HEAD_DIM
WINDOW = 128
REL_BUCKETS = 32
REL_MAX_DIST = 128
N_EXPERTS = 32
TOP_K = 4
D_EXPERT = D_MODEL
SWIGLU_LIMIT = 7.0
SWIGLU_ALPHA = 1.702
RMS_EPS = 1e-5

LANES = 128
SUBLANES = 8
NEG = -1e30
VMEM_LIMIT = 56 * 1024 * 1024

TM_PROJ = 512
TM_ROUTE = 512
BM_EXPERT = 512
SEQS_PER_STEP = 2
R_SORTED = TM_ROUTE * TOP_K + N_EXPERTS * SUBLANES

D_PROJ_PACKED = D_SSD + D_CONV + D_ATTN + 2 * D_KV + LANES


def _params(n_axes):
    return pltpu.CompilerParams(dimension_semantics=("arbitrary",) * n_axes,
                                vmem_limit_bytes=VMEM_LIMIT)


def _rms(x, g):
    return x * lax.rsqrt(jnp.mean(x * x, axis=-1, keepdims=True) + RMS_EPS) * g


def _silu(x):
    return x / (1.0 + jnp.exp(-x))


def _dot(a, b, **kw):
    return jnp.dot(a, b, preferred_element_type=F32, **kw)


def _t5_bucket_table():
    dist = CHUNK + np.arange(CHUNK)[:, None] - np.arange(2 * CHUNK)[None, :]
    in_window = (dist >= 0) & (dist < WINDOW)
    d = np.clip(dist, 0, REL_MAX_DIST)
    max_exact = REL_BUCKETS // 2
    large = max_exact + (np.log(np.maximum(d, 1).astype(np.float32) / max_exact)
                         / math.log(REL_MAX_DIST / max_exact)
                         * (REL_BUCKETS - max_exact)).astype(np.int32)
    large = np.minimum(large, REL_BUCKETS - 1)
    bucket = np.where(d < max_exact, d, large)
    return np.where(in_window, bucket, -1).astype(np.int32)


def _bias_kernel(rb_ref, bucket_ref, o_ref):
    bucket = bucket_ref[...]
    for h in range(ATTN_Q_HEADS):
        acc = jnp.full(bucket.shape, NEG, F32)
        for b in range(REL_BUCKETS):
            acc = jnp.where(bucket == b, rb_ref[b, h], acc)
        o_ref[h] = acc


def _bias_table(rel_bias):
    bucket = jnp.asarray(_t5_bucket_table())
    return pl.pallas_call(
        _bias_kernel,
        out_shape=jax.ShapeDtypeStruct((ATTN_Q_HEADS, CHUNK, 2 * CHUNK), F32),
        in_specs=[pl.BlockSpec(memory_space=pltpu.SMEM),
                  pl.BlockSpec(memory_space=pltpu.VMEM)],
        out_specs=pl.BlockSpec(memory_space=pltpu.VMEM),
        name="bias_table",
    )(rel_bias, bucket)


def _inproj_kernel(x_ref, g_ref, w_ref, z_ref, xbc_ref, q_ref, kv_ref, dt_ref):
    u = _rms(x_ref[...], g_ref[...]).astype(BF16)
    proj = _dot(u, w_ref[...])
    o = 0
    for ref in (z_ref, xbc_ref, q_ref, kv_ref, dt_ref):
        w = ref.shape[1]
        ref[...] = proj[:, o:o + w]
        o += w


def _inproj(x2, g, w_packed):
    t = x2.shape[0]
    widths = (D_SSD, D_CONV, D_ATTN, 2 * D_KV, LANES)
    return pl.pallas_call(
        _inproj_kernel,
        grid=(t // TM_PROJ,),
        in_specs=[pl.BlockSpec((TM_PROJ, D_MODEL), lambda i: (i, 0)),
                  pl.BlockSpec((1, D_MODEL), lambda i: (0, 0)),
                  pl.BlockSpec((D_MODEL, D_PROJ_PACKED), lambda i: (0, 0))],
        out_specs=[pl.BlockSpec((TM_PROJ, w), lambda i: (i, 0)) for w in widths],
        out_shape=[jax.ShapeDtypeStruct((t, w), F32) for w in widths],
        compiler_params=_params(1),
        name="inproj",
    )(x2, g, w_packed)


def _mixer_kernel(sink_ref, z_ref, xbc_ref, dt_ref, q_ref, kv_ref, kvp_ref, cw_ref, cb_ref, dtb_ref,
                  alog_ref, dskip_ref, gs_ref, expand_ref, bias_ref, ga_ref, o_ref, state_ref, xpad_ref):
    @pl.when(pl.program_id(1) == 0)
    def _():
        state_ref[...] = jnp.zeros(state_ref.shape, F32)
        xpad_ref[:, 0:SUBLANES, :] = jnp.zeros((xpad_ref.shape[0], SUBLANES, D_CONV), F32)

    for s in range(z_ref.shape[0]):
        y_ssd = _ssd_chunk(z_ref.at[s], xbc_ref.at[s], dt_ref.at[s], cw_ref, cb_ref, dtb_ref, alog_ref,
                           dskip_ref, gs_ref, expand_ref, state_ref.at[s], xpad_ref.at[s])
        y_attn = _swa_block(sink_ref, q_ref.at[s], kv_ref.at[s], kvp_ref.at[s], bias_ref, ga_ref)
        o_ref[s] = jnp.concatenate([y_ssd, y_attn], axis=1)


def _ssd_chunk(z_ref, xbc_ref, dt_ref, cw_ref, cb_ref, dtb_ref, alog_ref, dskip_ref, g_ref,
               expand_ref, state_ref, xpad_ref):
    L = CHUNK
    G, R, P, N = SSD_GROUPS, SSD_HEADS_PER_GROUP, SSD_HEAD_DIM, D_STATE
    GW = R * P

    xpad_ref[SUBLANES:SUBLANES + L, :] = xbc_ref[...]
    acc = jnp.broadcast_to(cb_ref[...], (L, D_CONV))
    for k in range(CONV_WIDTH):
        acc = acc + cw_ref[k:k + 1, :] * xpad_ref[pl.ds(SUBLANES - (CONV_WIDTH - 1) + k, L), :]
    xpad_ref[0:SUBLANES, :] = xpad_ref[L:L + SUBLANES, :]
    xbc = _silu(acc)
    xs = xbc[:, :D_SSD]
    bm = xbc[:, D_SSD:D_SSD + G * N]
    cm = xbc[:, D_SSD + G * N:]

    dtr = dt_ref[...] + dtb_ref[...]
    dt = jnp.maximum(dtr, 0.0) + jnp.log(1.0 + jnp.exp(-jnp.abs(dtr)))
    a_dt = dt * (-jnp.exp(alog_ref[...]))
    ri = lax.broadcasted_iota(jnp.int32, (L, L), 0)
    ci = lax.broadcasted_iota(jnp.int32, (L, L), 1)
    causal = ci <= ri
    a_cum = _dot(causal.astype(F32), a_dt, precision=HIGHEST)
    a_cum_t = _dot(a_dt.T, (ri <= ci).astype(F32), precision=HIGHEST)
    a_last = a_cum[L - 1:L, :]
    stack = jnp.concatenate(
        [dt, jnp.exp(a_cum), jnp.exp(a_last - a_cum),
         jnp.broadcast_to(jnp.exp(a_last), (SUBLANES, LANES))], axis=0)
    ex = _dot(stack, expand_ref[...], precision=HIGHEST)
    dt_x, ea_x, dte_x, cd_x = ex[0:L], ex[L:2 * L], ex[2 * L:3 * L], ex[3 * L:3 * L + 1]
    xdt = xs * dt_x

    ys = []
    for g in range(G):
        bm_g = bm[:, g * N:(g + 1) * N]
        cm_g = cm[:, g * N:(g + 1) * N].astype(BF16)
        cb = lax.dot_general(cm_g, bm_g.astype(BF16), (((1,), (1,)), ((), ())),
                             preferred_element_type=F32)
        xdt_g = xdt[:, g * GW:(g + 1) * GW]
        yd = []
        for r in range(R):
            h = g * R + r
            seg = a_cum[:, h:h + 1] - a_cum_t[h:h + 1, :]
            dec = jnp.exp(jnp.where(causal, seg, NEG))
            yd.append(_dot((cb * dec).astype(BF16), xdt_g[:, r * P:(r + 1) * P].astype(BF16)))
        y_diag = jnp.concatenate(yd, axis=1)
        st = state_ref[g]
        y_off = _dot(cm_g, st.astype(BF16)) * ea_x[:, g * GW:(g + 1) * GW]
        new = _dot(bm_g.T.astype(BF16), (xdt_g * dte_x[:, g * GW:(g + 1) * GW]).astype(BF16))
        state_ref[g] = st * cd_x[:, g * GW:(g + 1) * GW] + new
        ys.append(y_diag + y_off + xs[:, g * GW:(g + 1) * GW] * dskip_ref[:, g * GW:(g + 1) * GW])
    y = jnp.concatenate(ys, axis=1)
    return _rms(y * _silu(z_ref[...]), g_ref[...])


def _mixer(z, xbc, dt, q, kv, conv_w, conv_b, dt_bias, a_log, d_skip_x, ssd_norm_g, sinks, bias,
           attn_norm_g):
    bsz, seq, _ = z.shape
    nc = seq // CHUNK
    expand = np.zeros((LANES, D_SSD), np.float32)
    for h in range(SSD_HEADS):
        expand[h, h * SSD_HEAD_DIM:(h + 1) * SSD_HEAD_DIM] = 1.0
    ns = SEQS_PER_STEP
    row = lambda b, c: (b, c, 0)
    prev = lambda b, c: (b, jnp.maximum(c - 1, 0), 0)
    fixed = lambda b, c: (0, 0)
    return pl.pallas_call(
        _mixer_kernel,
        grid=(bsz // ns, nc),
        in_specs=[pl.BlockSpec(memory_space=pltpu.SMEM),
                  pl.BlockSpec((ns, CHUNK, D_SSD), row),
                  pl.BlockSpec((ns, CHUNK, D_CONV), row),
                  pl.BlockSpec((ns, CHUNK, LANES), row),
                  pl.BlockSpec((ns, CHUNK, D_ATTN), row),
                  pl.BlockSpec((ns, CHUNK, 2 * D_KV), row),
                  pl.BlockSpec((ns, CHUNK, 2 * D_KV), prev),
                  pl.BlockSpec((CONV_WIDTH, D_CONV), fixed),
                  pl.BlockSpec((1, D_CONV), fixed),
                  pl.BlockSpec((1, LANES), fixed),
                  pl.BlockSpec((1, LANES), fixed),
                  pl.BlockSpec((1, D_SSD), fixed),
                  pl.BlockSpec((1, D_SSD), fixed),
                  pl.BlockSpec((LANES, D_SSD), fixed),
                  pl.BlockSpec((ATTN_Q_HEADS, CHUNK, 2 * CHUNK), lambda b, c: (0, 0, 0)),
                  pl.BlockSpec((1, D_ATTN), fixed)],
        out_specs=pl.BlockSpec((ns, CHUNK, D_SSD + D_ATTN), row),
        out_shape=jax.ShapeDtypeStruct((bsz, seq, D_SSD + D_ATTN), F32),
        scratch_shapes=[pltpu.VMEM((ns, SSD_GROUPS, D_STATE, SSD_HEADS_PER_GROUP * SSD_HEAD_DIM), F32),
                        pltpu.VMEM((ns, CHUNK + SUBLANES, D_CONV), F32)],
        compiler_params=_params(2),
        name="mixer",
    )(sinks, z, xbc, dt, q, kv, kv, conv_w, conv_b, dt_bias, a_log, d_skip_x, ssd_norm_g,
      jnp.asarray(expand), bias, attn_norm_g)


def _swa_block(sink_ref, q_ref, kv_ref, kvp_ref, bias_ref, g_ref):
    L, Dh = CHUNK, ATTN_HEAD_DIM
    q = q_ref[...] * (1.0 / math.sqrt(Dh))
    kv = kv_ref[...]
    kvp = kvp_ref[...]
    col = lax.broadcasted_iota(jnp.int32, (L, 2 * L), 1)
    first_col = jnp.where(pl.program_id(1) > 0, 0, L)
    outs = []
    for hk in range(ATTN_KV_HEADS):
        ks = slice(hk * Dh, (hk + 1) * Dh)
        vs = slice(D_KV + hk * Dh, D_KV + (hk + 1) * Dh)
        kc = jnp.concatenate([kvp[:, ks], kv[:, ks]], axis=0).astype(BF16)
        vc = jnp.concatenate([kvp[:, vs], kv[:, vs]], axis=0).astype(BF16)
        for g in range(ATTN_Q_PER_KV):
            h = hk * ATTN_Q_PER_KV + g
            qh = q[:, h * Dh:(h + 1) * Dh].astype(BF16)
            s = lax.dot_general(qh, kc, (((1,), (1,)), ((), ())), preferred_element_type=F32)
            s = jnp.where(col >= first_col, s + bias_ref[h], NEG)
            sink = sink_ref[h]
            m = jnp.maximum(jnp.max(s, axis=-1, keepdims=True), sink)
            p = jnp.exp(s - m)
            denom = jnp.sum(p, axis=-1, keepdims=True) + jnp.exp(sink - m)
            outs.append(_dot(p.astype(BF16), vc) / denom)
    return _rms(jnp.concatenate(outs, axis=1), g_ref[...])


def _route_kernel(x_ref, y_ref, wo_ref, g_ref, wr_ref, br_ref,
                  h_ref, u_ref, slab_ref, post_ref, tab_ref, run_ref):
    tm = x_ref.shape[0]

    @pl.when(pl.program_id(0) == 0)
    def _():
        run_ref[...] = jnp.zeros(run_ref.shape, F32)

    h = x_ref[...] + _dot(y_ref[...].astype(BF16), wo_ref[...])
    h_ref[...] = h
    u = _rms(h, g_ref[...])
    ub = u.astype(BF16)
    u_ref[...] = ub
    E = N_EXPERTS
    logits = lax.dot_general(wr_ref[...], ub, (((1,), (1,)), ((), ())), preferred_element_type=F32)
    logits = logits + jnp.concatenate([br_ref[...]] * (tm // LANES), axis=1)

    eidx = lax.broadcasted_iota(jnp.int32, (E, tm), 0).astype(F32)
    vals, idxs = [], []
    cur = logits
    for _ in range(TOP_K):
        m = jnp.max(cur, axis=0, keepdims=True)
        ix = jnp.min(jnp.where(cur == m, eidx, float(E)), axis=0, keepdims=True)
        vals.append(m)
        idxs.append(ix)
        cur = jnp.where(eidx == ix, NEG, cur)
    es = [jnp.exp(v - vals[0]) for v in vals]
    den = es[0] + es[1] + es[2] + es[3]

    onehot = jnp.zeros((E, tm), F32)
    for ix in idxs:
        onehot = onehot + (eidx == ix).astype(F32)
    ri = lax.broadcasted_iota(jnp.int32, (tm, tm), 0)
    ci = lax.broadcasted_iota(jnp.int32, (tm, tm), 1)
    before = _dot(onehot.astype(BF16), (ri < ci).astype(BF16))
    cnt = jnp.sum(onehot, axis=1, keepdims=True)
    cnt8 = jnp.floor((cnt + (SUBLANES - 1)) * (1.0 / SUBLANES)) * SUBLANES
    el = lax.broadcasted_iota(jnp.int32, (E, E), 0)
    ec = lax.broadcasted_iota(jnp.int32, (E, E), 1)
    seg_start = _dot((ec < el).astype(F32), jnp.broadcast_to(cnt8, (E, LANES)),
                     precision=HIGHEST)[:, 0:1]
    where_to = before + seg_start

    rows = [jnp.sum(jnp.where(eidx == idxs[k], where_to, 0.0), axis=0, keepdims=True)
            for k in range(TOP_K)]
    rows += [es[k] / den for k in range(TOP_K)]
    stack = jnp.concatenate(rows, axis=0)
    post_ref[...] = stack.astype(jnp.int32)
    slab_ref[...] = jnp.concatenate([stack, jnp.zeros((LANES - 2 * TOP_K, tm), F32)], axis=0).T

    lane = lax.broadcasted_iota(jnp.int32, (E, LANES), 1)
    tab = jnp.where(lane == 0, cnt8, jnp.where(lane == 1, run_ref[...], jnp.where(lane == 2, seg_start, 0.0)))
    tab_ref[...] = tab.astype(jnp.int32)
    run_ref[...] = run_ref[...] + cnt8


def _route(x2, y_mix, wo, g, wr, br):
    t = x2.shape[0]
    tm = TM_ROUTE
    row = lambda i: (i, 0)
    fixed = lambda i: (0, 0)
    return pl.pallas_call(
        _route_kernel,
        grid=(t // tm,),
        in_specs=[pl.BlockSpec((tm, D_MODEL), row),
                  pl.BlockSpec((tm, D_SSD + D_ATTN), row),
                  pl.BlockSpec((D_SSD + D_ATTN, D_MODEL), fixed),
                  pl.BlockSpec((1, D_MODEL), fixed),
                  pl.BlockSpec((N_EXPERTS, D_MODEL), fixed),
                  pl.BlockSpec((N_EXPERTS, LANES), fixed)],
        out_specs=[pl.BlockSpec((tm, D_MODEL), row),
                   pl.BlockSpec((tm, D_MODEL), row),
                   pl.BlockSpec((tm, LANES), row),
                   pl.BlockSpec((SUBLANES, tm), lambda i: (0, i)),
                   pl.BlockSpec((N_EXPERTS, LANES), row)],
        out_shape=[jax.ShapeDtypeStruct((t, D_MODEL), F32),
                   jax.ShapeDtypeStruct((t, D_MODEL), BF16),
                   jax.ShapeDtypeStruct((t, LANES), F32),
                   jax.ShapeDtypeStruct((SUBLANES, t), jnp.int32),
                   jax.ShapeDtypeStruct((t // tm * N_EXPERTS, LANES), jnp.int32)],
        scratch_shapes=[pltpu.VMEM((N_EXPERTS, LANES), F32)],
        compiler_params=_params(1),
        name="route",
    )(x2, y_mix, wo, g, wr, br)


SEG_SIZE_BITS = (TM_ROUTE // SUBLANES).bit_length()


def _for_each_piece(tile, cnt_ref, segst_ref, segdst_ref, fn):
    def per_expert(e, carry):
        seg = tile * N_EXPERTS + e
        n = cnt_ref[seg] // SUBLANES
        src0 = segst_ref[seg]
        dst0 = segdst_ref[seg]
        for b in reversed(range(SEG_SIZE_BITS)):
            off = ((n >> (b + 1)) << (b + 1)) * SUBLANES

            @pl.when(((n >> b) & 1) == 1)
            def _():
                fn(SUBLANES << b, src0 + off, dst0 + off)
        return carry

    lax.fori_loop(0, N_EXPERTS, per_expert, 0)


def _rows(ref, row, n):
    return ref.at[pl.ds(pl.multiple_of(row, SUBLANES), n), :]


def _dispatch_kernel(cnt_ref, segst_ref, segdst_ref, zflag_ref, u_ref, post_ref, xs_ref,
                     buf_ref, zero_ref, sems, zsem):
    tm = u_ref.shape[0]
    i = pl.program_id(0)
    slot = i % 2
    bm = zero_ref.shape[0]

    @pl.when(pl.program_id(0) == 0)
    def _():
        zero_ref[...] = jnp.zeros(zero_ref.shape, F32)

        def fill(b):
            return pltpu.make_async_copy(zero_ref, xs_ref.at[pl.ds(b * bm, bm), :], zsem)

        def zstart(b, carry):
            @pl.when(zflag_ref[b] != 0)
            def _():
                fill(b).start()
            return carry

        def zwait(b, carry):
            @pl.when(zflag_ref[b] != 0)
            def _():
                fill(b).wait()
            return carry

        lax.fori_loop(0, zflag_ref.shape[0], zstart, 0)
        lax.fori_loop(0, zflag_ref.shape[0], zwait, 0)

    pos = post_ref[...]
    j = lax.broadcasted_iota(jnp.int32, (R_SORTED, tm), 0)
    sel = jnp.zeros((R_SORTED, tm), F32)
    for k in range(TOP_K):
        sel = jnp.where(j == pos[k:k + 1, :], 1.0, sel)
    buf_ref[slot] = _dot(sel.astype(BF16), u_ref[...])

    def copies(tile, slot_, action):
        def fn(n, src_row, dst_row):
            action(pltpu.make_async_copy(_rows(buf_ref.at[slot_], src_row, n),
                                         _rows(xs_ref, dst_row, n), sems.at[slot_]))
        _for_each_piece(tile, cnt_ref, segst_ref, segdst_ref, fn)

    copies(i, slot, lambda c: c.start())

    @pl.when(i > 0)
    def _():
        copies(i - 1, 1 - slot, lambda c: c.wait())

    @pl.when(i == pl.num_programs(0) - 1)
    def _():
        copies(i, slot, lambda c: c.wait())


def _dispatch(cnt8, segst, segdst, zero_flag, u, post):
    t = u.shape[0]
    tm = TM_ROUTE
    n_rows = zero_flag.shape[0] * BM_EXPERT
    return pl.pallas_call(
        _dispatch_kernel,
        grid_spec=pltpu.PrefetchScalarGridSpec(
            num_scalar_prefetch=4,
            grid=(t // tm,),
            in_specs=[pl.BlockSpec((tm, D_MODEL), lambda i, *_: (i, 0)),
                      pl.BlockSpec((SUBLANES, tm), lambda i, *_: (0, i))],
            out_specs=pl.BlockSpec(memory_space=pl.ANY),
            scratch_shapes=[pltpu.VMEM((2, R_SORTED, D_MODEL), F32),
                            pltpu.VMEM((BM_EXPERT, D_MODEL), F32),
                            pltpu.SemaphoreType.DMA((2,)),
                            pltpu.SemaphoreType.DMA]),
        out_shape=jax.ShapeDtypeStruct((n_rows, D_MODEL), F32),
        compiler_params=_params(1),
        name="dispatch",
    )(cnt8, segst, segdst, zero_flag, u, post)


def _expert_kernel(be_ref, nu_ref, first_ref, next_ref, xs_ref, w1_ref, b1_ref, w2_ref, b2_ref,
                   ys_ref, w1f_ref, w2f_ref, w1b_ref, w2b_ref, sems):
    i = pl.program_id(0)

    def fetch(e):
        return (pltpu.make_async_copy(w1_ref.at[e], w1f_ref, sems.at[0]),
                pltpu.make_async_copy(w2_ref.at[e], w2f_ref, sems.at[1]))

    @pl.when(i >= nu_ref[0])
    def _():
        ys_ref[...] = jnp.zeros(ys_ref.shape, F32)

    @pl.when(i < nu_ref[0])
    def _():
        e = be_ref[i]

        @pl.when(i == 0)
        def _():
            for c in fetch(e):
                c.start()

        @pl.when(first_ref[i] != 0)
        def _():
            for c in fetch(e):
                c.wait()
            w1b_ref[...] = w1f_ref[...].astype(BF16)
            w2b_ref[...] = w2f_ref[...].astype(BF16)

            @pl.when(next_ref[i] >= 0)
            def _():
                for c in fetch(next_ref[i]):
                    c.start()

        hid = _dot(xs_ref[...].astype(BF16), w1b_ref[...]) + b1_ref[e]
        x_glu = jnp.minimum(hid[:, :D_EXPERT], SWIGLU_LIMIT)
        x_lin = jnp.clip(hid[:, D_EXPERT:], -SWIGLU_LIMIT, SWIGLU_LIMIT)
        act = x_glu / (1.0 + jnp.exp(-SWIGLU_ALPHA * x_glu)) * (x_lin + 1.0)
        ys_ref[...] = _dot(act.astype(BF16), w2b_ref[...]) + b2_ref[e]


def _experts(blk_expert, n_used, first_flag, next_expert, xs, w1, b1, w2, b2):
    n_rows = xs.shape[0]
    bm = BM_EXPERT
    nblk = n_rows // bm
    last = lambda i, nu: jnp.maximum(jnp.minimum(i, nu[0] - 1), 0)
    row = lambda i, be, nu, *_: (last(i, nu), 0)
    whole = lambda i, *_: (0, 0, 0)
    return pl.pallas_call(
        _expert_kernel,
        grid_spec=pltpu.PrefetchScalarGridSpec(
            num_scalar_prefetch=4,
            grid=(nblk,),
            in_specs=[pl.BlockSpec((bm, D_MODEL), row),
                      pl.BlockSpec(memory_space=pl.ANY),
                      pl.BlockSpec((N_EXPERTS, 1, 2 * D_EXPERT), whole),
                      pl.BlockSpec(memory_space=pl.ANY),
                      pl.BlockSpec((N_EXPERTS, 1, D_MODEL), whole)],
            out_specs=pl.BlockSpec((bm, D_MODEL), lambda i, *_: (i, 0)),
            scratch_shapes=[pltpu.VMEM((D_MODEL, 2 * D_EXPERT), F32),
                            pltpu.VMEM((D_EXPERT, D_MODEL), F32),
                            pltpu.VMEM((D_MODEL, 2 * D_EXPERT), BF16),
                            pltpu.VMEM((D_EXPERT, D_MODEL), BF16),
                            pltpu.SemaphoreType.DMA((2,))]),
        out_shape=jax.ShapeDtypeStruct((n_rows, D_MODEL), F32),
        compiler_params=_params(1),
        name="experts",
    )(blk_expert, n_used, first_flag, next_expert, xs, w1, b1, w2, b2)


def _combine_kernel(cnt_ref, segst_ref, segdst_ref, ys_ref, h_ref, slab_ref, g_ref, o_ref,
                    buf_ref, sems):
    tm = h_ref.shape[0]
    i = pl.program_id(0)
    slot = i % 2

    def copies(tile, slot_, action):
        def fn(n, sorted_row, ys_row):
            action(pltpu.make_async_copy(_rows(ys_ref, ys_row, n),
                                         _rows(buf_ref.at[slot_], sorted_row, n), sems.at[slot_]))
        _for_each_piece(tile, cnt_ref, segst_ref, segdst_ref, fn)

    @pl.when(i == 0)
    def _():
        buf_ref[...] = jnp.zeros(buf_ref.shape, F32)
        copies(0, 0, lambda c: c.start())

    @pl.when(i + 1 < pl.num_programs(0))
    def _():
        copies(i + 1, 1 - slot, lambda c: c.start())

    copies(i, slot, lambda c: c.wait())

    slab = slab_ref[...]
    lane = lax.broadcasted_iota(jnp.int32, (tm, R_SORTED), 1).astype(F32)
    wmat = jnp.zeros((tm, R_SORTED), F32)
    for k in range(TOP_K):
        wmat = jnp.where(lane == slab[:, k:k + 1], slab[:, TOP_K + k:TOP_K + k + 1], wmat)
    moe = _dot(wmat.astype(BF16), buf_ref[slot].astype(BF16))
    o_ref[...] = _rms(h_ref[...] + moe, g_ref[...])


def _combine(cnt8, segst, segdst, ys, h, slab, g):
    t = h.shape[0]
    tm = TM_ROUTE
    return pl.pallas_call(
        _combine_kernel,
        grid_spec=pltpu.PrefetchScalarGridSpec(
            num_scalar_prefetch=3,
            grid=(t // tm,),
            in_specs=[pl.BlockSpec(memory_space=pl.ANY),
                      pl.BlockSpec((tm, D_MODEL), lambda i, *_: (i, 0)),
                      pl.BlockSpec((tm, LANES), lambda i, *_: (i, 0)),
                      pl.BlockSpec((1, D_MODEL), lambda i, *_: (0, 0))],
            out_specs=pl.BlockSpec((tm, D_MODEL), lambda i, *_: (i, 0)),
            scratch_shapes=[pltpu.VMEM((2, R_SORTED, D_MODEL), F32),
                            pltpu.SemaphoreType.DMA((2,))]),
        out_shape=jax.ShapeDtypeStruct((t, D_MODEL), F32),
        compiler_params=_params(1),
        name="combine",
    )(cnt8, segst, segdst, ys, h, slab, g)


def _pad_lanes(v, fill=0.0):
    return jnp.pad(v.reshape(1, -1), ((0, 0), (0, LANES - v.shape[-1])), constant_values=fill)


def kernel(x, norm_mix_g, w_in, conv_w, conv_b, dt_bias, a_log, d_skip, ssd_norm_g, attn_sinks,
           rel_bias, attn_norm_g, w_out, norm_ffn_g, w_router, b_router, w1, b1, w2, b2,
           norm_final_g):
    bsz, seq, d = x.shape
    t = bsz * seq
    nc = seq // CHUNK
    depth = w_in.shape[0]
    o2 = D_SSD + D_CONV
    o3 = o2 + SSD_HEADS
    bias = _bias_table(rel_bias)

    assert depth == 1, "the final norm is fused into the combine kernel: single layer only"
    h = x.reshape(t, d)
    for layer in range(depth):
        w = w_in[layer]
        w_packed = jnp.concatenate(
            [w[:, :o2], w[:, o3:], w[:, o2:o3], jnp.zeros((d, LANES - SSD_HEADS), w.dtype)],
            axis=1).astype(BF16)
        z, xbc, q, kv, dt = [a.reshape(bsz, seq, -1)
                             for a in _inproj(h, norm_mix_g[layer].reshape(1, d), w_packed)]
        y_mix = _mixer(z, xbc, dt, q, kv, conv_w[layer], conv_b[layer].reshape(1, -1),
                       _pad_lanes(dt_bias[layer]), _pad_lanes(a_log[layer]),
                       jnp.repeat(d_skip[layer], SSD_HEAD_DIM).reshape(1, -1),
                       ssd_norm_g[layer].reshape(1, -1), attn_sinks[layer], bias,
                       attn_norm_g[layer].reshape(1, -1)).reshape(t, -1)

        wr = w_router[layer].T.astype(BF16)
        br = jnp.broadcast_to(b_router[layer][:, None], (N_EXPERTS, LANES))
        h_mid, u, slab, post, tab = _route(
            h, y_mix, w_out[layer].astype(BF16), norm_ffn_g[layer].reshape(1, d), wr, br)

        bm = BM_EXPERT
        ntiles = t // TM_ROUTE
        nblk = (t * TOP_K + ntiles * N_EXPERTS * (SUBLANES - 1)) // bm + N_EXPERTS
        tab = tab.reshape(ntiles, N_EXPERTS, LANES)
        cnt8, off8, segst = tab[:, :, 0], tab[:, :, 1], tab[:, :, 2]
        total8 = off8[-1] + cnt8[-1]
        padded = ((total8 + bm - 1) // bm) * bm
        pend = jnp.cumsum(padded)
        segdst = (pend - padded)[None, :] + off8
        n_used = pend[-1:] // bm
        eids = jnp.arange(N_EXPERTS, dtype=jnp.int32)
        blk = jnp.arange(nblk, dtype=jnp.int32)
        blk_expert = jnp.minimum(
            jnp.sum((blk[:, None] * bm >= pend[None, :]).astype(jnp.int32), axis=1), N_EXPERTS - 1)
        after = jnp.concatenate([blk_expert[1:], blk_expert[-1:]])
        before = jnp.concatenate([blk_expert[:1], blk_expert[:-1]])
        zero_flag = ((blk >= n_used[0] - 1) | (after != blk_expert)).astype(jnp.int32)
        first_flag = ((blk == 0) | (before != blk_expert)).astype(jnp.int32)
        cand = jnp.where((eids[None, :] > eids[:, None]) & (padded[None, :] > 0), eids[None, :], N_EXPERTS)
        next_nonempty = jnp.min(cand, axis=1)
        next_nonempty = jnp.where(next_nonempty == N_EXPERTS, -1, next_nonempty)
        next_expert = jnp.sum(jnp.where(blk_expert[:, None] == eids[None, :], next_nonempty[None, :], 0), axis=1)
        cnt8, segst, segdst = cnt8.reshape(-1), segst.reshape(-1), segdst.reshape(-1)

        xs = _dispatch(cnt8, segst, segdst, zero_flag, u, post)
        ys = _experts(blk_expert, n_used, first_flag, next_expert, xs, w1[layer],
                      b1[layer].reshape(N_EXPERTS, 1, -1), w2[layer], b2[layer].reshape(N_EXPERTS, 1, -1))
        h = _combine(cnt8, segst, segdst, ys, h_mid, slab, norm_final_g.reshape(1, d))
    return h.reshape(bsz, seq, d)
```

```python
import functools
import math

import numpy as np
import jax
import jax.numpy as jnp
from jax import lax
from jax.experimental import pallas as pl
from jax.experimental.pallas import tpu as pltpu

F32 = jnp.float32
BF16 = jnp.bfloat16

D_MODEL = 1024
SSD_HEADS = 8
SSD_HEAD_DIM = 64
D_SSD = SSD_HEADS * SSD_HEAD_DIM
SSD_GROUPS = 2
SSD_HEADS_PER_GROUP = SSD_HEADS // SSD_GROUPS
D_STATE = 128
CONV_WIDTH = 4
CHUNK = 128
D_CONV = D_SSD + 2 * SSD_GROUPS * D_STATE
ATTN_Q_HEADS = 8
ATTN_KV_HEADS = 2
ATTN_Q_PER_KV = ATTN_Q_HEADS // ATTN_KV_HEADS
ATTN_HEAD_DIM = 64
D_ATTN = ATTN_Q_HEADS * ATTN_HEAD_DIM
D_KV = ATTN_KV_HEADS * ATTN_HEAD_DIM
WINDOW = 128
REL_BUCKETS = 32
REL_MAX_DIST = 128
N_EXPERTS = 32
TOP_K = 4
D_EXPERT = D_MODEL
SWIGLU_LIMIT = 7.0
SWIGLU_ALPHA = 1.702
RMS_EPS = 1e-5

LANES = 128
SUBLANES = 8
NEG = -1e30
VMEM_LIMIT = 56 * 1024 * 1024

TM_PROJ = 512
TM_ROUTE = 512
BM_EXPERT = 512
EXPERT_ROW_STEP = 128
SEQS_PER_STEP = 2
R_SORTED = TM_ROUTE * TOP_K + N_EXPERTS * SUBLANES

D_PROJ_PACKED = D_SSD + D_CONV + D_ATTN + 2 * D_KV + LANES


def _params(n_axes):
    return pltpu.CompilerParams(dimension_semantics=("arbitrary",) * n_axes,
                                vmem_limit_bytes=VMEM_LIMIT)


def _rms(x, g):
    return x * lax.rsqrt(jnp.mean(x * x, axis=-1, keepdims=True) + RMS_EPS) * g


def _silu(x):
    return x / (1.0 + jnp.exp(-x))


def _dot(a, b, **kw):
    return jnp.dot(a, b, preferred_element_type=F32, **kw)


def _dot_f32_by_mask(mask, x, mask_on_left=False):
    hi = x.astype(BF16)
    r1 = x - hi.astype(F32)
    mid = r1.astype(BF16)
    lo = (r1 - mid.astype(F32)).astype(BF16)
    out = None
    for piece in (hi, mid, lo):
        term = _dot(mask, piece) if mask_on_left else _dot(piece, mask)
        out = term if out is None else out + term
    return out


def _t5_bucket_table():
    dist = CHUNK + np.arange(CHUNK)[:, None] - np.arange(2 * CHUNK)[None, :]
    in_window = (dist >= 0) & (dist < WINDOW)
    d = np.clip(dist, 0, REL_MAX_DIST)
    max_exact = REL_BUCKETS // 2
    large = max_exact + (np.log(np.maximum(d, 1).astype(np.float32) / max_exact)
                         / math.log(REL_MAX_DIST / max_exact)
                         * (REL_BUCKETS - max_exact)).astype(np.int32)
    large = np.minimum(large, REL_BUCKETS - 1)
    bucket = np.where(d < max_exact, d, large)
    return np.where(in_window, bucket, -1).astype(np.int32)


def _bias_kernel(rb_ref, bucket_ref, o_ref):
    bucket = bucket_ref[...]
    for h in range(ATTN_Q_HEADS):
        acc = jnp.full(bucket.shape, NEG, F32)
        for b in range(REL_BUCKETS):
            acc = jnp.where(bucket == b, rb_ref[b, h], acc)
        o_ref[h] = acc


def _bias_table(rel_bias):
    bucket = jnp.asarray(_t5_bucket_table())
    return pl.pallas_call(
        _bias_kernel,
        out_shape=jax.ShapeDtypeStruct((ATTN_Q_HEADS, CHUNK, 2 * CHUNK), F32),
        in_specs=[pl.BlockSpec(memory_space=pltpu.SMEM),
                  pl.BlockSpec(memory_space=pltpu.VMEM)],
        out_specs=pl.BlockSpec(memory_space=pltpu.VMEM),
        name="bias_table",
    )(rel_bias, bucket)


def _inproj_kernel(x_ref, g_ref, w_ref, z_ref, xbc_ref, q_ref, kv_ref, dt_ref):
    u = _rms(x_ref[...], g_ref[...]).astype(BF16)
    proj = _dot(u, w_ref[...])
    o = 0
    for ref in (z_ref, xbc_ref, q_ref, kv_ref, dt_ref):
        w = ref.shape[1]
        ref[...] = proj[:, o:o + w]
        o += w


def _inproj(x2, g, w_packed):
    t = x2.shape[0]
    widths = (D_SSD, D_CONV, D_ATTN, 2 * D_KV, LANES)
    return pl.pallas_call(
        _inproj_kernel,
        grid=(t // TM_PROJ,),
        in_specs=[pl.BlockSpec((TM_PROJ, D_MODEL), lambda i: (i, 0)),
                  pl.BlockSpec((1, D_MODEL), lambda i: (0, 0)),
                  pl.BlockSpec((D_MODEL, D_PROJ_PACKED), lambda i: (0, 0))],
        out_specs=[pl.BlockSpec((TM_PROJ, w), lambda i: (i, 0)) for w in widths],
        out_shape=[jax.ShapeDtypeStruct((t, w), F32) for w in widths],
        compiler_params=_params(1),
        name="inproj",
    )(x2, g, w_packed)


def _mixer_kernel(sink_ref, z_ref, xbc_ref, dt_ref, q_ref, kv_ref, kvp_ref, cw_ref, cb_ref, dtb_ref,
                  alog_ref, dskip_ref, gs_ref, expand_ref, bias_ref, ga_ref, o_ref, state_ref, xpad_ref):
    @pl.when(pl.program_id(1) == 0)
    def _():
        state_ref[...] = jnp.zeros(state_ref.shape, F32)
        xpad_ref[...] = jnp.zeros(xpad_ref.shape, F32)

    for s in range(z_ref.shape[0]):
        y_ssd = _ssd_chunk(z_ref.at[s], xbc_ref.at[s], dt_ref.at[s], cw_ref, cb_ref, dtb_ref, alog_ref,
                           dskip_ref, gs_ref, expand_ref, state_ref.at[s], xpad_ref.at[s])
        y_attn = _swa_block(sink_ref, q_ref.at[s], kv_ref.at[s], kvp_ref.at[s], bias_ref, ga_ref)
        o_ref[s] = jnp.concatenate([y_ssd, y_attn], axis=1)


def _ssd_chunk(z_ref, xbc_ref, dt_ref, cw_ref, cb_ref, dtb_ref, alog_ref, dskip_ref, g_ref,
               expand_ref, state_ref, xpad_ref):
    L = CHUNK
    G, R, P, N = SSD_GROUPS, SSD_HEADS_PER_GROUP, SSD_HEAD_DIM, D_STATE
    GW = R * P

    x_cur = xbc_ref[...]
    xpad = jnp.concatenate([xpad_ref[...], x_cur], axis=0)
    xpad_ref[...] = x_cur[L - SUBLANES:, :]
    acc = cb_ref[...] + cw_ref[CONV_WIDTH - 1:CONV_WIDTH, :] * x_cur
    for d in range(1, CONV_WIDTH):
        k = CONV_WIDTH - 1 - d
        acc = acc + cw_ref[k:k + 1, :] * pltpu.roll(xpad, d, axis=0)[SUBLANES:, :]
    xbc = _silu(acc)
    xs = xbc[:, :D_SSD]
    bm = xbc[:, D_SSD:D_SSD + G * N]
    cm = xbc[:, D_SSD + G * N:]

    dtr = dt_ref[...] + dtb_ref[...]
    dt = jnp.maximum(dtr, 0.0) + jnp.log(1.0 + jnp.exp(-jnp.abs(dtr)))
    a_dt = dt * (-jnp.exp(alog_ref[...]))
    ri = lax.broadcasted_iota(jnp.int32, (L, L), 0)
    ci = lax.broadcasted_iota(jnp.int32, (L, L), 1)
    causal = ci <= ri
    a_cum = _dot_f32_by_mask(causal.astype(BF16), a_dt, mask_on_left=True)
    a_cum_t = _dot_f32_by_mask((ri <= ci).astype(BF16), a_dt.T)
    a_last = a_cum[L - 1:L, :]
    stack = jnp.concatenate(
        [dt, jnp.exp(a_cum), jnp.exp(a_last - a_cum),
         jnp.broadcast_to(jnp.exp(a_last), (SUBLANES, LANES))], axis=0)
    ex = _dot_f32_by_mask(expand_ref[...], stack)
    dt_x, ea_x, dte_x, cd_x = ex[0:L], ex[L:2 * L], ex[2 * L:3 * L], ex[3 * L:3 * L + 1]
    xdt = xs * dt_x

    ys = []
    for g in range(G):
        bm_g = bm[:, g * N:(g + 1) * N]
        cm_g = cm[:, g * N:(g + 1) * N].astype(BF16)
        cb = lax.dot_general(cm_g, bm_g.astype(BF16), (((1,), (1,)), ((), ())),
                             preferred_element_type=F32)
        xdt_g = xdt[:, g * GW:(g + 1) * GW]
        yd = []
        for r in range(R):
            h = g * R + r
            seg = a_cum[:, h:h + 1] - a_cum_t[h:h + 1, :]
            dec = jnp.exp(jnp.where(causal, seg, NEG))
            yd.append(_dot((cb * dec).astype(BF16), xdt_g[:, r * P:(r + 1) * P].astype(BF16)))
        y_diag = jnp.concatenate(yd, axis=1)
        st = state_ref[g]
        y_off = _dot(cm_g, st.astype(BF16)) * ea_x[:, g * GW:(g + 1) * GW]
        new = _dot(bm_g.T.astype(BF16), (xdt_g * dte_x[:, g * GW:(g + 1) * GW]).astype(BF16))
        state_ref[g] = st * cd_x[:, g * GW:(g + 1) * GW] + new
        ys.append(y_diag + y_off + xs[:, g * GW:(g + 1) * GW] * dskip_ref[:, g * GW:(g + 1) * GW])
    y = jnp.concatenate(ys, axis=1)
    return _rms(y * _silu(z_ref[...]), g_ref[...])


def _mixer(z, xbc, dt, q, kv, conv_w, conv_b, dt_bias, a_log, d_skip_x, ssd_norm_g, sinks, bias,
           attn_norm_g):
    bsz, seq, _ = z.shape
    nc = seq // CHUNK
    expand = np.zeros((LANES, D_SSD), np.float32)
    for h in range(SSD_HEADS):
        expand[h, h * SSD_HEAD_DIM:(h + 1) * SSD_HEAD_DIM] = 1.0
    ns = SEQS_PER_STEP
    row = lambda b, c: (b, c, 0)
    prev = lambda b, c: (b, jnp.maximum(c - 1, 0), 0)
    fixed = lambda b, c: (0, 0)
    return pl.pallas_call(
        _mixer_kernel,
        grid=(bsz // ns, nc),
        in_specs=[pl.BlockSpec(memory_space=pltpu.SMEM),
                  pl.BlockSpec((ns, CHUNK, D_SSD), row),
                  pl.BlockSpec((ns, CHUNK, D_CONV), row),
                  pl.BlockSpec((ns, CHUNK, LANES), row),
                  pl.BlockSpec((ns, CHUNK, D_ATTN), row),
                  pl.BlockSpec((ns, CHUNK, 2 * D_KV), row),
                  pl.BlockSpec((ns, CHUNK, 2 * D_KV), prev),
                  pl.BlockSpec((CONV_WIDTH, D_CONV), fixed),
                  pl.BlockSpec((1, D_CONV), fixed),
                  pl.BlockSpec((1, LANES), fixed),
                  pl.BlockSpec((1, LANES), fixed),
                  pl.BlockSpec((1, D_SSD), fixed),
                  pl.BlockSpec((1, D_SSD), fixed),
                  pl.BlockSpec((LANES, D_SSD), fixed),
                  pl.BlockSpec((ATTN_Q_HEADS, CHUNK, 2 * CHUNK), lambda b, c: (0, 0, 0)),
                  pl.BlockSpec((1, D_ATTN), fixed)],
        out_specs=pl.BlockSpec((ns, CHUNK, D_SSD + D_ATTN), row),
        out_shape=jax.ShapeDtypeStruct((bsz, seq, D_SSD + D_ATTN), F32),
        scratch_shapes=[pltpu.VMEM((ns, SSD_GROUPS, D_STATE, SSD_HEADS_PER_GROUP * SSD_HEAD_DIM), F32),
                        pltpu.VMEM((ns, SUBLANES, D_CONV), F32)],
        compiler_params=_params(2),
        name="mixer",
    )(sinks, z, xbc, dt, q, kv, kv, conv_w, conv_b, dt_bias, a_log, d_skip_x, ssd_norm_g,
      jnp.asarray(expand, dtype=BF16), bias, attn_norm_g)


def _swa_block(sink_ref, q_ref, kv_ref, kvp_ref, bias_ref, g_ref):
    L, Dh = CHUNK, ATTN_HEAD_DIM
    q = q_ref[...] * (1.0 / math.sqrt(Dh))
    kv = kv_ref[...]
    kvp = kvp_ref[...]
    col = lax.broadcasted_iota(jnp.int32, (L, 2 * L), 1)
    first_col = jnp.where(pl.program_id(1) > 0, 0, L)
    outs = []
    for hk in range(ATTN_KV_HEADS):
        ks = slice(hk * Dh, (hk + 1) * Dh)
        vs = slice(D_KV + hk * Dh, D_KV + (hk + 1) * Dh)
        kc = jnp.concatenate([kvp[:, ks], kv[:, ks]], axis=0).astype(BF16)
        vc = jnp.concatenate([kvp[:, vs], kv[:, vs]], axis=0).astype(BF16)
        for g in range(ATTN_Q_PER_KV):
            h = hk * ATTN_Q_PER_KV + g
            qh = q[:, h * Dh:(h + 1) * Dh].astype(BF16)
            s = lax.dot_general(qh, kc, (((1,), (1,)), ((), ())), preferred_element_type=F32)
            s = jnp.where(col >= first_col, s + bias_ref[h], NEG)
            sink = sink_ref[h]
            m = jnp.maximum(jnp.max(s, axis=-1, keepdims=True), sink)
            p = jnp.exp(s - m)
            denom = jnp.sum(p, axis=-1, keepdims=True) + jnp.exp(sink - m)
            outs.append(_dot(p.astype(BF16), vc) / denom)
    return _rms(jnp.concatenate(outs, axis=1), g_ref[...])


def _route_kernel(x_ref, y_ref, wo_ref, g_ref, wr_ref, br_ref,
                  h_ref, u_ref, slab_ref, post_ref, tab_ref, run_ref):
    tm = x_ref.shape[0]

    @pl.when(pl.program_id(0) == 0)
    def _():
        run_ref[...] = jnp.zeros(run_ref.shape, F32)

    h = x_ref[...] + _dot(y_ref[...].astype(BF16), wo_ref[...])
    h_ref[...] = h
    u = _rms(h, g_ref[...])
    ub = u.astype(BF16)
    u_ref[...] = ub
    E = N_EXPERTS
    logits = lax.dot_general(wr_ref[...], ub, (((1,), (1,)), ((), ())), preferred_element_type=F32)
    logits = logits + jnp.concatenate([br_ref[...]] * (tm // LANES), axis=1)

    eidx = lax.broadcasted_iota(jnp.int32, (E, tm), 0).astype(F32)
    vals, idxs = [], []
    cur = logits
    for _ in range(TOP_K):
        m = jnp.max(cur, axis=0, keepdims=True)
        ix = jnp.min(jnp.where(cur == m, eidx, float(E)), axis=0, keepdims=True)
        vals.append(m)
        idxs.append(ix)
        cur = jnp.where(eidx == ix, NEG, cur)
    es = [jnp.exp(v - vals[0]) for v in vals]
    den = es[0] + es[1] + es[2] + es[3]

    onehot = jnp.zeros((E, tm), F32)
    for ix in idxs:
        onehot = onehot + (eidx == ix).astype(F32)
    ri = lax.broadcasted_iota(jnp.int32, (tm, tm), 0)
    ci = lax.broadcasted_iota(jnp.int32, (tm, tm), 1)
    before = _dot(onehot.astype(BF16), (ri < ci).astype(BF16))
    cnt = jnp.sum(onehot, axis=1, keepdims=True)
    cnt8 = jnp.floor((cnt + (SUBLANES - 1)) * (1.0 / SUBLANES)) * SUBLANES
    el = lax.broadcasted_iota(jnp.int32, (E, E), 0)
    ec = lax.broadcasted_iota(jnp.int32, (E, E), 1)
    seg_start = _dot_f32_by_mask((ec < el).astype(BF16), jnp.broadcast_to(cnt8, (E, LANES)),
                                 mask_on_left=True)[:, 0:1]
    where_to = before + seg_start

    rows = [jnp.sum(jnp.where(eidx == idxs[k], where_to, 0.0), axis=0, keepdims=True)
            for k in range(TOP_K)]
    rows += [es[k] / den for k in range(TOP_K)]
    stack = jnp.concatenate(rows, axis=0)
    post_ref[...] = stack.astype(jnp.int32)
    slab_ref[...] = jnp.concatenate([stack, jnp.zeros((LANES - 2 * TOP_K, tm), F32)], axis=0).T

    lane = lax.broadcasted_iota(jnp.int32, (E, LANES), 1)
    tab = jnp.where(lane == 0, cnt8, jnp.where(lane == 1, run_ref[...], jnp.where(lane == 2, seg_start, 0.0)))
    tab_ref[...] = tab.astype(jnp.int32)
    run_ref[...] = run_ref[...] + cnt8


def _route(x2, y_mix, wo, g, wr, br):
    t = x2.shape[0]
    tm = TM_ROUTE
    row = lambda i: (i, 0)
    fixed = lambda i: (0, 0)
    return pl.pallas_call(
        _route_kernel,
        grid=(t // tm,),
        in_specs=[pl.BlockSpec((tm, D_MODEL), row),
                  pl.BlockSpec((tm, D_SSD + D_ATTN), row),
                  pl.BlockSpec((D_SSD + D_ATTN, D_MODEL), fixed),
                  pl.BlockSpec((1, D_MODEL), fixed),
                  pl.BlockSpec((N_EXPERTS, D_MODEL), fixed),
                  pl.BlockSpec((N_EXPERTS, LANES), fixed)],
        out_specs=[pl.BlockSpec((tm, D_MODEL), row),
                   pl.BlockSpec((tm, D_MODEL), row),
                   pl.BlockSpec((tm, LANES), row),
                   pl.BlockSpec((SUBLANES, tm), lambda i: (0, i)),
                   pl.BlockSpec((N_EXPERTS, LANES), row)],
        out_shape=[jax.ShapeDtypeStruct((t, D_MODEL), F32),
                   jax.ShapeDtypeStruct((t, D_MODEL), BF16),
                   jax.ShapeDtypeStruct((t, LANES), F32),
                   jax.ShapeDtypeStruct((SUBLANES, t), jnp.int32),
                   jax.ShapeDtypeStruct((t // tm * N_EXPERTS, LANES), jnp.int32)],
        scratch_shapes=[pltpu.VMEM((N_EXPERTS, LANES), F32)],
        compiler_params=_params(1),
        name="route",
    )(x2, y_mix, wo, g, wr, br)


SEG_SIZE_BITS = (TM_ROUTE // SUBLANES).bit_length()


def _for_each_piece(tile, cnt_ref, segst_ref, segdst_ref, fn):
    def per_expert(e, carry):
        seg = tile * N_EXPERTS + e
        n = cnt_ref[seg] // SUBLANES
        src0 = segst_ref[seg]
        dst0 = segdst_ref[seg]
        for b in reversed(range(SEG_SIZE_BITS)):
            off = ((n >> (b + 1)) << (b + 1)) * SUBLANES

            @pl.when(((n >> b) & 1) == 1)
            def _():
                fn(SUBLANES << b, src0 + off, dst0 + off)
        return carry

    lax.fori_loop(0, N_EXPERTS, per_expert, 0)


def _rows(ref, row, n):
    return ref.at[pl.ds(pl.multiple_of(row, SUBLANES), n), :]


def _dispatch_kernel(cnt_ref, segst_ref, segdst_ref, zflag_ref, u_ref, post_ref, xs_ref,
                     buf_ref, zero_ref, sems, zsem):
    tm = u_ref.shape[0]
    i = pl.program_id(0)
    slot = i % 2
    bm = zero_ref.shape[0]

    def fill(b, flag):
        return pltpu.make_async_copy(zero_ref, xs_ref.at[pl.ds(b * bm, bm), :], zsem.at[flag - 1])

    def for_flagged(flag, action):
        def body(b, carry):
            @pl.when(zflag_ref[b] == flag)
            def _():
                action(fill(b, flag))
            return carry

        lax.fori_loop(0, zflag_ref.shape[0], body, 0)

    @pl.when(i == 0)
    def _():
        zero_ref[...] = jnp.zeros(zero_ref.shape, F32)
        for_flagged(1, lambda c: c.start())
        for_flagged(2, lambda c: c.start())
        for_flagged(1, lambda c: c.wait())

    @pl.when(i == pl.num_programs(0) - 1)
    def _():
        for_flagged(2, lambda c: c.wait())

    pos = post_ref[...]
    j = lax.broadcasted_iota(jnp.int32, (R_SORTED, tm), 0)
    sel = jnp.zeros((R_SORTED, tm), F32)
    for k in range(TOP_K):
        sel = jnp.where(j == pos[k:k + 1, :], 1.0, sel)
    buf_ref[slot] = _dot(sel.astype(BF16), u_ref[...])

    def copies(tile, slot_, action):
        def fn(n, src_row, dst_row):
            action(pltpu.make_async_copy(_rows(buf_ref.at[slot_], src_row, n),
                                         _rows(xs_ref, dst_row, n), sems.at[slot_]))
        _for_each_piece(tile, cnt_ref, segst_ref, segdst_ref, fn)

    copies(i, slot, lambda c: c.start())

    @pl.when(i > 0)
    def _():
        copies(i - 1, 1 - slot, lambda c: c.wait())

    @pl.when(i == pl.num_programs(0) - 1)
    def _():
        copies(i, slot, lambda c: c.wait())


def _dispatch(cnt8, segst, segdst, zero_flag, u, post):
    t = u.shape[0]
    tm = TM_ROUTE
    n_rows = zero_flag.shape[0] * BM_EXPERT
    return pl.pallas_call(
        _dispatch_kernel,
        grid_spec=pltpu.PrefetchScalarGridSpec(
            num_scalar_prefetch=4,
            grid=(t // tm,),
            in_specs=[pl.BlockSpec((tm, D_MODEL), lambda i, *_: (i, 0)),
                      pl.BlockSpec((SUBLANES, tm), lambda i, *_: (0, i))],
            out_specs=pl.BlockSpec(memory_space=pl.ANY),
            scratch_shapes=[pltpu.VMEM((2, R_SORTED, D_MODEL), F32),
                            pltpu.VMEM((BM_EXPERT, D_MODEL), F32),
                            pltpu.SemaphoreType.DMA((2,)),
                            pltpu.SemaphoreType.DMA((2,))]),
        out_shape=jax.ShapeDtypeStruct((n_rows, D_MODEL), F32),
        compiler_params=_params(1),
        name="dispatch",
    )(cnt8, segst, segdst, zero_flag, u, post)


def _expert_kernel(be_ref, nu_ref, first_ref, next_ref, rows_ref, xs_ref, w1_ref, b1_ref, w2_ref, b2_ref,
                   ys_ref, w1f_ref, w2f_ref, w1b_ref, w2b_ref, sems):
    i = pl.program_id(0)

    def fetch(e):
        return (pltpu.make_async_copy(w1_ref.at[e], w1f_ref, sems.at[0]),
                pltpu.make_async_copy(w2_ref.at[e], w2f_ref, sems.at[1]))

    @pl.when(i >= nu_ref[0])
    def _():
        ys_ref[...] = jnp.zeros(ys_ref.shape, F32)

    @pl.when(i < nu_ref[0])
    def _():
        e = be_ref[i]

        @pl.when(i == 0)
        def _():
            for c in fetch(e):
                c.start()

        @pl.when(first_ref[i] != 0)
        def _():
            for c in fetch(e):
                c.wait()
            w1b_ref[...] = w1f_ref[...].astype(BF16)
            w2b_ref[...] = w2f_ref[...].astype(BF16)

            @pl.when(next_ref[i] >= 0)
            def _():
                for c in fetch(next_ref[i]):
                    c.start()

        bm = xs_ref.shape[0]
        for m in range(EXPERT_ROW_STEP, bm + 1, EXPERT_ROW_STEP):
            @pl.when(rows_ref[i] == m)
            def _(m=m):
                hid = _dot(xs_ref[0:m, :].astype(BF16), w1b_ref[...]) + b1_ref[e]
                x_glu = jnp.minimum(hid[:, :D_EXPERT], SWIGLU_LIMIT)
                x_lin = jnp.clip(hid[:, D_EXPERT:], -SWIGLU_LIMIT, SWIGLU_LIMIT)
                act = x_glu / (1.0 + jnp.exp(-SWIGLU_ALPHA * x_glu)) * (x_lin + 1.0)
                ys_ref[0:m, :] = _dot(act.astype(BF16), w2b_ref[...]) + b2_ref[e]
                if m < bm:
                    ys_ref[m:, :] = jnp.zeros((bm - m, D_MODEL), F32)


def _experts(blk_expert, n_used, first_flag, next_expert, blk_rows, xs, w1, b1, w2, b2):
    n_rows = xs.shape[0]
    bm = BM_EXPERT
    nblk = n_rows // bm
    last = lambda i, nu: jnp.maximum(jnp.minimum(i, nu[0] - 1), 0)
    row = lambda i, be, nu, *_: (last(i, nu), 0)
    whole = lambda i, *_: (0, 0, 0)
    return pl.pallas_call(
        _expert_kernel,
        grid_spec=pltpu.PrefetchScalarGridSpec(
            num_scalar_prefetch=5,
            grid=(nblk,),
            in_specs=[pl.BlockSpec((bm, D_MODEL), row),
                      pl.BlockSpec(memory_space=pl.ANY),
                      pl.BlockSpec((N_EXPERTS, 1, 2 * D_EXPERT), whole),
                      pl.BlockSpec(memory_space=pl.ANY),
                      pl.BlockSpec((N_EXPERTS, 1, D_MODEL), whole)],
            out_specs=pl.BlockSpec((bm, D_MODEL), lambda i, *_: (i, 0)),
            scratch_shapes=[pltpu.VMEM((D_MODEL, 2 * D_EXPERT), F32),
                            pltpu.VMEM((D_EXPERT, D_MODEL), F32),
                            pltpu.VMEM((D_MODEL, 2 * D_EXPERT), BF16),
                            pltpu.VMEM((D_EXPERT, D_MODEL), BF16),
                            pltpu.SemaphoreType.DMA((2,))]),
        out_shape=jax.ShapeDtypeStruct((n_rows, D_MODEL), F32),
        compiler_params=_params(1),
        name="experts",
    )(blk_expert, n_used, first_flag, next_expert, blk_rows, xs, w1, b1, w2, b2)


def _combine_kernel(cnt_ref, segst_ref, segdst_ref, ys_ref, h_ref, slab_ref, g_ref, o_ref,
                    buf_ref, sems):
    tm = h_ref.shape[0]
    i = pl.program_id(0)
    slot = i % 2

    def copies(tile, slot_, action):
        def fn(n, sorted_row, ys_row):
            action(pltpu.make_async_copy(_rows(ys_ref, ys_row, n),
                                         _rows(buf_ref.at[slot_], sorted_row, n), sems.at[slot_]))
        _for_each_piece(tile, cnt_ref, segst_ref, segdst_ref, fn)

    @pl.when(i == 0)
    def _():
        buf_ref[...] = jnp.zeros(buf_ref.shape, F32)
        copies(0, 0, lambda c: c.start())

    @pl.when(i + 1 < pl.num_programs(0))
    def _():
        copies(i + 1, 1 - slot, lambda c: c.start())

    copies(i, slot, lambda c: c.wait())

    slab = slab_ref[...]
    lane = lax.broadcasted_iota(jnp.int32, (tm, R_SORTED), 1).astype(F32)
    wmat = jnp.zeros((tm, R_SORTED), F32)
    for k in range(TOP_K):
        wmat = jnp.where(lane == slab[:, k:k + 1], slab[:, TOP_K + k:TOP_K + k + 1], wmat)
    moe = _dot(wmat.astype(BF16), buf_ref[slot].astype(BF16))
    o_ref[...] = _rms(h_ref[...] + moe, g_ref[...])


def _combine(cnt8, segst, segdst, ys, h, slab, g):
    t = h.shape[0]
    tm = TM_ROUTE
    return pl.pallas_call(
        _combine_kernel,
        grid_spec=pltpu.PrefetchScalarGridSpec(
            num_scalar_prefetch=3,
            grid=(t // tm,),
            in_specs=[pl.BlockSpec(memory_space=pl.ANY),
                      pl.BlockSpec((tm, D_MODEL), lambda i, *_: (i, 0)),
                      pl.BlockSpec((tm, LANES), lambda i, *_: (i, 0)),
                      pl.BlockSpec((1, D_MODEL), lambda i, *_: (0, 0))],
            out_specs=pl.BlockSpec((tm, D_MODEL), lambda i, *_: (i, 0)),
            scratch_shapes=[pltpu.VMEM((2, R_SORTED, D_MODEL), F32),
                            pltpu.SemaphoreType.DMA((2,))]),
        out_shape=jax.ShapeDtypeStruct((t, D_MODEL), F32),
        compiler_params=_params(1),
        name="combine",
    )(cnt8, segst, segdst, ys, h, slab, g)


def _pad_lanes(v, fill=0.0):
    return jnp.pad(v.reshape(1, -1), ((0, 0), (0, LANES - v.shape[-1])), constant_values=fill)


def kernel(x, norm_mix_g, w_in, conv_w, conv_b, dt_bias, a_log, d_skip, ssd_norm_g, attn_sinks,
           rel_bias, attn_norm_g, w_out, norm_ffn_g, w_router, b_router, w1, b1, w2, b2,
           norm_final_g):
    bsz, seq, d = x.shape
    t = bsz * seq
    nc = seq // CHUNK
    depth = w_in.shape[0]
    o2 = D_SSD + D_CONV
    o3 = o2 + SSD_HEADS
    bias = _bias_table(rel_bias)

    assert depth == 1, "the final norm is fused into the combine kernel: single layer only"
    h = x.reshape(t, d)
    for layer in range(depth):
        w = w_in[layer]
        w_packed = jnp.concatenate(
            [w[:, :o2], w[:, o3:], w[:, o2:o3], jnp.zeros((d, LANES - SSD_HEADS), w.dtype)],
            axis=1).astype(BF16)
        z, xbc, q, kv, dt = [a.reshape(bsz, seq, -1)
                             for a in _inproj(h, norm_mix_g[layer].reshape(1, d), w_packed)]
        y_mix = _mixer(z, xbc, dt, q, kv, conv_w[layer], conv_b[layer].reshape(1, -1),
                       _pad_lanes(dt_bias[layer]), _pad_lanes(a_log[layer]),
                       jnp.repeat(d_skip[layer], SSD_HEAD_DIM).reshape(1, -1),
                       ssd_norm_g[layer].reshape(1, -1), attn_sinks[layer], bias,
                       attn_norm_g[layer].reshape(1, -1)).reshape(t, -1)

        wr = w_router[layer].T.astype(BF16)
        br = jnp.broadcast_to(b_router[layer][:, None], (N_EXPERTS, LANES))
        h_mid, u, slab, post, tab = _route(
            h, y_mix, w_out[layer].astype(BF16), norm_ffn_g[layer].reshape(1, d), wr, br)

        bm = BM_EXPERT
        ntiles = t // TM_ROUTE
        nblk = (t * TOP_K + ntiles * N_EXPERTS * (SUBLANES - 1)) // bm + N_EXPERTS
        tab = tab.reshape(ntiles, N_EXPERTS, LANES)
        cnt8, off8, segst = tab[:, :, 0], tab[:, :, 1], tab[:, :, 2]
        total8 = off8[-1] + cnt8[-1]
        padded = ((total8 + bm - 1) // bm) * bm
        pend = jnp.cumsum(padded)
        segdst = (pend - padded)[None, :] + off8
        n_used = pend[-1:] // bm
        eids = jnp.arange(N_EXPERTS, dtype=jnp.int32)
        blk = jnp.arange(nblk, dtype=jnp.int32)
        blk_expert = jnp.minimum(
            jnp.sum((blk[:, None] * bm >= pend[None, :]).astype(jnp.int32), axis=1), N_EXPERTS - 1)
        after = jnp.concatenate([blk_expert[1:], blk_expert[-1:]])
        before = jnp.concatenate([blk_expert[:1], blk_expert[:-1]])
        zero_flag = jnp.where(blk >= n_used[0], 2,
                              ((blk == n_used[0] - 1) | (after != blk_expert)).astype(jnp.int32))
        e_end = jnp.sum(jnp.where(blk_expert[:, None] == eids[None, :],
                                  ((pend - padded) + total8)[None, :], 0), axis=1)
        valid = jnp.clip(e_end - blk * bm, 0, bm)
        blk_rows = ((valid + EXPERT_ROW_STEP - 1) // EXPERT_ROW_STEP) * EXPERT_ROW_STEP
        first_flag = ((blk == 0) | (before != blk_expert)).astype(jnp.int32)
        cand = jnp.where((eids[None, :] > eids[:, None]) & (padded[None, :] > 0), eids[None, :], N_EXPERTS)
        next_nonempty = jnp.min(cand, axis=1)
        next_nonempty = jnp.where(next_nonempty == N_EXPERTS, -1, next_nonempty)
        next_expert = jnp.sum(jnp.where(blk_expert[:, None] == eids[None, :], next_nonempty[None, :], 0), axis=1)
        cnt8, segst, segdst = cnt8.reshape(-1), segst.reshape(-1), segdst.reshape(-1)

        xs = _dispatch(cnt8, segst, segdst, zero_flag, u, post)
        ys = _experts(blk_expert, n_used, first_flag, next_expert, blk_rows, xs, w1[layer],
                      b1[layer].reshape(N_EXPERTS, 1, -1), w2[layer], b2[layer].reshape(N_EXPERTS, 1, -1))
        h = _combine(cnt8, segst, segdst, ys, h_mid, slab, norm_final_g.reshape(1, d))
    return h.reshape(bsz, seq, d)
```

```python
import functools
import math

import numpy as np
import jax
import jax.numpy as jnp
from jax import lax
from jax.experimental import pallas as pl
from jax.experimental.pallas import tpu as pltpu

F32 = jnp.float32
BF16 = jnp.bfloat16

D_MODEL = 1024
SSD_HEADS = 8
SSD_HEAD_DIM = 64
D_SSD = SSD_HEADS * SSD_HEAD_DIM
SSD_GROUPS = 2
SSD_HEADS_PER_GROUP = SSD_HEADS // SSD_GROUPS
D_STATE = 128
CONV_WIDTH = 4
CHUNK = 128
D_CONV = D_SSD + 2 * SSD_GROUPS * D_STATE
ATTN_Q_HEADS = 8
ATTN_KV_HEADS = 2
ATTN_Q_PER_KV = ATTN_Q_HEADS // ATTN_KV_HEADS
ATTN_HEAD_DIM = 64
D_ATTN = ATTN_Q_HEADS * ATTN_HEAD_DIM
D_KV = ATTN_KV_HEADS * ATTN_HEAD_DIM
WINDOW = 128
REL_BUCKETS = 32
REL_MAX_DIST = 128
N_EXPERTS = 32
TOP_K = 4
D_EXPERT = D_MODEL
SWIGLU_LIMIT = 7.0
SWIGLU_ALPHA = 1.702
RMS_EPS = 1e-5

LANES = 128
SUBLANES = 8
NEG = -1e30
VMEM_LIMIT = 56 * 1024 * 1024

TM_PROJ = 512
TM_ROUTE = 512
BM_EXPERT = 512
EXPERT_ROW_STEP = 128
SEQS_PER_STEP = 2
R_SORTED = TM_ROUTE * TOP_K + N_EXPERTS * SUBLANES

D_PROJ_PACKED = D_SSD + D_CONV + D_ATTN + 2 * D_KV + LANES


def _params(n_axes):
    return pltpu.CompilerParams(dimension_semantics=("arbitrary",) * n_axes,
                                vmem_limit_bytes=VMEM_LIMIT)


def _rms(x, g):
    return x * lax.rsqrt(jnp.mean(x * x, axis=-1, keepdims=True) + RMS_EPS) * g


def _silu(x):
    return x / (1.0 + jnp.exp(-x))


def _dot(a, b, **kw):
    return jnp.dot(a, b, preferred_element_type=F32, **kw)


def _dot_f32_by_mask(mask, x, mask_on_left=False):
    hi = x.astype(BF16)
    r1 = x - hi.astype(F32)
    mid = r1.astype(BF16)
    lo = (r1 - mid.astype(F32)).astype(BF16)
    out = None
    for piece in (hi, mid, lo):
        term = _dot(mask, piece) if mask_on_left else _dot(piece, mask)
        out = term if out is None else out + term
    return out


def _t5_bucket_table():
    dist = CHUNK + np.arange(CHUNK)[:, None] - np.arange(2 * CHUNK)[None, :]
    in_window = (dist >= 0) & (dist < WINDOW)
    d = np.clip(dist, 0, REL_MAX_DIST)
    max_exact = REL_BUCKETS // 2
    large = max_exact + (np.log(np.maximum(d, 1).astype(np.float32) / max_exact)
                         / math.log(REL_MAX_DIST / max_exact)
                         * (REL_BUCKETS - max_exact)).astype(np.int32)
    large = np.minimum(large, REL_BUCKETS - 1)
    bucket = np.where(d < max_exact, d, large)
    return np.where(in_window, bucket, -1).astype(np.int32)


def _bias_kernel(rb_ref, bucket_ref, o_ref):
    bucket = bucket_ref[...]
    for h in range(ATTN_Q_HEADS):
        acc = jnp.full(bucket.shape, NEG, F32)
        for b in range(REL_BUCKETS):
            acc = jnp.where(bucket == b, rb_ref[b, h], acc)
        o_ref[h] = acc


def _bias_table(rel_bias):
    bucket = jnp.asarray(_t5_bucket_table())
    return pl.pallas_call(
        _bias_kernel,
        out_shape=jax.ShapeDtypeStruct((ATTN_Q_HEADS, CHUNK, 2 * CHUNK), F32),
        in_specs=[pl.BlockSpec(memory_space=pltpu.SMEM),
                  pl.BlockSpec(memory_space=pltpu.VMEM)],
        out_specs=pl.BlockSpec(memory_space=pltpu.VMEM),
        name="bias_table",
    )(rel_bias, bucket)


def _inproj_kernel(x_ref, g_ref, w_ref, z_ref, xbc_ref, q_ref, kv_ref, dt_ref):
    u = _rms(x_ref[...], g_ref[...]).astype(BF16)
    proj = _dot(u, w_ref[...])
    o = 0
    for ref in (z_ref, xbc_ref, q_ref, kv_ref, dt_ref):
        w = ref.shape[1]
        ref[...] = proj[:, o:o + w]
        o += w


def _inproj(x2, g, w_packed):
    t = x2.shape[0]
    widths = (D_SSD, D_CONV, D_ATTN, 2 * D_KV, LANES)
    return pl.pallas_call(
        _inproj_kernel,
        grid=(t // TM_PROJ,),
        in_specs=[pl.BlockSpec((TM_PROJ, D_MODEL), lambda i: (i, 0)),
                  pl.BlockSpec((1, D_MODEL), lambda i: (0, 0)),
                  pl.BlockSpec((D_MODEL, D_PROJ_PACKED), lambda i: (0, 0))],
        out_specs=[pl.BlockSpec((TM_PROJ, w), lambda i: (i, 0)) for w in widths],
        out_shape=[jax.ShapeDtypeStruct((t, w), F32) for w in widths],
        compiler_params=_params(1),
        name="inproj",
    )(x2, g, w_packed)


def _mixer_kernel(sink_ref, z_ref, xbc_ref, dt_ref, q_ref, kv_ref, kvp_ref, cw_ref, cb_ref, dtb_ref,
                  alog_ref, dskip_ref, gs_ref, expand_ref, bias_ref, ga_ref, o_ref, state_ref, xpad_ref):
    @pl.when(pl.program_id(1) == 0)
    def _():
        state_ref[...] = jnp.zeros(state_ref.shape, F32)
        xpad_ref[...] = jnp.zeros(xpad_ref.shape, F32)

    for s in range(z_ref.shape[0]):
        y_ssd = _ssd_chunk(z_ref.at[s], xbc_ref.at[s], dt_ref.at[s], cw_ref, cb_ref, dtb_ref, alog_ref,
                           dskip_ref, gs_ref, expand_ref, state_ref.at[s], xpad_ref.at[s])
        y_attn = _swa_block(sink_ref, q_ref.at[s], kv_ref.at[s], kvp_ref.at[s], bias_ref, ga_ref)
        o_ref[s] = jnp.concatenate([y_ssd, y_attn], axis=1)


def _ssd_chunk(z_ref, xbc_ref, dt_ref, cw_ref, cb_ref, dtb_ref, alog_ref, dskip_ref, g_ref,
               expand_ref, state_ref, xpad_ref):
    L = CHUNK
    G, R, P, N = SSD_GROUPS, SSD_HEADS_PER_GROUP, SSD_HEAD_DIM, D_STATE
    GW = R * P

    x_cur = xbc_ref[...]
    xpad = jnp.concatenate([xpad_ref[...], x_cur], axis=0)
    xpad_ref[...] = x_cur[L - SUBLANES:, :]
    acc = cb_ref[...] + cw_ref[CONV_WIDTH - 1:CONV_WIDTH, :] * x_cur
    for d in range(1, CONV_WIDTH):
        k = CONV_WIDTH - 1 - d
        acc = acc + cw_ref[k:k + 1, :] * pltpu.roll(xpad, d, axis=0)[SUBLANES:, :]
    xbc = _silu(acc)
    xs = xbc[:, :D_SSD]
    bm = xbc[:, D_SSD:D_SSD + G * N]
    cm = xbc[:, D_SSD + G * N:]

    dtr = dt_ref[...] + dtb_ref[...]
    dt = jnp.maximum(dtr, 0.0) + jnp.log(1.0 + jnp.exp(-jnp.abs(dtr)))
    a_dt = dt * (-jnp.exp(alog_ref[...]))
    ri = lax.broadcasted_iota(jnp.int32, (L, L), 0)
    ci = lax.broadcasted_iota(jnp.int32, (L, L), 1)
    causal = ci <= ri
    a_cum = _dot_f32_by_mask(causal.astype(BF16), a_dt, mask_on_left=True)
    a_cum_t = _dot_f32_by_mask((ri <= ci).astype(BF16), a_dt.T)
    a_last = a_cum[L - 1:L, :]
    stack = jnp.concatenate(
        [dt, jnp.exp(a_cum), jnp.exp(a_last - a_cum),
         jnp.broadcast_to(jnp.exp(a_last), (SUBLANES, LANES))], axis=0)
    ex = _dot_f32_by_mask(expand_ref[...], stack)
    dt_x, ea_x, dte_x, cd_x = ex[0:L], ex[L:2 * L], ex[2 * L:3 * L], ex[3 * L:3 * L + 1]
    xdt = xs * dt_x

    ys = []
    for g in range(G):
        bm_g = bm[:, g * N:(g + 1) * N]
        cm_g = cm[:, g * N:(g + 1) * N].astype(BF16)
        cb = lax.dot_general(cm_g, bm_g.astype(BF16), (((1,), (1,)), ((), ())),
                             preferred_element_type=F32)
        xdt_g = xdt[:, g * GW:(g + 1) * GW]
        yd = []
        for r in range(R):
            h = g * R + r
            seg = a_cum[:, h:h + 1] - a_cum_t[h:h + 1, :]
            dec = jnp.exp(jnp.where(causal, seg, NEG))
            yd.append(_dot((cb * dec).astype(BF16), xdt_g[:, r * P:(r + 1) * P].astype(BF16)))
        y_diag = jnp.concatenate(yd, axis=1)
        st = state_ref[g]
        y_off = _dot(cm_g, st.astype(BF16)) * ea_x[:, g * GW:(g + 1) * GW]
        new = _dot(bm_g.T.astype(BF16), (xdt_g * dte_x[:, g * GW:(g + 1) * GW]).astype(BF16))
        state_ref[g] = st * cd_x[:, g * GW:(g + 1) * GW] + new
        ys.append(y_diag + y_off + xs[:, g * GW:(g + 1) * GW] * dskip_ref[:, g * GW:(g + 1) * GW])
    y = jnp.concatenate(ys, axis=1)
    return _rms(y * _silu(z_ref[...]), g_ref[...])


def _mixer(z, xbc, dt, q, kv, conv_w, conv_b, dt_bias, a_log, d_skip_x, ssd_norm_g, sinks, bias,
           attn_norm_g):
    bsz, seq, _ = z.shape
    nc = seq // CHUNK
    expand = np.zeros((LANES, D_SSD), np.float32)
    for h in range(SSD_HEADS):
        expand[h, h * SSD_HEAD_DIM:(h + 1) * SSD_HEAD_DIM] = 1.0
    ns = SEQS_PER_STEP
    row = lambda b, c: (b, c, 0)
    prev = lambda b, c: (b, jnp.maximum(c - 1, 0), 0)
    fixed = lambda b, c: (0, 0)
    return pl.pallas_call(
        _mixer_kernel,
        grid=(bsz // ns, nc),
        in_specs=[pl.BlockSpec(memory_space=pltpu.SMEM),
                  pl.BlockSpec((ns, CHUNK, D_SSD), row),
                  pl.BlockSpec((ns, CHUNK, D_CONV), row),
                  pl.BlockSpec((ns, CHUNK, LANES), row),
                  pl.BlockSpec((ns, CHUNK, D_ATTN), row),
                  pl.BlockSpec((ns, CHUNK, 2 * D_KV), row),
                  pl.BlockSpec((ns, CHUNK, 2 * D_KV), prev),
                  pl.BlockSpec((CONV_WIDTH, D_CONV), fixed),
                  pl.BlockSpec((1, D_CONV), fixed),
                  pl.BlockSpec((1, LANES), fixed),
                  pl.BlockSpec((1, LANES), fixed),
                  pl.BlockSpec((1, D_SSD), fixed),
                  pl.BlockSpec((1, D_SSD), fixed),
                  pl.BlockSpec((LANES, D_SSD), fixed),
                  pl.BlockSpec((ATTN_Q_HEADS, CHUNK, 2 * CHUNK), lambda b, c: (0, 0, 0)),
                  pl.BlockSpec((1, D_ATTN), fixed)],
        out_specs=pl.BlockSpec((ns, CHUNK, D_SSD + D_ATTN), row),
        out_shape=jax.ShapeDtypeStruct((bsz, seq, D_SSD + D_ATTN), F32),
        scratch_shapes=[pltpu.VMEM((ns, SSD_GROUPS, D_STATE, SSD_HEADS_PER_GROUP * SSD_HEAD_DIM), F32),
                        pltpu.VMEM((ns, SUBLANES, D_CONV), F32)],
        compiler_params=_params(2),
        name="mixer",
    )(sinks, z, xbc, dt, q, kv, kv, conv_w, conv_b, dt_bias, a_log, d_skip_x, ssd_norm_g,
      jnp.asarray(expand, dtype=BF16), bias, attn_norm_g)


def _swa_block(sink_ref, q_ref, kv_ref, kvp_ref, bias_ref, g_ref):
    L, Dh = CHUNK, ATTN_HEAD_DIM
    q = q_ref[...] * (1.0 / math.sqrt(Dh))
    kv = kv_ref[...]
    kvp = kvp_ref[...]
    col = lax.broadcasted_iota(jnp.int32, (L, 2 * L), 1)
    first_col = jnp.where(pl.program_id(1) > 0, 0, L)
    outs = []
    for hk in range(ATTN_KV_HEADS):
        ks = slice(hk * Dh, (hk + 1) * Dh)
        vs = slice(D_KV + hk * Dh, D_KV + (hk + 1) * Dh)
        kc = jnp.concatenate([kvp[:, ks], kv[:, ks]], axis=0).astype(BF16)
        vc = jnp.concatenate([kvp[:, vs], kv[:, vs]], axis=0).astype(BF16)
        for g in range(ATTN_Q_PER_KV):
            h = hk * ATTN_Q_PER_KV + g
            qh = q[:, h * Dh:(h + 1) * Dh].astype(BF16)
            s = lax.dot_general(qh, kc, (((1,), (1,)), ((), ())), preferred_element_type=F32)
            s = jnp.where(col >= first_col, s + bias_ref[h], NEG)
            sink = sink_ref[h]
            m = jnp.maximum(jnp.max(s, axis=-1, keepdims=True), sink)
            p = jnp.exp(s - m)
            denom = jnp.sum(p, axis=-1, keepdims=True) + jnp.exp(sink - m)
            outs.append(_dot(p.astype(BF16), vc) / denom)
    return _rms(jnp.concatenate(outs, axis=1), g_ref[...])


def _route_kernel(x_ref, y_ref, wo_ref, g_ref, wr_ref, br_ref,
                  h_ref, u_ref, slab_ref, post_ref, tab_ref, run_ref):
    tm = x_ref.shape[0]

    @pl.when(pl.program_id(0) == 0)
    def _():
        run_ref[...] = jnp.zeros(run_ref.shape, F32)

    h = x_ref[...] + _dot(y_ref[...].astype(BF16), wo_ref[...])
    h_ref[...] = h
    u = _rms(h, g_ref[...])
    ub = u.astype(BF16)
    u_ref[...] = ub
    E = N_EXPERTS
    logits = lax.dot_general(wr_ref[...], ub, (((1,), (1,)), ((), ())), preferred_element_type=F32)
    logits = logits + jnp.concatenate([br_ref[...]] * (tm // LANES), axis=1)

    eidx = lax.broadcasted_iota(jnp.int32, (E, tm), 0).astype(F32)
    vals, idxs = [], []
    cur = logits
    for _ in range(TOP_K):
        m = jnp.max(cur, axis=0, keepdims=True)
        ix = jnp.min(jnp.where(cur == m, eidx, float(E)), axis=0, keepdims=True)
        vals.append(m)
        idxs.append(ix)
        cur = jnp.where(eidx == ix, NEG, cur)
    es = [jnp.exp(v - vals[0]) for v in vals]
    den = es[0] + es[1] + es[2] + es[3]

    onehot = jnp.zeros((E, tm), F32)
    for ix in idxs:
        onehot = onehot + (eidx == ix).astype(F32)
    ri = lax.broadcasted_iota(jnp.int32, (tm, tm), 0)
    ci = lax.broadcasted_iota(jnp.int32, (tm, tm), 1)
    before = _dot(onehot.astype(BF16), (ri < ci).astype(BF16))
    cnt = jnp.sum(onehot, axis=1, keepdims=True)
    cnt8 = jnp.floor((cnt + (SUBLANES - 1)) * (1.0 / SUBLANES)) * SUBLANES
    el = lax.broadcasted_iota(jnp.int32, (E, E), 0)
    ec = lax.broadcasted_iota(jnp.int32, (E, E), 1)
    seg_start = _dot_f32_by_mask((ec < el).astype(BF16), jnp.broadcast_to(cnt8, (E, LANES)),
                                 mask_on_left=True)[:, 0:1]
    where_to = before + seg_start

    rows = [jnp.sum(jnp.where(eidx == idxs[k], where_to, 0.0), axis=0, keepdims=True)
            for k in range(TOP_K)]
    rows += [es[k] / den for k in range(TOP_K)]
    stack = jnp.concatenate(rows, axis=0)
    post_ref[...] = stack.astype(jnp.int32)
    slab_ref[...] = jnp.concatenate([stack, jnp.zeros((LANES - 2 * TOP_K, tm), F32)], axis=0).T

    lane = lax.broadcasted_iota(jnp.int32, (E, LANES), 1)
    tab = jnp.where(lane == 0, cnt8, jnp.where(lane == 1, run_ref[...], jnp.where(lane == 2, seg_start, 0.0)))
    tab_ref[...] = tab.astype(jnp.int32)
    run_ref[...] = run_ref[...] + cnt8


def _route(x2, y_mix, wo, g, wr, br):
    t = x2.shape[0]
    tm = TM_ROUTE
    row = lambda i: (i, 0)
    fixed = lambda i: (0, 0)
    return pl.pallas_call(
        _route_kernel,
        grid=(t // tm,),
        in_specs=[pl.BlockSpec((tm, D_MODEL), row),
                  pl.BlockSpec((tm, D_SSD + D_ATTN), row),
                  pl.BlockSpec((D_SSD + D_ATTN, D_MODEL), fixed),
                  pl.BlockSpec((1, D_MODEL), fixed),
                  pl.BlockSpec((N_EXPERTS, D_MODEL), fixed),
                  pl.BlockSpec((N_EXPERTS, LANES), fixed)],
        out_specs=[pl.BlockSpec((tm, D_MODEL), row),
                   pl.BlockSpec((tm, D_MODEL), row),
                   pl.BlockSpec((tm, LANES), row),
                   pl.BlockSpec((SUBLANES, tm), lambda i: (0, i)),
                   pl.BlockSpec((N_EXPERTS, LANES), row)],
        out_shape=[jax.ShapeDtypeStruct((t, D_MODEL), F32),
                   jax.ShapeDtypeStruct((t, D_MODEL), BF16),
                   jax.ShapeDtypeStruct((t, LANES), F32),
                   jax.ShapeDtypeStruct((SUBLANES, t), jnp.int32),
                   jax.ShapeDtypeStruct((t // tm * N_EXPERTS, LANES), jnp.int32)],
        scratch_shapes=[pltpu.VMEM((N_EXPERTS, LANES), F32)],
        compiler_params=_params(1),
        name="route",
    )(x2, y_mix, wo, g, wr, br)


SEG_SIZE_BITS = (TM_ROUTE // SUBLANES).bit_length()


TILE_SIZE_BITS = (R_SORTED // SUBLANES).bit_length()


def _piece_tables(cnt8, segst, segdst):
    n = (cnt8 // SUBLANES)[:, None, :]
    b = jnp.arange(SEG_SIZE_BITS, dtype=jnp.int32)[None, :, None]
    has = (n >> b) & 1
    off = ((n >> (b + 1)) << (b + 1)) * SUBLANES
    rank = jnp.cumsum(has, axis=2) - has
    place = (has[..., None] == 1) & (rank[..., None] == jnp.arange(N_EXPERTS, dtype=jnp.int32))
    dense = lambda v: jnp.sum(jnp.where(place, v[..., None], 0), axis=2).reshape(-1)
    return (jnp.sum(has, axis=2).reshape(-1), dense(segst[:, None, :] + off),
            dense(segdst[:, None, :] + off), jnp.sum(cnt8, axis=1) // SUBLANES)


def _for_each_piece(tile, pcnt_ref, psrc_ref, pdst_ref, fn):
    for b in range(SEG_SIZE_BITS):
        base = (tile * SEG_SIZE_BITS + b) * N_EXPERTS

        def body(p, carry, b=b, base=base):
            fn(SUBLANES << b, psrc_ref[base + p], pdst_ref[base + p])
            return carry

        lax.fori_loop(0, pcnt_ref[tile * SEG_SIZE_BITS + b], body, 0)


def _wait_rows(n_tiles8, descriptor):
    for b in range(TILE_SIZE_BITS):
        @pl.when(((n_tiles8 >> b) & 1) == 1)
        def _(b=b):
            descriptor(SUBLANES << b).wait()


def _rows(ref, row, n):
    if not isinstance(row, int):
        row = pl.multiple_of(row, SUBLANES)
    return ref.at[pl.ds(row, n), :]


def _dispatch_kernel(pcnt_ref, psrc_ref, pdst_ref, ntot_ref, zflag_ref, u_ref, post_ref, xs_ref,
                     buf_ref, zero_ref, sems, zsem):
    tm = u_ref.shape[0]
    i = pl.program_id(0)
    slot = i % 2
    bm = zero_ref.shape[0]

    def fill(b, flag):
        return pltpu.make_async_copy(zero_ref, xs_ref.at[pl.ds(b * bm, bm), :], zsem.at[flag - 1])

    def for_flagged(flag, action):
        def body(b, carry):
            @pl.when(zflag_ref[b] == flag)
            def _():
                action(fill(b, flag))
            return carry

        lax.fori_loop(0, zflag_ref.shape[0], body, 0)

    @pl.when(i == 0)
    def _():
        zero_ref[...] = jnp.zeros(zero_ref.shape, F32)
        for_flagged(1, lambda c: c.start())
        for_flagged(2, lambda c: c.start())
        for_flagged(1, lambda c: c.wait())

    @pl.when(i == pl.num_programs(0) - 1)
    def _():
        for_flagged(2, lambda c: c.wait())

    pos = post_ref[...]
    j = lax.broadcasted_iota(jnp.int32, (R_SORTED, tm), 0)
    sel = jnp.zeros((R_SORTED, tm), F32)
    for k in range(TOP_K):
        sel = jnp.where(j == pos[k:k + 1, :], 1.0, sel)
    buf_ref[slot] = _dot(sel.astype(BF16), u_ref[...])

    def write(slot_, n, src_row, dst_row):
        return pltpu.make_async_copy(_rows(buf_ref.at[slot_], src_row, n),
                                     _rows(xs_ref, dst_row, n), sems.at[slot_])

    _for_each_piece(i, pcnt_ref, psrc_ref, pdst_ref, lambda n, s, d: write(slot, n, s, d).start())

    @pl.when(i > 0)
    def _():
        _wait_rows(ntot_ref[i - 1], lambda n: write(1 - slot, n, 0, 0))

    @pl.when(i == pl.num_programs(0) - 1)
    def _():
        _wait_rows(ntot_ref[i], lambda n: write(slot, n, 0, 0))


def _dispatch(pieces, zero_flag, u, post):
    t = u.shape[0]
    tm = TM_ROUTE
    n_rows = zero_flag.shape[0] * BM_EXPERT
    return pl.pallas_call(
        _dispatch_kernel,
        grid_spec=pltpu.PrefetchScalarGridSpec(
            num_scalar_prefetch=5,
            grid=(t // tm,),
            in_specs=[pl.BlockSpec((tm, D_MODEL), lambda i, *_: (i, 0)),
                      pl.BlockSpec((SUBLANES, tm), lambda i, *_: (0, i))],
            out_specs=pl.BlockSpec(memory_space=pl.ANY),
            scratch_shapes=[pltpu.VMEM((2, R_SORTED, D_MODEL), F32),
                            pltpu.VMEM((BM_EXPERT, D_MODEL), F32),
                            pltpu.SemaphoreType.DMA((2,)),
                            pltpu.SemaphoreType.DMA((2,))]),
        out_shape=jax.ShapeDtypeStruct((n_rows, D_MODEL), F32),
        compiler_params=_params(1),
        name="dispatch",
    )(*pieces, zero_flag, u, post)


def _expert_kernel(be_ref, nu_ref, first_ref, next_ref, rows_ref, xs_ref, w1_ref, b1_ref, w2_ref, b2_ref,
                   ys_ref, w1f_ref, w2f_ref, w1b_ref, w2b_ref, sems):
    i = pl.program_id(0)

    def fetch(e):
        return (pltpu.make_async_copy(w1_ref.at[e], w1f_ref, sems.at[0]),
                pltpu.make_async_copy(w2_ref.at[e], w2f_ref, sems.at[1]))

    @pl.when(i >= nu_ref[0])
    def _():
        ys_ref[...] = jnp.zeros(ys_ref.shape, F32)

    @pl.when(i < nu_ref[0])
    def _():
        e = be_ref[i]

        @pl.when(i == 0)
        def _():
            for c in fetch(e):
                c.start()

        @pl.when(first_ref[i] != 0)
        def _():
            for c in fetch(e):
                c.wait()
            w1b_ref[...] = w1f_ref[...].astype(BF16)
            w2b_ref[...] = w2f_ref[...].astype(BF16)

            @pl.when(next_ref[i] >= 0)
            def _():
                for c in fetch(next_ref[i]):
                    c.start()

        bm = xs_ref.shape[0]
        for m in range(EXPERT_ROW_STEP, bm + 1, EXPERT_ROW_STEP):
            @pl.when(rows_ref[i] == m)
            def _(m=m):
                hid = _dot(xs_ref[0:m, :].astype(BF16), w1b_ref[...]) + b1_ref[e]
                x_glu = jnp.minimum(hid[:, :D_EXPERT], SWIGLU_LIMIT)
                x_lin = jnp.clip(hid[:, D_EXPERT:], -SWIGLU_LIMIT, SWIGLU_LIMIT)
                act = x_glu / (1.0 + jnp.exp(-SWIGLU_ALPHA * x_glu)) * (x_lin + 1.0)
                ys_ref[0:m, :] = _dot(act.astype(BF16), w2b_ref[...]) + b2_ref[e]
                if m < bm:
                    ys_ref[m:, :] = jnp.zeros((bm - m, D_MODEL), F32)


def _experts(blk_expert, n_used, first_flag, next_expert, blk_rows, xs, w1, b1, w2, b2):
    n_rows = xs.shape[0]
    bm = BM_EXPERT
    nblk = n_rows // bm
    last = lambda i, nu: jnp.maximum(jnp.minimum(i, nu[0] - 1), 0)
    row = lambda i, be, nu, *_: (last(i, nu), 0)
    whole = lambda i, *_: (0, 0, 0)
    return pl.pallas_call(
        _expert_kernel,
        grid_spec=pltpu.PrefetchScalarGridSpec(
            num_scalar_prefetch=5,
            grid=(nblk,),
            in_specs=[pl.BlockSpec((bm, D_MODEL), row),
                      pl.BlockSpec(memory_space=pl.ANY),
                      pl.BlockSpec((N_EXPERTS, 1, 2 * D_EXPERT), whole),
                      pl.BlockSpec(memory_space=pl.ANY),
                      pl.BlockSpec((N_EXPERTS, 1, D_MODEL), whole)],
            out_specs=pl.BlockSpec((bm, D_MODEL), lambda i, *_: (i, 0)),
            scratch_shapes=[pltpu.VMEM((D_MODEL, 2 * D_EXPERT), F32),
                            pltpu.VMEM((D_EXPERT, D_MODEL), F32),
                            pltpu.VMEM((D_MODEL, 2 * D_EXPERT), BF16),
                            pltpu.VMEM((D_EXPERT, D_MODEL), BF16),
                            pltpu.SemaphoreType.DMA((2,))]),
        out_shape=jax.ShapeDtypeStruct((n_rows, D_MODEL), F32),
        compiler_params=_params(1),
        name="experts",
    )(blk_expert, n_used, first_flag, next_expert, blk_rows, xs, w1, b1, w2, b2)


def _combine_kernel(pcnt_ref, psrc_ref, pdst_ref, ntot_ref, ys_ref, h_ref, slab_ref, g_ref, o_ref,
                    buf_ref, sems):
    tm = h_ref.shape[0]
    i = pl.program_id(0)
    slot = i % 2

    def read(slot_, n, sorted_row, ys_row):
        return pltpu.make_async_copy(_rows(ys_ref, ys_row, n),
                                     _rows(buf_ref.at[slot_], sorted_row, n), sems.at[slot_])

    def gather(tile, slot_):
        _for_each_piece(tile, pcnt_ref, psrc_ref, pdst_ref, lambda n, s, d: read(slot_, n, s, d).start())

    @pl.when(i == 0)
    def _():
        buf_ref[...] = jnp.zeros(buf_ref.shape, F32)
        gather(0, 0)

    @pl.when(i + 1 < pl.num_programs(0))
    def _():
        gather(i + 1, 1 - slot)

    _wait_rows(ntot_ref[i], lambda n: read(slot, n, 0, 0))

    slab = slab_ref[...]
    lane = lax.broadcasted_iota(jnp.int32, (tm, R_SORTED), 1).astype(F32)
    wmat = jnp.zeros((tm, R_SORTED), F32)
    for k in range(TOP_K):
        wmat = jnp.where(lane == slab[:, k:k + 1], slab[:, TOP_K + k:TOP_K + k + 1], wmat)
    moe = _dot(wmat.astype(BF16), buf_ref[slot].astype(BF16))
    o_ref[...] = _rms(h_ref[...] + moe, g_ref[...])


def _combine(pieces, ys, h, slab, g):
    t = h.shape[0]
    tm = TM_ROUTE
    return pl.pallas_call(
        _combine_kernel,
        grid_spec=pltpu.PrefetchScalarGridSpec(
            num_scalar_prefetch=4,
            grid=(t // tm,),
            in_specs=[pl.BlockSpec(memory_space=pl.ANY),
                      pl.BlockSpec((tm, D_MODEL), lambda i, *_: (i, 0)),
                      pl.BlockSpec((tm, LANES), lambda i, *_: (i, 0)),
                      pl.BlockSpec((1, D_MODEL), lambda i, *_: (0, 0))],
            out_specs=pl.BlockSpec((tm, D_MODEL), lambda i, *_: (i, 0)),
            scratch_shapes=[pltpu.VMEM((2, R_SORTED, D_MODEL), F32),
                            pltpu.SemaphoreType.DMA((2,))]),
        out_shape=jax.ShapeDtypeStruct((t, D_MODEL), F32),
        compiler_params=_params(1),
        name="combine",
    )(*pieces, ys, h, slab, g)


def _pad_lanes(v, fill=0.0):
    return jnp.pad(v.reshape(1, -1), ((0, 0), (0, LANES - v.shape[-1])), constant_values=fill)


def kernel(x, norm_mix_g, w_in, conv_w, conv_b, dt_bias, a_log, d_skip, ssd_norm_g, attn_sinks,
           rel_bias, attn_norm_g, w_out, norm_ffn_g, w_router, b_router, w1, b1, w2, b2,
           norm_final_g):
    bsz, seq, d = x.shape
    t = bsz * seq
    nc = seq // CHUNK
    depth = w_in.shape[0]
    o2 = D_SSD + D_CONV
    o3 = o2 + SSD_HEADS
    bias = _bias_table(rel_bias)

    assert depth == 1, "the final norm is fused into the combine kernel: single layer only"
    h = x.reshape(t, d)
    for layer in range(depth):
        w = w_in[layer]
        w_packed = jnp.concatenate(
            [w[:, :o2], w[:, o3:], w[:, o2:o3], jnp.zeros((d, LANES - SSD_HEADS), w.dtype)],
            axis=1).astype(BF16)
        z, xbc, q, kv, dt = [a.reshape(bsz, seq, -1)
                             for a in _inproj(h, norm_mix_g[layer].reshape(1, d), w_packed)]
        y_mix = _mixer(z, xbc, dt, q, kv, conv_w[layer], conv_b[layer].reshape(1, -1),
                       _pad_lanes(dt_bias[layer]), _pad_lanes(a_log[layer]),
                       jnp.repeat(d_skip[layer], SSD_HEAD_DIM).reshape(1, -1),
                       ssd_norm_g[layer].reshape(1, -1), attn_sinks[layer], bias,
                       attn_norm_g[layer].reshape(1, -1)).reshape(t, -1)

        wr = w_router[layer].T.astype(BF16)
        br = jnp.broadcast_to(b_router[layer][:, None], (N_EXPERTS, LANES))
        h_mid, u, slab, post, tab = _route(
            h, y_mix, w_out[layer].astype(BF16), norm_ffn_g[layer].reshape(1, d), wr, br)

        bm = BM_EXPERT
        ntiles = t // TM_ROUTE
        nblk = (t * TOP_K + ntiles * N_EXPERTS * (SUBLANES - 1)) // bm + N_EXPERTS
        tab = tab.reshape(ntiles, N_EXPERTS, LANES)
        cnt8, off8, segst = tab[:, :, 0], tab[:, :, 1], tab[:, :, 2]
        total8 = off8[-1] + cnt8[-1]
        padded = ((total8 + bm - 1) // bm) * bm
        pend = jnp.cumsum(padded)
        segdst = (pend - padded)[None, :] + off8
        n_used = pend[-1:] // bm
        eids = jnp.arange(N_EXPERTS, dtype=jnp.int32)
        blk = jnp.arange(nblk, dtype=jnp.int32)
        blk_expert = jnp.minimum(
            jnp.sum((blk[:, None] * bm >= pend[None, :]).astype(jnp.int32), axis=1), N_EXPERTS - 1)
        after = jnp.concatenate([blk_expert[1:], blk_expert[-1:]])
        before = jnp.concatenate([blk_expert[:1], blk_expert[:-1]])
        zero_flag = jnp.where(blk >= n_used[0], 2,
                              ((blk == n_used[0] - 1) | (after != blk_expert)).astype(jnp.int32))
        e_end = jnp.sum(jnp.where(blk_expert[:, None] == eids[None, :],
                                  ((pend - padded) + total8)[None, :], 0), axis=1)
        valid = jnp.clip(e_end - blk * bm, 0, bm)
        blk_rows = ((valid + EXPERT_ROW_STEP - 1) // EXPERT_ROW_STEP) * EXPERT_ROW_STEP
        first_flag = ((blk == 0) | (before != blk_expert)).astype(jnp.int32)
        cand = jnp.where((eids[None, :] > eids[:, None]) & (padded[None, :] > 0), eids[None, :], N_EXPERTS)
        next_nonempty = jnp.min(cand, axis=1)
        next_nonempty = jnp.where(next_nonempty == N_EXPERTS, -1, next_nonempty)
        next_expert = jnp.sum(jnp.where(blk_expert[:, None] == eids[None, :], next_nonempty[None, :], 0), axis=1)
        pieces = _piece_tables(cnt8, segst, segdst)

        xs = _dispatch(pieces, zero_flag, u, post)
        ys = _experts(blk_expert, n_used, first_flag, next_expert, blk_rows, xs, w1[layer],
                      b1[layer].reshape(N_EXPERTS, 1, -1), w2[layer], b2[layer].reshape(N_EXPERTS, 1, -1))
        h = _combine(pieces, ys, h_mid, slab, norm_final_g.reshape(1, d))
    return h.reshape(bsz, seq, d)
```

```python
import functools
import math

import numpy as np
import jax
import jax.numpy as jnp
from jax import lax
from jax.experimental import pallas as pl
from jax.experimental.pallas import tpu as pltpu

F32 = jnp.float32
BF16 = jnp.bfloat16

D_MODEL = 1024
SSD_HEADS = 8
SSD_HEAD_DIM = 64
D_SSD = SSD_HEADS * SSD_HEAD_DIM
SSD_GROUPS = 2
SSD_HEADS_PER_GROUP = SSD_HEADS // SSD_GROUPS
D_STATE = 128
CONV_WIDTH = 4
CHUNK = 128
D_CONV = D_SSD + 2 * SSD_GROUPS * D_STATE
ATTN_Q_HEADS = 8
ATTN_KV_HEADS = 2
ATTN_Q_PER_KV = ATTN_Q_HEADS // ATTN_KV_HEADS
ATTN_HEAD_DIM = 64
D_ATTN = ATTN_Q_HEADS * ATTN_HEAD_DIM
D_KV = ATTN_KV_HEADS * ATTN_HEAD_DIM
WINDOW = 128
REL_BUCKETS = 32
REL_MAX_DIST = 128
N_EXPERTS = 32
TOP_K = 4
D_EXPERT = D_MODEL
SWIGLU_LIMIT = 7.0
SWIGLU_ALPHA = 1.702
RMS_EPS = 1e-5

LANES = 128
SUBLANES = 8
NEG = -1e30
VMEM_LIMIT = 56 * 1024 * 1024

TM_PROJ = 512
TM_ROUTE = 512
BM_EXPERT = 1024
EXPERT_ROW_STEP = 128
SEQS_PER_STEP = 2
R_SORTED = TM_ROUTE * TOP_K + N_EXPERTS * SUBLANES

D_PROJ_PACKED = D_SSD + D_CONV + D_ATTN + 2 * D_KV + LANES


def _params(n_axes):
    return pltpu.CompilerParams(dimension_semantics=("arbitrary",) * n_axes,
                                vmem_limit_bytes=VMEM_LIMIT)


def _rms(x, g):
    return x * lax.rsqrt(jnp.mean(x * x, axis=-1, keepdims=True) + RMS_EPS) * g


def _silu(x):
    return x / (1.0 + jnp.exp(-x))


def _dot(a, b, **kw):
    return jnp.dot(a, b, preferred_element_type=F32, **kw)


def _dot_f32_by_mask(mask, x, mask_on_left=False):
    hi = x.astype(BF16)
    r1 = x - hi.astype(F32)
    mid = r1.astype(BF16)
    lo = (r1 - mid.astype(F32)).astype(BF16)
    out = None
    for piece in (hi, mid, lo):
        term = _dot(mask, piece) if mask_on_left else _dot(piece, mask)
        out = term if out is None else out + term
    return out


def _t5_bucket_table():
    dist = CHUNK + np.arange(CHUNK)[:, None] - np.arange(2 * CHUNK)[None, :]
    in_window = (dist >= 0) & (dist < WINDOW)
    d = np.clip(dist, 0, REL_MAX_DIST)
    max_exact = REL_BUCKETS // 2
    large = max_exact + (np.log(np.maximum(d, 1).astype(np.float32) / max_exact)
                         / math.log(REL_MAX_DIST / max_exact)
                         * (REL_BUCKETS - max_exact)).astype(np.int32)
    large = np.minimum(large, REL_BUCKETS - 1)
    bucket = np.where(d < max_exact, d, large)
    return np.where(in_window, bucket, -1).astype(np.int32)


def _bias_kernel(rb_ref, bucket_ref, o_ref):
    bucket = bucket_ref[...]
    for h in range(ATTN_Q_HEADS):
        acc = jnp.full(bucket.shape, NEG, F32)
        for b in range(REL_BUCKETS):
            acc = jnp.where(bucket == b, rb_ref[b, h], acc)
        o_ref[h] = acc


def _bias_table(rel_bias):
    bucket = jnp.asarray(_t5_bucket_table())
    return pl.pallas_call(
        _bias_kernel,
        out_shape=jax.ShapeDtypeStruct((ATTN_Q_HEADS, CHUNK, 2 * CHUNK), F32),
        in_specs=[pl.BlockSpec(memory_space=pltpu.SMEM),
                  pl.BlockSpec(memory_space=pltpu.VMEM)],
        out_specs=pl.BlockSpec(memory_space=pltpu.VMEM),
        name="bias_table",
    )(rel_bias, bucket)


def _inproj_kernel(x_ref, g_ref, w_ref, z_ref, xbc_ref, q_ref, kv_ref, dt_ref):
    u = _rms(x_ref[...], g_ref[...]).astype(BF16)
    proj = _dot(u, w_ref[...])
    o = 0
    for ref in (z_ref, xbc_ref, q_ref, kv_ref, dt_ref):
        w = ref.shape[1]
        ref[...] = proj[:, o:o + w]
        o += w


def _inproj(x2, g, w_packed):
    t = x2.shape[0]
    widths = (D_SSD, D_CONV, D_ATTN, 2 * D_KV, LANES)
    return pl.pallas_call(
        _inproj_kernel,
        grid=(t // TM_PROJ,),
        in_specs=[pl.BlockSpec((TM_PROJ, D_MODEL), lambda i: (i, 0)),
                  pl.BlockSpec((1, D_MODEL), lambda i: (0, 0)),
                  pl.BlockSpec((D_MODEL, D_PROJ_PACKED), lambda i: (0, 0))],
        out_specs=[pl.BlockSpec((TM_PROJ, w), lambda i: (i, 0)) for w in widths],
        out_shape=[jax.ShapeDtypeStruct((t, w), F32) for w in widths],
        compiler_params=_params(1),
        name="inproj",
    )(x2, g, w_packed)


def _mixer_kernel(sink_ref, z_ref, xbc_ref, dt_ref, q_ref, kv_ref, kvp_ref, cw_ref, cb_ref, dtb_ref,
                  alog_ref, dskip_ref, gs_ref, expand_ref, bias_ref, ga_ref, o_ref, state_ref, xpad_ref):
    @pl.when(pl.program_id(1) == 0)
    def _():
        state_ref[...] = jnp.zeros(state_ref.shape, F32)
        xpad_ref[...] = jnp.zeros(xpad_ref.shape, F32)

    for s in range(z_ref.shape[0]):
        y_ssd = _ssd_chunk(z_ref.at[s], xbc_ref.at[s], dt_ref.at[s], cw_ref, cb_ref, dtb_ref, alog_ref,
                           dskip_ref, gs_ref, expand_ref, state_ref.at[s], xpad_ref.at[s])
        y_attn = _swa_block(sink_ref, q_ref.at[s], kv_ref.at[s], kvp_ref.at[s], bias_ref, ga_ref)
        o_ref[s] = jnp.concatenate([y_ssd, y_attn], axis=1)


def _ssd_chunk(z_ref, xbc_ref, dt_ref, cw_ref, cb_ref, dtb_ref, alog_ref, dskip_ref, g_ref,
               expand_ref, state_ref, xpad_ref):
    L = CHUNK
    G, R, P, N = SSD_GROUPS, SSD_HEADS_PER_GROUP, SSD_HEAD_DIM, D_STATE
    GW = R * P

    x_cur = xbc_ref[...]
    xpad = jnp.concatenate([xpad_ref[...], x_cur], axis=0)
    xpad_ref[...] = x_cur[L - SUBLANES:, :]
    acc = cb_ref[...] + cw_ref[CONV_WIDTH - 1:CONV_WIDTH, :] * x_cur
    for d in range(1, CONV_WIDTH):
        k = CONV_WIDTH - 1 - d
        acc = acc + cw_ref[k:k + 1, :] * pltpu.roll(xpad, d, axis=0)[SUBLANES:, :]
    xbc = _silu(acc)
    xs = xbc[:, :D_SSD]
    bm = xbc[:, D_SSD:D_SSD + G * N]
    cm = xbc[:, D_SSD + G * N:]

    dtr = dt_ref[...] + dtb_ref[...]
    dt = jnp.maximum(dtr, 0.0) + jnp.log(1.0 + jnp.exp(-jnp.abs(dtr)))
    a_dt = dt * (-jnp.exp(alog_ref[...]))
    ri = lax.broadcasted_iota(jnp.int32, (L, L), 0)
    ci = lax.broadcasted_iota(jnp.int32, (L, L), 1)
    causal = ci <= ri
    a_cum = _dot_f32_by_mask(causal.astype(BF16), a_dt, mask_on_left=True)
    a_cum_t = _dot_f32_by_mask((ri <= ci).astype(BF16), a_dt.T)
    a_last = a_cum[L - 1:L, :]
    stack = jnp.concatenate(
        [dt, jnp.exp(a_cum), jnp.exp(a_last - a_cum),
         jnp.broadcast_to(jnp.exp(a_last), (SUBLANES, LANES))], axis=0)
    ex = _dot_f32_by_mask(expand_ref[...], stack)
    dt_x, ea_x, dte_x, cd_x = ex[0:L], ex[L:2 * L], ex[2 * L:3 * L], ex[3 * L:3 * L + 1]
    xdt = xs * dt_x

    ys = []
    for g in range(G):
        bm_g = bm[:, g * N:(g + 1) * N]
        cm_g = cm[:, g * N:(g + 1) * N].astype(BF16)
        cb = lax.dot_general(cm_g, bm_g.astype(BF16), (((1,), (1,)), ((), ())),
                             preferred_element_type=F32)
        xdt_g = xdt[:, g * GW:(g + 1) * GW]
        yd = []
        for r in range(R):
            h = g * R + r
            seg = a_cum[:, h:h + 1] - a_cum_t[h:h + 1, :]
            dec = jnp.exp(jnp.where(causal, seg, NEG))
            yd.append(_dot((cb * dec).astype(BF16), xdt_g[:, r * P:(r + 1) * P].astype(BF16)))
        y_diag = jnp.concatenate(yd, axis=1)
        st = state_ref[g]
        y_off = _dot(cm_g, st.astype(BF16)) * ea_x[:, g * GW:(g + 1) * GW]
        new = _dot(bm_g.T.astype(BF16), (xdt_g * dte_x[:, g * GW:(g + 1) * GW]).astype(BF16))
        state_ref[g] = st * cd_x[:, g * GW:(g + 1) * GW] + new
        ys.append(y_diag + y_off + xs[:, g * GW:(g + 1) * GW] * dskip_ref[:, g * GW:(g + 1) * GW])
    y = jnp.concatenate(ys, axis=1)
    return _rms(y * _silu(z_ref[...]), g_ref[...])


def _mixer(z, xbc, dt, q, kv, conv_w, conv_b, dt_bias, a_log, d_skip_x, ssd_norm_g, sinks, bias,
           attn_norm_g):
    bsz, seq, _ = z.shape
    nc = seq // CHUNK
    expand = np.zeros((LANES, D_SSD), np.float32)
    for h in range(SSD_HEADS):
        expand[h, h * SSD_HEAD_DIM:(h + 1) * SSD_HEAD_DIM] = 1.0
    ns = SEQS_PER_STEP
    row = lambda b, c: (b, c, 0)
    prev = lambda b, c: (b, jnp.maximum(c - 1, 0), 0)
    fixed = lambda b, c: (0, 0)
    return pl.pallas_call(
        _mixer_kernel,
        grid=(bsz // ns, nc),
        in_specs=[pl.BlockSpec(memory_space=pltpu.SMEM),
                  pl.BlockSpec((ns, CHUNK, D_SSD), row),
                  pl.BlockSpec((ns, CHUNK, D_CONV), row),
                  pl.BlockSpec((ns, CHUNK, LANES), row),
                  pl.BlockSpec((ns, CHUNK, D_ATTN), row),
                  pl.BlockSpec((ns, CHUNK, 2 * D_KV), row),
                  pl.BlockSpec((ns, CHUNK, 2 * D_KV), prev),
                  pl.BlockSpec((CONV_WIDTH, D_CONV), fixed),
                  pl.BlockSpec((1, D_CONV), fixed),
                  pl.BlockSpec((1, LANES), fixed),
                  pl.BlockSpec((1, LANES), fixed),
                  pl.BlockSpec((1, D_SSD), fixed),
                  pl.BlockSpec((1, D_SSD), fixed),
                  pl.BlockSpec((LANES, D_SSD), fixed),
                  pl.BlockSpec((ATTN_Q_HEADS, CHUNK, 2 * CHUNK), lambda b, c: (0, 0, 0)),
                  pl.BlockSpec((1, D_ATTN), fixed)],
        out_specs=pl.BlockSpec((ns, CHUNK, D_SSD + D_ATTN), row),
        out_shape=jax.ShapeDtypeStruct((bsz, seq, D_SSD + D_ATTN), F32),
        scratch_shapes=[pltpu.VMEM((ns, SSD_GROUPS, D_STATE, SSD_HEADS_PER_GROUP * SSD_HEAD_DIM), F32),
                        pltpu.VMEM((ns, SUBLANES, D_CONV), F32)],
        compiler_params=_params(2),
        name="mixer",
    )(sinks, z, xbc, dt, q, kv, kv, conv_w, conv_b, dt_bias, a_log, d_skip_x, ssd_norm_g,
      jnp.asarray(expand, dtype=BF16), bias, attn_norm_g)


def _swa_block(sink_ref, q_ref, kv_ref, kvp_ref, bias_ref, g_ref):
    L, Dh = CHUNK, ATTN_HEAD_DIM
    q = q_ref[...] * (1.0 / math.sqrt(Dh))
    kv = kv_ref[...]
    kvp = kvp_ref[...]
    col = lax.broadcasted_iota(jnp.int32, (L, 2 * L), 1)
    first_col = jnp.where(pl.program_id(1) > 0, 0, L)
    outs = []
    for hk in range(ATTN_KV_HEADS):
        ks = slice(hk * Dh, (hk + 1) * Dh)
        vs = slice(D_KV + hk * Dh, D_KV + (hk + 1) * Dh)
        kc = jnp.concatenate([kvp[:, ks], kv[:, ks]], axis=0).astype(BF16)
        vc = jnp.concatenate([kvp[:, vs], kv[:, vs]], axis=0).astype(BF16)
        for g in range(ATTN_Q_PER_KV):
            h = hk * ATTN_Q_PER_KV + g
            qh = q[:, h * Dh:(h + 1) * Dh].astype(BF16)
            s = lax.dot_general(qh, kc, (((1,), (1,)), ((), ())), preferred_element_type=F32)
            s = jnp.where(col >= first_col, s + bias_ref[h], NEG)
            sink = sink_ref[h]
            m = jnp.maximum(jnp.max(s, axis=-1, keepdims=True), sink)
            p = jnp.exp(s - m)
            denom = jnp.sum(p, axis=-1, keepdims=True) + jnp.exp(sink - m)
            outs.append(_dot(p.astype(BF16), vc) / denom)
    return _rms(jnp.concatenate(outs, axis=1), g_ref[...])


def _route_kernel(x_ref, y_ref, wo_ref, g_ref, wr_ref, br_ref,
                  h_ref, u_ref, slab_ref, post_ref, tab_ref, run_ref):
    tm = x_ref.shape[0]

    @pl.when(pl.program_id(0) == 0)
    def _():
        run_ref[...] = jnp.zeros(run_ref.shape, F32)

    h = x_ref[...] + _dot(y_ref[...].astype(BF16), wo_ref[...])
    h_ref[...] = h
    u = _rms(h, g_ref[...])
    ub = u.astype(BF16)
    u_ref[...] = ub
    E = N_EXPERTS
    logits = lax.dot_general(wr_ref[...], ub, (((1,), (1,)), ((), ())), preferred_element_type=F32)
    logits = logits + jnp.concatenate([br_ref[...]] * (tm // LANES), axis=1)

    eidx = lax.broadcasted_iota(jnp.int32, (E, tm), 0).astype(F32)
    vals, idxs = [], []
    cur = logits
    for _ in range(TOP_K):
        m = jnp.max(cur, axis=0, keepdims=True)
        ix = jnp.min(jnp.where(cur == m, eidx, float(E)), axis=0, keepdims=True)
        vals.append(m)
        idxs.append(ix)
        cur = jnp.where(eidx == ix, NEG, cur)
    es = [jnp.exp(v - vals[0]) for v in vals]
    den = es[0] + es[1] + es[2] + es[3]

    onehot = jnp.zeros((E, tm), F32)
    for ix in idxs:
        onehot = onehot + (eidx == ix).astype(F32)
    ri = lax.broadcasted_iota(jnp.int32, (tm, tm), 0)
    ci = lax.broadcasted_iota(jnp.int32, (tm, tm), 1)
    before = _dot(onehot.astype(BF16), (ri < ci).astype(BF16))
    cnt = jnp.sum(onehot, axis=1, keepdims=True)
    cnt8 = jnp.floor((cnt + (SUBLANES - 1)) * (1.0 / SUBLANES)) * SUBLANES
    el = lax.broadcasted_iota(jnp.int32, (E, E), 0)
    ec = lax.broadcasted_iota(jnp.int32, (E, E), 1)
    seg_start = _dot_f32_by_mask((ec < el).astype(BF16), jnp.broadcast_to(cnt8, (E, LANES)),
                                 mask_on_left=True)[:, 0:1]
    where_to = before + seg_start

    rows = [jnp.sum(jnp.where(eidx == idxs[k], where_to, 0.0), axis=0, keepdims=True)
            for k in range(TOP_K)]
    rows += [es[k] / den for k in range(TOP_K)]
    stack = jnp.concatenate(rows, axis=0)
    post_ref[...] = stack.astype(jnp.int32)
    slab_ref[...] = jnp.concatenate([stack, jnp.zeros((LANES - 2 * TOP_K, tm), F32)], axis=0).T

    lane = lax.broadcasted_iota(jnp.int32, (E, LANES), 1)
    tab = jnp.where(lane == 0, cnt8, jnp.where(lane == 1, run_ref[...], jnp.where(lane == 2, seg_start, 0.0)))
    tab_ref[...] = tab.astype(jnp.int32)
    run_ref[...] = run_ref[...] + cnt8


def _route(x2, y_mix, wo, g, wr, br):
    t = x2.shape[0]
    tm = TM_ROUTE
    row = lambda i: (i, 0)
    fixed = lambda i: (0, 0)
    return pl.pallas_call(
        _route_kernel,
        grid=(t // tm,),
        in_specs=[pl.BlockSpec((tm, D_MODEL), row),
                  pl.BlockSpec((tm, D_SSD + D_ATTN), row),
                  pl.BlockSpec((D_SSD + D_ATTN, D_MODEL), fixed),
                  pl.BlockSpec((1, D_MODEL), fixed),
                  pl.BlockSpec((N_EXPERTS, D_MODEL), fixed),
                  pl.BlockSpec((N_EXPERTS, LANES), fixed)],
        out_specs=[pl.BlockSpec((tm, D_MODEL), row),
                   pl.BlockSpec((tm, D_MODEL), row),
                   pl.BlockSpec((tm, LANES), row),
                   pl.BlockSpec((SUBLANES, tm), lambda i: (0, i)),
                   pl.BlockSpec((N_EXPERTS, LANES), row)],
        out_shape=[jax.ShapeDtypeStruct((t, D_MODEL), F32),
                   jax.ShapeDtypeStruct((t, D_MODEL), BF16),
                   jax.ShapeDtypeStruct((t, LANES), F32),
                   jax.ShapeDtypeStruct((SUBLANES, t), jnp.int32),
                   jax.ShapeDtypeStruct((t // tm * N_EXPERTS, LANES), jnp.int32)],
        scratch_shapes=[pltpu.VMEM((N_EXPERTS, LANES), F32)],
        compiler_params=_params(1),
        name="route",
    )(x2, y_mix, wo, g, wr, br)


SEG_SIZE_BITS = (TM_ROUTE // SUBLANES).bit_length()


TILE_SIZE_BITS = (R_SORTED // SUBLANES).bit_length()


def _piece_tables(cnt8, segst, segdst):
    n = (cnt8 // SUBLANES)[:, None, :]
    b = jnp.arange(SEG_SIZE_BITS, dtype=jnp.int32)[None, :, None]
    has = (n >> b) & 1
    off = ((n >> (b + 1)) << (b + 1)) * SUBLANES
    rank = jnp.cumsum(has, axis=2) - has
    place = (has[..., None] == 1) & (rank[..., None] == jnp.arange(N_EXPERTS, dtype=jnp.int32))
    dense = lambda v: jnp.sum(jnp.where(place, v[..., None], 0), axis=2).reshape(-1)
    return (jnp.sum(has, axis=2).reshape(-1), dense(segst[:, None, :] + off),
            dense(segdst[:, None, :] + off), jnp.sum(cnt8, axis=1) // SUBLANES)


def _for_each_piece(tile, pcnt_ref, psrc_ref, pdst_ref, fn):
    for b in range(SEG_SIZE_BITS):
        base = (tile * SEG_SIZE_BITS + b) * N_EXPERTS

        def body(p, carry, b=b, base=base):
            fn(SUBLANES << b, psrc_ref[base + p], pdst_ref[base + p])
            return carry

        lax.fori_loop(0, pcnt_ref[tile * SEG_SIZE_BITS + b], body, 0)


def _wait_rows(n_tiles8, descriptor):
    for b in range(TILE_SIZE_BITS):
        @pl.when(((n_tiles8 >> b) & 1) == 1)
        def _(b=b):
            descriptor(SUBLANES << b).wait()


def _rows(ref, row, n):
    if not isinstance(row, int):
        row = pl.multiple_of(row, SUBLANES)
    return ref.at[pl.ds(row, n), :]


def _dispatch_kernel(pcnt_ref, psrc_ref, pdst_ref, ntot_ref, zflag_ref, u_ref, post_ref, xs_ref,
                     buf_ref, zero_ref, sems, zsem):
    tm = u_ref.shape[0]
    i = pl.program_id(0)
    slot = i % 2
    bm = zero_ref.shape[0]

    def fill(b, flag):
        return pltpu.make_async_copy(zero_ref, xs_ref.at[pl.ds(b * bm, bm), :], zsem.at[flag - 1])

    def for_flagged(flag, action):
        def body(b, carry):
            @pl.when(zflag_ref[b] == flag)
            def _():
                action(fill(b, flag))
            return carry

        lax.fori_loop(0, zflag_ref.shape[0], body, 0)

    @pl.when(i == 0)
    def _():
        zero_ref[...] = jnp.zeros(zero_ref.shape, F32)
        for_flagged(1, lambda c: c.start())
        for_flagged(2, lambda c: c.start())
        for_flagged(1, lambda c: c.wait())

    @pl.when(i == pl.num_programs(0) - 1)
    def _():
        for_flagged(2, lambda c: c.wait())

    pos = post_ref[...]
    j = lax.broadcasted_iota(jnp.int32, (R_SORTED, tm), 0)
    sel = jnp.zeros((R_SORTED, tm), F32)
    for k in range(TOP_K):
        sel = jnp.where(j == pos[k:k + 1, :], 1.0, sel)
    buf_ref[slot] = _dot(sel.astype(BF16), u_ref[...])

    def write(slot_, n, src_row, dst_row):
        return pltpu.make_async_copy(_rows(buf_ref.at[slot_], src_row, n),
                                     _rows(xs_ref, dst_row, n), sems.at[slot_])

    _for_each_piece(i, pcnt_ref, psrc_ref, pdst_ref, lambda n, s, d: write(slot, n, s, d).start())

    @pl.when(i > 0)
    def _():
        _wait_rows(ntot_ref[i - 1], lambda n: write(1 - slot, n, 0, 0))

    @pl.when(i == pl.num_programs(0) - 1)
    def _():
        _wait_rows(ntot_ref[i], lambda n: write(slot, n, 0, 0))


def _dispatch(pieces, zero_flag, u, post):
    t = u.shape[0]
    tm = TM_ROUTE
    n_rows = zero_flag.shape[0] * BM_EXPERT
    return pl.pallas_call(
        _dispatch_kernel,
        grid_spec=pltpu.PrefetchScalarGridSpec(
            num_scalar_prefetch=5,
            grid=(t // tm,),
            in_specs=[pl.BlockSpec((tm, D_MODEL), lambda i, *_: (i, 0)),
                      pl.BlockSpec((SUBLANES, tm), lambda i, *_: (0, i))],
            out_specs=pl.BlockSpec(memory_space=pl.ANY),
            scratch_shapes=[pltpu.VMEM((2, R_SORTED, D_MODEL), F32),
                            pltpu.VMEM((BM_EXPERT, D_MODEL), F32),
                            pltpu.SemaphoreType.DMA((2,)),
                            pltpu.SemaphoreType.DMA((2,))]),
        out_shape=jax.ShapeDtypeStruct((n_rows, D_MODEL), F32),
        compiler_params=_params(1),
        name="dispatch",
    )(*pieces, zero_flag, u, post)


def _expert_kernel(be_ref, nu_ref, first_ref, next_ref, rows_ref, xs_ref, w1_ref, b1_ref, w2_ref, b2_ref,
                   ys_ref, w1f_ref, w2f_ref, w1b_ref, w2b_ref, sems):
    i = pl.program_id(0)

    def fetch(e):
        return (pltpu.make_async_copy(w1_ref.at[e], w1f_ref, sems.at[0]),
                pltpu.make_async_copy(w2_ref.at[e], w2f_ref, sems.at[1]))

    @pl.when(i >= nu_ref[0])
    def _():
        ys_ref[...] = jnp.zeros(ys_ref.shape, F32)

    @pl.when(i < nu_ref[0])
    def _():
        e = be_ref[i]

        @pl.when(i == 0)
        def _():
            for c in fetch(e):
                c.start()

        @pl.when(first_ref[i] != 0)
        def _():
            for c in fetch(e):
                c.wait()
            w1b_ref[...] = w1f_ref[...].astype(BF16)
            w2b_ref[...] = w2f_ref[...].astype(BF16)

            @pl.when(next_ref[i] >= 0)
            def _():
                for c in fetch(next_ref[i]):
                    c.start()

        bm = xs_ref.shape[0]
        for m in range(EXPERT_ROW_STEP, bm + 1, EXPERT_ROW_STEP):
            @pl.when(rows_ref[i] == m)
            def _(m=m):
                hid = _dot(xs_ref[0:m, :].astype(BF16), w1b_ref[...]) + b1_ref[e]
                x_glu = jnp.minimum(hid[:, :D_EXPERT], SWIGLU_LIMIT)
                x_lin = jnp.clip(hid[:, D_EXPERT:], -SWIGLU_LIMIT, SWIGLU_LIMIT)
                act = x_glu / (1.0 + jnp.exp(-SWIGLU_ALPHA * x_glu)) * (x_lin + 1.0)
                ys_ref[0:m, :] = _dot(act.astype(BF16), w2b_ref[...]) + b2_ref[e]
                if m < bm:
                    ys_ref[m:, :] = jnp.zeros((bm - m, D_MODEL), F32)


def _experts(blk_expert, n_used, first_flag, next_expert, blk_rows, xs, w1, b1, w2, b2):
    n_rows = xs.shape[0]
    bm = BM_EXPERT
    nblk = n_rows // bm
    last = lambda i, nu: jnp.maximum(jnp.minimum(i, nu[0] - 1), 0)
    row = lambda i, be, nu, *_: (last(i, nu), 0)
    whole = lambda i, *_: (0, 0, 0)
    return pl.pallas_call(
        _expert_kernel,
        grid_spec=pltpu.PrefetchScalarGridSpec(
            num_scalar_prefetch=5,
            grid=(nblk,),
            in_specs=[pl.BlockSpec((bm, D_MODEL), row),
                      pl.BlockSpec(memory_space=pl.ANY),
                      pl.BlockSpec((N_EXPERTS, 1, 2 * D_EXPERT), whole),
                      pl.BlockSpec(memory_space=pl.ANY),
                      pl.BlockSpec((N_EXPERTS, 1, D_MODEL), whole)],
            out_specs=pl.BlockSpec((bm, D_MODEL), lambda i, *_: (i, 0)),
            scratch_shapes=[pltpu.VMEM((D_MODEL, 2 * D_EXPERT), F32),
                            pltpu.VMEM((D_EXPERT, D_MODEL), F32),
                            pltpu.VMEM((D_MODEL, 2 * D_EXPERT), BF16),
                            pltpu.VMEM((D_EXPERT, D_MODEL), BF16),
                            pltpu.SemaphoreType.DMA((2,))]),
        out_shape=jax.ShapeDtypeStruct((n_rows, D_MODEL), F32),
        compiler_params=_params(1),
        name="experts",
    )(blk_expert, n_used, first_flag, next_expert, blk_rows, xs, w1, b1, w2, b2)


def _combine_kernel(pcnt_ref, psrc_ref, pdst_ref, ntot_ref, ys_ref, h_ref, slab_ref, g_ref, o_ref,
                    buf_ref, sems):
    tm = h_ref.shape[0]
    i = pl.program_id(0)
    slot = i % 2

    def read(slot_, n, sorted_row, ys_row):
        return pltpu.make_async_copy(_rows(ys_ref, ys_row, n),
                                     _rows(buf_ref.at[slot_], sorted_row, n), sems.at[slot_])

    def gather(tile, slot_):
        _for_each_piece(tile, pcnt_ref, psrc_ref, pdst_ref, lambda n, s, d: read(slot_, n, s, d).start())

    @pl.when(i == 0)
    def _():
        buf_ref[...] = jnp.zeros(buf_ref.shape, F32)
        gather(0, 0)

    @pl.when(i + 1 < pl.num_programs(0))
    def _():
        gather(i + 1, 1 - slot)

    _wait_rows(ntot_ref[i], lambda n: read(slot, n, 0, 0))

    slab = slab_ref[...]
    lane = lax.broadcasted_iota(jnp.int32, (tm, R_SORTED), 1).astype(F32)
    wmat = jnp.zeros((tm, R_SORTED), F32)
    for k in range(TOP_K):
        wmat = jnp.where(lane == slab[:, k:k + 1], slab[:, TOP_K + k:TOP_K + k + 1], wmat)
    moe = _dot(wmat.astype(BF16), buf_ref[slot].astype(BF16))
    o_ref[...] = _rms(h_ref[...] + moe, g_ref[...])


def _combine(pieces, ys, h, slab, g):
    t = h.shape[0]
    tm = TM_ROUTE
    return pl.pallas_call(
        _combine_kernel,
        grid_spec=pltpu.PrefetchScalarGridSpec(
            num_scalar_prefetch=4,
            grid=(t // tm,),
            in_specs=[pl.BlockSpec(memory_space=pl.ANY),
                      pl.BlockSpec((tm, D_MODEL), lambda i, *_: (i, 0)),
                      pl.BlockSpec((tm, LANES), lambda i, *_: (i, 0)),
                      pl.BlockSpec((1, D_MODEL), lambda i, *_: (0, 0))],
            out_specs=pl.BlockSpec((tm, D_MODEL), lambda i, *_: (i, 0)),
            scratch_shapes=[pltpu.VMEM((2, R_SORTED, D_MODEL), F32),
                            pltpu.SemaphoreType.DMA((2,))]),
        out_shape=jax.ShapeDtypeStruct((t, D_MODEL), F32),
        compiler_params=_params(1),
        name="combine",
    )(*pieces, ys, h, slab, g)


def _pad_lanes(v, fill=0.0):
    return jnp.pad(v.reshape(1, -1), ((0, 0), (0, LANES - v.shape[-1])), constant_values=fill)


def kernel(x, norm_mix_g, w_in, conv_w, conv_b, dt_bias, a_log, d_skip, ssd_norm_g, attn_sinks,
           rel_bias, attn_norm_g, w_out, norm_ffn_g, w_router, b_router, w1, b1, w2, b2,
           norm_final_g):
    bsz, seq, d = x.shape
    t = bsz * seq
    nc = seq // CHUNK
    depth = w_in.shape[0]
    o2 = D_SSD + D_CONV
    o3 = o2 + SSD_HEADS
    bias = _bias_table(rel_bias)

    assert depth == 1, "the final norm is fused into the combine kernel: single layer only"
    h = x.reshape(t, d)
    for layer in range(depth):
        w = w_in[layer]
        w_packed = jnp.concatenate(
            [w[:, :o2], w[:, o3:], w[:, o2:o3], jnp.zeros((d, LANES - SSD_HEADS), w.dtype)],
            axis=1).astype(BF16)
        z, xbc, q, kv, dt = [a.reshape(bsz, seq, -1)
                             for a in _inproj(h, norm_mix_g[layer].reshape(1, d), w_packed)]
        y_mix = _mixer(z, xbc, dt, q, kv, conv_w[layer], conv_b[layer].reshape(1, -1),
                       _pad_lanes(dt_bias[layer]), _pad_lanes(a_log[layer]),
                       jnp.repeat(d_skip[layer], SSD_HEAD_DIM).reshape(1, -1),
                       ssd_norm_g[layer].reshape(1, -1), attn_sinks[layer], bias,
                       attn_norm_g[layer].reshape(1, -1)).reshape(t, -1)

        wr = w_router[layer].T.astype(BF16)
        br = jnp.broadcast_to(b_router[layer][:, None], (N_EXPERTS, LANES))
        h_mid, u, slab, post, tab = _route(
            h, y_mix, w_out[layer].astype(BF16), norm_ffn_g[layer].reshape(1, d), wr, br)

        bm = BM_EXPERT
        ntiles = t // TM_ROUTE
        nblk = (t * TOP_K + ntiles * N_EXPERTS * (SUBLANES - 1)) // bm + N_EXPERTS
        tab = tab.reshape(ntiles, N_EXPERTS, LANES)
        cnt8, off8, segst = tab[:, :, 0], tab[:, :, 1], tab[:, :, 2]
        total8 = off8[-1] + cnt8[-1]
        padded = ((total8 + bm - 1) // bm) * bm
        pend = jnp.cumsum(padded)
        segdst = (pend - padded)[None, :] + off8
        n_used = pend[-1:] // bm
        eids = jnp.arange(N_EXPERTS, dtype=jnp.int32)
        blk = jnp.arange(nblk, dtype=jnp.int32)
        blk_expert = jnp.minimum(
            jnp.sum((blk[:, None] * bm >= pend[None, :]).astype(jnp.int32), axis=1), N_EXPERTS - 1)
        after = jnp.concatenate([blk_expert[1:], blk_expert[-1:]])
        before = jnp.concatenate([blk_expert[:1], blk_expert[:-1]])
        zero_flag = jnp.where(blk >= n_used[0], 2,
                              ((blk == n_used[0] - 1) | (after != blk_expert)).astype(jnp.int32))
        e_end = jnp.sum(jnp.where(blk_expert[:, None] == eids[None, :],
                                  ((pend - padded) + total8)[None, :], 0), axis=1)
        valid = jnp.clip(e_end - blk * bm, 0, bm)
        blk_rows = ((valid + EXPERT_ROW_STEP - 1) // EXPERT_ROW_STEP) * EXPERT_ROW_STEP
        first_flag = ((blk == 0) | (before != blk_expert)).astype(jnp.int32)
        cand = jnp.where((eids[None, :] > eids[:, None]) & (padded[None, :] > 0), eids[None, :], N_EXPERTS)
        next_nonempty = jnp.min(cand, axis=1)
        next_nonempty = jnp.where(next_nonempty == N_EXPERTS, -1, next_nonempty)
        next_expert = jnp.sum(jnp.where(blk_expert[:, None] == eids[None, :], next_nonempty[None, :], 0), axis=1)
        pieces = _piece_tables(cnt8, segst, segdst)

        xs = _dispatch(pieces, zero_flag, u, post)
        ys = _experts(blk_expert, n_used, first_flag, next_expert, blk_rows, xs, w1[layer],
                      b1[layer].reshape(N_EXPERTS, 1, -1), w2[layer], b2[layer].reshape(N_EXPERTS, 1, -1))
        h = _combine(pieces, ys, h_mid, slab, norm_final_g.reshape(1, d))
    return h.reshape(bsz, seq, d)
```

```python
import functools
import math

import numpy as np
import jax
import jax.numpy as jnp
from jax import lax
from jax.experimental import pallas as pl
from jax.experimental.pallas import tpu as pltpu

F32 = jnp.float32
BF16 = jnp.bfloat16

D_MODEL = 1024
SSD_HEADS = 8
SSD_HEAD_DIM = 64
D_SSD = SSD_HEADS * SSD_HEAD_DIM
SSD_GROUPS = 2
SSD_HEADS_PER_GROUP = SSD_HEADS // SSD_GROUPS
D_STATE = 128
CONV_WIDTH = 4
CHUNK = 128
D_CONV = D_SSD + 2 * SSD_GROUPS * D_STATE
ATTN_Q_HEADS = 8
ATTN_KV_HEADS = 2
ATTN_Q_PER_KV = ATTN_Q_HEADS // ATTN_KV_HEADS
ATTN_HEAD_DIM = 64
D_ATTN = ATTN_Q_HEADS * ATTN_HEAD_DIM
D_KV = ATTN_KV_HEADS * ATTN_HEAD_DIM
WINDOW = 128
REL_BUCKETS = 32
REL_MAX_DIST = 128
N_EXPERTS = 32
TOP_K = 4
D_EXPERT = D_MODEL
SWIGLU_LIMIT = 7.0
SWIGLU_ALPHA = 1.702
RMS_EPS = 1e-5

LANES = 128
SUBLANES = 8
NEG = -1e30
VMEM_LIMIT = 56 * 1024 * 1024

TM_PROJ = 512
TM_ROUTE = 512
BM_EXPERT = 512
EXPERT_ROW_STEP = 128
SEQS_PER_STEP = 2
R_SORTED = TM_ROUTE * TOP_K + N_EXPERTS * SUBLANES

D_PROJ_PACKED = D_SSD + D_CONV + D_ATTN + 2 * D_KV + LANES


def _params(n_axes):
    return pltpu.CompilerParams(dimension_semantics=("arbitrary",) * n_axes,
                                vmem_limit_bytes=VMEM_LIMIT)


def _rms(x, g):
    return x * lax.rsqrt(jnp.mean(x * x, axis=-1, keepdims=True) + RMS_EPS) * g


def _silu(x):
    return x / (1.0 + jnp.exp(-x))


def _dot(a, b, **kw):
    return jnp.dot(a, b, preferred_element_type=F32, **kw)


def _dot_f32_by_mask(mask, x, mask_on_left=False):
    hi = x.astype(BF16)
    r1 = x - hi.astype(F32)
    mid = r1.astype(BF16)
    lo = (r1 - mid.astype(F32)).astype(BF16)
    out = None
    for piece in (hi, mid, lo):
        term = _dot(mask, piece) if mask_on_left else _dot(piece, mask)
        out = term if out is None else out + term
    return out


def _t5_bucket_table():
    dist = CHUNK + np.arange(CHUNK)[:, None] - np.arange(2 * CHUNK)[None, :]
    in_window = (dist >= 0) & (dist < WINDOW)
    d = np.clip(dist, 0, REL_MAX_DIST)
    max_exact = REL_BUCKETS // 2
    large = max_exact + (np.log(np.maximum(d, 1).astype(np.float32) / max_exact)
                         / math.log(REL_MAX_DIST / max_exact)
                         * (REL_BUCKETS - max_exact)).astype(np.int32)
    large = np.minimum(large, REL_BUCKETS - 1)
    bucket = np.where(d < max_exact, d, large)
    return np.where(in_window, bucket, -1).astype(np.int32)


def _bias_kernel(rb_ref, bucket_ref, o_ref):
    bucket = bucket_ref[...]
    for h in range(ATTN_Q_HEADS):
        acc = jnp.full(bucket.shape, NEG, F32)
        for b in range(REL_BUCKETS):
            acc = jnp.where(bucket == b, rb_ref[b, h], acc)
        o_ref[h] = acc


def _bias_table(rel_bias):
    bucket = jnp.asarray(_t5_bucket_table())
    return pl.pallas_call(
        _bias_kernel,
        out_shape=jax.ShapeDtypeStruct((ATTN_Q_HEADS, CHUNK, 2 * CHUNK), F32),
        in_specs=[pl.BlockSpec(memory_space=pltpu.SMEM),
                  pl.BlockSpec(memory_space=pltpu.VMEM)],
        out_specs=pl.BlockSpec(memory_space=pltpu.VMEM),
        name="bias_table",
    )(rel_bias, bucket)


def _inproj_kernel(x_ref, g_ref, w_ref, z_ref, xbc_ref, q_ref, kv_ref, dt_ref, wb_ref):
    @pl.when(pl.program_id(0) == 0)
    def _():
        o2 = D_SSD + D_CONV
        o3 = o2 + SSD_HEADS
        n_qkv = D_ATTN + 2 * D_KV
        wb_ref[:, 0:o2] = w_ref[:, 0:o2].astype(BF16)
        wb_ref[:, o2:o2 + n_qkv] = w_ref[:, o3:o3 + n_qkv].astype(BF16)
        wb_ref[:, o2 + n_qkv:] = jnp.concatenate(
            [w_ref[:, o2:o3], jnp.zeros((D_MODEL, LANES - SSD_HEADS), F32)], axis=1).astype(BF16)

    u = _rms(x_ref[...], g_ref[...]).astype(BF16)
    proj = _dot(u, wb_ref[...])
    o = 0
    for ref in (z_ref, xbc_ref, q_ref, kv_ref, dt_ref):
        w = ref.shape[1]
        ref[...] = proj[:, o:o + w]
        o += w


def _inproj(x2, g, w_in):
    t = x2.shape[0]
    widths = (D_SSD, D_CONV, D_ATTN, 2 * D_KV, LANES)
    return pl.pallas_call(
        _inproj_kernel,
        grid=(t // TM_PROJ,),
        in_specs=[pl.BlockSpec((TM_PROJ, D_MODEL), lambda i: (i, 0)),
                  pl.BlockSpec((1, D_MODEL), lambda i: (0, 0)),
                  pl.BlockSpec(w_in.shape, lambda i: (0, 0))],
        out_specs=[pl.BlockSpec((TM_PROJ, w), lambda i: (i, 0)) for w in widths],
        out_shape=[jax.ShapeDtypeStruct((t, w), F32) for w in widths],
        scratch_shapes=[pltpu.VMEM((D_MODEL, D_PROJ_PACKED), BF16)],
        compiler_params=_params(1),
        name="inproj",
    )(x2, g, w_in)


def _mixer_kernel(sink_ref, z_ref, xbc_ref, dt_ref, q_ref, kv_ref, kvp_ref, cw_ref, cb_ref, dtb_ref,
                  alog_ref, dskip_ref, gs_ref, expand_ref, bias_ref, ga_ref, o_ref, state_ref, xpad_ref):
    @pl.when(pl.program_id(1) == 0)
    def _():
        state_ref[...] = jnp.zeros(state_ref.shape, F32)
        xpad_ref[...] = jnp.zeros(xpad_ref.shape, F32)

    for s in range(z_ref.shape[0]):
        y_ssd = _ssd_chunk(z_ref.at[s], xbc_ref.at[s], dt_ref.at[s], cw_ref, cb_ref, dtb_ref, alog_ref,
                           dskip_ref, gs_ref, expand_ref, state_ref.at[s], xpad_ref.at[s])
        y_attn = _swa_block(sink_ref, q_ref.at[s], kv_ref.at[s], kvp_ref.at[s], bias_ref, ga_ref)
        o_ref[s] = jnp.concatenate([y_ssd, y_attn], axis=1)


def _ssd_chunk(z_ref, xbc_ref, dt_ref, cw_ref, cb_ref, dtb_ref, alog_ref, dskip_ref, g_ref,
               expand_ref, state_ref, xpad_ref):
    L = CHUNK
    G, R, P, N = SSD_GROUPS, SSD_HEADS_PER_GROUP, SSD_HEAD_DIM, D_STATE
    GW = R * P

    x_cur = xbc_ref[...]
    xpad = jnp.concatenate([xpad_ref[...], x_cur], axis=0)
    xpad_ref[...] = x_cur[L - SUBLANES:, :]
    acc = cb_ref[...] + cw_ref[CONV_WIDTH - 1:CONV_WIDTH, :] * x_cur
    for d in range(1, CONV_WIDTH):
        k = CONV_WIDTH - 1 - d
        acc = acc + cw_ref[k:k + 1, :] * pltpu.roll(xpad, d, axis=0)[SUBLANES:, :]
    xbc = _silu(acc)
    xs = xbc[:, :D_SSD]
    bm = xbc[:, D_SSD:D_SSD + G * N]
    cm = xbc[:, D_SSD + G * N:]

    dtr = dt_ref[...] + dtb_ref[...]
    dt = jnp.maximum(dtr, 0.0) + jnp.log(1.0 + jnp.exp(-jnp.abs(dtr)))
    a_dt = dt * (-jnp.exp(alog_ref[...]))
    ri = lax.broadcasted_iota(jnp.int32, (L, L), 0)
    ci = lax.broadcasted_iota(jnp.int32, (L, L), 1)
    causal = ci <= ri
    a_cum = _dot_f32_by_mask(causal.astype(BF16), a_dt, mask_on_left=True)
    a_cum_t = _dot_f32_by_mask((ri <= ci).astype(BF16), a_dt.T)
    a_last = a_cum[L - 1:L, :]
    stack = jnp.concatenate(
        [dt, jnp.exp(a_cum), jnp.exp(a_last - a_cum),
         jnp.broadcast_to(jnp.exp(a_last), (SUBLANES, LANES))], axis=0)
    ex = _dot_f32_by_mask(expand_ref[...], stack)
    dt_x, ea_x, dte_x, cd_x = ex[0:L], ex[L:2 * L], ex[2 * L:3 * L], ex[3 * L:3 * L + 1]
    xdt = xs * dt_x

    ys = []
    for g in range(G):
        bm_g = bm[:, g * N:(g + 1) * N]
        cm_g = cm[:, g * N:(g + 1) * N].astype(BF16)
        cb = lax.dot_general(cm_g, bm_g.astype(BF16), (((1,), (1,)), ((), ())),
                             preferred_element_type=F32)
        xdt_g = xdt[:, g * GW:(g + 1) * GW]
        yd = []
        for r in range(R):
            h = g * R + r
            seg = a_cum[:, h:h + 1] - a_cum_t[h:h + 1, :]
            dec = jnp.exp(jnp.where(causal, seg, NEG))
            yd.append(_dot((cb * dec).astype(BF16), xdt_g[:, r * P:(r + 1) * P].astype(BF16)))
        y_diag = jnp.concatenate(yd, axis=1)
        st = state_ref[g]
        y_off = _dot(cm_g, st.astype(BF16)) * ea_x[:, g * GW:(g + 1) * GW]
        new = _dot(bm_g.T.astype(BF16), (xdt_g * dte_x[:, g * GW:(g + 1) * GW]).astype(BF16))
        state_ref[g] = st * cd_x[:, g * GW:(g + 1) * GW] + new
        ys.append(y_diag + y_off + xs[:, g * GW:(g + 1) * GW] * dskip_ref[:, g * GW:(g + 1) * GW])
    y = jnp.concatenate(ys, axis=1)
    return _rms(y * _silu(z_ref[...]), g_ref[...])


def _mixer(z, xbc, dt, q, kv, conv_w, conv_b, dt_bias, a_log, d_skip_x, ssd_norm_g, sinks, bias,
           attn_norm_g):
    bsz, seq, _ = z.shape
    nc = seq // CHUNK
    expand = np.zeros((LANES, D_SSD), np.float32)
    for h in range(SSD_HEADS):
        expand[h, h * SSD_HEAD_DIM:(h + 1) * SSD_HEAD_DIM] = 1.0
    ns = SEQS_PER_STEP
    row = lambda b, c: (b, c, 0)
    prev = lambda b, c: (b, jnp.maximum(c - 1, 0), 0)
    fixed = lambda b, c: (0, 0)
    return pl.pallas_call(
        _mixer_kernel,
        grid=(bsz // ns, nc),
        in_specs=[pl.BlockSpec(memory_space=pltpu.SMEM),
                  pl.BlockSpec((ns, CHUNK, D_SSD), row),
                  pl.BlockSpec((ns, CHUNK, D_CONV), row),
                  pl.BlockSpec((ns, CHUNK, LANES), row),
                  pl.BlockSpec((ns, CHUNK, D_ATTN), row),
                  pl.BlockSpec((ns, CHUNK, 2 * D_KV), row),
                  pl.BlockSpec((ns, CHUNK, 2 * D_KV), prev),
                  pl.BlockSpec((CONV_WIDTH, D_CONV), fixed),
                  pl.BlockSpec((1, D_CONV), fixed),
                  pl.BlockSpec((1, LANES), fixed),
                  pl.BlockSpec((1, LANES), fixed),
                  pl.BlockSpec((1, D_SSD), fixed),
                  pl.BlockSpec((1, D_SSD), fixed),
                  pl.BlockSpec((LANES, D_SSD), fixed),
                  pl.BlockSpec((ATTN_Q_HEADS, CHUNK, 2 * CHUNK), lambda b, c: (0, 0, 0)),
                  pl.BlockSpec((1, D_ATTN), fixed)],
        out_specs=pl.BlockSpec((ns, CHUNK, D_SSD + D_ATTN), row),
        out_shape=jax.ShapeDtypeStruct((bsz, seq, D_SSD + D_ATTN), F32),
        scratch_shapes=[pltpu.VMEM((ns, SSD_GROUPS, D_STATE, SSD_HEADS_PER_GROUP * SSD_HEAD_DIM), F32),
                        pltpu.VMEM((ns, SUBLANES, D_CONV), F32)],
        compiler_params=_params(2),
        name="mixer",
    )(sinks, z, xbc, dt, q, kv, kv, conv_w, conv_b, dt_bias, a_log, d_skip_x, ssd_norm_g,
      jnp.asarray(expand, dtype=BF16), bias, attn_norm_g)


def _swa_block(sink_ref, q_ref, kv_ref, kvp_ref, bias_ref, g_ref):
    L, Dh = CHUNK, ATTN_HEAD_DIM
    q = q_ref[...] * (1.0 / math.sqrt(Dh))
    kv = kv_ref[...]
    kvp = kvp_ref[...]
    col = lax.broadcasted_iota(jnp.int32, (L, 2 * L), 1)
    first_col = jnp.where(pl.program_id(1) > 0, 0, L)
    outs = []
    for hk in range(ATTN_KV_HEADS):
        ks = slice(hk * Dh, (hk + 1) * Dh)
        vs = slice(D_KV + hk * Dh, D_KV + (hk + 1) * Dh)
        kc = jnp.concatenate([kvp[:, ks], kv[:, ks]], axis=0).astype(BF16)
        vc = jnp.concatenate([kvp[:, vs], kv[:, vs]], axis=0).astype(BF16)
        for g in range(ATTN_Q_PER_KV):
            h = hk * ATTN_Q_PER_KV + g
            qh = q[:, h * Dh:(h + 1) * Dh].astype(BF16)
            s = lax.dot_general(qh, kc, (((1,), (1,)), ((), ())), preferred_element_type=F32)
            s = jnp.where(col >= first_col, s + bias_ref[h], NEG)
            sink = sink_ref[h]
            m = jnp.maximum(jnp.max(s, axis=-1, keepdims=True), sink)
            p = jnp.exp(s - m)
            denom = jnp.sum(p, axis=-1, keepdims=True) + jnp.exp(sink - m)
            outs.append(_dot(p.astype(BF16), vc) / denom)
    return _rms(jnp.concatenate(outs, axis=1), g_ref[...])


def _route_kernel(x_ref, y_ref, wo_ref, g_ref, wr_ref, br_ref,
                  h_ref, u_ref, slab_ref, post_ref, tab_ref, run_ref, wob_ref):
    tm = x_ref.shape[0]

    @pl.when(pl.program_id(0) == 0)
    def _():
        run_ref[...] = jnp.zeros(run_ref.shape, F32)
        wob_ref[...] = wo_ref[...].astype(BF16)

    h = x_ref[...] + _dot(y_ref[...].astype(BF16), wob_ref[...])
    h_ref[...] = h
    u = _rms(h, g_ref[...])
    ub = u.astype(BF16)
    u_ref[...] = ub
    E = N_EXPERTS
    logits = lax.dot_general(wr_ref[...], ub, (((1,), (1,)), ((), ())), preferred_element_type=F32)
    logits = logits + jnp.concatenate([br_ref[...]] * (tm // LANES), axis=1)

    eidx = lax.broadcasted_iota(jnp.int32, (E, tm), 0).astype(F32)
    vals, idxs = [], []
    cur = logits
    for _ in range(TOP_K):
        m = jnp.max(cur, axis=0, keepdims=True)
        ix = jnp.min(jnp.where(cur == m, eidx, float(E)), axis=0, keepdims=True)
        vals.append(m)
        idxs.append(ix)
        cur = jnp.where(eidx == ix, NEG, cur)
    es = [jnp.exp(v - vals[0]) for v in vals]
    den = es[0] + es[1] + es[2] + es[3]

    onehot = jnp.zeros((E, tm), F32)
    for ix in idxs:
        onehot = onehot + (eidx == ix).astype(F32)
    ri = lax.broadcasted_iota(jnp.int32, (tm, tm), 0)
    ci = lax.broadcasted_iota(jnp.int32, (tm, tm), 1)
    before = _dot(onehot.astype(BF16), (ri < ci).astype(BF16))
    cnt = jnp.sum(onehot, axis=1, keepdims=True)
    cnt8 = jnp.floor((cnt + (SUBLANES - 1)) * (1.0 / SUBLANES)) * SUBLANES
    el = lax.broadcasted_iota(jnp.int32, (E, E), 0)
    ec = lax.broadcasted_iota(jnp.int32, (E, E), 1)
    seg_start = _dot_f32_by_mask((ec < el).astype(BF16), jnp.broadcast_to(cnt8, (E, LANES)),
                                 mask_on_left=True)[:, 0:1]
    where_to = before + seg_start

    rows = [jnp.sum(jnp.where(eidx == idxs[k], where_to, 0.0), axis=0, keepdims=True)
            for k in range(TOP_K)]
    rows += [es[k] / den for k in range(TOP_K)]
    stack = jnp.concatenate(rows, axis=0)
    post_ref[...] = stack.astype(jnp.int32)
    slab_ref[...] = jnp.concatenate([stack, jnp.zeros((LANES - 2 * TOP_K, tm), F32)], axis=0).T

    lane = lax.broadcasted_iota(jnp.int32, (E, LANES), 1)
    tab = jnp.where(lane == 0, cnt8, jnp.where(lane == 1, run_ref[...], jnp.where(lane == 2, seg_start, 0.0)))
    tab_ref[...] = tab.astype(jnp.int32)
    run_ref[...] = run_ref[...] + cnt8


def _route(x2, y_mix, wo, g, wr, br):
    t = x2.shape[0]
    tm = TM_ROUTE
    row = lambda i: (i, 0)
    fixed = lambda i: (0, 0)
    return pl.pallas_call(
        _route_kernel,
        grid=(t // tm,),
        in_specs=[pl.BlockSpec((tm, D_MODEL), row),
                  pl.BlockSpec((tm, D_SSD + D_ATTN), row),
                  pl.BlockSpec((D_SSD + D_ATTN, D_MODEL), fixed),
                  pl.BlockSpec((1, D_MODEL), fixed),
                  pl.BlockSpec((N_EXPERTS, D_MODEL), fixed),
                  pl.BlockSpec((N_EXPERTS, LANES), fixed)],
        out_specs=[pl.BlockSpec((tm, D_MODEL), row),
                   pl.BlockSpec((tm, D_MODEL), row),
                   pl.BlockSpec((tm, LANES), row),
                   pl.BlockSpec((SUBLANES, tm), lambda i: (0, i)),
                   pl.BlockSpec((N_EXPERTS, LANES), row)],
        out_shape=[jax.ShapeDtypeStruct((t, D_MODEL), F32),
                   jax.ShapeDtypeStruct((t, D_MODEL), BF16),
                   jax.ShapeDtypeStruct((t, LANES), F32),
                   jax.ShapeDtypeStruct((SUBLANES, t), jnp.int32),
                   jax.ShapeDtypeStruct((t // tm * N_EXPERTS, LANES), jnp.int32)],
        scratch_shapes=[pltpu.VMEM((N_EXPERTS, LANES), F32),
                        pltpu.VMEM((D_SSD + D_ATTN, D_MODEL), BF16)],
        compiler_params=_params(1),
        name="route",
    )(x2, y_mix, wo, g, wr, br)


SEG_SIZE_BITS = (TM_ROUTE // SUBLANES).bit_length()


TILE_SIZE_BITS = (R_SORTED // SUBLANES).bit_length()


def _piece_tables(cnt8, segst, segdst):
    n = (cnt8 // SUBLANES)[:, None, :]
    b = jnp.arange(SEG_SIZE_BITS, dtype=jnp.int32)[None, :, None]
    has = (n >> b) & 1
    off = ((n >> (b + 1)) << (b + 1)) * SUBLANES
    rank = jnp.cumsum(has, axis=2) - has
    place = (has[..., None] == 1) & (rank[..., None] == jnp.arange(N_EXPERTS, dtype=jnp.int32))
    dense = lambda v: jnp.sum(jnp.where(place, v[..., None], 0), axis=2).reshape(-1)
    return (jnp.sum(has, axis=2).reshape(-1), dense(segst[:, None, :] + off),
            dense(segdst[:, None, :] + off), jnp.sum(cnt8, axis=1) // SUBLANES)


def _for_each_piece(tile, pcnt_ref, psrc_ref, pdst_ref, fn):
    for b in range(SEG_SIZE_BITS):
        base = (tile * SEG_SIZE_BITS + b) * N_EXPERTS

        def body(p, carry, b=b, base=base):
            fn(SUBLANES << b, psrc_ref[base + p], pdst_ref[base + p])
            return carry

        lax.fori_loop(0, pcnt_ref[tile * SEG_SIZE_BITS + b], body, 0)


def _wait_rows(n_tiles8, descriptor):
    for b in range(TILE_SIZE_BITS):
        @pl.when(((n_tiles8 >> b) & 1) == 1)
        def _(b=b):
            descriptor(SUBLANES << b).wait()


def _rows(ref, row, n):
    if not isinstance(row, int):
        row = pl.multiple_of(row, SUBLANES)
    return ref.at[pl.ds(row, n), :]


def _dispatch_kernel(pcnt_ref, psrc_ref, pdst_ref, ntot_ref, zflag_ref, u_ref, post_ref, xs_ref,
                     buf_ref, zero_ref, sems, zsem):
    tm = u_ref.shape[0]
    i = pl.program_id(0)
    slot = i % 2
    bm = zero_ref.shape[0]

    def fill(b, flag):
        return pltpu.make_async_copy(zero_ref, xs_ref.at[pl.ds(b * bm, bm), :], zsem.at[flag - 1])

    def for_flagged(flag, action):
        def body(b, carry):
            @pl.when(zflag_ref[b] == flag)
            def _():
                action(fill(b, flag))
            return carry

        lax.fori_loop(0, zflag_ref.shape[0], body, 0)

    @pl.when(i == 0)
    def _():
        zero_ref[...] = jnp.zeros(zero_ref.shape, F32)
        for_flagged(1, lambda c: c.start())
        for_flagged(2, lambda c: c.start())
        for_flagged(1, lambda c: c.wait())

    @pl.when(i == pl.num_programs(0) - 1)
    def _():
        for_flagged(2, lambda c: c.wait())

    pos = post_ref[...]
    j = lax.broadcasted_iota(jnp.int32, (R_SORTED, tm), 0)
    sel = jnp.zeros((R_SORTED, tm), F32)
    for k in range(TOP_K):
        sel = jnp.where(j == pos[k:k + 1, :], 1.0, sel)
    buf_ref[slot] = _dot(sel.astype(BF16), u_ref[...])

    def write(slot_, n, src_row, dst_row):
        return pltpu.make_async_copy(_rows(buf_ref.at[slot_], src_row, n),
                                     _rows(xs_ref, dst_row, n), sems.at[slot_])

    _for_each_piece(i, pcnt_ref, psrc_ref, pdst_ref, lambda n, s, d: write(slot, n, s, d).start())

    @pl.when(i > 0)
    def _():
        _wait_rows(ntot_ref[i - 1], lambda n: write(1 - slot, n, 0, 0))

    @pl.when(i == pl.num_programs(0) - 1)
    def _():
        _wait_rows(ntot_ref[i], lambda n: write(slot, n, 0, 0))


def _dispatch(pieces, zero_flag, u, post):
    t = u.shape[0]
    tm = TM_ROUTE
    n_rows = zero_flag.shape[0] * EXPERT_ROW_STEP
    return pl.pallas_call(
        _dispatch_kernel,
        grid_spec=pltpu.PrefetchScalarGridSpec(
            num_scalar_prefetch=5,
            grid=(t // tm,),
            in_specs=[pl.BlockSpec((tm, D_MODEL), lambda i, *_: (i, 0)),
                      pl.BlockSpec((SUBLANES, tm), lambda i, *_: (0, i))],
            out_specs=pl.BlockSpec(memory_space=pl.ANY),
            scratch_shapes=[pltpu.VMEM((2, R_SORTED, D_MODEL), F32),
                            pltpu.VMEM((EXPERT_ROW_STEP, D_MODEL), F32),
                            pltpu.SemaphoreType.DMA((2,)),
                            pltpu.SemaphoreType.DMA((2,))]),
        out_shape=jax.ShapeDtypeStruct((n_rows, D_MODEL), F32),
        compiler_params=_params(1),
        name="dispatch",
    )(*pieces, zero_flag, u, post)


def _expert_kernel(be_ref, nu_ref, first_ref, next_ref, rows_ref, xs_ref, w1_ref, b1_ref, w2_ref, b2_ref,
                   ys_ref, w1f_ref, w2f_ref, w1b_ref, w2b_ref, sems):
    i = pl.program_id(0)

    def fetch(e):
        return (pltpu.make_async_copy(w1_ref.at[e], w1f_ref, sems.at[0]),
                pltpu.make_async_copy(w2_ref.at[e], w2f_ref, sems.at[1]))

    @pl.when(i >= nu_ref[0])
    def _():
        ys_ref[...] = jnp.zeros(ys_ref.shape, F32)

    @pl.when(i < nu_ref[0])
    def _():
        e = be_ref[i]

        @pl.when(i == 0)
        def _():
            for c in fetch(e):
                c.start()

        @pl.when(first_ref[i] != 0)
        def _():
            for c in fetch(e):
                c.wait()
            w1b_ref[...] = w1f_ref[...].astype(BF16)
            w2b_ref[...] = w2f_ref[...].astype(BF16)

            @pl.when(next_ref[i] >= 0)
            def _():
                for c in fetch(next_ref[i]):
                    c.start()

        bm = xs_ref.shape[0]
        for m in range(EXPERT_ROW_STEP, bm + 1, EXPERT_ROW_STEP):
            @pl.when(rows_ref[i] == m)
            def _(m=m):
                hid = _dot(xs_ref[0:m, :].astype(BF16), w1b_ref[...]) + b1_ref[e]
                x_glu = jnp.minimum(hid[:, :D_EXPERT], SWIGLU_LIMIT)
                x_lin = jnp.clip(hid[:, D_EXPERT:], -SWIGLU_LIMIT, SWIGLU_LIMIT)
                act = x_glu / (1.0 + jnp.exp(-SWIGLU_ALPHA * x_glu)) * (x_lin + 1.0)
                ys_ref[0:m, :] = _dot(act.astype(BF16), w2b_ref[...]) + b2_ref[e]
                if m < bm:
                    ys_ref[m:, :] = jnp.zeros((bm - m, D_MODEL), F32)


def _experts(blk_expert, n_used, first_flag, next_expert, blk_rows, xs, w1, b1, w2, b2):
    n_rows = xs.shape[0]
    bm = BM_EXPERT
    nblk = n_rows // bm
    last = lambda i, nu: jnp.maximum(jnp.minimum(i, nu[0] - 1), 0)
    row = lambda i, be, nu, *_: (last(i, nu), 0)
    whole = lambda i, *_: (0, 0, 0)
    return pl.pallas_call(
        _expert_kernel,
        grid_spec=pltpu.PrefetchScalarGridSpec(
            num_scalar_prefetch=5,
            grid=(nblk,),
            in_specs=[pl.BlockSpec((bm, D_MODEL), row),
                      pl.BlockSpec(memory_space=pl.ANY),
                      pl.BlockSpec((N_EXPERTS, 1, 2 * D_EXPERT), whole),
                      pl.BlockSpec(memory_space=pl.ANY),
                      pl.BlockSpec((N_EXPERTS, 1, D_MODEL), whole)],
            out_specs=pl.BlockSpec((bm, D_MODEL), lambda i, *_: (i, 0)),
            scratch_shapes=[pltpu.VMEM((D_MODEL, 2 * D_EXPERT), F32),
                            pltpu.VMEM((D_EXPERT, D_MODEL), F32),
                            pltpu.VMEM((D_MODEL, 2 * D_EXPERT), BF16),
                            pltpu.VMEM((D_EXPERT, D_MODEL), BF16),
                            pltpu.SemaphoreType.DMA((2,))]),
        out_shape=jax.ShapeDtypeStruct((n_rows, D_MODEL), F32),
        compiler_params=_params(1),
        name="experts",
    )(blk_expert, n_used, first_flag, next_expert, blk_rows, xs, w1, b1, w2, b2)


def _combine_kernel(pcnt_ref, psrc_ref, pdst_ref, ntot_ref, ys_ref, h_ref, slab_ref, g_ref, o_ref,
                    buf_ref, sems):
    tm = h_ref.shape[0]
    i = pl.program_id(0)
    slot = i % 2

    def read(slot_, n, sorted_row, ys_row):
        return pltpu.make_async_copy(_rows(ys_ref, ys_row, n),
                                     _rows(buf_ref.at[slot_], sorted_row, n), sems.at[slot_])

    def gather(tile, slot_):
        _for_each_piece(tile, pcnt_ref, psrc_ref, pdst_ref, lambda n, s, d: read(slot_, n, s, d).start())

    @pl.when(i == 0)
    def _():
        buf_ref[...] = jnp.zeros(buf_ref.shape, F32)
        gather(0, 0)

    @pl.when(i + 1 < pl.num_programs(0))
    def _():
        gather(i + 1, 1 - slot)

    _wait_rows(ntot_ref[i], lambda n: read(slot, n, 0, 0))

    slab = slab_ref[...]
    lane = lax.broadcasted_iota(jnp.int32, (tm, R_SORTED), 1).astype(F32)
    wmat = jnp.zeros((tm, R_SORTED), F32)
    for k in range(TOP_K):
        wmat = jnp.where(lane == slab[:, k:k + 1], slab[:, TOP_K + k:TOP_K + k + 1], wmat)
    moe = _dot(wmat.astype(BF16), buf_ref[slot].astype(BF16))
    o_ref[...] = _rms(h_ref[...] + moe, g_ref[...])


def _combine(pieces, ys, h, slab, g):
    t = h.shape[0]
    tm = TM_ROUTE
    return pl.pallas_call(
        _combine_kernel,
        grid_spec=pltpu.PrefetchScalarGridSpec(
            num_scalar_prefetch=4,
            grid=(t // tm,),
            in_specs=[pl.BlockSpec(memory_space=pl.ANY),
                      pl.BlockSpec((tm, D_MODEL), lambda i, *_: (i, 0)),
                      pl.BlockSpec((tm, LANES), lambda i, *_: (i, 0)),
                      pl.BlockSpec((1, D_MODEL), lambda i, *_: (0, 0))],
            out_specs=pl.BlockSpec((tm, D_MODEL), lambda i, *_: (i, 0)),
            scratch_shapes=[pltpu.VMEM((2, R_SORTED, D_MODEL), F32),
                            pltpu.SemaphoreType.DMA((2,))]),
        out_shape=jax.ShapeDtypeStruct((t, D_MODEL), F32),
        compiler_params=_params(1),
        name="combine",
    )(*pieces, ys, h, slab, g)


def _pad_lanes(v, fill=0.0):
    return jnp.pad(v.reshape(1, -1), ((0, 0), (0, LANES - v.shape[-1])), constant_values=fill)


def kernel(x, norm_mix_g, w_in, conv_w, conv_b, dt_bias, a_log, d_skip, ssd_norm_g, attn_sinks,
           rel_bias, attn_norm_g, w_out, norm_ffn_g, w_router, b_router, w1, b1, w2, b2,
           norm_final_g):
    bsz, seq, d = x.shape
    t = bsz * seq
    nc = seq // CHUNK
    depth = w_in.shape[0]
    bias = _bias_table(rel_bias)

    assert depth == 1, "the final norm is fused into the combine kernel: single layer only"
    h = x.reshape(t, d)
    for layer in range(depth):
        z, xbc, q, kv, dt = [a.reshape(bsz, seq, -1)
                             for a in _inproj(h, norm_mix_g[layer].reshape(1, d), w_in[layer])]
        y_mix = _mixer(z, xbc, dt, q, kv, conv_w[layer], conv_b[layer].reshape(1, -1),
                       _pad_lanes(dt_bias[layer]), _pad_lanes(a_log[layer]),
                       jnp.repeat(d_skip[layer], SSD_HEAD_DIM).reshape(1, -1),
                       ssd_norm_g[layer].reshape(1, -1), attn_sinks[layer], bias,
                       attn_norm_g[layer].reshape(1, -1)).reshape(t, -1)

        wr = w_router[layer].T.astype(BF16)
        br = jnp.broadcast_to(b_router[layer][:, None], (N_EXPERTS, LANES))
        h_mid, u, slab, post, tab = _route(
            h, y_mix, w_out[layer], norm_ffn_g[layer].reshape(1, d), wr, br)

        bm = BM_EXPERT
        ntiles = t // TM_ROUTE
        nblk = (t * TOP_K + ntiles * N_EXPERTS * (SUBLANES - 1)) // bm + N_EXPERTS
        tab = tab.reshape(ntiles, N_EXPERTS, LANES)
        cnt8, off8, segst = tab[:, :, 0], tab[:, :, 1], tab[:, :, 2]
        total8 = off8[-1] + cnt8[-1]
        padded = ((total8 + bm - 1) // bm) * bm
        pend = jnp.cumsum(padded)
        segdst = (pend - padded)[None, :] + off8
        n_used = pend[-1:] // bm
        eids = jnp.arange(N_EXPERTS, dtype=jnp.int32)
        blk = jnp.arange(nblk, dtype=jnp.int32)
        blk_expert = jnp.minimum(
            jnp.sum((blk[:, None] * bm >= pend[None, :]).astype(jnp.int32), axis=1), N_EXPERTS - 1)
        before = jnp.concatenate([blk_expert[:1], blk_expert[:-1]])
        e_end = jnp.sum(jnp.where(blk_expert[:, None] == eids[None, :],
                                  ((pend - padded) + total8)[None, :], 0), axis=1)
        valid = jnp.where(blk < n_used[0], jnp.clip(e_end - blk * bm, 0, bm), 0)
        blk_rows = ((valid + EXPERT_ROW_STEP - 1) // EXPERT_ROW_STEP) * EXPERT_ROW_STEP
        piece_lo = jnp.arange(bm // EXPERT_ROW_STEP, dtype=jnp.int32)[None, :] * EXPERT_ROW_STEP
        zero_flag = jnp.where(piece_lo >= valid[:, None], 2,
                              (piece_lo + EXPERT_ROW_STEP > valid[:, None]).astype(jnp.int32)).reshape(-1)
        first_flag = ((blk == 0) | (before != blk_expert)).astype(jnp.int32)
        cand = jnp.where((eids[None, :] > eids[:, None]) & (padded[None, :] > 0), eids[None, :], N_EXPERTS)
        next_nonempty = jnp.min(cand, axis=1)
        next_nonempty = jnp.where(next_nonempty == N_EXPERTS, -1, next_nonempty)
        next_expert = jnp.sum(jnp.where(blk_expert[:, None] == eids[None, :], next_nonempty[None, :], 0), axis=1)
        pieces = _piece_tables(cnt8, segst, segdst)

        xs = _dispatch(pieces, zero_flag, u, post)
        ys = _experts(blk_expert, n_used, first_flag, next_expert, blk_rows, xs, w1[layer],
                      b1[layer].reshape(N_EXPERTS, 1, -1), w2[layer], b2[layer].reshape(N_EXPERTS, 1, -1))
        h = _combine(pieces, ys, h_mid, slab, norm_final_g.reshape(1, d))
    return h.reshape(bsz, seq, d)
```

```python
import functools
import math

import numpy as np
import jax
import jax.numpy as jnp
from jax import lax
from jax.experimental import pallas as pl
from jax.experimental.pallas import tpu as pltpu

F32 = jnp.float32
BF16 = jnp.bfloat16

D_MODEL = 1024
SSD_HEADS = 8
SSD_HEAD_DIM = 64
D_SSD = SSD_HEADS * SSD_HEAD_DIM
SSD_GROUPS = 2
SSD_HEADS_PER_GROUP = SSD_HEADS // SSD_GROUPS
D_STATE = 128
CONV_WIDTH = 4
CHUNK = 128
D_CONV = D_SSD + 2 * SSD_GROUPS * D_STATE
ATTN_Q_HEADS = 8
ATTN_KV_HEADS = 2
ATTN_Q_PER_KV = ATTN_Q_HEADS // ATTN_KV_HEADS
ATTN_HEAD_DIM = 64
D_ATTN = ATTN_Q_HEADS * ATTN_HEAD_DIM
D_KV = ATTN_KV_HEADS * ATTN_HEAD_DIM
WINDOW = 128
REL_BUCKETS = 32
REL_MAX_DIST = 128
N_EXPERTS = 32
TOP_K = 4
D_EXPERT = D_MODEL
SWIGLU_LIMIT = 7.0
SWIGLU_ALPHA = 1.702
RMS_EPS = 1e-5

LANES = 128
SUBLANES = 8
NEG = -1e30
VMEM_LIMIT = 56 * 1024 * 1024

TM_PROJ = 512
TM_ROUTE = 512
BM_EXPERT = 512
EXPERT_ROW_STEP = 128
SEQS_PER_STEP = 2
R_SORTED = TM_ROUTE * TOP_K + N_EXPERTS * SUBLANES

D_PROJ_PACKED = D_SSD + D_CONV + D_ATTN + 2 * D_KV + LANES


def _params(n_axes):
    return pltpu.CompilerParams(dimension_semantics=("arbitrary",) * n_axes,
                                vmem_limit_bytes=VMEM_LIMIT)


def _rms(x, g):
    return x * lax.rsqrt(jnp.mean(x * x, axis=-1, keepdims=True) + RMS_EPS) * g


def _silu(x):
    return x / (1.0 + jnp.exp(-x))


def _dot(a, b, **kw):
    return jnp.dot(a, b, preferred_element_type=F32, **kw)


def _dot_f32_by_mask(mask, x, mask_on_left=False):
    hi = x.astype(BF16)
    r1 = x - hi.astype(F32)
    mid = r1.astype(BF16)
    lo = (r1 - mid.astype(F32)).astype(BF16)
    out = None
    for piece in (hi, mid, lo):
        term = _dot(mask, piece) if mask_on_left else _dot(piece, mask)
        out = term if out is None else out + term
    return out


def _t5_bucket_table():
    dist = CHUNK + np.arange(CHUNK)[:, None] - np.arange(2 * CHUNK)[None, :]
    in_window = (dist >= 0) & (dist < WINDOW)
    d = np.clip(dist, 0, REL_MAX_DIST)
    max_exact = REL_BUCKETS // 2
    large = max_exact + (np.log(np.maximum(d, 1).astype(np.float32) / max_exact)
                         / math.log(REL_MAX_DIST / max_exact)
                         * (REL_BUCKETS - max_exact)).astype(np.int32)
    large = np.minimum(large, REL_BUCKETS - 1)
    bucket = np.where(d < max_exact, d, large)
    return np.where(in_window, bucket, -1).astype(np.int32)


def _bias_kernel(rb_ref, bucket_ref, o_ref):
    bucket = bucket_ref[...]
    col = lax.broadcasted_iota(jnp.int32, bucket.shape, 1)
    for h in range(ATTN_Q_HEADS):
        acc = jnp.full(bucket.shape, NEG, F32)
        for b in range(REL_BUCKETS):
            acc = jnp.where(bucket == b, rb_ref[b, h], acc)
        o_ref[1, h] = acc
        o_ref[0, h] = jnp.where(col >= CHUNK, acc, NEG)


def _bias_table(rel_bias):
    bucket = jnp.asarray(_t5_bucket_table())
    return pl.pallas_call(
        _bias_kernel,
        out_shape=jax.ShapeDtypeStruct((2, ATTN_Q_HEADS, CHUNK, 2 * CHUNK), F32),
        in_specs=[pl.BlockSpec(memory_space=pltpu.SMEM),
                  pl.BlockSpec(memory_space=pltpu.VMEM)],
        out_specs=pl.BlockSpec(memory_space=pltpu.VMEM),
        name="bias_table",
    )(rel_bias, bucket)


def _inproj_kernel(x_ref, g_ref, w_ref, z_ref, xbc_ref, q_ref, kv_ref, dt_ref, wb_ref):
    @pl.when(pl.program_id(0) == 0)
    def _():
        o2 = D_SSD + D_CONV
        o3 = o2 + SSD_HEADS
        n_qkv = D_ATTN + 2 * D_KV
        wb_ref[:, 0:o2] = w_ref[0, :, 0:o2].astype(BF16)
        wb_ref[:, o2:o2 + n_qkv] = w_ref[0, :, o3:o3 + n_qkv].astype(BF16)
        wb_ref[:, o2 + n_qkv:] = jnp.concatenate(
            [w_ref[0, :, o2:o3], jnp.zeros((D_MODEL, LANES - SSD_HEADS), F32)], axis=1).astype(BF16)

    u = _rms(x_ref[...], g_ref[...]).astype(BF16)
    proj = _dot(u, wb_ref[...])
    o = 0
    for ref in (z_ref, xbc_ref, q_ref, kv_ref, dt_ref):
        w = ref.shape[1]
        ref[...] = proj[:, o:o + w]
        o += w


def _inproj(x2, g, w_in, layer):
    t = x2.shape[0]
    widths = (D_SSD, D_CONV, D_ATTN, 2 * D_KV, LANES)
    return pl.pallas_call(
        _inproj_kernel,
        grid=(t // TM_PROJ,),
        in_specs=[pl.BlockSpec((TM_PROJ, D_MODEL), lambda i: (i, 0)),
                  pl.BlockSpec((1, D_MODEL), lambda i: (0, 0)),
                  pl.BlockSpec((1,) + w_in.shape[1:], lambda i: (layer, 0, 0))],
        out_specs=[pl.BlockSpec((TM_PROJ, w), lambda i: (i, 0)) for w in widths],
        out_shape=[jax.ShapeDtypeStruct((t, w), F32) for w in widths],
        scratch_shapes=[pltpu.VMEM((D_MODEL, D_PROJ_PACKED), BF16)],
        compiler_params=_params(1),
        name="inproj",
    )(x2, g, w_in)


def _mixer_kernel(sink_ref, z_ref, xbc_ref, dt_ref, q_ref, kv_ref, kvp_ref, cw_ref, cb_ref, dtb_ref,
                  alog_ref, dskip_ref, gs_ref, expand_ref, bias_ref, ga_ref, o_ref, state_ref, xpad_ref):
    @pl.when(pl.program_id(1) == 0)
    def _():
        state_ref[...] = jnp.zeros(state_ref.shape, F32)
        xpad_ref[...] = jnp.zeros(xpad_ref.shape, F32)

    for s in range(z_ref.shape[0]):
        y_ssd = _ssd_chunk(z_ref.at[s], xbc_ref.at[s], dt_ref.at[s], cw_ref, cb_ref, dtb_ref, alog_ref,
                           dskip_ref, gs_ref, expand_ref, state_ref.at[s], xpad_ref.at[s])
        y_attn = _swa_block(sink_ref, q_ref.at[s], kv_ref.at[s], kvp_ref.at[s], bias_ref, ga_ref)
        o_ref[s] = jnp.concatenate([y_ssd, y_attn], axis=1)


def _ssd_chunk(z_ref, xbc_ref, dt_ref, cw_ref, cb_ref, dtb_ref, alog_ref, dskip_ref, g_ref,
               expand_ref, state_ref, xpad_ref):
    L = CHUNK
    G, R, P, N = SSD_GROUPS, SSD_HEADS_PER_GROUP, SSD_HEAD_DIM, D_STATE
    GW = R * P

    x_cur = xbc_ref[...]
    xpad = jnp.concatenate([xpad_ref[...], x_cur], axis=0)
    xpad_ref[...] = x_cur[L - SUBLANES:, :]
    acc = cb_ref[...] + cw_ref[CONV_WIDTH - 1:CONV_WIDTH, :] * x_cur
    for d in range(1, CONV_WIDTH):
        k = CONV_WIDTH - 1 - d
        acc = acc + cw_ref[k:k + 1, :] * pltpu.roll(xpad, d, axis=0)[SUBLANES:, :]
    xbc = _silu(acc)
    xs = xbc[:, :D_SSD]
    bm = xbc[:, D_SSD:D_SSD + G * N]
    cm = xbc[:, D_SSD + G * N:]

    dtr = dt_ref[...] + dtb_ref[...]
    dt = jnp.maximum(dtr, 0.0) + jnp.log(1.0 + jnp.exp(-jnp.abs(dtr)))
    a_dt = dt * (-jnp.exp(alog_ref[...]))
    ri = lax.broadcasted_iota(jnp.int32, (L, L), 0)
    ci = lax.broadcasted_iota(jnp.int32, (L, L), 1)
    causal = ci <= ri
    a_cum = _dot_f32_by_mask(causal.astype(BF16), a_dt, mask_on_left=True)
    a_cum_t = _dot_f32_by_mask((ri <= ci).astype(BF16), a_dt.T)
    a_last = a_cum[L - 1:L, :]
    stack = jnp.concatenate(
        [dt, jnp.exp(a_cum), jnp.exp(a_last - a_cum),
         jnp.broadcast_to(jnp.exp(a_last), (SUBLANES, LANES))], axis=0)
    ex = _dot_f32_by_mask(expand_ref[...], stack)
    dt_x, ea_x, dte_x, cd_x = ex[0:L], ex[L:2 * L], ex[2 * L:3 * L], ex[3 * L:3 * L + 1]
    xdt = xs * dt_x

    ys = []
    for g in range(G):
        bm_g = bm[:, g * N:(g + 1) * N]
        cm_g = cm[:, g * N:(g + 1) * N].astype(BF16)
        cb = lax.dot_general(cm_g, bm_g.astype(BF16), (((1,), (1,)), ((), ())),
                             preferred_element_type=F32)
        xdt_g = xdt[:, g * GW:(g + 1) * GW]
        yd = []
        for r in range(R):
            h = g * R + r
            seg = a_cum[:, h:h + 1] - a_cum_t[h:h + 1, :]
            dec = jnp.exp(jnp.where(causal, seg, NEG))
            yd.append(_dot((cb * dec).astype(BF16), xdt_g[:, r * P:(r + 1) * P].astype(BF16)))
        y_diag = jnp.concatenate(yd, axis=1)
        st = state_ref[g]
        y_off = _dot(cm_g, st.astype(BF16)) * ea_x[:, g * GW:(g + 1) * GW]
        new = _dot(bm_g.T.astype(BF16), (xdt_g * dte_x[:, g * GW:(g + 1) * GW]).astype(BF16))
        state_ref[g] = st * cd_x[:, g * GW:(g + 1) * GW] + new
        ys.append(y_diag + y_off + xs[:, g * GW:(g + 1) * GW] * dskip_ref[:, g * GW:(g + 1) * GW])
    y = jnp.concatenate(ys, axis=1)
    return _rms(y * _silu(z_ref[...]), g_ref[...])


def _mixer(z, xbc, dt, q, kv, conv_w, conv_b, dt_bias, a_log, d_skip_x, ssd_norm_g, sinks, bias,
           attn_norm_g):
    bsz, seq, _ = z.shape
    nc = seq // CHUNK
    expand = np.zeros((LANES, D_SSD), np.float32)
    for h in range(SSD_HEADS):
        expand[h, h * SSD_HEAD_DIM:(h + 1) * SSD_HEAD_DIM] = 1.0
    ns = SEQS_PER_STEP
    row = lambda b, c: (b, c, 0)
    prev = lambda b, c: (b, jnp.maximum(c - 1, 0), 0)
    fixed = lambda b, c: (0, 0)
    return pl.pallas_call(
        _mixer_kernel,
        grid=(bsz // ns, nc),
        in_specs=[pl.BlockSpec(memory_space=pltpu.SMEM),
                  pl.BlockSpec((ns, CHUNK, D_SSD), row),
                  pl.BlockSpec((ns, CHUNK, D_CONV), row),
                  pl.BlockSpec((ns, CHUNK, LANES), row),
                  pl.BlockSpec((ns, CHUNK, D_ATTN), row),
                  pl.BlockSpec((ns, CHUNK, 2 * D_KV), row),
                  pl.BlockSpec((ns, CHUNK, 2 * D_KV), prev),
                  pl.BlockSpec((CONV_WIDTH, D_CONV), fixed),
                  pl.BlockSpec((1, D_CONV), fixed),
                  pl.BlockSpec((1, LANES), fixed),
                  pl.BlockSpec((1, LANES), fixed),
                  pl.BlockSpec((1, D_SSD), fixed),
                  pl.BlockSpec((1, D_SSD), fixed),
                  pl.BlockSpec((LANES, D_SSD), fixed),
                  pl.BlockSpec((2, ATTN_Q_HEADS, CHUNK, 2 * CHUNK), lambda b, c: (0, 0, 0, 0)),
                  pl.BlockSpec((1, D_ATTN), fixed)],
        out_specs=pl.BlockSpec((ns, CHUNK, D_SSD + D_ATTN), row),
        out_shape=jax.ShapeDtypeStruct((bsz, seq, D_SSD + D_ATTN), F32),
        scratch_shapes=[pltpu.VMEM((ns, SSD_GROUPS, D_STATE, SSD_HEADS_PER_GROUP * SSD_HEAD_DIM), F32),
                        pltpu.VMEM((ns, SUBLANES, D_CONV), F32)],
        compiler_params=_params(2),
        name="mixer",
    )(sinks, z, xbc, dt, q, kv, kv, conv_w, conv_b, dt_bias, a_log, d_skip_x, ssd_norm_g,
      jnp.asarray(expand, dtype=BF16), bias, attn_norm_g)


def _swa_block(sink_ref, q_ref, kv_ref, kvp_ref, bias_ref, g_ref):
    L, Dh = CHUNK, ATTN_HEAD_DIM
    q = q_ref[...] * (1.0 / math.sqrt(Dh))
    kv = kv_ref[...]
    kvp = kvp_ref[...]
    has_prev = jnp.minimum(pl.program_id(1), 1)
    outs = []
    for hk in range(ATTN_KV_HEADS):
        ks = slice(hk * Dh, (hk + 1) * Dh)
        vs = slice(D_KV + hk * Dh, D_KV + (hk + 1) * Dh)
        kc = jnp.concatenate([kvp[:, ks], kv[:, ks]], axis=0).astype(BF16)
        vc = jnp.concatenate([kvp[:, vs], kv[:, vs]], axis=0).astype(BF16)
        for g in range(ATTN_Q_PER_KV):
            h = hk * ATTN_Q_PER_KV + g
            qh = q[:, h * Dh:(h + 1) * Dh].astype(BF16)
            s = lax.dot_general(qh, kc, (((1,), (1,)), ((), ())), preferred_element_type=F32)
            s = s + bias_ref[has_prev, h]
            sink = sink_ref[h]
            m = jnp.maximum(jnp.max(s, axis=-1, keepdims=True), sink)
            p = jnp.exp(s - m)
            denom = jnp.sum(p, axis=-1, keepdims=True) + jnp.exp(sink - m)
            outs.append(_dot(p.astype(BF16), vc) / denom)
    return _rms(jnp.concatenate(outs, axis=1), g_ref[...])


def _route_kernel(x_ref, y_ref, wo_ref, g_ref, wr_ref, br_ref,
                  h_ref, u_ref, slab_ref, post_ref, tab_ref, run_ref, wob_ref):
    tm = x_ref.shape[0]

    @pl.when(pl.program_id(0) == 0)
    def _():
        run_ref[...] = jnp.zeros(run_ref.shape, F32)
        wob_ref[...] = wo_ref[0].astype(BF16)

    h = x_ref[...] + _dot(y_ref[...].astype(BF16), wob_ref[...])
    h_ref[...] = h
    u = _rms(h, g_ref[...])
    ub = u.astype(BF16)
    u_ref[...] = ub
    E = N_EXPERTS
    logits = lax.dot_general(wr_ref[...], ub, (((1,), (1,)), ((), ())), preferred_element_type=F32)
    logits = logits + jnp.concatenate([br_ref[...]] * (tm // LANES), axis=1)

    eidx = lax.broadcasted_iota(jnp.int32, (E, tm), 0).astype(F32)
    vals, idxs = [], []
    cur = logits
    for _ in range(TOP_K):
        m = jnp.max(cur, axis=0, keepdims=True)
        ix = jnp.min(jnp.where(cur == m, eidx, float(E)), axis=0, keepdims=True)
        vals.append(m)
        idxs.append(ix)
        cur = jnp.where(eidx == ix, NEG, cur)
    es = [jnp.exp(v - vals[0]) for v in vals]
    den = es[0] + es[1] + es[2] + es[3]

    onehot = jnp.zeros((E, tm), F32)
    for ix in idxs:
        onehot = onehot + (eidx == ix).astype(F32)
    ri = lax.broadcasted_iota(jnp.int32, (tm, tm), 0)
    ci = lax.broadcasted_iota(jnp.int32, (tm, tm), 1)
    before = _dot(onehot.astype(BF16), (ri < ci).astype(BF16))
    cnt = jnp.sum(onehot, axis=1, keepdims=True)
    cnt8 = jnp.floor((cnt + (SUBLANES - 1)) * (1.0 / SUBLANES)) * SUBLANES
    el = lax.broadcasted_iota(jnp.int32, (E, E), 0)
    ec = lax.broadcasted_iota(jnp.int32, (E, E), 1)
    seg_start = _dot_f32_by_mask((ec < el).astype(BF16), jnp.broadcast_to(cnt8, (E, LANES)),
                                 mask_on_left=True)[:, 0:1]
    where_to = before + seg_start

    rows = [jnp.sum(jnp.where(eidx == idxs[k], where_to, 0.0), axis=0, keepdims=True)
            for k in range(TOP_K)]
    rows += [es[k] / den for k in range(TOP_K)]
    stack = jnp.concatenate(rows, axis=0)
    post_ref[...] = stack.astype(jnp.int32)
    slab_ref[...] = jnp.concatenate([stack, jnp.zeros((LANES - 2 * TOP_K, tm), F32)], axis=0).T

    lane = lax.broadcasted_iota(jnp.int32, (E, LANES), 1)
    tab = jnp.where(lane == 0, cnt8, jnp.where(lane == 1, run_ref[...], jnp.where(lane == 2, seg_start, 0.0)))
    tab_ref[...] = tab.astype(jnp.int32)
    run_ref[...] = run_ref[...] + cnt8


def _route(x2, y_mix, w_out, layer, g, wr, br):
    t = x2.shape[0]
    tm = TM_ROUTE
    row = lambda i: (i, 0)
    fixed = lambda i: (0, 0)
    return pl.pallas_call(
        _route_kernel,
        grid=(t // tm,),
        in_specs=[pl.BlockSpec((tm, D_MODEL), row),
                  pl.BlockSpec((tm, D_SSD + D_ATTN), row),
                  pl.BlockSpec((1, D_SSD + D_ATTN, D_MODEL), lambda i: (layer, 0, 0)),
                  pl.BlockSpec((1, D_MODEL), fixed),
                  pl.BlockSpec((N_EXPERTS, D_MODEL), fixed),
                  pl.BlockSpec((N_EXPERTS, LANES), fixed)],
        out_specs=[pl.BlockSpec((tm, D_MODEL), row),
                   pl.BlockSpec((tm, D_MODEL), row),
                   pl.BlockSpec((tm, LANES), row),
                   pl.BlockSpec((SUBLANES, tm), lambda i: (0, i)),
                   pl.BlockSpec((N_EXPERTS, LANES), row)],
        out_shape=[jax.ShapeDtypeStruct((t, D_MODEL), F32),
                   jax.ShapeDtypeStruct((t, D_MODEL), BF16),
                   jax.ShapeDtypeStruct((t, LANES), F32),
                   jax.ShapeDtypeStruct((SUBLANES, t), jnp.int32),
                   jax.ShapeDtypeStruct((t // tm * N_EXPERTS, LANES), jnp.int32)],
        scratch_shapes=[pltpu.VMEM((N_EXPERTS, LANES), F32),
                        pltpu.VMEM((D_SSD + D_ATTN, D_MODEL), BF16)],
        compiler_params=_params(1),
        name="route",
    )(x2, y_mix, w_out, g, wr, br)


SEG_SIZE_BITS = (TM_ROUTE // SUBLANES).bit_length()


TILE_SIZE_BITS = (R_SORTED // SUBLANES).bit_length()


def _piece_tables(cnt8, segst, segdst):
    n = (cnt8 // SUBLANES)[:, None, :]
    b = jnp.arange(SEG_SIZE_BITS, dtype=jnp.int32)[None, :, None]
    has = (n >> b) & 1
    off = ((n >> (b + 1)) << (b + 1)) * SUBLANES
    rank = jnp.cumsum(has, axis=2) - has
    place = (has[..., None] == 1) & (rank[..., None] == jnp.arange(N_EXPERTS, dtype=jnp.int32))
    dense = lambda v: jnp.sum(jnp.where(place, v[..., None], 0), axis=2).reshape(-1)
    return (jnp.sum(has, axis=2).reshape(-1), dense(segst[:, None, :] + off),
            dense(segdst[:, None, :] + off), jnp.sum(cnt8, axis=1) // SUBLANES)


def _for_each_piece(tile, pcnt_ref, psrc_ref, pdst_ref, fn):
    for b in range(SEG_SIZE_BITS):
        base = (tile * SEG_SIZE_BITS + b) * N_EXPERTS

        def body(p, carry, b=b, base=base):
            fn(SUBLANES << b, psrc_ref[base + p], pdst_ref[base + p])
            return carry

        lax.fori_loop(0, pcnt_ref[tile * SEG_SIZE_BITS + b], body, 0)


def _wait_rows(n_tiles8, descriptor):
    for b in range(TILE_SIZE_BITS):
        @pl.when(((n_tiles8 >> b) & 1) == 1)
        def _(b=b):
            descriptor(SUBLANES << b).wait()


def _rows(ref, row, n):
    if not isinstance(row, int):
        row = pl.multiple_of(row, SUBLANES)
    return ref.at[pl.ds(row, n), :]


def _dispatch_kernel(pcnt_ref, psrc_ref, pdst_ref, ntot_ref, zflag_ref, u_ref, post_ref, xs_ref,
                     buf_ref, zero_ref, sems, zsem):
    tm = u_ref.shape[0]
    i = pl.program_id(0)
    slot = i % 2
    bm = zero_ref.shape[0]

    def fill(b, flag):
        return pltpu.make_async_copy(zero_ref, xs_ref.at[pl.ds(b * bm, bm), :], zsem.at[flag - 1])

    def for_flagged(flag, action):
        def body(b, carry):
            @pl.when(zflag_ref[b] == flag)
            def _():
                action(fill(b, flag))
            return carry

        lax.fori_loop(0, zflag_ref.shape[0], body, 0)

    @pl.when(i == 0)
    def _():
        zero_ref[...] = jnp.zeros(zero_ref.shape, F32)
        for_flagged(1, lambda c: c.start())
        for_flagged(2, lambda c: c.start())
        for_flagged(1, lambda c: c.wait())

    @pl.when(i == pl.num_programs(0) - 1)
    def _():
        for_flagged(2, lambda c: c.wait())

    pos = post_ref[...]
    j = lax.broadcasted_iota(jnp.int32, (R_SORTED, tm), 0)
    sel = jnp.zeros((R_SORTED, tm), F32)
    for k in range(TOP_K):
        sel = jnp.where(j == pos[k:k + 1, :], 1.0, sel)
    buf_ref[slot] = _dot(sel.astype(BF16), u_ref[...])

    def write(slot_, n, src_row, dst_row):
        return pltpu.make_async_copy(_rows(buf_ref.at[slot_], src_row, n),
                                     _rows(xs_ref, dst_row, n), sems.at[slot_])

    _for_each_piece(i, pcnt_ref, psrc_ref, pdst_ref, lambda n, s, d: write(slot, n, s, d).start())

    @pl.when(i > 0)
    def _():
        _wait_rows(ntot_ref[i - 1], lambda n: write(1 - slot, n, 0, 0))

    @pl.when(i == pl.num_programs(0) - 1)
    def _():
        _wait_rows(ntot_ref[i], lambda n: write(slot, n, 0, 0))


def _dispatch(pieces, zero_flag, u, post):
    t = u.shape[0]
    tm = TM_ROUTE
    n_rows = zero_flag.shape[0] * EXPERT_ROW_STEP
    return pl.pallas_call(
        _dispatch_kernel,
        grid_spec=pltpu.PrefetchScalarGridSpec(
            num_scalar_prefetch=5,
            grid=(t // tm,),
            in_specs=[pl.BlockSpec((tm, D_MODEL), lambda i, *_: (i, 0)),
                      pl.BlockSpec((SUBLANES, tm), lambda i, *_: (0, i))],
            out_specs=pl.BlockSpec(memory_space=pl.ANY),
            scratch_shapes=[pltpu.VMEM((2, R_SORTED, D_MODEL), F32),
                            pltpu.VMEM((EXPERT_ROW_STEP, D_MODEL), F32),
                            pltpu.SemaphoreType.DMA((2,)),
                            pltpu.SemaphoreType.DMA((2,))]),
        out_shape=jax.ShapeDtypeStruct((n_rows, D_MODEL), F32),
        compiler_params=_params(1),
        name="dispatch",
    )(*pieces, zero_flag, u, post)


def _expert_kernel(be_ref, nu_ref, first_ref, next_ref, rows_ref, xs_ref, w1_ref, b1_ref, w2_ref, b2_ref,
                   ys_ref, w1f_ref, w2f_ref, w1b_ref, w2b_ref, sems):
    i = pl.program_id(0)

    def fetch(e):
        return (pltpu.make_async_copy(w1_ref.at[e], w1f_ref, sems.at[0]),
                pltpu.make_async_copy(w2_ref.at[e], w2f_ref, sems.at[1]))

    @pl.when(i >= nu_ref[0])
    def _():
        ys_ref[...] = jnp.zeros(ys_ref.shape, F32)

    @pl.when(i < nu_ref[0])
    def _():
        e = be_ref[i]

        @pl.when(i == 0)
        def _():
            for c in fetch(e):
                c.start()

        @pl.when(first_ref[i] != 0)
        def _():
            for c in fetch(e):
                c.wait()
            w1b_ref[...] = w1f_ref[...].astype(BF16)
            w2b_ref[...] = w2f_ref[...].astype(BF16)

            @pl.when(next_ref[i] >= 0)
            def _():
                for c in fetch(next_ref[i]):
                    c.start()

        bm = xs_ref.shape[0]
        for m in range(EXPERT_ROW_STEP, bm + 1, EXPERT_ROW_STEP):
            @pl.when(rows_ref[i] == m)
            def _(m=m):
                hid = _dot(xs_ref[0:m, :].astype(BF16), w1b_ref[...]) + b1_ref[e]
                x_glu = jnp.minimum(hid[:, :D_EXPERT], SWIGLU_LIMIT)
                x_lin = jnp.clip(hid[:, D_EXPERT:], -SWIGLU_LIMIT, SWIGLU_LIMIT)
                act = x_glu / (1.0 + jnp.exp(-SWIGLU_ALPHA * x_glu)) * (x_lin + 1.0)
                ys_ref[0:m, :] = _dot(act.astype(BF16), w2b_ref[...]) + b2_ref[e]
                if m < bm:
                    ys_ref[m:, :] = jnp.zeros((bm - m, D_MODEL), F32)


def _experts(blk_expert, n_used, first_flag, next_expert, blk_rows, xs, w1, b1, w2, b2):
    n_rows = xs.shape[0]
    bm = BM_EXPERT
    nblk = n_rows // bm
    last = lambda i, nu: jnp.maximum(jnp.minimum(i, nu[0] - 1), 0)
    row = lambda i, be, nu, *_: (last(i, nu), 0)
    whole = lambda i, *_: (0, 0, 0)
    return pl.pallas_call(
        _expert_kernel,
        grid_spec=pltpu.PrefetchScalarGridSpec(
            num_scalar_prefetch=5,
            grid=(nblk,),
            in_specs=[pl.BlockSpec((bm, D_MODEL), row),
                      pl.BlockSpec(memory_space=pl.ANY),
                      pl.BlockSpec((N_EXPERTS, 1, 2 * D_EXPERT), whole),
                      pl.BlockSpec(memory_space=pl.ANY),
                      pl.BlockSpec((N_EXPERTS, 1, D_MODEL), whole)],
            out_specs=pl.BlockSpec((bm, D_MODEL), lambda i, *_: (i, 0)),
            scratch_shapes=[pltpu.VMEM((D_MODEL, 2 * D_EXPERT), F32),
                            pltpu.VMEM((D_EXPERT, D_MODEL), F32),
                            pltpu.VMEM((D_MODEL, 2 * D_EXPERT), BF16),
                            pltpu.VMEM((D_EXPERT, D_MODEL), BF16),
                            pltpu.SemaphoreType.DMA((2,))]),
        out_shape=jax.ShapeDtypeStruct((n_rows, D_MODEL), F32),
        compiler_params=_params(1),
        name="experts",
    )(blk_expert, n_used, first_flag, next_expert, blk_rows, xs, w1, b1, w2, b2)


def _combine_kernel(pcnt_ref, psrc_ref, pdst_ref, ntot_ref, ys_ref, h_ref, slab_ref, g_ref, o_ref,
                    buf_ref, sems):
    tm = h_ref.shape[0]
    i = pl.program_id(0)
    slot = i % 2

    def read(slot_, n, sorted_row, ys_row):
        return pltpu.make_async_copy(_rows(ys_ref, ys_row, n),
                                     _rows(buf_ref.at[slot_], sorted_row, n), sems.at[slot_])

    def gather(tile, slot_):
        _for_each_piece(tile, pcnt_ref, psrc_ref, pdst_ref, lambda n, s, d: read(slot_, n, s, d).start())

    @pl.when(i == 0)
    def _():
        buf_ref[...] = jnp.zeros(buf_ref.shape, F32)
        gather(0, 0)

    @pl.when(i + 1 < pl.num_programs(0))
    def _():
        gather(i + 1, 1 - slot)

    _wait_rows(ntot_ref[i], lambda n: read(slot, n, 0, 0))

    slab = slab_ref[...]
    lane = lax.broadcasted_iota(jnp.int32, (tm, R_SORTED), 1).astype(F32)
    wmat = jnp.zeros((tm, R_SORTED), F32)
    for k in range(TOP_K):
        wmat = jnp.where(lane == slab[:, k:k + 1], slab[:, TOP_K + k:TOP_K + k + 1], wmat)
    moe = _dot(wmat.astype(BF16), buf_ref[slot].astype(BF16))
    o_ref[...] = _rms(h_ref[...] + moe, g_ref[...])


def _combine(pieces, ys, h, slab, g):
    t = h.shape[0]
    tm = TM_ROUTE
    return pl.pallas_call(
        _combine_kernel,
        grid_spec=pltpu.PrefetchScalarGridSpec(
            num_scalar_prefetch=4,
            grid=(t // tm,),
            in_specs=[pl.BlockSpec(memory_space=pl.ANY),
                      pl.BlockSpec((tm, D_MODEL), lambda i, *_: (i, 0)),
                      pl.BlockSpec((tm, LANES), lambda i, *_: (i, 0)),
                      pl.BlockSpec((1, D_MODEL), lambda i, *_: (0, 0))],
            out_specs=pl.BlockSpec((tm, D_MODEL), lambda i, *_: (i, 0)),
            scratch_shapes=[pltpu.VMEM((2, R_SORTED, D_MODEL), F32),
                            pltpu.SemaphoreType.DMA((2,))]),
        out_shape=jax.ShapeDtypeStruct((t, D_MODEL), F32),
        compiler_params=_params(1),
        name="combine",
    )(*pieces, ys, h, slab, g)


def _pad_lanes(v, fill=0.0):
    return jnp.pad(v.reshape(1, -1), ((0, 0), (0, LANES - v.shape[-1])), constant_values=fill)


def kernel(x, norm_mix_g, w_in, conv_w, conv_b, dt_bias, a_log, d_skip, ssd_norm_g, attn_sinks,
           rel_bias, attn_norm_g, w_out, norm_ffn_g, w_router, b_router, w1, b1, w2, b2,
           norm_final_g):
    bsz, seq, d = x.shape
    t = bsz * seq
    nc = seq // CHUNK
    depth = w_in.shape[0]
    bias = _bias_table(rel_bias)

    assert depth == 1, "the final norm is fused into the combine kernel: single layer only"
    h = x.reshape(t, d)
    for layer in range(depth):
        z, xbc, q, kv, dt = [a.reshape(bsz, seq, -1)
                             for a in _inproj(h, norm_mix_g[layer].reshape(1, d), w_in, layer)]
        y_mix = _mixer(z, xbc, dt, q, kv, conv_w[layer], conv_b[layer].reshape(1, -1),
                       _pad_lanes(dt_bias[layer]), _pad_lanes(a_log[layer]),
                       jnp.repeat(d_skip[layer], SSD_HEAD_DIM).reshape(1, -1),
                       ssd_norm_g[layer].reshape(1, -1), attn_sinks[layer], bias,
                       attn_norm_g[layer].reshape(1, -1)).reshape(t, -1)

        wr = w_router[layer].T.astype(BF16)
        br = jnp.broadcast_to(b_router[layer][:, None], (N_EXPERTS, LANES))
        h_mid, u, slab, post, tab = _route(
            h, y_mix, w_out, layer, norm_ffn_g[layer].reshape(1, d), wr, br)

        bm = BM_EXPERT
        ntiles = t // TM_ROUTE
        nblk = (t * TOP_K + ntiles * N_EXPERTS * (SUBLANES - 1)) // bm + N_EXPERTS
        tab = tab.reshape(ntiles, N_EXPERTS, LANES)
        cnt8, off8, segst = tab[:, :, 0], tab[:, :, 1], tab[:, :, 2]
        total8 = off8[-1] + cnt8[-1]
        padded = ((total8 + bm - 1) // bm) * bm
        pend = jnp.cumsum(padded)
        segdst = (pend - padded)[None, :] + off8
        n_used = pend[-1:] // bm
        eids = jnp.arange(N_EXPERTS, dtype=jnp.int32)
        blk = jnp.arange(nblk, dtype=jnp.int32)
        blk_expert = jnp.minimum(
            jnp.sum((blk[:, None] * bm >= pend[None, :]).astype(jnp.int32), axis=1), N_EXPERTS - 1)
        e_end = jnp.sum(jnp.where(blk_expert[:, None] == eids[None, :],
                                  ((pend - padded) + total8)[None, :], 0), axis=1)
        valid = jnp.where(blk < n_used[0], jnp.clip(e_end - blk * bm, 0, bm), 0)
        blk_rows = ((valid + EXPERT_ROW_STEP - 1) // EXPERT_ROW_STEP) * EXPERT_ROW_STEP
        piece_lo = jnp.arange(bm // EXPERT_ROW_STEP, dtype=jnp.int32)[None, :] * EXPERT_ROW_STEP
        zero_flag = jnp.where(piece_lo >= valid[:, None], 2,
                              (piece_lo + EXPERT_ROW_STEP > valid[:, None]).astype(jnp.int32)).reshape(-1)
        before = jnp.concatenate([blk_expert[:1], blk_expert[:-1]])
        first_flag = ((blk == 0) | (before != blk_expert)).astype(jnp.int32)
        cand = jnp.where((eids[None, :] > eids[:, None]) & (padded[None, :] > 0), eids[None, :], N_EXPERTS)
        next_nonempty = jnp.min(cand, axis=1)
        next_nonempty = jnp.where(next_nonempty == N_EXPERTS, -1, next_nonempty)
        next_expert = jnp.sum(jnp.where(blk_expert[:, None] == eids[None, :], next_nonempty[None, :], 0), axis=1)
        pieces = _piece_tables(cnt8, segst, segdst)

        xs = _dispatch(pieces, zero_flag, u, post)
        ys = _experts(blk_expert, n_used, first_flag, next_expert, blk_rows, xs, w1[layer],
                      b1[layer].reshape(N_EXPERTS, 1, -1), w2[layer], b2[layer].reshape(N_EXPERTS, 1, -1))
        h = _combine(pieces, ys, h_mid, slab, norm_final_g.reshape(1, d))
    return h.reshape(bsz, seq, d)
```

```python
import functools
import math

import numpy as np
import jax
import jax.numpy as jnp
from jax import lax
from jax.experimental import pallas as pl
from jax.experimental.pallas import tpu as pltpu

F32 = jnp.float32
BF16 = jnp.bfloat16

D_MODEL = 1024
SSD_HEADS = 8
SSD_HEAD_DIM = 64
D_SSD = SSD_HEADS * SSD_HEAD_DIM
SSD_GROUPS = 2
SSD_HEADS_PER_GROUP = SSD_HEADS // SSD_GROUPS
D_STATE = 128
CONV_WIDTH = 4
CHUNK = 128
D_CONV = D_SSD + 2 * SSD_GROUPS * D_STATE
ATTN_Q_HEADS = 8
ATTN_KV_HEADS = 2
ATTN_Q_PER_KV = ATTN_Q_HEADS // ATTN_KV_HEADS
ATTN_HEAD_DIM = 64
D_ATTN = ATTN_Q_HEADS * ATTN_HEAD_DIM
D_KV = ATTN_KV_HEADS * ATTN_HEAD_DIM
WINDOW = 128
REL_BUCKETS = 32
REL_MAX_DIST = 128
N_EXPERTS = 32
TOP_K = 4
D_EXPERT = D_MODEL
SWIGLU_LIMIT = 7.0
SWIGLU_ALPHA = 1.702
RMS_EPS = 1e-5

LANES = 128
SUBLANES = 8
NEG = -1e30
VMEM_LIMIT = 56 * 1024 * 1024

TM_PROJ = 512
TM_ROUTE = 512
BM_EXPERT = 512
EXPERT_ROW_STEP = 128
SEQS_PER_STEP = 2
CHUNKS_PER_STEP = 2
R_SORTED = TM_ROUTE * TOP_K + N_EXPERTS * SUBLANES

D_PROJ_PACKED = D_SSD + D_CONV + D_ATTN + 2 * D_KV + LANES


def _params(n_axes):
    return pltpu.CompilerParams(dimension_semantics=("arbitrary",) * n_axes,
                                vmem_limit_bytes=VMEM_LIMIT)


def _rms(x, g):
    return x * lax.rsqrt(jnp.mean(x * x, axis=-1, keepdims=True) + RMS_EPS) * g


def _silu(x):
    return x / (1.0 + jnp.exp(-x))


def _dot(a, b, **kw):
    return jnp.dot(a, b, preferred_element_type=F32, **kw)


def _dot_f32_by_mask(mask, x, mask_on_left=False):
    hi = x.astype(BF16)
    r1 = x - hi.astype(F32)
    mid = r1.astype(BF16)
    lo = (r1 - mid.astype(F32)).astype(BF16)
    out = None
    for piece in (hi, mid, lo):
        term = _dot(mask, piece) if mask_on_left else _dot(piece, mask)
        out = term if out is None else out + term
    return out


def _t5_bucket_table():
    dist = CHUNK + np.arange(CHUNK)[:, None] - np.arange(2 * CHUNK)[None, :]
    in_window = (dist >= 0) & (dist < WINDOW)
    d = np.clip(dist, 0, REL_MAX_DIST)
    max_exact = REL_BUCKETS // 2
    large = max_exact + (np.log(np.maximum(d, 1).astype(np.float32) / max_exact)
                         / math.log(REL_MAX_DIST / max_exact)
                         * (REL_BUCKETS - max_exact)).astype(np.int32)
    large = np.minimum(large, REL_BUCKETS - 1)
    bucket = np.where(d < max_exact, d, large)
    return np.where(in_window, bucket, -1).astype(np.int32)


def _bias_kernel(rb_ref, bucket_ref, o_ref):
    bucket = bucket_ref[...]
    col = lax.broadcasted_iota(jnp.int32, bucket.shape, 1)
    for h in range(ATTN_Q_HEADS):
        acc = jnp.full(bucket.shape, NEG, F32)
        for b in range(REL_BUCKETS):
            acc = jnp.where(bucket == b, rb_ref[b, h], acc)
        o_ref[1, h] = acc
        o_ref[0, h] = jnp.where(col >= CHUNK, acc, NEG)


def _bias_table(rel_bias):
    bucket = jnp.asarray(_t5_bucket_table())
    return pl.pallas_call(
        _bias_kernel,
        out_shape=jax.ShapeDtypeStruct((2, ATTN_Q_HEADS, CHUNK, 2 * CHUNK), F32),
        in_specs=[pl.BlockSpec(memory_space=pltpu.SMEM),
                  pl.BlockSpec(memory_space=pltpu.VMEM)],
        out_specs=pl.BlockSpec(memory_space=pltpu.VMEM),
        name="bias_table",
    )(rel_bias, bucket)


def _mixer_kernel(sink_ref, x_ref, gm_ref, w_ref, cw_ref, cb_ref, dtb_ref, alog_ref, dskip_ref, gs_ref,
                  expand_ref, bias_ref, ga_ref, o_ref, wb_ref, state_ref, xpad_ref, kvp_ref):
    j = pl.program_id(1)

    @pl.when((pl.program_id(0) == 0) & (j == 0))
    def _():
        o2 = D_SSD + D_CONV
        o3 = o2 + SSD_HEADS
        n_qkv = D_ATTN + 2 * D_KV
        wb_ref[:, 0:o2] = w_ref[0, :, 0:o2].astype(BF16)
        wb_ref[:, o2:o2 + n_qkv] = w_ref[0, :, o3:o3 + n_qkv].astype(BF16)
        wb_ref[:, o2 + n_qkv:] = jnp.concatenate(
            [w_ref[0, :, o2:o3], jnp.zeros((D_MODEL, LANES - SSD_HEADS), F32)], axis=1).astype(BF16)

    @pl.when(j == 0)
    def _():
        state_ref[...] = jnp.zeros(state_ref.shape, F32)
        xpad_ref[...] = jnp.zeros(xpad_ref.shape, F32)
        kvp_ref[...] = jnp.zeros(kvp_ref.shape, F32)

    ns, rows, _ = x_ref.shape
    bounds = np.cumsum([0, D_SSD, D_CONV, D_ATTN, 2 * D_KV, LANES])
    pieces = [(c, s) for c in range(rows // CHUNK) for s in range(ns)]

    def project(c, s):
        x = x_ref[s, c * CHUNK:(c + 1) * CHUNK, :]
        return _dot(_rms(x, gm_ref[...]).astype(BF16), wb_ref[...])

    proj = project(*pieces[0])
    for k, (c, s) in enumerate(pieces):
        proj_next = project(*pieces[k + 1]) if k + 1 < len(pieces) else None
        z, xbc, q, kv, dt = [proj[:, a:b] for a, b in zip(bounds[:-1], bounds[1:])]
        has_prev = jnp.minimum(j, 1) if c == 0 else 1
        y_ssd = _ssd_chunk(z, xbc, dt, cw_ref, cb_ref, dtb_ref, alog_ref, dskip_ref, gs_ref,
                           expand_ref, state_ref.at[s], xpad_ref.at[s])
        y_attn = _swa_block(sink_ref, q, kv, kvp_ref[s], has_prev, bias_ref, ga_ref)
        kvp_ref[s] = kv
        o_ref[s, c * CHUNK:(c + 1) * CHUNK, :] = jnp.concatenate([y_ssd, y_attn], axis=1)
        proj = proj_next


def _ssd_chunk(z, x_cur, dt_raw, cw_ref, cb_ref, dtb_ref, alog_ref, dskip_ref, g_ref,
               expand_ref, state_ref, xpad_ref):
    L = CHUNK
    G, R, P, N = SSD_GROUPS, SSD_HEADS_PER_GROUP, SSD_HEAD_DIM, D_STATE
    GW = R * P

    xpad = jnp.concatenate([xpad_ref[...], x_cur], axis=0)
    xpad_ref[...] = x_cur[L - SUBLANES:, :]
    acc = cb_ref[...] + cw_ref[CONV_WIDTH - 1:CONV_WIDTH, :] * x_cur
    for d in range(1, CONV_WIDTH):
        k = CONV_WIDTH - 1 - d
        acc = acc + cw_ref[k:k + 1, :] * pltpu.roll(xpad, d, axis=0)[SUBLANES:, :]
    xbc = _silu(acc)
    xs = xbc[:, :D_SSD]
    bm = xbc[:, D_SSD:D_SSD + G * N]
    cm = xbc[:, D_SSD + G * N:]

    dtr = dt_raw + dtb_ref[...]
    dt = jnp.maximum(dtr, 0.0) + jnp.log(1.0 + jnp.exp(-jnp.abs(dtr)))
    a_dt = dt * (-jnp.exp(alog_ref[...]))
    ri = lax.broadcasted_iota(jnp.int32, (L, L), 0)
    ci = lax.broadcasted_iota(jnp.int32, (L, L), 1)
    causal = ci <= ri
    a_cum = _dot_f32_by_mask(causal.astype(BF16), a_dt, mask_on_left=True)
    a_cum_t = _dot_f32_by_mask((ri <= ci).astype(BF16), a_dt.T)
    a_last = a_cum[L - 1:L, :]
    stack = jnp.concatenate(
        [dt, jnp.exp(a_cum), jnp.exp(a_last - a_cum),
         jnp.broadcast_to(jnp.exp(a_last), (SUBLANES, LANES))], axis=0)
    ex = _dot_f32_by_mask(expand_ref[...], stack)
    dt_x, ea_x, dte_x, cd_x = ex[0:L], ex[L:2 * L], ex[2 * L:3 * L], ex[3 * L:3 * L + 1]
    xdt = xs * dt_x

    ys = []
    for g in range(G):
        bm_g = bm[:, g * N:(g + 1) * N]
        cm_g = cm[:, g * N:(g + 1) * N].astype(BF16)
        cb = lax.dot_general(cm_g, bm_g.astype(BF16), (((1,), (1,)), ((), ())),
                             preferred_element_type=F32)
        xdt_g = xdt[:, g * GW:(g + 1) * GW]
        yd = []
        for r in range(R):
            h = g * R + r
            seg = a_cum[:, h:h + 1] - a_cum_t[h:h + 1, :]
            dec = jnp.exp(jnp.where(causal, seg, NEG))
            yd.append(_dot((cb * dec).astype(BF16), xdt_g[:, r * P:(r + 1) * P].astype(BF16)))
        y_diag = jnp.concatenate(yd, axis=1)
        st = state_ref[g]
        y_off = _dot(cm_g, st.astype(BF16)) * ea_x[:, g * GW:(g + 1) * GW]
        new = _dot(bm_g.T.astype(BF16), (xdt_g * dte_x[:, g * GW:(g + 1) * GW]).astype(BF16))
        state_ref[g] = st * cd_x[:, g * GW:(g + 1) * GW] + new
        ys.append(y_diag + y_off + xs[:, g * GW:(g + 1) * GW] * dskip_ref[:, g * GW:(g + 1) * GW])
    y = jnp.concatenate(ys, axis=1)
    return _rms(y * _silu(z), g_ref[...])


def _mixer(x, norm_g, w_in, layer, conv_w, conv_b, dt_bias, a_log, d_skip_x, ssd_norm_g, sinks, bias,
           attn_norm_g):
    bsz, seq, _ = x.shape
    rows = CHUNKS_PER_STEP * CHUNK
    nsteps = seq // rows
    expand = np.zeros((LANES, D_SSD), np.float32)
    for h in range(SSD_HEADS):
        expand[h, h * SSD_HEAD_DIM:(h + 1) * SSD_HEAD_DIM] = 1.0
    ns = SEQS_PER_STEP
    row = lambda b, j: (b, j, 0)
    fixed = lambda b, j: (0, 0)
    return pl.pallas_call(
        _mixer_kernel,
        grid=(bsz // ns, nsteps),
        in_specs=[pl.BlockSpec(memory_space=pltpu.SMEM),
                  pl.BlockSpec((ns, rows, D_MODEL), row),
                  pl.BlockSpec((1, D_MODEL), fixed),
                  pl.BlockSpec((1,) + w_in.shape[1:], lambda b, j: (layer, 0, 0)),
                  pl.BlockSpec((CONV_WIDTH, D_CONV), fixed),
                  pl.BlockSpec((1, D_CONV), fixed),
                  pl.BlockSpec((1, LANES), fixed),
                  pl.BlockSpec((1, LANES), fixed),
                  pl.BlockSpec((1, D_SSD), fixed),
                  pl.BlockSpec((1, D_SSD), fixed),
                  pl.BlockSpec((LANES, D_SSD), fixed),
                  pl.BlockSpec((2, ATTN_Q_HEADS, CHUNK, 2 * CHUNK), lambda b, j: (0, 0, 0, 0)),
                  pl.BlockSpec((1, D_ATTN), fixed)],
        out_specs=pl.BlockSpec((ns, rows, D_SSD + D_ATTN), row),
        out_shape=jax.ShapeDtypeStruct((bsz, seq, D_SSD + D_ATTN), F32),
        scratch_shapes=[pltpu.VMEM((D_MODEL, D_PROJ_PACKED), BF16),
                        pltpu.VMEM((ns, SSD_GROUPS, D_STATE, SSD_HEADS_PER_GROUP * SSD_HEAD_DIM), F32),
                        pltpu.VMEM((ns, SUBLANES, D_CONV), F32),
                        pltpu.VMEM((ns, CHUNK, 2 * D_KV), F32)],
        compiler_params=_params(2),
        name="mixer",
    )(sinks, x, norm_g, w_in, conv_w, conv_b, dt_bias, a_log, d_skip_x, ssd_norm_g,
      jnp.asarray(expand, dtype=BF16), bias, attn_norm_g)


def _swa_block(sink_ref, q, kv, kvp, has_prev, bias_ref, g_ref):
    L, Dh = CHUNK, ATTN_HEAD_DIM
    q = q * (1.0 / math.sqrt(Dh))
    outs = []
    for hk in range(ATTN_KV_HEADS):
        ks = slice(hk * Dh, (hk + 1) * Dh)
        vs = slice(D_KV + hk * Dh, D_KV + (hk + 1) * Dh)
        kc = jnp.concatenate([kvp[:, ks], kv[:, ks]], axis=0).astype(BF16)
        vc = jnp.concatenate([kvp[:, vs], kv[:, vs]], axis=0).astype(BF16)
        for g in range(ATTN_Q_PER_KV):
            h = hk * ATTN_Q_PER_KV + g
            qh = q[:, h * Dh:(h + 1) * Dh].astype(BF16)
            s = lax.dot_general(qh, kc, (((1,), (1,)), ((), ())), preferred_element_type=F32)
            s = s + bias_ref[has_prev, h]
            sink = sink_ref[h]
            m = jnp.maximum(jnp.max(s, axis=-1, keepdims=True), sink)
            p = jnp.exp(s - m)
            denom = jnp.sum(p, axis=-1, keepdims=True) + jnp.exp(sink - m)
            outs.append(_dot(p.astype(BF16), vc) / denom)
    return _rms(jnp.concatenate(outs, axis=1), g_ref[...])


def _route_kernel(x_ref, y_ref, wo_ref, g_ref, wr_ref, br_ref,
                  h_ref, u_ref, slab_ref, post_ref, tab_ref, run_ref, wob_ref):
    tm = x_ref.shape[0]

    @pl.when(pl.program_id(0) == 0)
    def _():
        run_ref[...] = jnp.zeros(run_ref.shape, F32)
        wob_ref[...] = wo_ref[0].astype(BF16)

    h = x_ref[...] + _dot(y_ref[...].astype(BF16), wob_ref[...])
    h_ref[...] = h
    u = _rms(h, g_ref[...])
    ub = u.astype(BF16)
    u_ref[...] = ub
    E = N_EXPERTS
    logits = lax.dot_general(wr_ref[...], ub, (((1,), (1,)), ((), ())), preferred_element_type=F32)
    logits = logits + jnp.concatenate([br_ref[...]] * (tm // LANES), axis=1)

    eidx = lax.broadcasted_iota(jnp.int32, (E, tm), 0).astype(F32)
    vals, idxs = [], []
    cur = logits
    for _ in range(TOP_K):
        m = jnp.max(cur, axis=0, keepdims=True)
        ix = jnp.min(jnp.where(cur == m, eidx, float(E)), axis=0, keepdims=True)
        vals.append(m)
        idxs.append(ix)
        cur = jnp.where(eidx == ix, NEG, cur)
    es = [jnp.exp(v - vals[0]) for v in vals]
    den = es[0] + es[1] + es[2] + es[3]

    onehot = jnp.zeros((E, tm), F32)
    for ix in idxs:
        onehot = onehot + (eidx == ix).astype(F32)
    ri = lax.broadcasted_iota(jnp.int32, (tm, tm), 0)
    ci = lax.broadcasted_iota(jnp.int32, (tm, tm), 1)
    before = _dot(onehot.astype(BF16), (ri < ci).astype(BF16))
    cnt = jnp.sum(onehot, axis=1, keepdims=True)
    cnt8 = jnp.floor((cnt + (SUBLANES - 1)) * (1.0 / SUBLANES)) * SUBLANES
    el = lax.broadcasted_iota(jnp.int32, (E, E), 0)
    ec = lax.broadcasted_iota(jnp.int32, (E, E), 1)
    seg_start = _dot_f32_by_mask((ec < el).astype(BF16), jnp.broadcast_to(cnt8, (E, LANES)),
                                 mask_on_left=True)[:, 0:1]
    where_to = before + seg_start

    rows = [jnp.sum(jnp.where(eidx == idxs[k], where_to, 0.0), axis=0, keepdims=True)
            for k in range(TOP_K)]
    rows += [es[k] / den for k in range(TOP_K)]
    stack = jnp.concatenate(rows, axis=0)
    post_ref[...] = stack.astype(jnp.int32)
    slab_ref[...] = jnp.concatenate([stack, jnp.zeros((LANES - 2 * TOP_K, tm), F32)], axis=0).T

    lane = lax.broadcasted_iota(jnp.int32, (E, LANES), 1)
    tab = jnp.where(lane == 0, cnt8, jnp.where(lane == 1, run_ref[...], jnp.where(lane == 2, seg_start, 0.0)))
    tab_ref[...] = tab.astype(jnp.int32)
    run_ref[...] = run_ref[...] + cnt8


def _route(x2, y_mix, w_out, layer, g, wr, br):
    t = x2.shape[0]
    tm = TM_ROUTE
    row = lambda i: (i, 0)
    fixed = lambda i: (0, 0)
    return pl.pallas_call(
        _route_kernel,
        grid=(t // tm,),
        in_specs=[pl.BlockSpec((tm, D_MODEL), row),
                  pl.BlockSpec((tm, D_SSD + D_ATTN), row),
                  pl.BlockSpec((1, D_SSD + D_ATTN, D_MODEL), lambda i: (layer, 0, 0)),
                  pl.BlockSpec((1, D_MODEL), fixed),
                  pl.BlockSpec((N_EXPERTS, D_MODEL), fixed),
                  pl.BlockSpec((N_EXPERTS, LANES), fixed)],
        out_specs=[pl.BlockSpec((tm, D_MODEL), row),
                   pl.BlockSpec((tm, D_MODEL), row),
                   pl.BlockSpec((tm, LANES), row),
                   pl.BlockSpec((SUBLANES, tm), lambda i: (0, i)),
                   pl.BlockSpec((N_EXPERTS, LANES), row)],
        out_shape=[jax.ShapeDtypeStruct((t, D_MODEL), F32),
                   jax.ShapeDtypeStruct((t, D_MODEL), BF16),
                   jax.ShapeDtypeStruct((t, LANES), F32),
                   jax.ShapeDtypeStruct((SUBLANES, t), jnp.int32),
                   jax.ShapeDtypeStruct((t // tm * N_EXPERTS, LANES), jnp.int32)],
        scratch_shapes=[pltpu.VMEM((N_EXPERTS, LANES), F32),
                        pltpu.VMEM((D_SSD + D_ATTN, D_MODEL), BF16)],
        compiler_params=_params(1),
        name="route",
    )(x2, y_mix, w_out, g, wr, br)


SEG_SIZE_BITS = (TM_ROUTE // SUBLANES).bit_length()


TILE_SIZE_BITS = (R_SORTED // SUBLANES).bit_length()


def _piece_tables(cnt8, segst, segdst):
    n = (cnt8 // SUBLANES)[:, None, :]
    b = jnp.arange(SEG_SIZE_BITS, dtype=jnp.int32)[None, :, None]
    has = (n >> b) & 1
    off = ((n >> (b + 1)) << (b + 1)) * SUBLANES
    rank = jnp.cumsum(has, axis=2) - has
    place = (has[..., None] == 1) & (rank[..., None] == jnp.arange(N_EXPERTS, dtype=jnp.int32))
    dense = lambda v: jnp.sum(jnp.where(place, v[..., None], 0), axis=2).reshape(-1)
    return (jnp.sum(has, axis=2).reshape(-1), dense(segst[:, None, :] + off),
            dense(segdst[:, None, :] + off), jnp.sum(cnt8, axis=1) // SUBLANES)


def _for_each_piece(tile, pcnt_ref, psrc_ref, pdst_ref, fn):
    for b in range(SEG_SIZE_BITS):
        base = (tile * SEG_SIZE_BITS + b) * N_EXPERTS

        def body(p, carry, b=b, base=base):
            fn(SUBLANES << b, psrc_ref[base + p], pdst_ref[base + p])
            return carry

        lax.fori_loop(0, pcnt_ref[tile * SEG_SIZE_BITS + b], body, 0)


def _wait_rows(n_tiles8, descriptor):
    for b in range(TILE_SIZE_BITS):
        @pl.when(((n_tiles8 >> b) & 1) == 1)
        def _(b=b):
            descriptor(SUBLANES << b).wait()


def _rows(ref, row, n):
    if not isinstance(row, int):
        row = pl.multiple_of(row, SUBLANES)
    return ref.at[pl.ds(row, n), :]


def _dispatch_kernel(pcnt_ref, psrc_ref, pdst_ref, ntot_ref, zflag_ref, u_ref, post_ref, xs_ref,
                     buf_ref, zero_ref, sems, zsem):
    tm = u_ref.shape[0]
    i = pl.program_id(0)
    slot = i % 2
    bm = zero_ref.shape[0]

    def fill(b, flag):
        return pltpu.make_async_copy(zero_ref, xs_ref.at[pl.ds(b * bm, bm), :], zsem.at[flag - 1])

    def for_flagged(flag, action):
        def body(b, carry):
            @pl.when(zflag_ref[b] == flag)
            def _():
                action(fill(b, flag))
            return carry

        lax.fori_loop(0, zflag_ref.shape[0], body, 0)

    @pl.when(i == 0)
    def _():
        zero_ref[...] = jnp.zeros(zero_ref.shape, F32)
        for_flagged(1, lambda c: c.start())
        for_flagged(2, lambda c: c.start())
        for_flagged(1, lambda c: c.wait())

    @pl.when(i == pl.num_programs(0) - 1)
    def _():
        for_flagged(2, lambda c: c.wait())

    pos = post_ref[...]
    j = lax.broadcasted_iota(jnp.int32, (R_SORTED, tm), 0)
    sel = jnp.zeros((R_SORTED, tm), F32)
    for k in range(TOP_K):
        sel = jnp.where(j == pos[k:k + 1, :], 1.0, sel)
    buf_ref[slot] = _dot(sel.astype(BF16), u_ref[...])

    def write(slot_, n, src_row, dst_row):
        return pltpu.make_async_copy(_rows(buf_ref.at[slot_], src_row, n),
                                     _rows(xs_ref, dst_row, n), sems.at[slot_])

    _for_each_piece(i, pcnt_ref, psrc_ref, pdst_ref, lambda n, s, d: write(slot, n, s, d).start())

    @pl.when(i > 0)
    def _():
        _wait_rows(ntot_ref[i - 1], lambda n: write(1 - slot, n, 0, 0))

    @pl.when(i == pl.num_programs(0) - 1)
    def _():
        _wait_rows(ntot_ref[i], lambda n: write(slot, n, 0, 0))


def _dispatch(pieces, zero_flag, u, post):
    t = u.shape[0]
    tm = TM_ROUTE
    n_rows = zero_flag.shape[0] * EXPERT_ROW_STEP
    return pl.pallas_call(
        _dispatch_kernel,
        grid_spec=pltpu.PrefetchScalarGridSpec(
            num_scalar_prefetch=5,
            grid=(t // tm,),
            in_specs=[pl.BlockSpec((tm, D_MODEL), lambda i, *_: (i, 0)),
                      pl.BlockSpec((SUBLANES, tm), lambda i, *_: (0, i))],
            out_specs=pl.BlockSpec(memory_space=pl.ANY),
            scratch_shapes=[pltpu.VMEM((2, R_SORTED, D_MODEL), F32),
                            pltpu.VMEM((EXPERT_ROW_STEP, D_MODEL), F32),
                            pltpu.SemaphoreType.DMA((2,)),
                            pltpu.SemaphoreType.DMA((2,))]),
        out_shape=jax.ShapeDtypeStruct((n_rows, D_MODEL), F32),
        compiler_params=_params(1),
        name="dispatch",
    )(*pieces, zero_flag, u, post)


def _expert_kernel(be_ref, nu_ref, first_ref, next_ref, rows_ref, xs_ref, w1_ref, b1_ref, w2_ref, b2_ref,
                   ys_ref, w1f_ref, w2f_ref, w1b_ref, w2b_ref, sems):
    i = pl.program_id(0)

    def fetch(e):
        return (pltpu.make_async_copy(w1_ref.at[e], w1f_ref, sems.at[0]),
                pltpu.make_async_copy(w2_ref.at[e], w2f_ref, sems.at[1]))

    @pl.when(i >= nu_ref[0])
    def _():
        ys_ref[...] = jnp.zeros(ys_ref.shape, F32)

    @pl.when(i < nu_ref[0])
    def _():
        e = be_ref[i]

        @pl.when(i == 0)
        def _():
            for c in fetch(e):
                c.start()

        @pl.when(first_ref[i] != 0)
        def _():
            for c in fetch(e):
                c.wait()
            w1b_ref[...] = w1f_ref[...].astype(BF16)
            w2b_ref[...] = w2f_ref[...].astype(BF16)

            @pl.when(next_ref[i] >= 0)
            def _():
                for c in fetch(next_ref[i]):
                    c.start()

        bm = xs_ref.shape[0]
        for m in range(EXPERT_ROW_STEP, bm + 1, EXPERT_ROW_STEP):
            @pl.when(rows_ref[i] == m)
            def _(m=m):
                hid = _dot(xs_ref[0:m, :].astype(BF16), w1b_ref[...]) + b1_ref[e]
                x_glu = jnp.minimum(hid[:, :D_EXPERT], SWIGLU_LIMIT)
                x_lin = jnp.clip(hid[:, D_EXPERT:], -SWIGLU_LIMIT, SWIGLU_LIMIT)
                act = x_glu / (1.0 + jnp.exp(-SWIGLU_ALPHA * x_glu)) * (x_lin + 1.0)
                ys_ref[0:m, :] = _dot(act.astype(BF16), w2b_ref[...]) + b2_ref[e]
                if m < bm:
                    ys_ref[m:, :] = jnp.zeros((bm - m, D_MODEL), F32)


def _experts(blk_expert, n_used, first_flag, next_expert, blk_rows, xs, w1, b1, w2, b2):
    n_rows = xs.shape[0]
    bm = BM_EXPERT
    nblk = n_rows // bm
    last = lambda i, nu: jnp.maximum(jnp.minimum(i, nu[0] - 1), 0)
    row = lambda i, be, nu, *_: (last(i, nu), 0)
    whole = lambda i, *_: (0, 0, 0)
    return pl.pallas_call(
        _expert_kernel,
        grid_spec=pltpu.PrefetchScalarGridSpec(
            num_scalar_prefetch=5,
            grid=(nblk,),
            in_specs=[pl.BlockSpec((bm, D_MODEL), row),
                      pl.BlockSpec(memory_space=pl.ANY),
                      pl.BlockSpec((N_EXPERTS, 1, 2 * D_EXPERT), whole),
                      pl.BlockSpec(memory_space=pl.ANY),
                      pl.BlockSpec((N_EXPERTS, 1, D_MODEL), whole)],
            out_specs=pl.BlockSpec((bm, D_MODEL), lambda i, *_: (i, 0)),
            scratch_shapes=[pltpu.VMEM((D_MODEL, 2 * D_EXPERT), F32),
                            pltpu.VMEM((D_EXPERT, D_MODEL), F32),
                            pltpu.VMEM((D_MODEL, 2 * D_EXPERT), BF16),
                            pltpu.VMEM((D_EXPERT, D_MODEL), BF16),
                            pltpu.SemaphoreType.DMA((2,))]),
        out_shape=jax.ShapeDtypeStruct((n_rows, D_MODEL), F32),
        compiler_params=_params(1),
        name="experts",
    )(blk_expert, n_used, first_flag, next_expert, blk_rows, xs, w1, b1, w2, b2)


def _combine_kernel(pcnt_ref, psrc_ref, pdst_ref, ntot_ref, ys_ref, h_ref, slab_ref, g_ref, o_ref,
                    buf_ref, sems):
    tm = h_ref.shape[0]
    i = pl.program_id(0)
    slot = i % 2

    def read(slot_, n, sorted_row, ys_row):
        return pltpu.make_async_copy(_rows(ys_ref, ys_row, n),
                                     _rows(buf_ref.at[slot_], sorted_row, n), sems.at[slot_])

    def gather(tile, slot_):
        _for_each_piece(tile, pcnt_ref, psrc_ref, pdst_ref, lambda n, s, d: read(slot_, n, s, d).start())

    @pl.when(i == 0)
    def _():
        buf_ref[...] = jnp.zeros(buf_ref.shape, F32)
        gather(0, 0)

    @pl.when(i + 1 < pl.num_programs(0))
    def _():
        gather(i + 1, 1 - slot)

    _wait_rows(ntot_ref[i], lambda n: read(slot, n, 0, 0))

    slab = slab_ref[...]
    lane = lax.broadcasted_iota(jnp.int32, (tm, R_SORTED), 1).astype(F32)
    wmat = jnp.zeros((tm, R_SORTED), F32)
    for k in range(TOP_K):
        wmat = jnp.where(lane == slab[:, k:k + 1], slab[:, TOP_K + k:TOP_K + k + 1], wmat)
    moe = _dot(wmat.astype(BF16), buf_ref[slot].astype(BF16))
    o_ref[...] = _rms(h_ref[...] + moe, g_ref[...])


def _combine(pieces, ys, h, slab, g):
    t = h.shape[0]
    tm = TM_ROUTE
    return pl.pallas_call(
        _combine_kernel,
        grid_spec=pltpu.PrefetchScalarGridSpec(
            num_scalar_prefetch=4,
            grid=(t // tm,),
            in_specs=[pl.BlockSpec(memory_space=pl.ANY),
                      pl.BlockSpec((tm, D_MODEL), lambda i, *_: (i, 0)),
                      pl.BlockSpec((tm, LANES), lambda i, *_: (i, 0)),
                      pl.BlockSpec((1, D_MODEL), lambda i, *_: (0, 0))],
            out_specs=pl.BlockSpec((tm, D_MODEL), lambda i, *_: (i, 0)),
            scratch_shapes=[pltpu.VMEM((2, R_SORTED, D_MODEL), F32),
                            pltpu.SemaphoreType.DMA((2,))]),
        out_shape=jax.ShapeDtypeStruct((t, D_MODEL), F32),
        compiler_params=_params(1),
        name="combine",
    )(*pieces, ys, h, slab, g)


def _pad_lanes(v, fill=0.0):
    return jnp.pad(v.reshape(1, -1), ((0, 0), (0, LANES - v.shape[-1])), constant_values=fill)


def kernel(x, norm_mix_g, w_in, conv_w, conv_b, dt_bias, a_log, d_skip, ssd_norm_g, attn_sinks,
           rel_bias, attn_norm_g, w_out, norm_ffn_g, w_router, b_router, w1, b1, w2, b2,
           norm_final_g):
    bsz, seq, d = x.shape
    t = bsz * seq
    nc = seq // CHUNK
    depth = w_in.shape[0]
    bias = _bias_table(rel_bias)

    assert depth == 1, "the final norm is fused into the combine kernel: single layer only"
    h = x.reshape(t, d)
    for layer in range(depth):
        y_mix = _mixer(h.reshape(bsz, seq, d), norm_mix_g[layer].reshape(1, d), w_in, layer,
                       conv_w[layer], conv_b[layer].reshape(1, -1),
                       _pad_lanes(dt_bias[layer]), _pad_lanes(a_log[layer]),
                       jnp.repeat(d_skip[layer], SSD_HEAD_DIM).reshape(1, -1),
                       ssd_norm_g[layer].reshape(1, -1), attn_sinks[layer], bias,
                       attn_norm_g[layer].reshape(1, -1)).reshape(t, -1)

        wr = w_router[layer].T.astype(BF16)
        br = jnp.broadcast_to(b_router[layer][:, None], (N_EXPERTS, LANES))
        h_mid, u, slab, post, tab = _route(
            h, y_mix, w_out, layer, norm_ffn_g[layer].reshape(1, d), wr, br)

        bm = BM_EXPERT
        ntiles = t // TM_ROUTE
        nblk = (t * TOP_K + ntiles * N_EXPERTS * (SUBLANES - 1)) // bm + N_EXPERTS
        tab = tab.reshape(ntiles, N_EXPERTS, LANES)
        cnt8, off8, segst = tab[:, :, 0], tab[:, :, 1], tab[:, :, 2]
        total8 = off8[-1] + cnt8[-1]
        padded = ((total8 + bm - 1) // bm) * bm
        pend = jnp.cumsum(padded)
        segdst = (pend - padded)[None, :] + off8
        n_used = pend[-1:] // bm
        eids = jnp.arange(N_EXPERTS, dtype=jnp.int32)
        blk = jnp.arange(nblk, dtype=jnp.int32)
        blk_expert = jnp.minimum(
            jnp.sum((blk[:, None] * bm >= pend[None, :]).astype(jnp.int32), axis=1), N_EXPERTS - 1)
        e_end = jnp.sum(jnp.where(blk_expert[:, None] == eids[None, :],
                                  ((pend - padded) + total8)[None, :], 0), axis=1)
        valid = jnp.where(blk < n_used[0], jnp.clip(e_end - blk * bm, 0, bm), 0)
        blk_rows = ((valid + EXPERT_ROW_STEP - 1) // EXPERT_ROW_STEP) * EXPERT_ROW_STEP
        piece_lo = jnp.arange(bm // EXPERT_ROW_STEP, dtype=jnp.int32)[None, :] * EXPERT_ROW_STEP
        zero_flag = jnp.where(piece_lo >= valid[:, None], 2,
                              (piece_lo + EXPERT_ROW_STEP > valid[:, None]).astype(jnp.int32)).reshape(-1)
        before = jnp.concatenate([blk_expert[:1], blk_expert[:-1]])
        first_flag = ((blk == 0) | (before != blk_expert)).astype(jnp.int32)
        cand = jnp.where((eids[None, :] > eids[:, None]) & (padded[None, :] > 0), eids[None, :], N_EXPERTS)
        next_nonempty = jnp.min(cand, axis=1)
        next_nonempty = jnp.where(next_nonempty == N_EXPERTS, -1, next_nonempty)
        next_expert = jnp.sum(jnp.where(blk_expert[:, None] == eids[None, :], next_nonempty[None, :], 0), axis=1)
        pieces = _piece_tables(cnt8, segst, segdst)

        xs = _dispatch(pieces, zero_flag, u, post)
        ys = _experts(blk_expert, n_used, first_flag, next_expert, blk_rows, xs, w1[layer],
                      b1[layer].reshape(N_EXPERTS, 1, -1), w2[layer], b2[layer].reshape(N_EXPERTS, 1, -1))
        h = _combine(pieces, ys, h_mid, slab, norm_final_g.reshape(1, d))
    return h.reshape(bsz, seq, d)
```

```python
import functools
import math

import numpy as np
import jax
import jax.numpy as jnp
from jax import lax
from jax.experimental import pallas as pl
from jax.experimental.pallas import tpu as pltpu

F32 = jnp.float32
BF16 = jnp.bfloat16

D_MODEL = 1024
SSD_HEADS = 8
SSD_HEAD_DIM = 64
D_SSD = SSD_HEADS * SSD_HEAD_DIM
SSD_GROUPS = 2
SSD_HEADS_PER_GROUP = SSD_HEADS // SSD_GROUPS
D_STATE = 128
CONV_WIDTH = 4
CHUNK = 128
D_CONV = D_SSD + 2 * SSD_GROUPS * D_STATE
ATTN_Q_HEADS = 8
ATTN_KV_HEADS = 2
ATTN_Q_PER_KV = ATTN_Q_HEADS // ATTN_KV_HEADS
ATTN_HEAD_DIM = 64
D_ATTN = ATTN_Q_HEADS * ATTN_HEAD_DIM
D_KV = ATTN_KV_HEADS * ATTN_HEAD_DIM
WINDOW = 128
REL_BUCKETS = 32
REL_MAX_DIST = 128
N_EXPERTS = 32
TOP_K = 4
D_EXPERT = D_MODEL
SWIGLU_LIMIT = 7.0
SWIGLU_ALPHA = 1.702
RMS_EPS = 1e-5

LANES = 128
SUBLANES = 8
NEG = -1e30
VMEM_LIMIT = 56 * 1024 * 1024

TM_PROJ = 512
TM_ROUTE = 512
BM_EXPERT = 512
EXPERT_ROW_STEP = 128
SEQS_PER_STEP = 2
CHUNKS_PER_STEP = 2
R_SORTED = TM_ROUTE * TOP_K + N_EXPERTS * SUBLANES

D_PROJ_PACKED = D_SSD + D_CONV + D_ATTN + 2 * D_KV + LANES


def _params(n_axes):
    return pltpu.CompilerParams(dimension_semantics=("arbitrary",) * n_axes,
                                vmem_limit_bytes=VMEM_LIMIT)


def _rms(x, g):
    return x * lax.rsqrt(jnp.mean(x * x, axis=-1, keepdims=True) + RMS_EPS) * g


def _silu(x):
    return x / (1.0 + jnp.exp(-x))


def _dot(a, b, **kw):
    return jnp.dot(a, b, preferred_element_type=F32, **kw)


def _dot_f32_by_mask(mask, x, mask_on_left=False):
    hi = x.astype(BF16)
    r1 = x - hi.astype(F32)
    mid = r1.astype(BF16)
    lo = (r1 - mid.astype(F32)).astype(BF16)
    out = None
    for piece in (hi, mid, lo):
        term = _dot(mask, piece) if mask_on_left else _dot(piece, mask)
        out = term if out is None else out + term
    return out


def _t5_bucket_table():
    dist = CHUNK + np.arange(CHUNK)[:, None] - np.arange(2 * CHUNK)[None, :]
    in_window = (dist >= 0) & (dist < WINDOW)
    d = np.clip(dist, 0, REL_MAX_DIST)
    max_exact = REL_BUCKETS // 2
    large = max_exact + (np.log(np.maximum(d, 1).astype(np.float32) / max_exact)
                         / math.log(REL_MAX_DIST / max_exact)
                         * (REL_BUCKETS - max_exact)).astype(np.int32)
    large = np.minimum(large, REL_BUCKETS - 1)
    bucket = np.where(d < max_exact, d, large)
    return np.where(in_window, bucket, -1).astype(np.int32)


def _bias_kernel(rb_ref, bucket_ref, o_ref):
    bucket = bucket_ref[...]
    col = lax.broadcasted_iota(jnp.int32, bucket.shape, 1)
    for h in range(ATTN_Q_HEADS):
        acc = jnp.full(bucket.shape, NEG, F32)
        for b in range(REL_BUCKETS):
            acc = jnp.where(bucket == b, rb_ref[b, h], acc)
        o_ref[1, h] = acc
        o_ref[0, h] = jnp.where(col >= CHUNK, acc, NEG)


def _bias_table(rel_bias):
    bucket = jnp.asarray(_t5_bucket_table())
    return pl.pallas_call(
        _bias_kernel,
        out_shape=jax.ShapeDtypeStruct((2, ATTN_Q_HEADS, CHUNK, 2 * CHUNK), F32),
        in_specs=[pl.BlockSpec(memory_space=pltpu.SMEM),
                  pl.BlockSpec(memory_space=pltpu.VMEM)],
        out_specs=pl.BlockSpec(memory_space=pltpu.VMEM),
        name="bias_table",
    )(rel_bias, bucket)


def _mixer_kernel(sink_ref, x_ref, gm_ref, w_ref, cw_ref, cb_ref, dtb_ref, alog_ref, dskip_ref, gs_ref,
                  expand_ref, bias_ref, ga_ref, o_ref, wb_ref, state_ref, xpad_ref, kvp_ref):
    j = pl.program_id(1)

    @pl.when((pl.program_id(0) == 0) & (j == 0))
    def _():
        o2 = D_SSD + D_CONV
        o3 = o2 + SSD_HEADS
        n_qkv = D_ATTN + 2 * D_KV
        for lo, src_lo, n in ((0, 0, o2), (o2, o3, n_qkv)):
            for r in range(0, n, LANES):
                wb_ref[:, lo + r:lo + r + LANES] = w_ref[0, src_lo + r:src_lo + r + LANES, :].T.astype(BF16)
        wb_ref[:, o2 + n_qkv:] = jnp.concatenate(
            [w_ref[0, o2:o3, :], jnp.zeros((LANES - SSD_HEADS, D_MODEL), F32)], axis=0).T.astype(BF16)

    @pl.when(j == 0)
    def _():
        state_ref[...] = jnp.zeros(state_ref.shape, F32)
        xpad_ref[...] = jnp.zeros(xpad_ref.shape, F32)
        kvp_ref[...] = jnp.zeros(kvp_ref.shape, F32)

    ns, rows, _ = x_ref.shape
    bounds = np.cumsum([0, D_SSD, D_CONV, D_ATTN, 2 * D_KV, LANES])
    pieces = [(c, s) for c in range(rows // CHUNK) for s in range(ns)]

    def project(c, s):
        x = x_ref[s, c * CHUNK:(c + 1) * CHUNK, :]
        return _dot(_rms(x, gm_ref[...]).astype(BF16), wb_ref[...])

    proj = project(*pieces[0])
    for k, (c, s) in enumerate(pieces):
        proj_next = project(*pieces[k + 1]) if k + 1 < len(pieces) else None
        z, xbc, q, kv, dt = [proj[:, a:b] for a, b in zip(bounds[:-1], bounds[1:])]
        has_prev = jnp.minimum(j, 1) if c == 0 else 1
        y_ssd = _ssd_chunk(z, xbc, dt, cw_ref, cb_ref, dtb_ref, alog_ref, dskip_ref, gs_ref,
                           expand_ref, state_ref.at[s], xpad_ref.at[s])
        y_attn = _swa_block(sink_ref, q, kv, kvp_ref[s], has_prev, bias_ref, ga_ref)
        kvp_ref[s] = kv
        o_ref[s, c * CHUNK:(c + 1) * CHUNK, :] = jnp.concatenate([y_ssd, y_attn], axis=1)
        proj = proj_next


def _ssd_chunk(z, x_cur, dt_raw, cw_ref, cb_ref, dtb_ref, alog_ref, dskip_ref, g_ref,
               expand_ref, state_ref, xpad_ref):
    L = CHUNK
    G, R, P, N = SSD_GROUPS, SSD_HEADS_PER_GROUP, SSD_HEAD_DIM, D_STATE
    GW = R * P

    xpad = jnp.concatenate([xpad_ref[...], x_cur], axis=0)
    xpad_ref[...] = x_cur[L - SUBLANES:, :]
    acc = cb_ref[...] + cw_ref[CONV_WIDTH - 1:CONV_WIDTH, :] * x_cur
    for d in range(1, CONV_WIDTH):
        k = CONV_WIDTH - 1 - d
        acc = acc + cw_ref[k:k + 1, :] * pltpu.roll(xpad, d, axis=0)[SUBLANES:, :]
    xbc = _silu(acc)
    xs = xbc[:, :D_SSD]
    bm = xbc[:, D_SSD:D_SSD + G * N]
    cm = xbc[:, D_SSD + G * N:]

    dtr = dt_raw + dtb_ref[...]
    dt = jnp.maximum(dtr, 0.0) + jnp.log(1.0 + jnp.exp(-jnp.abs(dtr)))
    a_dt = dt * (-jnp.exp(alog_ref[...]))
    ri = lax.broadcasted_iota(jnp.int32, (L, L), 0)
    ci = lax.broadcasted_iota(jnp.int32, (L, L), 1)
    causal = ci <= ri
    a_cum = _dot_f32_by_mask(causal.astype(BF16), a_dt, mask_on_left=True)
    a_cum_t = _dot_f32_by_mask((ri <= ci).astype(BF16), a_dt.T)
    a_last = a_cum[L - 1:L, :]
    stack = jnp.concatenate(
        [dt, jnp.exp(a_cum), jnp.exp(a_last - a_cum),
         jnp.broadcast_to(jnp.exp(a_last), (SUBLANES, LANES))], axis=0)
    ex = _dot_f32_by_mask(expand_ref[...], stack)
    dt_x, ea_x, dte_x, cd_x = ex[0:L], ex[L:2 * L], ex[2 * L:3 * L], ex[3 * L:3 * L + 1]
    xdt = xs * dt_x

    ys = []
    for g in range(G):
        bm_g = bm[:, g * N:(g + 1) * N]
        cm_g = cm[:, g * N:(g + 1) * N].astype(BF16)
        cb = lax.dot_general(cm_g, bm_g.astype(BF16), (((1,), (1,)), ((), ())),
                             preferred_element_type=F32)
        xdt_g = xdt[:, g * GW:(g + 1) * GW]
        yd = []
        for r in range(R):
            h = g * R + r
            seg = a_cum[:, h:h + 1] - a_cum_t[h:h + 1, :]
            dec = jnp.exp(jnp.where(causal, seg, NEG))
            yd.append(_dot((cb * dec).astype(BF16), xdt_g[:, r * P:(r + 1) * P].astype(BF16)))
        y_diag = jnp.concatenate(yd, axis=1)
        st = state_ref[g]
        y_off = _dot(cm_g, st.astype(BF16)) * ea_x[:, g * GW:(g + 1) * GW]
        new = _dot(bm_g.T.astype(BF16), (xdt_g * dte_x[:, g * GW:(g + 1) * GW]).astype(BF16))
        state_ref[g] = st * cd_x[:, g * GW:(g + 1) * GW] + new
        ys.append(y_diag + y_off + xs[:, g * GW:(g + 1) * GW] * dskip_ref[:, g * GW:(g + 1) * GW])
    y = jnp.concatenate(ys, axis=1)
    return _rms(y * _silu(z), g_ref[...])


def _mixer(x, norm_g, w_in_t, layer, conv_w, conv_b, dt_bias, a_log, d_skip_x, ssd_norm_g, sinks, bias,
           attn_norm_g):
    bsz, seq, _ = x.shape
    rows = CHUNKS_PER_STEP * CHUNK
    nsteps = seq // rows
    expand = np.zeros((LANES, D_SSD), np.float32)
    for h in range(SSD_HEADS):
        expand[h, h * SSD_HEAD_DIM:(h + 1) * SSD_HEAD_DIM] = 1.0
    ns = SEQS_PER_STEP
    row = lambda b, j: (b, j, 0)
    fixed = lambda b, j: (0, 0)
    return pl.pallas_call(
        _mixer_kernel,
        grid=(bsz // ns, nsteps),
        in_specs=[pl.BlockSpec(memory_space=pltpu.SMEM),
                  pl.BlockSpec((ns, rows, D_MODEL), row),
                  pl.BlockSpec((1, D_MODEL), fixed),
                  pl.BlockSpec((1,) + w_in_t.shape[1:], lambda b, j: (layer, 0, 0)),
                  pl.BlockSpec((CONV_WIDTH, D_CONV), fixed),
                  pl.BlockSpec((1, D_CONV), fixed),
                  pl.BlockSpec((1, LANES), fixed),
                  pl.BlockSpec((1, LANES), fixed),
                  pl.BlockSpec((1, D_SSD), fixed),
                  pl.BlockSpec((1, D_SSD), fixed),
                  pl.BlockSpec((LANES, D_SSD), fixed),
                  pl.BlockSpec((2, ATTN_Q_HEADS, CHUNK, 2 * CHUNK), lambda b, j: (0, 0, 0, 0)),
                  pl.BlockSpec((1, D_ATTN), fixed)],
        out_specs=pl.BlockSpec((ns, rows, D_SSD + D_ATTN), row),
        out_shape=jax.ShapeDtypeStruct((bsz, seq, D_SSD + D_ATTN), F32),
        scratch_shapes=[pltpu.VMEM((D_MODEL, D_PROJ_PACKED), BF16),
                        pltpu.VMEM((ns, SSD_GROUPS, D_STATE, SSD_HEADS_PER_GROUP * SSD_HEAD_DIM), F32),
                        pltpu.VMEM((ns, SUBLANES, D_CONV), F32),
                        pltpu.VMEM((ns, CHUNK, 2 * D_KV), F32)],
        compiler_params=_params(2),
        name="mixer",
    )(sinks, x, norm_g, w_in_t, conv_w, conv_b, dt_bias, a_log, d_skip_x, ssd_norm_g,
      jnp.asarray(expand, dtype=BF16), bias, attn_norm_g)


def _swa_block(sink_ref, q, kv, kvp, has_prev, bias_ref, g_ref):
    L, Dh = CHUNK, ATTN_HEAD_DIM
    q = q * (1.0 / math.sqrt(Dh))
    outs = []
    for hk in range(ATTN_KV_HEADS):
        ks = slice(hk * Dh, (hk + 1) * Dh)
        vs = slice(D_KV + hk * Dh, D_KV + (hk + 1) * Dh)
        kc = jnp.concatenate([kvp[:, ks], kv[:, ks]], axis=0).astype(BF16)
        vc = jnp.concatenate([kvp[:, vs], kv[:, vs]], axis=0).astype(BF16)
        for g in range(ATTN_Q_PER_KV):
            h = hk * ATTN_Q_PER_KV + g
            qh = q[:, h * Dh:(h + 1) * Dh].astype(BF16)
            s = lax.dot_general(qh, kc, (((1,), (1,)), ((), ())), preferred_element_type=F32)
            s = s + bias_ref[has_prev, h]
            sink = sink_ref[h]
            m = jnp.maximum(jnp.max(s, axis=-1, keepdims=True), sink)
            p = jnp.exp(s - m)
            denom = jnp.sum(p, axis=-1, keepdims=True) + jnp.exp(sink - m)
            outs.append(_dot(p.astype(BF16), vc) / denom)
    return _rms(jnp.concatenate(outs, axis=1), g_ref[...])


def _route_kernel(x_ref, y_ref, wo_ref, g_ref, wr_ref, br_ref,
                  h_ref, u_ref, slab_ref, post_ref, tab_ref, run_ref, wob_ref):
    tm = x_ref.shape[0]

    @pl.when(pl.program_id(0) == 0)
    def _():
        run_ref[...] = jnp.zeros(run_ref.shape, F32)
        wob_ref[...] = wo_ref[0].astype(BF16)

    h = x_ref[...] + _dot(y_ref[...].astype(BF16), wob_ref[...])
    h_ref[...] = h
    u = _rms(h, g_ref[...])
    ub = u.astype(BF16)
    u_ref[...] = ub
    E = N_EXPERTS
    logits = lax.dot_general(wr_ref[...], ub, (((1,), (1,)), ((), ())), preferred_element_type=F32)
    logits = logits + jnp.concatenate([br_ref[...]] * (tm // LANES), axis=1)

    eidx = lax.broadcasted_iota(jnp.int32, (E, tm), 0).astype(F32)
    vals, idxs = [], []
    cur = logits
    for _ in range(TOP_K):
        m = jnp.max(cur, axis=0, keepdims=True)
        ix = jnp.min(jnp.where(cur == m, eidx, float(E)), axis=0, keepdims=True)
        vals.append(m)
        idxs.append(ix)
        cur = jnp.where(eidx == ix, NEG, cur)
    es = [jnp.exp(v - vals[0]) for v in vals]
    den = es[0] + es[1] + es[2] + es[3]

    onehot = jnp.zeros((E, tm), F32)
    for ix in idxs:
        onehot = onehot + (eidx == ix).astype(F32)
    ri = lax.broadcasted_iota(jnp.int32, (tm, tm), 0)
    ci = lax.broadcasted_iota(jnp.int32, (tm, tm), 1)
    before = _dot(onehot.astype(BF16), (ri < ci).astype(BF16))
    cnt = jnp.sum(onehot, axis=1, keepdims=True)
    cnt8 = jnp.floor((cnt + (SUBLANES - 1)) * (1.0 / SUBLANES)) * SUBLANES
    el = lax.broadcasted_iota(jnp.int32, (E, E), 0)
    ec = lax.broadcasted_iota(jnp.int32, (E, E), 1)
    seg_start = _dot_f32_by_mask((ec < el).astype(BF16), jnp.broadcast_to(cnt8, (E, LANES)),
                                 mask_on_left=True)[:, 0:1]
    where_to = before + seg_start

    rows = [jnp.sum(jnp.where(eidx == idxs[k], where_to, 0.0), axis=0, keepdims=True)
            for k in range(TOP_K)]
    rows += [es[k] / den for k in range(TOP_K)]
    stack = jnp.concatenate(rows, axis=0)
    post_ref[...] = stack.astype(jnp.int32)
    slab_ref[...] = jnp.concatenate([stack, jnp.zeros((LANES - 2 * TOP_K, tm), F32)], axis=0).T

    lane = lax.broadcasted_iota(jnp.int32, (E, LANES), 1)
    tab = jnp.where(lane == 0, cnt8, jnp.where(lane == 1, run_ref[...], jnp.where(lane == 2, seg_start, 0.0)))
    tab_ref[...] = tab.astype(jnp.int32)
    run_ref[...] = run_ref[...] + cnt8


def _route(x2, y_mix, w_out, layer, g, wr, br):
    t = x2.shape[0]
    tm = TM_ROUTE
    row = lambda i: (i, 0)
    fixed = lambda i: (0, 0)
    return pl.pallas_call(
        _route_kernel,
        grid=(t // tm,),
        in_specs=[pl.BlockSpec((tm, D_MODEL), row),
                  pl.BlockSpec((tm, D_SSD + D_ATTN), row),
                  pl.BlockSpec((1, D_SSD + D_ATTN, D_MODEL), lambda i: (layer, 0, 0)),
                  pl.BlockSpec((1, D_MODEL), fixed),
                  pl.BlockSpec((N_EXPERTS, D_MODEL), fixed),
                  pl.BlockSpec((N_EXPERTS, LANES), fixed)],
        out_specs=[pl.BlockSpec((tm, D_MODEL), row),
                   pl.BlockSpec((tm, D_MODEL), row),
                   pl.BlockSpec((tm, LANES), row),
                   pl.BlockSpec((SUBLANES, tm), lambda i: (0, i)),
                   pl.BlockSpec((N_EXPERTS, LANES), row)],
        out_shape=[jax.ShapeDtypeStruct((t, D_MODEL), F32),
                   jax.ShapeDtypeStruct((t, D_MODEL), BF16),
                   jax.ShapeDtypeStruct((t, LANES), F32),
                   jax.ShapeDtypeStruct((SUBLANES, t), jnp.int32),
                   jax.ShapeDtypeStruct((t // tm * N_EXPERTS, LANES), jnp.int32)],
        scratch_shapes=[pltpu.VMEM((N_EXPERTS, LANES), F32),
                        pltpu.VMEM((D_SSD + D_ATTN, D_MODEL), BF16)],
        compiler_params=_params(1),
        name="route",
    )(x2, y_mix, w_out, g, wr, br)


SEG_SIZE_BITS = (TM_ROUTE // SUBLANES).bit_length()


TILE_SIZE_BITS = (R_SORTED // SUBLANES).bit_length()


def _piece_tables(cnt8, segst, segdst):
    n = (cnt8 // SUBLANES)[:, None, :]
    b = jnp.arange(SEG_SIZE_BITS, dtype=jnp.int32)[None, :, None]
    has = (n >> b) & 1
    off = ((n >> (b + 1)) << (b + 1)) * SUBLANES
    rank = jnp.cumsum(has, axis=2) - has
    place = (has[..., None] == 1) & (rank[..., None] == jnp.arange(N_EXPERTS, dtype=jnp.int32))
    dense = lambda v: jnp.sum(jnp.where(place, v[..., None], 0), axis=2).reshape(-1)
    return (jnp.sum(has, axis=2).reshape(-1), dense(segst[:, None, :] + off),
            dense(segdst[:, None, :] + off), jnp.sum(cnt8, axis=1) // SUBLANES)


def _for_each_piece(tile, pcnt_ref, psrc_ref, pdst_ref, fn):
    for b in range(SEG_SIZE_BITS):
        base = (tile * SEG_SIZE_BITS + b) * N_EXPERTS

        def body(p, carry, b=b, base=base):
            fn(SUBLANES << b, psrc_ref[base + p], pdst_ref[base + p])
            return carry

        lax.fori_loop(0, pcnt_ref[tile * SEG_SIZE_BITS + b], body, 0)


def _wait_rows(n_tiles8, descriptor):
    for b in range(TILE_SIZE_BITS):
        @pl.when(((n_tiles8 >> b) & 1) == 1)
        def _(b=b):
            descriptor(SUBLANES << b).wait()


def _rows(ref, row, n):
    if not isinstance(row, int):
        row = pl.multiple_of(row, SUBLANES)
    return ref.at[pl.ds(row, n), :]


def _dispatch_kernel(pcnt_ref, psrc_ref, pdst_ref, ntot_ref, zflag_ref, u_ref, post_ref, xs_ref,
                     buf_ref, zero_ref, sems, zsem):
    tm = u_ref.shape[0]
    i = pl.program_id(0)
    slot = i % 2
    bm = zero_ref.shape[0]

    def fill(b, flag):
        return pltpu.make_async_copy(zero_ref, xs_ref.at[pl.ds(b * bm, bm), :], zsem.at[flag - 1])

    def for_flagged(flag, action):
        def body(b, carry):
            @pl.when(zflag_ref[b] == flag)
            def _():
                action(fill(b, flag))
            return carry

        lax.fori_loop(0, zflag_ref.shape[0], body, 0)

    @pl.when(i == 0)
    def _():
        zero_ref[...] = jnp.zeros(zero_ref.shape, F32)
        for_flagged(1, lambda c: c.start())
        for_flagged(2, lambda c: c.start())
        for_flagged(1, lambda c: c.wait())

    @pl.when(i == pl.num_programs(0) - 1)
    def _():
        for_flagged(2, lambda c: c.wait())

    pos = post_ref[...]
    j = lax.broadcasted_iota(jnp.int32, (R_SORTED, tm), 0)
    sel = jnp.zeros((R_SORTED, tm), F32)
    for k in range(TOP_K):
        sel = jnp.where(j == pos[k:k + 1, :], 1.0, sel)
    buf_ref[slot] = _dot(sel.astype(BF16), u_ref[...])

    def write(slot_, n, src_row, dst_row):
        return pltpu.make_async_copy(_rows(buf_ref.at[slot_], src_row, n),
                                     _rows(xs_ref, dst_row, n), sems.at[slot_])

    _for_each_piece(i, pcnt_ref, psrc_ref, pdst_ref, lambda n, s, d: write(slot, n, s, d).start())

    @pl.when(i > 0)
    def _():
        _wait_rows(ntot_ref[i - 1], lambda n: write(1 - slot, n, 0, 0))

    @pl.when(i == pl.num_programs(0) - 1)
    def _():
        _wait_rows(ntot_ref[i], lambda n: write(slot, n, 0, 0))


def _dispatch(pieces, zero_flag, u, post):
    t = u.shape[0]
    tm = TM_ROUTE
    n_rows = zero_flag.shape[0] * EXPERT_ROW_STEP
    return pl.pallas_call(
        _dispatch_kernel,
        grid_spec=pltpu.PrefetchScalarGridSpec(
            num_scalar_prefetch=5,
            grid=(t // tm,),
            in_specs=[pl.BlockSpec((tm, D_MODEL), lambda i, *_: (i, 0)),
                      pl.BlockSpec((SUBLANES, tm), lambda i, *_: (0, i))],
            out_specs=pl.BlockSpec(memory_space=pl.ANY),
            scratch_shapes=[pltpu.VMEM((2, R_SORTED, D_MODEL), F32),
                            pltpu.VMEM((EXPERT_ROW_STEP, D_MODEL), F32),
                            pltpu.SemaphoreType.DMA((2,)),
                            pltpu.SemaphoreType.DMA((2,))]),
        out_shape=jax.ShapeDtypeStruct((n_rows, D_MODEL), F32),
        compiler_params=_params(1),
        name="dispatch",
    )(*pieces, zero_flag, u, post)


def _expert_kernel(be_ref, nu_ref, first_ref, next_ref, rows_ref, xs_ref, w1_ref, b1_ref, w2_ref, b2_ref,
                   ys_ref, w1f_ref, w2f_ref, w1b_ref, w2b_ref, sems):
    i = pl.program_id(0)

    def fetch(e):
        return (pltpu.make_async_copy(w1_ref.at[e], w1f_ref, sems.at[0]),
                pltpu.make_async_copy(w2_ref.at[e], w2f_ref, sems.at[1]))

    @pl.when(i >= nu_ref[0])
    def _():
        ys_ref[...] = jnp.zeros(ys_ref.shape, F32)

    @pl.when(i < nu_ref[0])
    def _():
        e = be_ref[i]

        @pl.when(i == 0)
        def _():
            for c in fetch(e):
                c.start()

        @pl.when(first_ref[i] != 0)
        def _():
            for c in fetch(e):
                c.wait()
            w1b_ref[...] = w1f_ref[...].astype(BF16)
            w2b_ref[...] = w2f_ref[...].astype(BF16)

            @pl.when(next_ref[i] >= 0)
            def _():
                for c in fetch(next_ref[i]):
                    c.start()

        bm = xs_ref.shape[0]
        for m in range(EXPERT_ROW_STEP, bm + 1, EXPERT_ROW_STEP):
            @pl.when(rows_ref[i] == m)
            def _(m=m):
                hid = _dot(xs_ref[0:m, :].astype(BF16), w1b_ref[...]) + b1_ref[pl.ds(e, 1), :]
                x_glu = jnp.minimum(hid[:, :D_EXPERT], SWIGLU_LIMIT)
                x_lin = jnp.clip(hid[:, D_EXPERT:], -SWIGLU_LIMIT, SWIGLU_LIMIT)
                act = x_glu / (1.0 + jnp.exp(-SWIGLU_ALPHA * x_glu)) * (x_lin + 1.0)
                ys_ref[0:m, :] = _dot(act.astype(BF16), w2b_ref[...]) + b2_ref[pl.ds(e, 1), :]
                if m < bm:
                    ys_ref[m:, :] = jnp.zeros((bm - m, D_MODEL), F32)


def _experts(blk_expert, n_used, first_flag, next_expert, blk_rows, xs, w1, b1, w2, b2):
    n_rows = xs.shape[0]
    bm = BM_EXPERT
    nblk = n_rows // bm
    last = lambda i, nu: jnp.maximum(jnp.minimum(i, nu[0] - 1), 0)
    row = lambda i, be, nu, *_: (last(i, nu), 0)
    whole = lambda i, *_: (0, 0)
    return pl.pallas_call(
        _expert_kernel,
        grid_spec=pltpu.PrefetchScalarGridSpec(
            num_scalar_prefetch=5,
            grid=(nblk,),
            in_specs=[pl.BlockSpec((bm, D_MODEL), row),
                      pl.BlockSpec(memory_space=pl.ANY),
                      pl.BlockSpec((N_EXPERTS, 2 * D_EXPERT), whole),
                      pl.BlockSpec(memory_space=pl.ANY),
                      pl.BlockSpec((N_EXPERTS, D_MODEL), whole)],
            out_specs=pl.BlockSpec((bm, D_MODEL), lambda i, *_: (i, 0)),
            scratch_shapes=[pltpu.VMEM((D_MODEL, 2 * D_EXPERT), F32),
                            pltpu.VMEM((D_EXPERT, D_MODEL), F32),
                            pltpu.VMEM((D_MODEL, 2 * D_EXPERT), BF16),
                            pltpu.VMEM((D_EXPERT, D_MODEL), BF16),
                            pltpu.SemaphoreType.DMA((2,))]),
        out_shape=jax.ShapeDtypeStruct((n_rows, D_MODEL), F32),
        compiler_params=_params(1),
        name="experts",
    )(blk_expert, n_used, first_flag, next_expert, blk_rows, xs, w1, b1, w2, b2)


def _combine_kernel(pcnt_ref, psrc_ref, pdst_ref, ntot_ref, ys_ref, h_ref, slab_ref, g_ref, o_ref,
                    buf_ref, sems):
    tm = h_ref.shape[0]
    i = pl.program_id(0)
    slot = i % 2

    def read(slot_, n, sorted_row, ys_row):
        return pltpu.make_async_copy(_rows(ys_ref, ys_row, n),
                                     _rows(buf_ref.at[slot_], sorted_row, n), sems.at[slot_])

    def gather(tile, slot_):
        _for_each_piece(tile, pcnt_ref, psrc_ref, pdst_ref, lambda n, s, d: read(slot_, n, s, d).start())

    @pl.when(i == 0)
    def _():
        buf_ref[...] = jnp.zeros(buf_ref.shape, F32)
        gather(0, 0)

    @pl.when(i + 1 < pl.num_programs(0))
    def _():
        gather(i + 1, 1 - slot)

    _wait_rows(ntot_ref[i], lambda n: read(slot, n, 0, 0))

    slab = slab_ref[...]
    lane = lax.broadcasted_iota(jnp.int32, (tm, R_SORTED), 1).astype(F32)
    wmat = jnp.zeros((tm, R_SORTED), F32)
    for k in range(TOP_K):
        wmat = jnp.where(lane == slab[:, k:k + 1], slab[:, TOP_K + k:TOP_K + k + 1], wmat)
    moe = _dot(wmat.astype(BF16), buf_ref[slot].astype(BF16))
    o_ref[...] = _rms(h_ref[...] + moe, g_ref[...])


def _combine(pieces, ys, h, slab, g):
    t = h.shape[0]
    tm = TM_ROUTE
    return pl.pallas_call(
        _combine_kernel,
        grid_spec=pltpu.PrefetchScalarGridSpec(
            num_scalar_prefetch=4,
            grid=(t // tm,),
            in_specs=[pl.BlockSpec(memory_space=pl.ANY),
                      pl.BlockSpec((tm, D_MODEL), lambda i, *_: (i, 0)),
                      pl.BlockSpec((tm, LANES), lambda i, *_: (i, 0)),
                      pl.BlockSpec((1, D_MODEL), lambda i, *_: (0, 0))],
            out_specs=pl.BlockSpec((tm, D_MODEL), lambda i, *_: (i, 0)),
            scratch_shapes=[pltpu.VMEM((2, R_SORTED, D_MODEL), F32),
                            pltpu.SemaphoreType.DMA((2,))]),
        out_shape=jax.ShapeDtypeStruct((t, D_MODEL), F32),
        compiler_params=_params(1),
        name="combine",
    )(*pieces, ys, h, slab, g)


def _pad_lanes(v, fill=0.0):
    return jnp.pad(v.reshape(1, -1), ((0, 0), (0, LANES - v.shape[-1])), constant_values=fill)


def kernel(x, norm_mix_g, w_in, conv_w, conv_b, dt_bias, a_log, d_skip, ssd_norm_g, attn_sinks,
           rel_bias, attn_norm_g, w_out, norm_ffn_g, w_router, b_router, w1, b1, w2, b2,
           norm_final_g):
    bsz, seq, d = x.shape
    t = bsz * seq
    nc = seq // CHUNK
    depth = w_in.shape[0]
    bias = _bias_table(rel_bias)

    assert depth == 1, "the final norm is fused into the combine kernel: single layer only"
    h = x.reshape(t, d)
    for layer in range(depth):
        y_mix = _mixer(h.reshape(bsz, seq, d), norm_mix_g[layer].reshape(1, d), jnp.swapaxes(w_in, 1, 2), layer,
                       conv_w[layer], conv_b[layer].reshape(1, -1),
                       _pad_lanes(dt_bias[layer]), _pad_lanes(a_log[layer]),
                       jnp.repeat(d_skip[layer], SSD_HEAD_DIM).reshape(1, -1),
                       ssd_norm_g[layer].reshape(1, -1), attn_sinks[layer], bias,
                       attn_norm_g[layer].reshape(1, -1)).reshape(t, -1)

        wr = w_router[layer].T.astype(BF16)
        br = jnp.broadcast_to(b_router[layer][:, None], (N_EXPERTS, LANES))
        h_mid, u, slab, post, tab = _route(
            h, y_mix, w_out, layer, norm_ffn_g[layer].reshape(1, d), wr, br)

        bm = BM_EXPERT
        ntiles = t // TM_ROUTE
        nblk = (t * TOP_K + ntiles * N_EXPERTS * (SUBLANES - 1)) // bm + N_EXPERTS
        tab = tab.reshape(ntiles, N_EXPERTS, LANES)
        cnt8, off8, segst = tab[:, :, 0], tab[:, :, 1], tab[:, :, 2]
        total8 = off8[-1] + cnt8[-1]
        padded = ((total8 + bm - 1) // bm) * bm
        pend = jnp.cumsum(padded)
        segdst = (pend - padded)[None, :] + off8
        n_used = pend[-1:] // bm
        eids = jnp.arange(N_EXPERTS, dtype=jnp.int32)
        blk = jnp.arange(nblk, dtype=jnp.int32)
        blk_expert = jnp.minimum(
            jnp.sum((blk[:, None] * bm >= pend[None, :]).astype(jnp.int32), axis=1), N_EXPERTS - 1)
        e_end = jnp.sum(jnp.where(blk_expert[:, None] == eids[None, :],
                                  ((pend - padded) + total8)[None, :], 0), axis=1)
        valid = jnp.where(blk < n_used[0], jnp.clip(e_end - blk * bm, 0, bm), 0)
        blk_rows = ((valid + EXPERT_ROW_STEP - 1) // EXPERT_ROW_STEP) * EXPERT_ROW_STEP
        piece_lo = jnp.arange(bm // EXPERT_ROW_STEP, dtype=jnp.int32)[None, :] * EXPERT_ROW_STEP
        zero_flag = jnp.where(piece_lo >= valid[:, None], 2,
                              (piece_lo + EXPERT_ROW_STEP > valid[:, None]).astype(jnp.int32)).reshape(-1)
        before = jnp.concatenate([blk_expert[:1], blk_expert[:-1]])
        first_flag = ((blk == 0) | (before != blk_expert)).astype(jnp.int32)
        cand = jnp.where((eids[None, :] > eids[:, None]) & (padded[None, :] > 0), eids[None, :], N_EXPERTS)
        next_nonempty = jnp.min(cand, axis=1)
        next_nonempty = jnp.where(next_nonempty == N_EXPERTS, -1, next_nonempty)
        next_expert = jnp.sum(jnp.where(blk_expert[:, None] == eids[None, :], next_nonempty[None, :], 0), axis=1)
        pieces = _piece_tables(cnt8, segst, segdst)

        xs = _dispatch(pieces, zero_flag, u, post)
        ys = _experts(blk_expert, n_used, first_flag, next_expert, blk_rows, xs, w1[layer],
                      b1[layer], w2[layer], b2[layer])
        h = _combine(pieces, ys, h_mid, slab, norm_final_g.reshape(1, d))
    return h.reshape(bsz, seq, d)
```

```python
import math

import numpy as np
import jax
import jax.numpy as jnp
from jax import lax
from jax.experimental import pallas as pl
from jax.experimental.pallas import tpu as pltpu

F32 = jnp.float32
BF16 = jnp.bfloat16

D_MODEL = 1024
SSD_HEADS = 8
SSD_HEAD_DIM = 64
D_SSD = SSD_HEADS * SSD_HEAD_DIM
SSD_GROUPS = 2
SSD_HEADS_PER_GROUP = SSD_HEADS // SSD_GROUPS
D_STATE = 128
CONV_WIDTH = 4
CHUNK = 128
D_CONV = D_SSD + 2 * SSD_GROUPS * D_STATE
ATTN_Q_HEADS = 8
ATTN_KV_HEADS = 2
ATTN_Q_PER_KV = ATTN_Q_HEADS // ATTN_KV_HEADS
ATTN_HEAD_DIM = 64
D_ATTN = ATTN_Q_HEADS * ATTN_HEAD_DIM
D_KV = ATTN_KV_HEADS * ATTN_HEAD_DIM
WINDOW = 128
REL_BUCKETS = 32
REL_MAX_DIST = 128
N_EXPERTS = 32
TOP_K = 4
D_EXPERT = D_MODEL
SWIGLU_LIMIT = 7.0
SWIGLU_ALPHA = 1.702
RMS_EPS = 1e-5

LANES = 128
SUBLANES = 8
V7X_VMEM_BYTES = 64 * 2 ** 20
VMEM_LIMIT = V7X_VMEM_BYTES * 7 // 8
NEG = -1e30
assert WINDOW == CHUNK, "the attention kernel sees exactly the current and the previous block"

TM_ROUTE = 512
BM_EXPERT = 512
EXPERT_ROW_STEP = 128
SEQS_PER_STEP = 2
CHUNKS_PER_STEP = 2
R_SORTED = TM_ROUTE * TOP_K + N_EXPERTS * SUBLANES

D_PROJ_PACKED = D_SSD + D_CONV + D_ATTN + 2 * D_KV + LANES


def _params(n_axes):
    return pltpu.CompilerParams(dimension_semantics=("arbitrary",) * n_axes,
                                vmem_limit_bytes=VMEM_LIMIT)


def _rms(x, g):
    return x * lax.rsqrt(jnp.mean(x * x, axis=-1, keepdims=True) + RMS_EPS) * g


def _silu(x):
    return x / (1.0 + jnp.exp(-x))


def _dot(a, b, **kw):
    return jnp.dot(a, b, preferred_element_type=F32, **kw)


def _dot_f32_by_mask(mask, x, mask_on_left=False):
    hi = x.astype(BF16)
    r1 = x - hi.astype(F32)
    mid = r1.astype(BF16)
    lo = (r1 - mid.astype(F32)).astype(BF16)
    out = None
    for piece in (hi, mid, lo):
        term = _dot(mask, piece) if mask_on_left else _dot(piece, mask)
        out = term if out is None else out + term
    return out


D_PACKED = D_MODEL // 2


def _pack_bf16_pairs(x):
    w = x.shape[1] // 2
    hi = lax.bitcast_convert_type(x[:, :w], jnp.int32)
    lo = lax.shift_right_logical(lax.bitcast_convert_type(x[:, w:], jnp.int32), 16)
    return hi | lo


def _unpack_bf16_pairs(v):
    hi = lax.bitcast_convert_type(v & jnp.int32(-65536), F32).astype(BF16)
    lo = lax.bitcast_convert_type(lax.shift_left(v, 16), F32).astype(BF16)
    return jnp.concatenate([hi, lo], axis=1)


def _t5_bucket_table():
    dist = CHUNK + np.arange(CHUNK)[:, None] - np.arange(2 * CHUNK)[None, :]
    in_window = (dist >= 0) & (dist < WINDOW)
    d = np.clip(dist, 0, REL_MAX_DIST)
    max_exact = REL_BUCKETS // 2
    large = max_exact + (np.log(np.maximum(d, 1).astype(np.float32) / max_exact)
                         / math.log(REL_MAX_DIST / max_exact)
                         * (REL_BUCKETS - max_exact)).astype(np.int32)
    large = np.minimum(large, REL_BUCKETS - 1)
    bucket = np.where(d < max_exact, d, large)
    return np.where(in_window, bucket, -1).astype(np.int32)


def _bias_kernel(rb_ref, bucket_ref, o_ref):
    bucket = bucket_ref[...]
    col = lax.broadcasted_iota(jnp.int32, bucket.shape, 1)
    for h in range(ATTN_Q_HEADS):
        acc = jnp.full(bucket.shape, NEG, F32)
        for b in range(REL_BUCKETS):
            acc = jnp.where(bucket == b, rb_ref[b, h], acc)
        o_ref[1, h] = acc
        o_ref[0, h] = jnp.where(col >= CHUNK, acc, NEG)


def _bias_table(rel_bias):
    bucket = jnp.asarray(_t5_bucket_table())
    return pl.pallas_call(
        _bias_kernel,
        out_shape=jax.ShapeDtypeStruct((2, ATTN_Q_HEADS, CHUNK, 2 * CHUNK), F32),
        in_specs=[pl.BlockSpec(memory_space=pltpu.SMEM),
                  pl.BlockSpec(memory_space=pltpu.VMEM)],
        out_specs=pl.BlockSpec(memory_space=pltpu.VMEM),
        name="bias_table",
    )(rel_bias, bucket)


def _mixer_kernel(sink_ref, x_ref, gm_ref, w_ref, cw_ref, cb_ref, dtb_ref, alog_ref, dskip_ref, gs_ref,
                  expand_ref, bias_ref, ga_ref, o_ref, wb_ref, state_ref, xpad_ref, kvp_ref):
    j = pl.program_id(1)

    @pl.when((pl.program_id(0) == 0) & (j == 0))
    def _():
        o2 = D_SSD + D_CONV
        o3 = o2 + SSD_HEADS
        n_qkv = D_ATTN + 2 * D_KV
        for lo, src_lo, n in ((0, 0, o2), (o2, o3, n_qkv)):
            for r in range(0, n, LANES):
                wb_ref[:, lo + r:lo + r + LANES] = w_ref[0, src_lo + r:src_lo + r + LANES, :].T.astype(BF16)
        wb_ref[:, o2 + n_qkv:] = jnp.concatenate(
            [w_ref[0, o2:o3, :], jnp.zeros((LANES - SSD_HEADS, D_MODEL), F32)], axis=0).T.astype(BF16)

    @pl.when(j == 0)
    def _():
        state_ref[...] = jnp.zeros(state_ref.shape, F32)
        xpad_ref[...] = jnp.zeros(xpad_ref.shape, F32)
        kvp_ref[...] = jnp.zeros(kvp_ref.shape, F32)

    ns, rows, _ = x_ref.shape
    bounds = np.cumsum([0, D_SSD, D_CONV, D_ATTN, 2 * D_KV, LANES])
    pieces = [(c, s) for c in range(rows // CHUNK) for s in range(ns)]

    def project(c, s):
        x = x_ref[s, c * CHUNK:(c + 1) * CHUNK, :]
        return _dot(_rms(x, gm_ref[...]).astype(BF16), wb_ref[...])

    proj = project(*pieces[0])
    for k, (c, s) in enumerate(pieces):
        proj_next = project(*pieces[k + 1]) if k + 1 < len(pieces) else None
        z, xbc, q, kv, dt = [proj[:, a:b] for a, b in zip(bounds[:-1], bounds[1:])]
        has_prev = jnp.minimum(j, 1) if c == 0 else 1
        y_ssd = _ssd_chunk(z, xbc, dt, cw_ref, cb_ref, dtb_ref, alog_ref, dskip_ref, gs_ref,
                           expand_ref, state_ref.at[s], xpad_ref.at[s])
        y_attn = _swa_block(sink_ref, q, kv, kvp_ref[s], has_prev, bias_ref, ga_ref)
        kvp_ref[s] = kv
        o_ref[s, c * CHUNK:(c + 1) * CHUNK, :] = jnp.concatenate([y_ssd, y_attn], axis=1)
        proj = proj_next


def _ssd_chunk(z, x_cur, dt_raw, cw_ref, cb_ref, dtb_ref, alog_ref, dskip_ref, g_ref,
               expand_ref, state_ref, xpad_ref):
    L = CHUNK
    G, R, P, N = SSD_GROUPS, SSD_HEADS_PER_GROUP, SSD_HEAD_DIM, D_STATE
    GW = R * P

    xpad = jnp.concatenate([xpad_ref[...], x_cur], axis=0)
    xpad_ref[...] = x_cur[L - SUBLANES:, :]
    acc = cb_ref[...] + cw_ref[CONV_WIDTH - 1:CONV_WIDTH, :] * x_cur
    for d in range(1, CONV_WIDTH):
        k = CONV_WIDTH - 1 - d
        acc = acc + cw_ref[k:k + 1, :] * pltpu.roll(xpad, d, axis=0)[SUBLANES:, :]
    xbc = _silu(acc)
    xs = xbc[:, :D_SSD]
    bm = xbc[:, D_SSD:D_SSD + G * N]
    cm = xbc[:, D_SSD + G * N:]

    dtr = dt_raw + dtb_ref[...]
    dt = jnp.maximum(dtr, 0.0) + jnp.log(1.0 + jnp.exp(-jnp.abs(dtr)))
    a_dt = dt * (-jnp.exp(alog_ref[...]))
    ri = lax.broadcasted_iota(jnp.int32, (L, L), 0)
    ci = lax.broadcasted_iota(jnp.int32, (L, L), 1)
    causal = ci <= ri
    a_cum = _dot_f32_by_mask(causal.astype(BF16), a_dt, mask_on_left=True)
    a_cum_t = _dot_f32_by_mask((ri <= ci).astype(BF16), a_dt.T)
    a_last = a_cum[L - 1:L, :]
    stack = jnp.concatenate(
        [dt, jnp.exp(a_cum), jnp.exp(a_last - a_cum),
         jnp.broadcast_to(jnp.exp(a_last), (SUBLANES, LANES))], axis=0)
    ex = _dot_f32_by_mask(expand_ref[...], stack)
    dt_x, ea_x, dte_x, cd_x = ex[0:L], ex[L:2 * L], ex[2 * L:3 * L], ex[3 * L:3 * L + 1]
    xdt = xs * dt_x

    ys = []
    for g in range(G):
        bm_g = bm[:, g * N:(g + 1) * N]
        cm_g = cm[:, g * N:(g + 1) * N].astype(BF16)
        cb = lax.dot_general(cm_g, bm_g.astype(BF16), (((1,), (1,)), ((), ())),
                             preferred_element_type=F32)
        xdt_g = xdt[:, g * GW:(g + 1) * GW]
        yd = []
        for r in range(R):
            h = g * R + r
            seg = a_cum[:, h:h + 1] - a_cum_t[h:h + 1, :]
            dec = jnp.exp(jnp.where(causal, seg, NEG))
            yd.append(_dot((cb * dec).astype(BF16), xdt_g[:, r * P:(r + 1) * P].astype(BF16)))
        y_diag = jnp.concatenate(yd, axis=1)
        st = state_ref[g]
        y_off = _dot(cm_g, st.astype(BF16)) * ea_x[:, g * GW:(g + 1) * GW]
        new = _dot(bm_g.T.astype(BF16), (xdt_g * dte_x[:, g * GW:(g + 1) * GW]).astype(BF16))
        state_ref[g] = st * cd_x[:, g * GW:(g + 1) * GW] + new
        ys.append(y_diag + y_off + xs[:, g * GW:(g + 1) * GW] * dskip_ref[:, g * GW:(g + 1) * GW])
    y = jnp.concatenate(ys, axis=1)
    return _rms(y * _silu(z), g_ref[...])


def _mixer(x, norm_g, w_in_t, layer, conv_w, conv_b, dt_bias, a_log, d_skip_x, ssd_norm_g, sinks, bias,
           attn_norm_g):
    bsz, seq, _ = x.shape
    rows = CHUNKS_PER_STEP * CHUNK
    nsteps = seq // rows
    expand = np.zeros((LANES, D_SSD), np.float32)
    for h in range(SSD_HEADS):
        expand[h, h * SSD_HEAD_DIM:(h + 1) * SSD_HEAD_DIM] = 1.0
    ns = SEQS_PER_STEP
    row = lambda b, j: (b, j, 0)
    fixed = lambda b, j: (0, 0)
    return pl.pallas_call(
        _mixer_kernel,
        grid=(bsz // ns, nsteps),
        in_specs=[pl.BlockSpec(memory_space=pltpu.SMEM),
                  pl.BlockSpec((ns, rows, D_MODEL), row),
                  pl.BlockSpec((1, D_MODEL), fixed),
                  pl.BlockSpec((1,) + w_in_t.shape[1:], lambda b, j: (layer, 0, 0)),
                  pl.BlockSpec((CONV_WIDTH, D_CONV), fixed),
                  pl.BlockSpec((1, D_CONV), fixed),
                  pl.BlockSpec((1, LANES), fixed),
                  pl.BlockSpec((1, LANES), fixed),
                  pl.BlockSpec((1, D_SSD), fixed),
                  pl.BlockSpec((1, D_SSD), fixed),
                  pl.BlockSpec((LANES, D_SSD), fixed),
                  pl.BlockSpec((2, ATTN_Q_HEADS, CHUNK, 2 * CHUNK), lambda b, j: (0, 0, 0, 0)),
                  pl.BlockSpec((1, D_ATTN), fixed)],
        out_specs=pl.BlockSpec((ns, rows, D_SSD + D_ATTN), row),
        out_shape=jax.ShapeDtypeStruct((bsz, seq, D_SSD + D_ATTN), F32),
        scratch_shapes=[pltpu.VMEM((D_MODEL, D_PROJ_PACKED), BF16),
                        pltpu.VMEM((ns, SSD_GROUPS, D_STATE, SSD_HEADS_PER_GROUP * SSD_HEAD_DIM), F32),
                        pltpu.VMEM((ns, SUBLANES, D_CONV), F32),
                        pltpu.VMEM((ns, CHUNK, 2 * D_KV), F32)],
        compiler_params=_params(2),
        name="mixer",
    )(sinks, x, norm_g, w_in_t, conv_w, conv_b, dt_bias, a_log, d_skip_x, ssd_norm_g,
      jnp.asarray(expand, dtype=BF16), bias, attn_norm_g)


def _swa_block(sink_ref, q, kv, kvp, has_prev, bias_ref, g_ref):
    L, Dh = CHUNK, ATTN_HEAD_DIM
    q = q * (1.0 / math.sqrt(Dh))
    outs = []
    for hk in range(ATTN_KV_HEADS):
        ks = slice(hk * Dh, (hk + 1) * Dh)
        vs = slice(D_KV + hk * Dh, D_KV + (hk + 1) * Dh)
        kc = jnp.concatenate([kvp[:, ks], kv[:, ks]], axis=0).astype(BF16)
        vc = jnp.concatenate([kvp[:, vs], kv[:, vs]], axis=0).astype(BF16)
        for g in range(ATTN_Q_PER_KV):
            h = hk * ATTN_Q_PER_KV + g
            qh = q[:, h * Dh:(h + 1) * Dh].astype(BF16)
            s = lax.dot_general(qh, kc, (((1,), (1,)), ((), ())), preferred_element_type=F32)
            s = s + bias_ref[has_prev, h]
            sink = sink_ref[h]
            m = jnp.maximum(jnp.max(s, axis=-1, keepdims=True), sink)
            p = jnp.exp(s - m)
            denom = jnp.sum(p, axis=-1, keepdims=True) + jnp.exp(sink - m)
            outs.append(_dot(p.astype(BF16), vc) / denom)
    return _rms(jnp.concatenate(outs, axis=1), g_ref[...])


def _route_kernel(x_ref, y_ref, wo_ref, g_ref, wr_ref, br_ref,
                  h_ref, u_ref, slab_ref, post_ref, tab_ref, run_ref, wob_ref):
    tm = x_ref.shape[0]

    @pl.when(pl.program_id(0) == 0)
    def _():
        run_ref[...] = jnp.zeros(run_ref.shape, F32)
        wob_ref[...] = wo_ref[0].astype(BF16)

    h = x_ref[...] + _dot(y_ref[...].astype(BF16), wob_ref[...])
    h_ref[...] = h
    u = _rms(h, g_ref[...])
    ub = u.astype(BF16)
    u_ref[...] = ub
    E = N_EXPERTS
    logits = lax.dot_general(wr_ref[...], ub, (((1,), (1,)), ((), ())), preferred_element_type=F32)
    logits = logits + jnp.concatenate([br_ref[...]] * (tm // LANES), axis=1)

    eidx = lax.broadcasted_iota(jnp.int32, (E, tm), 0).astype(F32)
    vals, idxs = [], []
    cur = logits
    for _ in range(TOP_K):
        m = jnp.max(cur, axis=0, keepdims=True)
        ix = jnp.min(jnp.where(cur == m, eidx, float(E)), axis=0, keepdims=True)
        vals.append(m)
        idxs.append(ix)
        cur = jnp.where(eidx == ix, NEG, cur)
    es = [jnp.exp(v - vals[0]) for v in vals]
    den = sum(es[1:], es[0])

    onehot = jnp.zeros((E, tm), F32)
    for ix in idxs:
        onehot = onehot + (eidx == ix).astype(F32)
    ri = lax.broadcasted_iota(jnp.int32, (tm, tm), 0)
    ci = lax.broadcasted_iota(jnp.int32, (tm, tm), 1)
    before = _dot(onehot.astype(BF16), (ri < ci).astype(BF16))
    cnt = jnp.sum(onehot, axis=1, keepdims=True)
    cnt8 = jnp.floor((cnt + (SUBLANES - 1)) * (1.0 / SUBLANES)) * SUBLANES
    el = lax.broadcasted_iota(jnp.int32, (E, E), 0)
    ec = lax.broadcasted_iota(jnp.int32, (E, E), 1)
    seg_start = _dot_f32_by_mask((ec < el).astype(BF16), jnp.broadcast_to(cnt8, (E, LANES)),
                                 mask_on_left=True)[:, 0:1]
    where_to = before + seg_start

    rows = [jnp.sum(jnp.where(eidx == idxs[k], where_to, 0.0), axis=0, keepdims=True)
            for k in range(TOP_K)]
    rows += [es[k] / den for k in range(TOP_K)]
    stack = jnp.concatenate(rows, axis=0)
    post_ref[...] = stack.astype(jnp.int32)
    slab_ref[...] = jnp.concatenate([stack, jnp.zeros((LANES - 2 * TOP_K, tm), F32)], axis=0).T

    lane = lax.broadcasted_iota(jnp.int32, (E, LANES), 1)
    tab = jnp.where(lane == 0, cnt8, jnp.where(lane == 1, run_ref[...], jnp.where(lane == 2, seg_start, 0.0)))
    tab_ref[...] = tab.astype(jnp.int32)
    run_ref[...] = run_ref[...] + cnt8


def _route(x2, y_mix, w_out, layer, g, wr, br):
    t = x2.shape[0]
    tm = TM_ROUTE
    row = lambda i: (i, 0)
    fixed = lambda i: (0, 0)
    return pl.pallas_call(
        _route_kernel,
        grid=(t // tm,),
        in_specs=[pl.BlockSpec((tm, D_MODEL), row),
                  pl.BlockSpec((tm, D_SSD + D_ATTN), row),
                  pl.BlockSpec((1, D_SSD + D_ATTN, D_MODEL), lambda i: (layer, 0, 0)),
                  pl.BlockSpec((1, D_MODEL), fixed),
                  pl.BlockSpec((N_EXPERTS, D_MODEL), fixed),
                  pl.BlockSpec((N_EXPERTS, LANES), fixed)],
        out_specs=[pl.BlockSpec((tm, D_MODEL), row),
                   pl.BlockSpec((tm, D_MODEL), row),
                   pl.BlockSpec((tm, LANES), row),
                   pl.BlockSpec((SUBLANES, tm), lambda i: (0, i)),
                   pl.BlockSpec((N_EXPERTS, LANES), row)],
        out_shape=[jax.ShapeDtypeStruct((t, D_MODEL), F32),
                   jax.ShapeDtypeStruct((t, D_MODEL), BF16),
                   jax.ShapeDtypeStruct((t, LANES), F32),
                   jax.ShapeDtypeStruct((SUBLANES, t), jnp.int32),
                   jax.ShapeDtypeStruct((t // tm * N_EXPERTS, LANES), jnp.int32)],
        scratch_shapes=[pltpu.VMEM((N_EXPERTS, LANES), F32),
                        pltpu.VMEM((D_SSD + D_ATTN, D_MODEL), BF16)],
        compiler_params=_params(1),
        name="route",
    )(x2, y_mix, w_out, g, wr, br)


SEG_SIZE_BITS = (TM_ROUTE // SUBLANES).bit_length()


TILE_SIZE_BITS = (R_SORTED // SUBLANES).bit_length()


def _piece_tables(cnt8, segst, segdst):
    n = (cnt8 // SUBLANES)[:, None, :]
    b = jnp.arange(SEG_SIZE_BITS, dtype=jnp.int32)[None, :, None]
    has = (n >> b) & 1
    off = ((n >> (b + 1)) << (b + 1)) * SUBLANES
    rank = jnp.cumsum(has, axis=2) - has
    place = (has[..., None] == 1) & (rank[..., None] == jnp.arange(N_EXPERTS, dtype=jnp.int32))
    dense = lambda v: jnp.sum(jnp.where(place, v[..., None], 0), axis=2).reshape(-1)
    return (jnp.sum(has, axis=2).reshape(-1), dense(segst[:, None, :] + off),
            dense(segdst[:, None, :] + off), jnp.sum(cnt8, axis=1) // SUBLANES)


def _for_each_piece(tile, pcnt_ref, psrc_ref, pdst_ref, fn):
    for b in range(SEG_SIZE_BITS):
        base = (tile * SEG_SIZE_BITS + b) * N_EXPERTS

        def body(p, carry, b=b, base=base):
            fn(SUBLANES << b, psrc_ref[base + p], pdst_ref[base + p])
            return carry

        lax.fori_loop(0, pcnt_ref[tile * SEG_SIZE_BITS + b], body, 0)


def _wait_rows(n_tiles8, descriptor):
    for b in range(TILE_SIZE_BITS):
        @pl.when(((n_tiles8 >> b) & 1) == 1)
        def _(b=b):
            descriptor(SUBLANES << b).wait()


def _rows(ref, row, n):
    if not isinstance(row, int):
        row = pl.multiple_of(row, SUBLANES)
    return ref.at[pl.ds(row, n), :]


def _dispatch_kernel(pcnt_ref, psrc_ref, pdst_ref, ntot_ref, zflag_ref, u_ref, post_ref, xs_ref,
                     buf_ref, zero_ref, sems, zsem):
    tm = u_ref.shape[0]
    i = pl.program_id(0)
    slot = i % 2
    bm = zero_ref.shape[0]

    def fill(b, flag):
        return pltpu.make_async_copy(zero_ref, xs_ref.at[pl.ds(b * bm, bm), :], zsem.at[flag - 1])

    def for_flagged(flag, action):
        def body(b, carry):
            @pl.when(zflag_ref[b] == flag)
            def _():
                action(fill(b, flag))
            return carry

        lax.fori_loop(0, zflag_ref.shape[0], body, 0)

    @pl.when(i == 0)
    def _():
        zero_ref[...] = jnp.zeros(zero_ref.shape, jnp.int32)
        for_flagged(1, lambda c: c.start())
        for_flagged(2, lambda c: c.start())
        for_flagged(1, lambda c: c.wait())

    @pl.when(i == pl.num_programs(0) - 1)
    def _():
        for_flagged(2, lambda c: c.wait())

    pos = post_ref[...]
    j = lax.broadcasted_iota(jnp.int32, (R_SORTED, tm), 0)
    sel = jnp.zeros((R_SORTED, tm), F32)
    for k in range(TOP_K):
        sel = jnp.where(j == pos[k:k + 1, :], 1.0, sel)
    buf_ref[slot] = _pack_bf16_pairs(_dot(sel.astype(BF16), u_ref[...]))

    def write(slot_, n, src_row, dst_row):
        return pltpu.make_async_copy(_rows(buf_ref.at[slot_], src_row, n),
                                     _rows(xs_ref, dst_row, n), sems.at[slot_])

    _for_each_piece(i, pcnt_ref, psrc_ref, pdst_ref, lambda n, s, d: write(slot, n, s, d).start())

    @pl.when(i > 0)
    def _():
        _wait_rows(ntot_ref[i - 1], lambda n: write(1 - slot, n, 0, 0))

    @pl.when(i == pl.num_programs(0) - 1)
    def _():
        _wait_rows(ntot_ref[i], lambda n: write(slot, n, 0, 0))


def _dispatch(pieces, zero_flag, u, post):
    t = u.shape[0]
    tm = TM_ROUTE
    n_rows = zero_flag.shape[0] * EXPERT_ROW_STEP
    return pl.pallas_call(
        _dispatch_kernel,
        grid_spec=pltpu.PrefetchScalarGridSpec(
            num_scalar_prefetch=5,
            grid=(t // tm,),
            in_specs=[pl.BlockSpec((tm, D_MODEL), lambda i, *_: (i, 0)),
                      pl.BlockSpec((SUBLANES, tm), lambda i, *_: (0, i))],
            out_specs=pl.BlockSpec(memory_space=pl.ANY),
            scratch_shapes=[pltpu.VMEM((2, R_SORTED, D_PACKED), jnp.int32),
                            pltpu.VMEM((EXPERT_ROW_STEP, D_PACKED), jnp.int32),
                            pltpu.SemaphoreType.DMA((2,)),
                            pltpu.SemaphoreType.DMA((2,))]),
        out_shape=jax.ShapeDtypeStruct((n_rows, D_PACKED), jnp.int32),
        compiler_params=_params(1),
        name="dispatch",
    )(*pieces, zero_flag, u, post)


def _expert_kernel(be_ref, nu_ref, first_ref, next_ref, rows_ref, xs_ref, w1_ref, b1_ref, w2_ref, b2_ref,
                   ys_ref, w1f_ref, w2f_ref, w1b_ref, w2b_ref, sems):
    i = pl.program_id(0)

    def fetch(e):
        return (pltpu.make_async_copy(w1_ref.at[e], w1f_ref, sems.at[0]),
                pltpu.make_async_copy(w2_ref.at[e], w2f_ref, sems.at[1]))

    @pl.when(i >= nu_ref[0])
    def _():
        ys_ref[...] = jnp.zeros(ys_ref.shape, jnp.int32)

    @pl.when(i < nu_ref[0])
    def _():
        e = be_ref[i]

        @pl.when(i == 0)
        def _():
            for c in fetch(e):
                c.start()

        @pl.when(first_ref[i] != 0)
        def _():
            for c in fetch(e):
                c.wait()
            w1b_ref[...] = w1f_ref[...].astype(BF16)
            w2b_ref[...] = w2f_ref[...].astype(BF16)

            @pl.when(next_ref[i] >= 0)
            def _():
                for c in fetch(next_ref[i]):
                    c.start()

        bm = xs_ref.shape[0]
        for m in range(EXPERT_ROW_STEP, bm + 1, EXPERT_ROW_STEP):
            @pl.when(rows_ref[i] == m)
            def _(m=m):
                hid = _dot(_unpack_bf16_pairs(xs_ref[0:m, :]), w1b_ref[...]) + b1_ref[pl.ds(e, 1), :]
                x_glu = jnp.minimum(hid[:, :D_EXPERT], SWIGLU_LIMIT)
                x_lin = jnp.clip(hid[:, D_EXPERT:], -SWIGLU_LIMIT, SWIGLU_LIMIT)
                act = x_glu / (1.0 + jnp.exp(-SWIGLU_ALPHA * x_glu)) * (x_lin + 1.0)
                y = _dot(act.astype(BF16), w2b_ref[...]) + b2_ref[pl.ds(e, 1), :]
                ys_ref[0:m, :] = _pack_bf16_pairs(y.astype(BF16).astype(F32))
                if m < bm:
                    ys_ref[m:, :] = jnp.zeros((bm - m, D_PACKED), jnp.int32)


def _experts(blk_expert, n_used, first_flag, next_expert, blk_rows, xs, w1, b1, w2, b2):
    n_rows = xs.shape[0]
    bm = BM_EXPERT
    nblk = n_rows // bm
    last = lambda i, nu: jnp.maximum(jnp.minimum(i, nu[0] - 1), 0)
    row = lambda i, be, nu, *_: (last(i, nu), 0)
    whole = lambda i, *_: (0, 0)
    return pl.pallas_call(
        _expert_kernel,
        grid_spec=pltpu.PrefetchScalarGridSpec(
            num_scalar_prefetch=5,
            grid=(nblk,),
            in_specs=[pl.BlockSpec((bm, D_PACKED), row),
                      pl.BlockSpec(memory_space=pl.ANY),
                      pl.BlockSpec((N_EXPERTS, 2 * D_EXPERT), whole),
                      pl.BlockSpec(memory_space=pl.ANY),
                      pl.BlockSpec((N_EXPERTS, D_MODEL), whole)],
            out_specs=pl.BlockSpec((bm, D_PACKED), lambda i, *_: (i, 0)),
            scratch_shapes=[pltpu.VMEM((D_MODEL, 2 * D_EXPERT), F32),
                            pltpu.VMEM((D_EXPERT, D_MODEL), F32),
                            pltpu.VMEM((D_MODEL, 2 * D_EXPERT), BF16),
                            pltpu.VMEM((D_EXPERT, D_MODEL), BF16),
                            pltpu.SemaphoreType.DMA((2,))]),
        out_shape=jax.ShapeDtypeStruct((n_rows, D_PACKED), jnp.int32),
        compiler_params=_params(1),
        name="experts",
    )(blk_expert, n_used, first_flag, next_expert, blk_rows, xs, w1, b1, w2, b2)


def _combine_kernel(pcnt_ref, psrc_ref, pdst_ref, ntot_ref, ys_ref, h_ref, slab_ref, g_ref, o_ref,
                    buf_ref, sems):
    tm = h_ref.shape[0]
    i = pl.program_id(0)
    slot = i % 2

    def read(slot_, n, sorted_row, ys_row):
        return pltpu.make_async_copy(_rows(ys_ref, ys_row, n),
                                     _rows(buf_ref.at[slot_], sorted_row, n), sems.at[slot_])

    def gather(tile, slot_):
        _for_each_piece(tile, pcnt_ref, psrc_ref, pdst_ref, lambda n, s, d: read(slot_, n, s, d).start())

    @pl.when(i == 0)
    def _():
        buf_ref[...] = jnp.zeros(buf_ref.shape, jnp.int32)
        gather(0, 0)

    @pl.when(i + 1 < pl.num_programs(0))
    def _():
        gather(i + 1, 1 - slot)

    _wait_rows(ntot_ref[i], lambda n: read(slot, n, 0, 0))

    slab = slab_ref[...]
    lane = lax.broadcasted_iota(jnp.int32, (tm, R_SORTED), 1).astype(F32)
    wmat = jnp.zeros((tm, R_SORTED), F32)
    for k in range(TOP_K):
        wmat = jnp.where(lane == slab[:, k:k + 1], slab[:, TOP_K + k:TOP_K + k + 1], wmat)
    moe = _dot(wmat.astype(BF16), _unpack_bf16_pairs(buf_ref[slot]))
    o_ref[...] = _rms(h_ref[...] + moe, g_ref[...])


def _combine(pieces, ys, h, slab, g):
    t = h.shape[0]
    tm = TM_ROUTE
    return pl.pallas_call(
        _combine_kernel,
        grid_spec=pltpu.PrefetchScalarGridSpec(
            num_scalar_prefetch=4,
            grid=(t // tm,),
            in_specs=[pl.BlockSpec(memory_space=pl.ANY),
                      pl.BlockSpec((tm, D_MODEL), lambda i, *_: (i, 0)),
                      pl.BlockSpec((tm, LANES), lambda i, *_: (i, 0)),
                      pl.BlockSpec((1, D_MODEL), lambda i, *_: (0, 0))],
            out_specs=pl.BlockSpec((tm, D_MODEL), lambda i, *_: (i, 0)),
            scratch_shapes=[pltpu.VMEM((2, R_SORTED, D_PACKED), jnp.int32),
                            pltpu.SemaphoreType.DMA((2,))]),
        out_shape=jax.ShapeDtypeStruct((t, D_MODEL), F32),
        compiler_params=_params(1),
        name="combine",
    )(*pieces, ys, h, slab, g)


def _pad_lanes(v, fill=0.0):
    return jnp.pad(v.reshape(1, -1), ((0, 0), (0, LANES - v.shape[-1])), constant_values=fill)


def kernel(x, norm_mix_g, w_in, conv_w, conv_b, dt_bias, a_log, d_skip, ssd_norm_g, attn_sinks,
           rel_bias, attn_norm_g, w_out, norm_ffn_g, w_router, b_router, w1, b1, w2, b2,
           norm_final_g):
    bsz, seq, d = x.shape
    t = bsz * seq
    nc = seq // CHUNK
    depth = w_in.shape[0]
    bias = _bias_table(rel_bias)

    assert depth == 1, "the final norm is fused into the combine kernel: single layer only"
    h = x.reshape(t, d)
    for layer in range(depth):
        y_mix = _mixer(h.reshape(bsz, seq, d), norm_mix_g[layer].reshape(1, d), jnp.swapaxes(w_in, 1, 2), layer,
                       conv_w[layer], conv_b[layer].reshape(1, -1),
                       _pad_lanes(dt_bias[layer]), _pad_lanes(a_log[layer]),
                       jnp.repeat(d_skip[layer], SSD_HEAD_DIM).reshape(1, -1),
                       ssd_norm_g[layer].reshape(1, -1), attn_sinks[layer], bias,
                       attn_norm_g[layer].reshape(1, -1)).reshape(t, -1)

        wr = w_router[layer].T.astype(BF16)
        br = jnp.broadcast_to(b_router[layer][:, None], (N_EXPERTS, LANES))
        h_mid, u, slab, post, tab = _route(
            h, y_mix, w_out, layer, norm_ffn_g[layer].reshape(1, d), wr, br)

        bm = BM_EXPERT
        ntiles = t // TM_ROUTE
        nblk = (t * TOP_K + ntiles * N_EXPERTS * (SUBLANES - 1)) // bm + N_EXPERTS
        tab = tab.reshape(ntiles, N_EXPERTS, LANES)
        cnt8, off8, segst = tab[:, :, 0], tab[:, :, 1], tab[:, :, 2]
        total8 = off8[-1] + cnt8[-1]
        padded = ((total8 + bm - 1) // bm) * bm
        pend = jnp.cumsum(padded)
        segdst = (pend - padded)[None, :] + off8
        n_used = pend[-1:] // bm
        eids = jnp.arange(N_EXPERTS, dtype=jnp.int32)
        blk = jnp.arange(nblk, dtype=jnp.int32)
        blk_expert = jnp.minimum(
            jnp.sum((blk[:, None] * bm >= pend[None, :]).astype(jnp.int32), axis=1), N_EXPERTS - 1)
        e_end = jnp.sum(jnp.where(blk_expert[:, None] == eids[None, :],
                                  ((pend - padded) + total8)[None, :], 0), axis=1)
        valid = jnp.where(blk < n_used[0], jnp.clip(e_end - blk * bm, 0, bm), 0)
        blk_rows = ((valid + EXPERT_ROW_STEP - 1) // EXPERT_ROW_STEP) * EXPERT_ROW_STEP
        piece_lo = jnp.arange(bm // EXPERT_ROW_STEP, dtype=jnp.int32)[None, :] * EXPERT_ROW_STEP
        zero_flag = jnp.where(piece_lo >= valid[:, None], 2,
                              (piece_lo + EXPERT_ROW_STEP > valid[:, None]).astype(jnp.int32)).reshape(-1)
        before = jnp.concatenate([blk_expert[:1], blk_expert[:-1]])
        first_flag = ((blk == 0) | (before != blk_expert)).astype(jnp.int32)
        cand = jnp.where((eids[None, :] > eids[:, None]) & (padded[None, :] > 0), eids[None, :], N_EXPERTS)
        next_nonempty = jnp.min(cand, axis=1)
        next_nonempty = jnp.where(next_nonempty == N_EXPERTS, -1, next_nonempty)
        next_expert = jnp.sum(jnp.where(blk_expert[:, None] == eids[None, :], next_nonempty[None, :], 0), axis=1)
        pieces = _piece_tables(cnt8, segst, segdst)

        xs = _dispatch(pieces, zero_flag, u, post)
        ys = _experts(blk_expert, n_used, first_flag, next_expert, blk_rows, xs, w1[layer],
                      b1[layer], w2[layer], b2[layer])
        h = _combine(pieces, ys, h_mid, slab, norm_final_g.reshape(1, d))
    return h.reshape(bsz, seq, d)
```

```python
import math

import numpy as np
import jax
import jax.numpy as jnp
from jax import lax
from jax.experimental import pallas as pl
from jax.experimental.pallas import tpu as pltpu

F32 = jnp.float32
BF16 = jnp.bfloat16

D_MODEL = 1024
SSD_HEADS = 8
SSD_HEAD_DIM = 64
D_SSD = SSD_HEADS * SSD_HEAD_DIM
SSD_GROUPS = 2
SSD_HEADS_PER_GROUP = SSD_HEADS // SSD_GROUPS
D_STATE = 128
CONV_WIDTH = 4
CHUNK = 128
D_CONV = D_SSD + 2 * SSD_GROUPS * D_STATE
ATTN_Q_HEADS = 8
ATTN_KV_HEADS = 2
ATTN_Q_PER_KV = ATTN_Q_HEADS // ATTN_KV_HEADS
ATTN_HEAD_DIM = 64
D_ATTN = ATTN_Q_HEADS * ATTN_HEAD_DIM
D_KV = ATTN_KV_HEADS * ATTN_HEAD_DIM
WINDOW = 128
REL_BUCKETS = 32
REL_MAX_DIST = 128
N_EXPERTS = 32
TOP_K = 4
D_EXPERT = D_MODEL
SWIGLU_LIMIT = 7.0
SWIGLU_ALPHA = 1.702
RMS_EPS = 1e-5

LANES = 128
SUBLANES = 8
V7X_VMEM_BYTES = 64 * 2 ** 20
VMEM_LIMIT = V7X_VMEM_BYTES * 7 // 8
NEG = -1e30
assert WINDOW == CHUNK, "the attention kernel sees exactly the current and the previous block"

TM_ROUTE = 512
BM_EXPERT = 512
EXPERT_ROW_STEP = 128
SEQS_PER_STEP = 2
CHUNKS_PER_STEP = 4
R_SORTED = TM_ROUTE * TOP_K + N_EXPERTS * SUBLANES

D_PROJ_PACKED = D_SSD + D_CONV + D_ATTN + 2 * D_KV + LANES


def _params(n_axes):
    return pltpu.CompilerParams(dimension_semantics=("arbitrary",) * n_axes,
                                vmem_limit_bytes=VMEM_LIMIT)


def _rms(x, g):
    return x * lax.rsqrt(jnp.mean(x * x, axis=-1, keepdims=True) + RMS_EPS) * g


def _silu(x):
    return x / (1.0 + jnp.exp(-x))


def _dot(a, b, **kw):
    return jnp.dot(a, b, preferred_element_type=F32, **kw)


def _dot_f32_by_mask(mask, x, mask_on_left=False):
    hi = x.astype(BF16)
    r1 = x - hi.astype(F32)
    mid = r1.astype(BF16)
    lo = (r1 - mid.astype(F32)).astype(BF16)
    out = None
    for piece in (hi, mid, lo):
        term = _dot(mask, piece) if mask_on_left else _dot(piece, mask)
        out = term if out is None else out + term
    return out


D_PACKED = D_MODEL // 2


def _pack_bf16_pairs(x):
    w = x.shape[1] // 2
    hi = lax.bitcast_convert_type(x[:, :w], jnp.int32)
    lo = lax.shift_right_logical(lax.bitcast_convert_type(x[:, w:], jnp.int32), 16)
    return hi | lo


def _unpack_bf16_pairs(v):
    hi = lax.bitcast_convert_type(v & jnp.int32(-65536), F32).astype(BF16)
    lo = lax.bitcast_convert_type(lax.shift_left(v, 16), F32).astype(BF16)
    return jnp.concatenate([hi, lo], axis=1)


def _t5_bucket_table():
    dist = CHUNK + np.arange(CHUNK)[:, None] - np.arange(2 * CHUNK)[None, :]
    in_window = (dist >= 0) & (dist < WINDOW)
    d = np.clip(dist, 0, REL_MAX_DIST)
    max_exact = REL_BUCKETS // 2
    large = max_exact + (np.log(np.maximum(d, 1).astype(np.float32) / max_exact)
                         / math.log(REL_MAX_DIST / max_exact)
                         * (REL_BUCKETS - max_exact)).astype(np.int32)
    large = np.minimum(large, REL_BUCKETS - 1)
    bucket = np.where(d < max_exact, d, large)
    return np.where(in_window, bucket, -1).astype(np.int32)


def _bias_kernel(rb_ref, bucket_ref, o_ref):
    bucket = bucket_ref[...]
    col = lax.broadcasted_iota(jnp.int32, bucket.shape, 1)
    for h in range(ATTN_Q_HEADS):
        acc = jnp.full(bucket.shape, NEG, F32)
        for b in range(REL_BUCKETS):
            acc = jnp.where(bucket == b, rb_ref[b, h], acc)
        o_ref[1, h] = acc
        o_ref[0, h] = jnp.where(col >= CHUNK, acc, NEG)


def _bias_table(rel_bias):
    bucket = jnp.asarray(_t5_bucket_table())
    return pl.pallas_call(
        _bias_kernel,
        out_shape=jax.ShapeDtypeStruct((2, ATTN_Q_HEADS, CHUNK, 2 * CHUNK), F32),
        in_specs=[pl.BlockSpec(memory_space=pltpu.SMEM),
                  pl.BlockSpec(memory_space=pltpu.VMEM)],
        out_specs=pl.BlockSpec(memory_space=pltpu.VMEM),
        name="bias_table",
    )(rel_bias, bucket)


def _mixer_kernel(sink_ref, x_ref, gm_ref, w_ref, cw_ref, cb_ref, dtb_ref, alog_ref, dskip_ref, gs_ref,
                  expand_ref, bias_ref, ga_ref, o_ref, wb_ref, state_ref, xpad_ref, kvp_ref):
    j = pl.program_id(1)

    @pl.when((pl.program_id(0) == 0) & (j == 0))
    def _():
        o2 = D_SSD + D_CONV
        o3 = o2 + SSD_HEADS
        n_qkv = D_ATTN + 2 * D_KV
        for lo, src_lo, n in ((0, 0, o2), (o2, o3, n_qkv)):
            for r in range(0, n, LANES):
                wb_ref[:, lo + r:lo + r + LANES] = w_ref[0, src_lo + r:src_lo + r + LANES, :].T.astype(BF16)
        wb_ref[:, o2 + n_qkv:] = jnp.concatenate(
            [w_ref[0, o2:o3, :], jnp.zeros((LANES - SSD_HEADS, D_MODEL), F32)], axis=0).T.astype(BF16)

    @pl.when(j == 0)
    def _():
        state_ref[...] = jnp.zeros(state_ref.shape, F32)
        xpad_ref[...] = jnp.zeros(xpad_ref.shape, F32)
        kvp_ref[...] = jnp.zeros(kvp_ref.shape, F32)

    ns, rows, _ = x_ref.shape
    bounds = np.cumsum([0, D_SSD, D_CONV, D_ATTN, 2 * D_KV, LANES])
    pieces = [(c, s) for c in range(rows // CHUNK) for s in range(ns)]

    def project(c, s):
        x = x_ref[s, c * CHUNK:(c + 1) * CHUNK, :]
        return _dot(_rms(x, gm_ref[...]).astype(BF16), wb_ref[...])

    proj = project(*pieces[0])
    for k, (c, s) in enumerate(pieces):
        proj_next = project(*pieces[k + 1]) if k + 1 < len(pieces) else None
        z, xbc, q, kv, dt = [proj[:, a:b] for a, b in zip(bounds[:-1], bounds[1:])]
        has_prev = jnp.minimum(j, 1) if c == 0 else 1
        y_ssd = _ssd_chunk(z, xbc, dt, cw_ref, cb_ref, dtb_ref, alog_ref, dskip_ref, gs_ref,
                           expand_ref, state_ref.at[s], xpad_ref.at[s])
        y_attn = _swa_block(sink_ref, q, kv, kvp_ref[s], has_prev, bias_ref, ga_ref)
        kvp_ref[s] = kv
        o_ref[s, c * CHUNK:(c + 1) * CHUNK, :] = jnp.concatenate([y_ssd, y_attn], axis=1).astype(BF16)
        proj = proj_next


def _ssd_chunk(z, x_cur, dt_raw, cw_ref, cb_ref, dtb_ref, alog_ref, dskip_ref, g_ref,
               expand_ref, state_ref, xpad_ref):
    L = CHUNK
    G, R, P, N = SSD_GROUPS, SSD_HEADS_PER_GROUP, SSD_HEAD_DIM, D_STATE
    GW = R * P

    xpad = jnp.concatenate([xpad_ref[...], x_cur], axis=0)
    xpad_ref[...] = x_cur[L - SUBLANES:, :]
    acc = cb_ref[...] + cw_ref[CONV_WIDTH - 1:CONV_WIDTH, :] * x_cur
    for d in range(1, CONV_WIDTH):
        k = CONV_WIDTH - 1 - d
        acc = acc + cw_ref[k:k + 1, :] * pltpu.roll(xpad, d, axis=0)[SUBLANES:, :]
    xbc = _silu(acc)
    xs = xbc[:, :D_SSD]
    bm = xbc[:, D_SSD:D_SSD + G * N]
    cm = xbc[:, D_SSD + G * N:]

    dtr = dt_raw + dtb_ref[...]
    dt = jnp.maximum(dtr, 0.0) + jnp.log(1.0 + jnp.exp(-jnp.abs(dtr)))
    a_dt = dt * (-jnp.exp(alog_ref[...]))
    ri = lax.broadcasted_iota(jnp.int32, (L, L), 0)
    ci = lax.broadcasted_iota(jnp.int32, (L, L), 1)
    causal = ci <= ri
    a_cum = _dot_f32_by_mask(causal.astype(BF16), a_dt, mask_on_left=True)
    a_cum_t = _dot_f32_by_mask((ri <= ci).astype(BF16), a_dt.T)
    a_last = a_cum[L - 1:L, :]
    stack = jnp.concatenate(
        [dt, jnp.exp(a_cum), jnp.exp(a_last - a_cum),
         jnp.broadcast_to(jnp.exp(a_last), (SUBLANES, LANES))], axis=0)
    ex = _dot_f32_by_mask(expand_ref[...], stack)
    dt_x, ea_x, dte_x, cd_x = ex[0:L], ex[L:2 * L], ex[2 * L:3 * L], ex[3 * L:3 * L + 1]
    xdt = xs * dt_x

    ys = []
    for g in range(G):
        bm_g = bm[:, g * N:(g + 1) * N]
        cm_g = cm[:, g * N:(g + 1) * N].astype(BF16)
        cb = lax.dot_general(cm_g, bm_g.astype(BF16), (((1,), (1,)), ((), ())),
                             preferred_element_type=F32)
        xdt_g = xdt[:, g * GW:(g + 1) * GW]
        yd = []
        for r in range(R):
            h = g * R + r
            seg = a_cum[:, h:h + 1] - a_cum_t[h:h + 1, :]
            dec = jnp.exp(jnp.where(causal, seg, NEG))
            yd.append(_dot((cb * dec).astype(BF16), xdt_g[:, r * P:(r + 1) * P].astype(BF16)))
        y_diag = jnp.concatenate(yd, axis=1)
        st = state_ref[g]
        y_off = _dot(cm_g, st.astype(BF16)) * ea_x[:, g * GW:(g + 1) * GW]
        new = _dot(bm_g.T.astype(BF16), (xdt_g * dte_x[:, g * GW:(g + 1) * GW]).astype(BF16))
        state_ref[g] = st * cd_x[:, g * GW:(g + 1) * GW] + new
        ys.append(y_diag + y_off + xs[:, g * GW:(g + 1) * GW] * dskip_ref[:, g * GW:(g + 1) * GW])
    y = jnp.concatenate(ys, axis=1)
    return _rms(y * _silu(z), g_ref[...])


def _mixer(x, norm_g, w_in_t, layer, conv_w, conv_b, dt_bias, a_log, d_skip_x, ssd_norm_g, sinks, bias,
           attn_norm_g):
    bsz, seq, _ = x.shape
    rows = CHUNKS_PER_STEP * CHUNK
    nsteps = seq // rows
    expand = np.zeros((LANES, D_SSD), np.float32)
    for h in range(SSD_HEADS):
        expand[h, h * SSD_HEAD_DIM:(h + 1) * SSD_HEAD_DIM] = 1.0
    ns = SEQS_PER_STEP
    row = lambda b, j: (b, j, 0)
    fixed = lambda b, j: (0, 0)
    return pl.pallas_call(
        _mixer_kernel,
        grid=(bsz // ns, nsteps),
        in_specs=[pl.BlockSpec(memory_space=pltpu.SMEM),
                  pl.BlockSpec((ns, rows, D_MODEL), row),
                  pl.BlockSpec((1, D_MODEL), fixed),
                  pl.BlockSpec((1,) + w_in_t.shape[1:], lambda b, j: (layer, 0, 0)),
                  pl.BlockSpec((CONV_WIDTH, D_CONV), fixed),
                  pl.BlockSpec((1, D_CONV), fixed),
                  pl.BlockSpec((1, LANES), fixed),
                  pl.BlockSpec((1, LANES), fixed),
                  pl.BlockSpec((1, D_SSD), fixed),
                  pl.BlockSpec((1, D_SSD), fixed),
                  pl.BlockSpec((LANES, D_SSD), fixed),
                  pl.BlockSpec((2, ATTN_Q_HEADS, CHUNK, 2 * CHUNK), lambda b, j: (0, 0, 0, 0)),
                  pl.BlockSpec((1, D_ATTN), fixed)],
        out_specs=pl.BlockSpec((ns, rows, D_SSD + D_ATTN), row),
        out_shape=jax.ShapeDtypeStruct((bsz, seq, D_SSD + D_ATTN), BF16),
        scratch_shapes=[pltpu.VMEM((D_MODEL, D_PROJ_PACKED), BF16),
                        pltpu.VMEM((ns, SSD_GROUPS, D_STATE, SSD_HEADS_PER_GROUP * SSD_HEAD_DIM), F32),
                        pltpu.VMEM((ns, SUBLANES, D_CONV), F32),
                        pltpu.VMEM((ns, CHUNK, 2 * D_KV), F32)],
        compiler_params=_params(2),
        name="mixer",
    )(sinks, x, norm_g, w_in_t, conv_w, conv_b, dt_bias, a_log, d_skip_x, ssd_norm_g,
      jnp.asarray(expand, dtype=BF16), bias, attn_norm_g)


def _swa_block(sink_ref, q, kv, kvp, has_prev, bias_ref, g_ref):
    L, Dh = CHUNK, ATTN_HEAD_DIM
    q = q * (1.0 / math.sqrt(Dh))
    outs = []
    for hk in range(ATTN_KV_HEADS):
        ks = slice(hk * Dh, (hk + 1) * Dh)
        vs = slice(D_KV + hk * Dh, D_KV + (hk + 1) * Dh)
        kc = jnp.concatenate([kvp[:, ks], kv[:, ks]], axis=0).astype(BF16)
        vc = jnp.concatenate([kvp[:, vs], kv[:, vs]], axis=0).astype(BF16)
        for g in range(ATTN_Q_PER_KV):
            h = hk * ATTN_Q_PER_KV + g
            qh = q[:, h * Dh:(h + 1) * Dh].astype(BF16)
            s = lax.dot_general(qh, kc, (((1,), (1,)), ((), ())), preferred_element_type=F32)
            s = s + bias_ref[has_prev, h]
            sink = sink_ref[h]
            m = jnp.maximum(jnp.max(s, axis=-1, keepdims=True), sink)
            p = jnp.exp(s - m)
            denom = jnp.sum(p, axis=-1, keepdims=True) + jnp.exp(sink - m)
            outs.append(_dot(p.astype(BF16), vc) / denom)
    return _rms(jnp.concatenate(outs, axis=1), g_ref[...])


def _route_kernel(x_ref, y_ref, wo_ref, g_ref, wr_ref, br_ref, ahead_ref,
                  h_ref, u_ref, slab_ref, post_ref, tab_ref, run_ref, wob_ref):
    tm = x_ref.shape[0]

    @pl.when(pl.program_id(0) == 0)
    def _():
        run_ref[...] = jnp.zeros(run_ref.shape, F32)
        wob_ref[...] = wo_ref[0].astype(BF16)

    h = x_ref[...] + _dot(y_ref[...], wob_ref[...])
    h_ref[...] = h
    u = _rms(h, g_ref[...])
    ub = u.astype(BF16)
    u_ref[...] = ub
    E = N_EXPERTS
    logits = lax.dot_general(wr_ref[...], ub, (((1,), (1,)), ((), ())), preferred_element_type=F32)
    logits = logits + jnp.concatenate([br_ref[...]] * (tm // LANES), axis=1)

    eidx = lax.broadcasted_iota(jnp.int32, (E, tm), 0).astype(F32)
    vals, idxs = [], []
    cur = logits
    for _ in range(TOP_K):
        m = jnp.max(cur, axis=0, keepdims=True)
        ix = jnp.min(jnp.where(cur == m, eidx, float(E)), axis=0, keepdims=True)
        vals.append(m)
        idxs.append(ix)
        cur = jnp.where(eidx == ix, NEG, cur)
    es = [jnp.exp(v - vals[0]) for v in vals]
    den = sum(es[1:], es[0])

    onehot = jnp.zeros((E, tm), F32)
    for ix in idxs:
        onehot = onehot + (eidx == ix).astype(F32)
    before = _dot(onehot.astype(BF16), ahead_ref[...])
    cnt = jnp.sum(onehot, axis=1, keepdims=True)
    cnt8 = jnp.floor((cnt + (SUBLANES - 1)) * (1.0 / SUBLANES)) * SUBLANES
    el = lax.broadcasted_iota(jnp.int32, (E, E), 0)
    ec = lax.broadcasted_iota(jnp.int32, (E, E), 1)
    seg_start = _dot_f32_by_mask((ec < el).astype(BF16), jnp.broadcast_to(cnt8, (E, LANES)),
                                 mask_on_left=True)[:, 0:1]
    where_to = before + seg_start

    rows = [jnp.sum(jnp.where(eidx == idxs[k], where_to, 0.0), axis=0, keepdims=True)
            for k in range(TOP_K)]
    rows += [es[k] / den for k in range(TOP_K)]
    stack = jnp.concatenate(rows, axis=0)
    post_ref[...] = stack.astype(jnp.int32)
    slab_ref[...] = jnp.concatenate([stack, jnp.zeros((LANES - 2 * TOP_K, tm), F32)], axis=0).T

    lane = lax.broadcasted_iota(jnp.int32, (E, LANES), 1)
    tab = jnp.where(lane == 0, cnt8, jnp.where(lane == 1, run_ref[...], jnp.where(lane == 2, seg_start, 0.0)))
    tab_ref[...] = tab.astype(jnp.int32)
    run_ref[...] = run_ref[...] + cnt8


def _route(x2, y_mix, w_out, layer, g, wr, br):
    t = x2.shape[0]
    tm = TM_ROUTE
    row = lambda i: (i, 0)
    fixed = lambda i: (0, 0)
    return pl.pallas_call(
        _route_kernel,
        grid=(t // tm,),
        in_specs=[pl.BlockSpec((tm, D_MODEL), row),
                  pl.BlockSpec((tm, D_SSD + D_ATTN), row),
                  pl.BlockSpec((1, D_SSD + D_ATTN, D_MODEL), lambda i: (layer, 0, 0)),
                  pl.BlockSpec((1, D_MODEL), fixed),
                  pl.BlockSpec((N_EXPERTS, D_MODEL), fixed),
                  pl.BlockSpec((N_EXPERTS, LANES), fixed),
                  pl.BlockSpec((tm, tm), fixed)],
        out_specs=[pl.BlockSpec((tm, D_MODEL), row),
                   pl.BlockSpec((tm, D_MODEL), row),
                   pl.BlockSpec((tm, LANES), row),
                   pl.BlockSpec((SUBLANES, tm), lambda i: (0, i)),
                   pl.BlockSpec((N_EXPERTS, LANES), row)],
        out_shape=[jax.ShapeDtypeStruct((t, D_MODEL), F32),
                   jax.ShapeDtypeStruct((t, D_MODEL), BF16),
                   jax.ShapeDtypeStruct((t, LANES), F32),
                   jax.ShapeDtypeStruct((SUBLANES, t), jnp.int32),
                   jax.ShapeDtypeStruct((t // tm * N_EXPERTS, LANES), jnp.int32)],
        scratch_shapes=[pltpu.VMEM((N_EXPERTS, LANES), F32),
                        pltpu.VMEM((D_SSD + D_ATTN, D_MODEL), BF16)],
        compiler_params=_params(1),
        name="route",
    )(x2, y_mix, w_out, g, wr, br, jnp.asarray(np.triu(np.ones((tm, tm), np.float32), 1), dtype=BF16))


SEG_SIZE_BITS = (TM_ROUTE // SUBLANES).bit_length()


TILE_SIZE_BITS = (R_SORTED // SUBLANES).bit_length()


def _piece_tables(cnt8, segst, segdst):
    n = (cnt8 // SUBLANES)[:, None, :]
    b = jnp.arange(SEG_SIZE_BITS, dtype=jnp.int32)[None, :, None]
    has = (n >> b) & 1
    off = ((n >> (b + 1)) << (b + 1)) * SUBLANES
    rank = jnp.cumsum(has, axis=2) - has
    place = (has[..., None] == 1) & (rank[..., None] == jnp.arange(N_EXPERTS, dtype=jnp.int32))
    dense = lambda v: jnp.sum(jnp.where(place, v[..., None], 0), axis=2).reshape(-1)
    return (jnp.sum(has, axis=2).reshape(-1), dense(segst[:, None, :] + off),
            dense(segdst[:, None, :] + off), jnp.sum(cnt8, axis=1) // SUBLANES)


def _for_each_piece(tile, pcnt_ref, psrc_ref, pdst_ref, fn):
    for b in range(SEG_SIZE_BITS):
        base = (tile * SEG_SIZE_BITS + b) * N_EXPERTS

        def body(p, carry, b=b, base=base):
            fn(SUBLANES << b, psrc_ref[base + p], pdst_ref[base + p])
            return carry

        lax.fori_loop(0, pcnt_ref[tile * SEG_SIZE_BITS + b], body, 0)


def _wait_rows(n_tiles8, descriptor):
    for b in range(TILE_SIZE_BITS):
        @pl.when(((n_tiles8 >> b) & 1) == 1)
        def _(b=b):
            descriptor(SUBLANES << b).wait()


def _rows(ref, row, n):
    if not isinstance(row, int):
        row = pl.multiple_of(row, SUBLANES)
    return ref.at[pl.ds(row, n), :]


def _dispatch_kernel(pcnt_ref, psrc_ref, pdst_ref, ntot_ref, zflag_ref, u_ref, post_ref, xs_ref,
                     buf_ref, zero_ref, sems, zsem):
    tm = u_ref.shape[0]
    i = pl.program_id(0)
    slot = i % 2
    bm = zero_ref.shape[0]

    def fill(b, flag):
        return pltpu.make_async_copy(zero_ref, xs_ref.at[pl.ds(b * bm, bm), :], zsem.at[flag - 1])

    def for_flagged(flag, action):
        def body(b, carry):
            @pl.when(zflag_ref[b] == flag)
            def _():
                action(fill(b, flag))
            return carry

        lax.fori_loop(0, zflag_ref.shape[0], body, 0)

    @pl.when(i == 0)
    def _():
        zero_ref[...] = jnp.zeros(zero_ref.shape, jnp.int32)
        for_flagged(1, lambda c: c.start())
        for_flagged(2, lambda c: c.start())
        for_flagged(1, lambda c: c.wait())

    @pl.when(i == pl.num_programs(0) - 1)
    def _():
        for_flagged(2, lambda c: c.wait())

    pos = post_ref[...]
    j = lax.broadcasted_iota(jnp.int32, (R_SORTED, tm), 0)
    sel = jnp.zeros((R_SORTED, tm), F32)
    for k in range(TOP_K):
        sel = jnp.where(j == pos[k:k + 1, :], 1.0, sel)
    buf_ref[slot] = _pack_bf16_pairs(_dot(sel.astype(BF16), u_ref[...]))

    def write(slot_, n, src_row, dst_row):
        return pltpu.make_async_copy(_rows(buf_ref.at[slot_], src_row, n),
                                     _rows(xs_ref, dst_row, n), sems.at[slot_])

    _for_each_piece(i, pcnt_ref, psrc_ref, pdst_ref, lambda n, s, d: write(slot, n, s, d).start())

    @pl.when(i > 0)
    def _():
        _wait_rows(ntot_ref[i - 1], lambda n: write(1 - slot, n, 0, 0))

    @pl.when(i == pl.num_programs(0) - 1)
    def _():
        _wait_rows(ntot_ref[i], lambda n: write(slot, n, 0, 0))


def _dispatch(pieces, zero_flag, u, post):
    t = u.shape[0]
    tm = TM_ROUTE
    n_rows = zero_flag.shape[0] * EXPERT_ROW_STEP
    return pl.pallas_call(
        _dispatch_kernel,
        grid_spec=pltpu.PrefetchScalarGridSpec(
            num_scalar_prefetch=5,
            grid=(t // tm,),
            in_specs=[pl.BlockSpec((tm, D_MODEL), lambda i, *_: (i, 0)),
                      pl.BlockSpec((SUBLANES, tm), lambda i, *_: (0, i))],
            out_specs=pl.BlockSpec(memory_space=pl.ANY),
            scratch_shapes=[pltpu.VMEM((2, R_SORTED, D_PACKED), jnp.int32),
                            pltpu.VMEM((EXPERT_ROW_STEP, D_PACKED), jnp.int32),
                            pltpu.SemaphoreType.DMA((2,)),
                            pltpu.SemaphoreType.DMA((2,))]),
        out_shape=jax.ShapeDtypeStruct((n_rows, D_PACKED), jnp.int32),
        compiler_params=_params(1),
        name="dispatch",
    )(*pieces, zero_flag, u, post)


def _expert_kernel(be_ref, nu_ref, first_ref, next_ref, rows_ref, xs_ref, w1_ref, b1_ref, w2_ref, b2_ref,
                   ys_ref, w1f_ref, w2f_ref, w1b_ref, w2b_ref, sems):
    i = pl.program_id(0)

    def fetch(e):
        return (pltpu.make_async_copy(w1_ref.at[e], w1f_ref, sems.at[0]),
                pltpu.make_async_copy(w2_ref.at[e], w2f_ref, sems.at[1]))

    @pl.when(i >= nu_ref[0])
    def _():
        ys_ref[...] = jnp.zeros(ys_ref.shape, jnp.int32)

    @pl.when(i < nu_ref[0])
    def _():
        e = be_ref[i]

        @pl.when(i == 0)
        def _():
            for c in fetch(e):
                c.start()

        @pl.when(first_ref[i] != 0)
        def _():
            for c in fetch(e):
                c.wait()
            w1b_ref[...] = w1f_ref[...].astype(BF16)
            w2b_ref[...] = w2f_ref[...].astype(BF16)

            @pl.when(next_ref[i] >= 0)
            def _():
                for c in fetch(next_ref[i]):
                    c.start()

        bm = xs_ref.shape[0]
        for m in range(EXPERT_ROW_STEP, bm + 1, EXPERT_ROW_STEP):
            @pl.when(rows_ref[i] == m)
            def _(m=m):
                hid = _dot(_unpack_bf16_pairs(xs_ref[0:m, :]), w1b_ref[...]) + b1_ref[pl.ds(e, 1), :]
                x_glu = jnp.minimum(hid[:, :D_EXPERT], SWIGLU_LIMIT)
                x_lin = jnp.clip(hid[:, D_EXPERT:], -SWIGLU_LIMIT, SWIGLU_LIMIT)
                act = x_glu / (1.0 + jnp.exp(-SWIGLU_ALPHA * x_glu)) * (x_lin + 1.0)
                y = _dot(act.astype(BF16), w2b_ref[...]) + b2_ref[pl.ds(e, 1), :]
                ys_ref[0:m, :] = _pack_bf16_pairs(y.astype(BF16).astype(F32))
                if m < bm:
                    ys_ref[m:, :] = jnp.zeros((bm - m, D_PACKED), jnp.int32)


def _experts(blk_expert, n_used, first_flag, next_expert, blk_rows, xs, w1, b1, w2, b2):
    n_rows = xs.shape[0]
    bm = BM_EXPERT
    nblk = n_rows // bm
    last = lambda i, nu: jnp.maximum(jnp.minimum(i, nu[0] - 1), 0)
    row = lambda i, be, nu, *_: (last(i, nu), 0)
    whole = lambda i, *_: (0, 0)
    return pl.pallas_call(
        _expert_kernel,
        grid_spec=pltpu.PrefetchScalarGridSpec(
            num_scalar_prefetch=5,
            grid=(nblk,),
            in_specs=[pl.BlockSpec((bm, D_PACKED), row),
                      pl.BlockSpec(memory_space=pl.ANY),
                      pl.BlockSpec((N_EXPERTS, 2 * D_EXPERT), whole),
                      pl.BlockSpec(memory_space=pl.ANY),
                      pl.BlockSpec((N_EXPERTS, D_MODEL), whole)],
            out_specs=pl.BlockSpec((bm, D_PACKED), lambda i, *_: (i, 0)),
            scratch_shapes=[pltpu.VMEM((D_MODEL, 2 * D_EXPERT), F32),
                            pltpu.VMEM((D_EXPERT, D_MODEL), F32),
                            pltpu.VMEM((D_MODEL, 2 * D_EXPERT), BF16),
                            pltpu.VMEM((D_EXPERT, D_MODEL), BF16),
                            pltpu.SemaphoreType.DMA((2,))]),
        out_shape=jax.ShapeDtypeStruct((n_rows, D_PACKED), jnp.int32),
        compiler_params=_params(1),
        name="experts",
    )(blk_expert, n_used, first_flag, next_expert, blk_rows, xs, w1, b1, w2, b2)


def _combine_kernel(pcnt_ref, psrc_ref, pdst_ref, ntot_ref, ys_ref, h_ref, slab_ref, g_ref, o_ref,
                    buf_ref, sems):
    tm = h_ref.shape[0]
    i = pl.program_id(0)
    slot = i % 2

    def read(slot_, n, sorted_row, ys_row):
        return pltpu.make_async_copy(_rows(ys_ref, ys_row, n),
                                     _rows(buf_ref.at[slot_], sorted_row, n), sems.at[slot_])

    def gather(tile, slot_):
        _for_each_piece(tile, pcnt_ref, psrc_ref, pdst_ref, lambda n, s, d: read(slot_, n, s, d).start())

    @pl.when(i == 0)
    def _():
        buf_ref[...] = jnp.zeros(buf_ref.shape, jnp.int32)
        gather(0, 0)

    @pl.when(i + 1 < pl.num_programs(0))
    def _():
        gather(i + 1, 1 - slot)

    _wait_rows(ntot_ref[i], lambda n: read(slot, n, 0, 0))

    slab = slab_ref[...]
    lane = lax.broadcasted_iota(jnp.int32, (tm, R_SORTED), 1).astype(F32)
    wmat = jnp.zeros((tm, R_SORTED), F32)
    for k in range(TOP_K):
        wmat = jnp.where(lane == slab[:, k:k + 1], slab[:, TOP_K + k:TOP_K + k + 1], wmat)
    moe = _dot(wmat.astype(BF16), _unpack_bf16_pairs(buf_ref[slot]))
    o_ref[...] = _rms(h_ref[...] + moe, g_ref[...])


def _combine(pieces, ys, h, slab, g):
    t = h.shape[0]
    tm = TM_ROUTE
    return pl.pallas_call(
        _combine_kernel,
        grid_spec=pltpu.PrefetchScalarGridSpec(
            num_scalar_prefetch=4,
            grid=(t // tm,),
            in_specs=[pl.BlockSpec(memory_space=pl.ANY),
                      pl.BlockSpec((tm, D_MODEL), lambda i, *_: (i, 0)),
                      pl.BlockSpec((tm, LANES), lambda i, *_: (i, 0)),
                      pl.BlockSpec((1, D_MODEL), lambda i, *_: (0, 0))],
            out_specs=pl.BlockSpec((tm, D_MODEL), lambda i, *_: (i, 0)),
            scratch_shapes=[pltpu.VMEM((2, R_SORTED, D_PACKED), jnp.int32),
                            pltpu.SemaphoreType.DMA((2,))]),
        out_shape=jax.ShapeDtypeStruct((t, D_MODEL), F32),
        compiler_params=_params(1),
        name="combine",
    )(*pieces, ys, h, slab, g)


def _pad_lanes(v, fill=0.0):
    return jnp.pad(v.reshape(1, -1), ((0, 0), (0, LANES - v.shape[-1])), constant_values=fill)


def kernel(x, norm_mix_g, w_in, conv_w, conv_b, dt_bias, a_log, d_skip, ssd_norm_g, attn_sinks,
           rel_bias, attn_norm_g, w_out, norm_ffn_g, w_router, b_router, w1, b1, w2, b2,
           norm_final_g):
    bsz, seq, d = x.shape
    t = bsz * seq
    nc = seq // CHUNK
    depth = w_in.shape[0]
    bias = _bias_table(rel_bias)

    assert depth == 1, "the final norm is fused into the combine kernel: single layer only"
    h = x.reshape(t, d)
    for layer in range(depth):
        y_mix = _mixer(h.reshape(bsz, seq, d), norm_mix_g[layer].reshape(1, d), jnp.swapaxes(w_in, 1, 2), layer,
                       conv_w[layer], conv_b[layer].reshape(1, -1),
                       _pad_lanes(dt_bias[layer]), _pad_lanes(a_log[layer]),
                       jnp.repeat(d_skip[layer], SSD_HEAD_DIM).reshape(1, -1),
                       ssd_norm_g[layer].reshape(1, -1), attn_sinks[layer], bias,
                       attn_norm_g[layer].reshape(1, -1)).reshape(t, -1)

        wr = w_router[layer].T.astype(BF16)
        br = jnp.broadcast_to(b_router[layer][:, None], (N_EXPERTS, LANES))
        h_mid, u, slab, post, tab = _route(
            h, y_mix, w_out, layer, norm_ffn_g[layer].reshape(1, d), wr, br)

        bm = BM_EXPERT
        ntiles = t // TM_ROUTE
        nblk = (t * TOP_K + ntiles * N_EXPERTS * (SUBLANES - 1)) // bm + N_EXPERTS
        tab = tab.reshape(ntiles, N_EXPERTS, LANES)
        cnt8, off8, segst = tab[:, :, 0], tab[:, :, 1], tab[:, :, 2]
        total8 = off8[-1] + cnt8[-1]
        padded = ((total8 + bm - 1) // bm) * bm
        pend = jnp.cumsum(padded)
        segdst = (pend - padded)[None, :] + off8
        n_used = pend[-1:] // bm
        eids = jnp.arange(N_EXPERTS, dtype=jnp.int32)
        blk = jnp.arange(nblk, dtype=jnp.int32)
        blk_expert = jnp.minimum(
            jnp.sum((blk[:, None] * bm >= pend[None, :]).astype(jnp.int32), axis=1), N_EXPERTS - 1)
        e_end = jnp.sum(jnp.where(blk_expert[:, None] == eids[None, :],
                                  ((pend - padded) + total8)[None, :], 0), axis=1)
        valid = jnp.where(blk < n_used[0], jnp.clip(e_end - blk * bm, 0, bm), 0)
        blk_rows = ((valid + EXPERT_ROW_STEP - 1) // EXPERT_ROW_STEP) * EXPERT_ROW_STEP
        piece_lo = jnp.arange(bm // EXPERT_ROW_STEP, dtype=jnp.int32)[None, :] * EXPERT_ROW_STEP
        zero_flag = jnp.where(piece_lo >= valid[:, None], 2,
                              (piece_lo + EXPERT_ROW_STEP > valid[:, None]).astype(jnp.int32)).reshape(-1)
        before = jnp.concatenate([blk_expert[:1], blk_expert[:-1]])
        first_flag = ((blk == 0) | (before != blk_expert)).astype(jnp.int32)
        cand = jnp.where((eids[None, :] > eids[:, None]) & (padded[None, :] > 0), eids[None, :], N_EXPERTS)
        next_nonempty = jnp.min(cand, axis=1)
        next_nonempty = jnp.where(next_nonempty == N_EXPERTS, -1, next_nonempty)
        next_expert = jnp.sum(jnp.where(blk_expert[:, None] == eids[None, :], next_nonempty[None, :], 0), axis=1)
        pieces = _piece_tables(cnt8, segst, segdst)

        xs = _dispatch(pieces, zero_flag, u, post)
        ys = _experts(blk_expert, n_used, first_flag, next_expert, blk_rows, xs, w1[layer],
                      b1[layer], w2[layer], b2[layer])
        h = _combine(pieces, ys, h_mid, slab, norm_final_g.reshape(1, d))
    return h.reshape(bsz, seq, d)
```

```python
import math

import numpy as np
import jax
import jax.numpy as jnp
from jax import lax
from jax.experimental import pallas as pl
from jax.experimental.pallas import tpu as pltpu

F32 = jnp.float32
BF16 = jnp.bfloat16

D_MODEL = 1024
SSD_HEADS = 8
SSD_HEAD_DIM = 64
D_SSD = SSD_HEADS * SSD_HEAD_DIM
SSD_GROUPS = 2
SSD_HEADS_PER_GROUP = SSD_HEADS // SSD_GROUPS
D_STATE = 128
CONV_WIDTH = 4
CHUNK = 128
D_CONV = D_SSD + 2 * SSD_GROUPS * D_STATE
ATTN_Q_HEADS = 8
ATTN_KV_HEADS = 2
ATTN_Q_PER_KV = ATTN_Q_HEADS // ATTN_KV_HEADS
ATTN_HEAD_DIM = 64
D_ATTN = ATTN_Q_HEADS * ATTN_HEAD_DIM
D_KV = ATTN_KV_HEADS * ATTN_HEAD_DIM
WINDOW = 128
REL_BUCKETS = 32
REL_MAX_DIST = 128
N_EXPERTS = 32
TOP_K = 4
D_EXPERT = D_MODEL
SWIGLU_LIMIT = 7.0
SWIGLU_ALPHA = 1.702
RMS_EPS = 1e-5

LANES = 128
SUBLANES = 8
V7X_VMEM_BYTES = 64 * 2 ** 20
VMEM_LIMIT = V7X_VMEM_BYTES * 7 // 8
NEG = -1e30
assert WINDOW == CHUNK, "the attention kernel sees exactly the current and the previous block"

TM_ROUTE = 512
BM_EXPERT = 512
EXPERT_ROW_STEP = 128
SEQS_PER_STEP = 2
CHUNKS_PER_STEP = 4
R_SORTED = TM_ROUTE * TOP_K + N_EXPERTS * SUBLANES

D_PROJ_PACKED = D_SSD + D_CONV + D_ATTN + 2 * D_KV + LANES


def _params(n_axes):
    return pltpu.CompilerParams(dimension_semantics=("arbitrary",) * n_axes,
                                vmem_limit_bytes=VMEM_LIMIT)


def _rms(x, g):
    return x * lax.rsqrt(jnp.mean(x * x, axis=-1, keepdims=True) + RMS_EPS) * g


def _silu(x):
    return x / (1.0 + jnp.exp(-x))


def _dot(a, b, **kw):
    return jnp.dot(a, b, preferred_element_type=F32, **kw)


def _dot_f32_by_mask(mask, x, mask_on_left=False, pieces=3):
    out = None
    rest = x
    for i in range(pieces):
        piece = rest.astype(BF16)
        if i + 1 < pieces:
            rest = rest - piece.astype(F32)
        term = _dot(mask, piece) if mask_on_left else _dot(piece, mask)
        out = term if out is None else out + term
    return out


D_PACKED = D_MODEL // 2


def _pack_bf16_pairs(x):
    w = x.shape[1] // 2
    hi = lax.bitcast_convert_type(x[:, :w], jnp.int32)
    lo = lax.shift_right_logical(lax.bitcast_convert_type(x[:, w:], jnp.int32), 16)
    return hi | lo


def _unpack_bf16_pairs(v):
    hi = lax.bitcast_convert_type(v & jnp.int32(-65536), F32).astype(BF16)
    lo = lax.bitcast_convert_type(lax.shift_left(v, 16), F32).astype(BF16)
    return jnp.concatenate([hi, lo], axis=1)


def _t5_bucket_table():
    dist = CHUNK + np.arange(CHUNK)[:, None] - np.arange(2 * CHUNK)[None, :]
    in_window = (dist >= 0) & (dist < WINDOW)
    d = np.clip(dist, 0, REL_MAX_DIST)
    max_exact = REL_BUCKETS // 2
    large = max_exact + (np.log(np.maximum(d, 1).astype(np.float32) / max_exact)
                         / math.log(REL_MAX_DIST / max_exact)
                         * (REL_BUCKETS - max_exact)).astype(np.int32)
    large = np.minimum(large, REL_BUCKETS - 1)
    bucket = np.where(d < max_exact, d, large)
    return np.where(in_window, bucket, -1).astype(np.int32)


def _bias_kernel(rb_ref, bucket_ref, o_ref):
    bucket = bucket_ref[...]
    col = lax.broadcasted_iota(jnp.int32, bucket.shape, 1)
    for h in range(ATTN_Q_HEADS):
        acc = jnp.full(bucket.shape, NEG, F32)
        for b in range(REL_BUCKETS):
            acc = jnp.where(bucket == b, rb_ref[b, h], acc)
        o_ref[1, h] = acc
        o_ref[0, h] = jnp.where(col >= CHUNK, acc, NEG)


def _bias_table(rel_bias):
    bucket = jnp.asarray(_t5_bucket_table())
    return pl.pallas_call(
        _bias_kernel,
        out_shape=jax.ShapeDtypeStruct((2, ATTN_Q_HEADS, CHUNK, 2 * CHUNK), F32),
        in_specs=[pl.BlockSpec(memory_space=pltpu.SMEM),
                  pl.BlockSpec(memory_space=pltpu.VMEM)],
        out_specs=pl.BlockSpec(memory_space=pltpu.VMEM),
        name="bias_table",
    )(rel_bias, bucket)


def _mixer_kernel(sink_ref, x_ref, gm_ref, w_ref, cw_ref, cb_ref, dtb_ref, alog_ref, dskip_ref, gs_ref,
                  expand_ref, bias_ref, ga_ref, o_ref, wb_ref, state_ref, xpad_ref, kvp_ref):
    j = pl.program_id(1)

    @pl.when((pl.program_id(0) == 0) & (j == 0))
    def _():
        o2 = D_SSD + D_CONV
        o3 = o2 + SSD_HEADS
        n_qkv = D_ATTN + 2 * D_KV
        for lo, src_lo, n in ((0, 0, o2), (o2, o3, n_qkv)):
            for r in range(0, n, LANES):
                wb_ref[:, lo + r:lo + r + LANES] = w_ref[0, src_lo + r:src_lo + r + LANES, :].T.astype(BF16)
        wb_ref[:, o2 + n_qkv:] = jnp.concatenate(
            [w_ref[0, o2:o3, :], jnp.zeros((LANES - SSD_HEADS, D_MODEL), F32)], axis=0).T.astype(BF16)

    @pl.when(j == 0)
    def _():
        state_ref[...] = jnp.zeros(state_ref.shape, F32)
        xpad_ref[...] = jnp.zeros(xpad_ref.shape, F32)
        kvp_ref[...] = jnp.zeros(kvp_ref.shape, F32)

    ns, rows, _ = x_ref.shape
    bounds = np.cumsum([0, D_SSD, D_CONV, D_ATTN, 2 * D_KV, LANES])
    pieces = [(c, s) for c in range(rows // CHUNK) for s in range(ns)]

    def project(c, s):
        x = x_ref[s, c * CHUNK:(c + 1) * CHUNK, :]
        return _dot(_rms(x, gm_ref[...]).astype(BF16), wb_ref[...])

    proj = project(*pieces[0])
    for k, (c, s) in enumerate(pieces):
        proj_next = project(*pieces[k + 1]) if k + 1 < len(pieces) else None
        z, xbc, q, kv, dt = [proj[:, a:b] for a, b in zip(bounds[:-1], bounds[1:])]
        has_prev = jnp.minimum(j, 1) if c == 0 else 1
        y_ssd = _ssd_chunk(z, xbc, dt, cw_ref, cb_ref, dtb_ref, alog_ref, dskip_ref, gs_ref,
                           expand_ref, state_ref.at[s], xpad_ref.at[s])
        y_attn = _swa_block(sink_ref, q, kv, kvp_ref[s], has_prev, bias_ref, ga_ref)
        kvp_ref[s] = kv
        o_ref[s, c * CHUNK:(c + 1) * CHUNK, :] = jnp.concatenate([y_ssd, y_attn], axis=1).astype(BF16)
        proj = proj_next


def _ssd_chunk(z, x_cur, dt_raw, cw_ref, cb_ref, dtb_ref, alog_ref, dskip_ref, g_ref,
               expand_ref, state_ref, xpad_ref):
    L = CHUNK
    G, R, P, N = SSD_GROUPS, SSD_HEADS_PER_GROUP, SSD_HEAD_DIM, D_STATE
    GW = R * P

    xpad = jnp.concatenate([xpad_ref[...], x_cur], axis=0)
    xpad_ref[...] = x_cur[L - SUBLANES:, :]
    acc = cb_ref[...] + cw_ref[CONV_WIDTH - 1:CONV_WIDTH, :] * x_cur
    for d in range(1, CONV_WIDTH):
        k = CONV_WIDTH - 1 - d
        acc = acc + cw_ref[k:k + 1, :] * pltpu.roll(xpad, d, axis=0)[SUBLANES:, :]
    xbc = _silu(acc)
    xs = xbc[:, :D_SSD]
    bm = xbc[:, D_SSD:D_SSD + G * N]
    cm = xbc[:, D_SSD + G * N:]

    dtr = dt_raw + dtb_ref[...]
    dt = jnp.maximum(dtr, 0.0) + jnp.log(1.0 + jnp.exp(-jnp.abs(dtr)))
    a_dt = dt * (-jnp.exp(alog_ref[...]))
    ri = lax.broadcasted_iota(jnp.int32, (L, L), 0)
    ci = lax.broadcasted_iota(jnp.int32, (L, L), 1)
    causal = ci <= ri
    a_cum = _dot_f32_by_mask(causal.astype(BF16), a_dt, mask_on_left=True)
    a_cum_t = _dot_f32_by_mask((ri <= ci).astype(BF16), a_dt.T)
    a_last = a_cum[L - 1:L, :]
    stack = jnp.concatenate(
        [dt, jnp.exp(a_cum), jnp.exp(a_last - a_cum),
         jnp.broadcast_to(jnp.exp(a_last), (SUBLANES, LANES))], axis=0)
    ex = _dot_f32_by_mask(expand_ref[...], stack, pieces=2)
    dt_x, ea_x, dte_x, cd_x = ex[0:L], ex[L:2 * L], ex[2 * L:3 * L], ex[3 * L:3 * L + 1]
    xdt = xs * dt_x

    ys = []
    for g in range(G):
        bm_g = bm[:, g * N:(g + 1) * N]
        cm_g = cm[:, g * N:(g + 1) * N].astype(BF16)
        cb = lax.dot_general(cm_g, bm_g.astype(BF16), (((1,), (1,)), ((), ())),
                             preferred_element_type=F32)
        xdt_g = xdt[:, g * GW:(g + 1) * GW]
        yd = []
        for r in range(R):
            h = g * R + r
            seg = a_cum[:, h:h + 1] - a_cum_t[h:h + 1, :]
            dec = jnp.exp(jnp.where(causal, seg, NEG))
            yd.append(_dot((cb * dec).astype(BF16), xdt_g[:, r * P:(r + 1) * P].astype(BF16)))
        y_diag = jnp.concatenate(yd, axis=1)
        st = state_ref[g]
        y_off = _dot(cm_g, st.astype(BF16)) * ea_x[:, g * GW:(g + 1) * GW]
        new = _dot(bm_g.T.astype(BF16), (xdt_g * dte_x[:, g * GW:(g + 1) * GW]).astype(BF16))
        state_ref[g] = st * cd_x[:, g * GW:(g + 1) * GW] + new
        ys.append(y_diag + y_off + xs[:, g * GW:(g + 1) * GW] * dskip_ref[:, g * GW:(g + 1) * GW])
    y = jnp.concatenate(ys, axis=1)
    return _rms(y * _silu(z), g_ref[...])


def _mixer(x, norm_g, w_in_t, layer, conv_w, conv_b, dt_bias, a_log, d_skip_x, ssd_norm_g, sinks, bias,
           attn_norm_g):
    bsz, seq, _ = x.shape
    rows = CHUNKS_PER_STEP * CHUNK
    nsteps = seq // rows
    expand = np.zeros((LANES, D_SSD), np.float32)
    for h in range(SSD_HEADS):
        expand[h, h * SSD_HEAD_DIM:(h + 1) * SSD_HEAD_DIM] = 1.0
    ns = SEQS_PER_STEP
    row = lambda b, j: (b, j, 0)
    fixed = lambda b, j: (0, 0)
    return pl.pallas_call(
        _mixer_kernel,
        grid=(bsz // ns, nsteps),
        in_specs=[pl.BlockSpec(memory_space=pltpu.SMEM),
                  pl.BlockSpec((ns, rows, D_MODEL), row),
                  pl.BlockSpec((1, D_MODEL), fixed),
                  pl.BlockSpec((1,) + w_in_t.shape[1:], lambda b, j: (layer, 0, 0)),
                  pl.BlockSpec((CONV_WIDTH, D_CONV), fixed),
                  pl.BlockSpec((1, D_CONV), fixed),
                  pl.BlockSpec((1, LANES), fixed),
                  pl.BlockSpec((1, LANES), fixed),
                  pl.BlockSpec((1, D_SSD), fixed),
                  pl.BlockSpec((1, D_SSD), fixed),
                  pl.BlockSpec((LANES, D_SSD), fixed),
                  pl.BlockSpec((2, ATTN_Q_HEADS, CHUNK, 2 * CHUNK), lambda b, j: (0, 0, 0, 0)),
                  pl.BlockSpec((1, D_ATTN), fixed)],
        out_specs=pl.BlockSpec((ns, rows, D_SSD + D_ATTN), row),
        out_shape=jax.ShapeDtypeStruct((bsz, seq, D_SSD + D_ATTN), BF16),
        scratch_shapes=[pltpu.VMEM((D_MODEL, D_PROJ_PACKED), BF16),
                        pltpu.VMEM((ns, SSD_GROUPS, D_STATE, SSD_HEADS_PER_GROUP * SSD_HEAD_DIM), F32),
                        pltpu.VMEM((ns, SUBLANES, D_CONV), F32),
                        pltpu.VMEM((ns, CHUNK, 2 * D_KV), F32)],
        compiler_params=_params(2),
        name="mixer",
    )(sinks, x, norm_g, w_in_t, conv_w, conv_b, dt_bias, a_log, d_skip_x, ssd_norm_g,
      jnp.asarray(expand, dtype=BF16), bias, attn_norm_g)


def _swa_block(sink_ref, q, kv, kvp, has_prev, bias_ref, g_ref):
    L, Dh = CHUNK, ATTN_HEAD_DIM
    q = q * (1.0 / math.sqrt(Dh))
    outs = []
    for hk in range(ATTN_KV_HEADS):
        ks = slice(hk * Dh, (hk + 1) * Dh)
        vs = slice(D_KV + hk * Dh, D_KV + (hk + 1) * Dh)
        kc = jnp.concatenate([kvp[:, ks], kv[:, ks]], axis=0).astype(BF16)
        vc = jnp.concatenate([kvp[:, vs], kv[:, vs]], axis=0).astype(BF16)
        for g in range(ATTN_Q_PER_KV):
            h = hk * ATTN_Q_PER_KV + g
            qh = q[:, h * Dh:(h + 1) * Dh].astype(BF16)
            s = lax.dot_general(qh, kc, (((1,), (1,)), ((), ())), preferred_element_type=F32)
            s = s + bias_ref[has_prev, h]
            sink = sink_ref[h]
            m = jnp.maximum(jnp.max(s, axis=-1, keepdims=True), sink)
            p = jnp.exp(s - m)
            denom = jnp.sum(p, axis=-1, keepdims=True) + jnp.exp(sink - m)
            outs.append(_dot(p.astype(BF16), vc) / denom)
    return _rms(jnp.concatenate(outs, axis=1), g_ref[...])


def _route_kernel(x_ref, y_ref, wo_ref, g_ref, wr_ref, br_ref, ahead_ref,
                  h_ref, u_ref, slab_ref, post_ref, tab_ref, run_ref, wob_ref):
    tm = x_ref.shape[0]

    @pl.when(pl.program_id(0) == 0)
    def _():
        run_ref[...] = jnp.zeros(run_ref.shape, F32)
        wob_ref[...] = wo_ref[0].astype(BF16)

    h = x_ref[...] + _dot(y_ref[...], wob_ref[...])
    h_ref[...] = h
    u = _rms(h, g_ref[...])
    ub = u.astype(BF16)
    u_ref[...] = ub
    E = N_EXPERTS
    logits = lax.dot_general(wr_ref[...], ub, (((1,), (1,)), ((), ())), preferred_element_type=F32)
    logits = logits + jnp.concatenate([br_ref[...]] * (tm // LANES), axis=1)

    eidx = lax.broadcasted_iota(jnp.int32, (E, tm), 0).astype(F32)
    vals, idxs = [], []
    cur = logits
    for _ in range(TOP_K):
        m = jnp.max(cur, axis=0, keepdims=True)
        ix = jnp.min(jnp.where(cur == m, eidx, float(E)), axis=0, keepdims=True)
        vals.append(m)
        idxs.append(ix)
        cur = jnp.where(eidx == ix, NEG, cur)
    es = [jnp.exp(v - vals[0]) for v in vals]
    den = sum(es[1:], es[0])

    onehot = jnp.zeros((E, tm), F32)
    for ix in idxs:
        onehot = onehot + (eidx == ix).astype(F32)
    before = _dot(onehot.astype(BF16), ahead_ref[...])
    cnt = jnp.sum(onehot, axis=1, keepdims=True)
    cnt8 = jnp.floor((cnt + (SUBLANES - 1)) * (1.0 / SUBLANES)) * SUBLANES
    el = lax.broadcasted_iota(jnp.int32, (E, E), 0)
    ec = lax.broadcasted_iota(jnp.int32, (E, E), 1)
    seg_start = _dot_f32_by_mask((ec < el).astype(BF16), jnp.broadcast_to(cnt8, (E, LANES)),
                                 mask_on_left=True)[:, 0:1]
    where_to = before + seg_start

    rows = [jnp.sum(jnp.where(eidx == idxs[k], where_to, 0.0), axis=0, keepdims=True)
            for k in range(TOP_K)]
    rows += [es[k] / den for k in range(TOP_K)]
    stack = jnp.concatenate(rows, axis=0)
    post_ref[...] = stack.astype(jnp.int32)
    slab_ref[...] = jnp.concatenate([stack, jnp.zeros((LANES - 2 * TOP_K, tm), F32)], axis=0).T

    lane = lax.broadcasted_iota(jnp.int32, (E, LANES), 1)
    tab = jnp.where(lane == 0, cnt8, jnp.where(lane == 1, run_ref[...], jnp.where(lane == 2, seg_start, 0.0)))
    tab_ref[...] = tab.astype(jnp.int32)
    run_ref[...] = run_ref[...] + cnt8


def _route(x2, y_mix, w_out, layer, g, wr, br):
    t = x2.shape[0]
    tm = TM_ROUTE
    row = lambda i: (i, 0)
    fixed = lambda i: (0, 0)
    return pl.pallas_call(
        _route_kernel,
        grid=(t // tm,),
        in_specs=[pl.BlockSpec((tm, D_MODEL), row),
                  pl.BlockSpec((tm, D_SSD + D_ATTN), row),
                  pl.BlockSpec((1, D_SSD + D_ATTN, D_MODEL), lambda i: (layer, 0, 0)),
                  pl.BlockSpec((1, D_MODEL), fixed),
                  pl.BlockSpec((N_EXPERTS, D_MODEL), fixed),
                  pl.BlockSpec((N_EXPERTS, LANES), fixed),
                  pl.BlockSpec((tm, tm), fixed)],
        out_specs=[pl.BlockSpec((tm, D_MODEL), row),
                   pl.BlockSpec((tm, D_MODEL), row),
                   pl.BlockSpec((tm, LANES), row),
                   pl.BlockSpec((SUBLANES, tm), lambda i: (0, i)),
                   pl.BlockSpec((N_EXPERTS, LANES), row)],
        out_shape=[jax.ShapeDtypeStruct((t, D_MODEL), F32),
                   jax.ShapeDtypeStruct((t, D_MODEL), BF16),
                   jax.ShapeDtypeStruct((t, LANES), F32),
                   jax.ShapeDtypeStruct((SUBLANES, t), jnp.int32),
                   jax.ShapeDtypeStruct((t // tm * N_EXPERTS, LANES), jnp.int32)],
        scratch_shapes=[pltpu.VMEM((N_EXPERTS, LANES), F32),
                        pltpu.VMEM((D_SSD + D_ATTN, D_MODEL), BF16)],
        compiler_params=_params(1),
        name="route",
    )(x2, y_mix, w_out, g, wr, br, jnp.asarray(np.triu(np.ones((tm, tm), np.float32), 1), dtype=BF16))


SEG_SIZE_BITS = (TM_ROUTE // SUBLANES).bit_length()


TILE_SIZE_BITS = (R_SORTED // SUBLANES).bit_length()


def _piece_tables(cnt8, segst, segdst):
    n = (cnt8 // SUBLANES)[:, None, :]
    b = jnp.arange(SEG_SIZE_BITS, dtype=jnp.int32)[None, :, None]
    has = (n >> b) & 1
    off = ((n >> (b + 1)) << (b + 1)) * SUBLANES
    rank = jnp.cumsum(has, axis=2) - has
    place = (has[..., None] == 1) & (rank[..., None] == jnp.arange(N_EXPERTS, dtype=jnp.int32))
    dense = lambda v: jnp.sum(jnp.where(place, v[..., None], 0), axis=2).reshape(-1)
    return (jnp.sum(has, axis=2).reshape(-1), dense(segst[:, None, :] + off),
            dense(segdst[:, None, :] + off), jnp.sum(cnt8, axis=1) // SUBLANES)


def _for_each_piece(tile, pcnt_ref, psrc_ref, pdst_ref, fn):
    for b in range(SEG_SIZE_BITS):
        base = (tile * SEG_SIZE_BITS + b) * N_EXPERTS

        def body(p, carry, b=b, base=base):
            fn(SUBLANES << b, psrc_ref[base + p], pdst_ref[base + p])
            return carry

        lax.fori_loop(0, pcnt_ref[tile * SEG_SIZE_BITS + b], body, 0)


def _wait_rows(n_tiles8, descriptor):
    for b in range(TILE_SIZE_BITS):
        @pl.when(((n_tiles8 >> b) & 1) == 1)
        def _(b=b):
            descriptor(SUBLANES << b).wait()


def _rows(ref, row, n):
    if not isinstance(row, int):
        row = pl.multiple_of(row, SUBLANES)
    return ref.at[pl.ds(row, n), :]


def _dispatch_kernel(pcnt_ref, psrc_ref, pdst_ref, ntot_ref, zflag_ref, u_ref, post_ref, xs_ref,
                     buf_ref, zero_ref, sems, zsem):
    tm = u_ref.shape[0]
    i = pl.program_id(0)
    slot = i % 2
    bm = zero_ref.shape[0]

    def fill(b, flag):
        return pltpu.make_async_copy(zero_ref, xs_ref.at[pl.ds(b * bm, bm), :], zsem.at[flag - 1])

    def for_flagged(flag, action):
        def body(b, carry):
            @pl.when(zflag_ref[b] == flag)
            def _():
                action(fill(b, flag))
            return carry

        lax.fori_loop(0, zflag_ref.shape[0], body, 0)

    @pl.when(i == 0)
    def _():
        zero_ref[...] = jnp.zeros(zero_ref.shape, jnp.int32)
        for_flagged(1, lambda c: c.start())
        for_flagged(2, lambda c: c.start())
        for_flagged(1, lambda c: c.wait())

    @pl.when(i == pl.num_programs(0) - 1)
    def _():
        for_flagged(2, lambda c: c.wait())

    pos = post_ref[...]
    j = lax.broadcasted_iota(jnp.int32, (R_SORTED, tm), 0)
    sel = jnp.zeros((R_SORTED, tm), F32)
    for k in range(TOP_K):
        sel = jnp.where(j == pos[k:k + 1, :], 1.0, sel)
    buf_ref[slot] = _pack_bf16_pairs(_dot(sel.astype(BF16), u_ref[...]))

    def write(slot_, n, src_row, dst_row):
        return pltpu.make_async_copy(_rows(buf_ref.at[slot_], src_row, n),
                                     _rows(xs_ref, dst_row, n), sems.at[slot_])

    _for_each_piece(i, pcnt_ref, psrc_ref, pdst_ref, lambda n, s, d: write(slot, n, s, d).start())

    @pl.when(i > 0)
    def _():
        _wait_rows(ntot_ref[i - 1], lambda n: write(1 - slot, n, 0, 0))

    @pl.when(i == pl.num_programs(0) - 1)
    def _():
        _wait_rows(ntot_ref[i], lambda n: write(slot, n, 0, 0))


def _dispatch(pieces, zero_flag, u, post):
    t = u.shape[0]
    tm = TM_ROUTE
    n_rows = zero_flag.shape[0] * EXPERT_ROW_STEP
    return pl.pallas_call(
        _dispatch_kernel,
        grid_spec=pltpu.PrefetchScalarGridSpec(
            num_scalar_prefetch=5,
            grid=(t // tm,),
            in_specs=[pl.BlockSpec((tm, D_MODEL), lambda i, *_: (i, 0)),
                      pl.BlockSpec((SUBLANES, tm), lambda i, *_: (0, i))],
            out_specs=pl.BlockSpec(memory_space=pl.ANY),
            scratch_shapes=[pltpu.VMEM((2, R_SORTED, D_PACKED), jnp.int32),
                            pltpu.VMEM((EXPERT_ROW_STEP, D_PACKED), jnp.int32),
                            pltpu.SemaphoreType.DMA((2,)),
                            pltpu.SemaphoreType.DMA((2,))]),
        out_shape=jax.ShapeDtypeStruct((n_rows, D_PACKED), jnp.int32),
        compiler_params=_params(1),
        name="dispatch",
    )(*pieces, zero_flag, u, post)


def _expert_kernel(be_ref, nu_ref, first_ref, next_ref, rows_ref, xs_ref, w1_ref, b1_ref, w2_ref, b2_ref,
                   ys_ref, w1f_ref, w2f_ref, w1b_ref, w2b_ref, sems):
    i = pl.program_id(0)

    def fetch(e):
        return (pltpu.make_async_copy(w1_ref.at[e], w1f_ref, sems.at[0]),
                pltpu.make_async_copy(w2_ref.at[e], w2f_ref, sems.at[1]))

    @pl.when(i >= nu_ref[0])
    def _():
        ys_ref[...] = jnp.zeros(ys_ref.shape, jnp.int32)

    @pl.when(i < nu_ref[0])
    def _():
        e = be_ref[i]

        @pl.when(i == 0)
        def _():
            for c in fetch(e):
                c.start()

        @pl.when(first_ref[i] != 0)
        def _():
            for c in fetch(e):
                c.wait()
            w1b_ref[...] = w1f_ref[...].astype(BF16)
            w2b_ref[...] = w2f_ref[...].astype(BF16)

            @pl.when(next_ref[i] >= 0)
            def _():
                for c in fetch(next_ref[i]):
                    c.start()

        bm = xs_ref.shape[0]
        for m in range(EXPERT_ROW_STEP, bm + 1, EXPERT_ROW_STEP):
            @pl.when(rows_ref[i] == m)
            def _(m=m):
                hid = _dot(_unpack_bf16_pairs(xs_ref[0:m, :]), w1b_ref[...]) + b1_ref[pl.ds(e, 1), :]
                x_glu = jnp.minimum(hid[:, :D_EXPERT], SWIGLU_LIMIT)
                x_lin = jnp.clip(hid[:, D_EXPERT:], -SWIGLU_LIMIT, SWIGLU_LIMIT)
                act = x_glu / (1.0 + jnp.exp(-SWIGLU_ALPHA * x_glu)) * (x_lin + 1.0)
                y = _dot(act.astype(BF16), w2b_ref[...]) + b2_ref[pl.ds(e, 1), :]
                ys_ref[0:m, :] = _pack_bf16_pairs(y.astype(BF16).astype(F32))
                if m < bm:
                    ys_ref[m:, :] = jnp.zeros((bm - m, D_PACKED), jnp.int32)


def _experts(blk_expert, n_used, first_flag, next_expert, blk_rows, xs, w1, b1, w2, b2):
    n_rows = xs.shape[0]
    bm = BM_EXPERT
    nblk = n_rows // bm
    last = lambda i, nu: jnp.maximum(jnp.minimum(i, nu[0] - 1), 0)
    row = lambda i, be, nu, *_: (last(i, nu), 0)
    whole = lambda i, *_: (0, 0)
    return pl.pallas_call(
        _expert_kernel,
        grid_spec=pltpu.PrefetchScalarGridSpec(
            num_scalar_prefetch=5,
            grid=(nblk,),
            in_specs=[pl.BlockSpec((bm, D_PACKED), row),
                      pl.BlockSpec(memory_space=pl.ANY),
                      pl.BlockSpec((N_EXPERTS, 2 * D_EXPERT), whole),
                      pl.BlockSpec(memory_space=pl.ANY),
                      pl.BlockSpec((N_EXPERTS, D_MODEL), whole)],
            out_specs=pl.BlockSpec((bm, D_PACKED), lambda i, *_: (i, 0)),
            scratch_shapes=[pltpu.VMEM((D_MODEL, 2 * D_EXPERT), F32),
                            pltpu.VMEM((D_EXPERT, D_MODEL), F32),
                            pltpu.VMEM((D_MODEL, 2 * D_EXPERT), BF16),
                            pltpu.VMEM((D_EXPERT, D_MODEL), BF16),
                            pltpu.SemaphoreType.DMA((2,))]),
        out_shape=jax.ShapeDtypeStruct((n_rows, D_PACKED), jnp.int32),
        compiler_params=_params(1),
        name="experts",
    )(blk_expert, n_used, first_flag, next_expert, blk_rows, xs, w1, b1, w2, b2)


def _combine_kernel(pcnt_ref, psrc_ref, pdst_ref, ntot_ref, ys_ref, h_ref, slab_ref, g_ref, o_ref,
                    buf_ref, sems):
    tm = h_ref.shape[0]
    i = pl.program_id(0)
    slot = i % 2

    def read(slot_, n, sorted_row, ys_row):
        return pltpu.make_async_copy(_rows(ys_ref, ys_row, n),
                                     _rows(buf_ref.at[slot_], sorted_row, n), sems.at[slot_])

    def gather(tile, slot_):
        _for_each_piece(tile, pcnt_ref, psrc_ref, pdst_ref, lambda n, s, d: read(slot_, n, s, d).start())

    @pl.when(i == 0)
    def _():
        buf_ref[...] = jnp.zeros(buf_ref.shape, jnp.int32)
        gather(0, 0)

    @pl.when(i + 1 < pl.num_programs(0))
    def _():
        gather(i + 1, 1 - slot)

    _wait_rows(ntot_ref[i], lambda n: read(slot, n, 0, 0))

    slab = slab_ref[...]
    lane = lax.broadcasted_iota(jnp.int32, (tm, R_SORTED), 1).astype(F32)
    wmat = jnp.zeros((tm, R_SORTED), F32)
    for k in range(TOP_K):
        wmat = jnp.where(lane == slab[:, k:k + 1], slab[:, TOP_K + k:TOP_K + k + 1], wmat)
    moe = _dot(wmat.astype(BF16), _unpack_bf16_pairs(buf_ref[slot]))
    o_ref[...] = _rms(h_ref[...] + moe, g_ref[...])


def _combine(pieces, ys, h, slab, g):
    t = h.shape[0]
    tm = TM_ROUTE
    return pl.pallas_call(
        _combine_kernel,
        grid_spec=pltpu.PrefetchScalarGridSpec(
            num_scalar_prefetch=4,
            grid=(t // tm,),
            in_specs=[pl.BlockSpec(memory_space=pl.ANY),
                      pl.BlockSpec((tm, D_MODEL), lambda i, *_: (i, 0)),
                      pl.BlockSpec((tm, LANES), lambda i, *_: (i, 0)),
                      pl.BlockSpec((1, D_MODEL), lambda i, *_: (0, 0))],
            out_specs=pl.BlockSpec((tm, D_MODEL), lambda i, *_: (i, 0)),
            scratch_shapes=[pltpu.VMEM((2, R_SORTED, D_PACKED), jnp.int32),
                            pltpu.SemaphoreType.DMA((2,))]),
        out_shape=jax.ShapeDtypeStruct((t, D_MODEL), F32),
        compiler_params=_params(1),
        name="combine",
    )(*pieces, ys, h, slab, g)


def _pad_lanes(v, fill=0.0):
    return jnp.pad(v.reshape(1, -1), ((0, 0), (0, LANES - v.shape[-1])), constant_values=fill)


def kernel(x, norm_mix_g, w_in, conv_w, conv_b, dt_bias, a_log, d_skip, ssd_norm_g, attn_sinks,
           rel_bias, attn_norm_g, w_out, norm_ffn_g, w_router, b_router, w1, b1, w2, b2,
           norm_final_g):
    bsz, seq, d = x.shape
    t = bsz * seq
    nc = seq // CHUNK
    depth = w_in.shape[0]
    bias = _bias_table(rel_bias)

    assert depth == 1, "the final norm is fused into the combine kernel: single layer only"
    h = x.reshape(t, d)
    for layer in range(depth):
        y_mix = _mixer(h.reshape(bsz, seq, d), norm_mix_g[layer].reshape(1, d), jnp.swapaxes(w_in, 1, 2), layer,
                       conv_w[layer], conv_b[layer].reshape(1, -1),
                       _pad_lanes(dt_bias[layer]), _pad_lanes(a_log[layer]),
                       jnp.repeat(d_skip[layer], SSD_HEAD_DIM).reshape(1, -1),
                       ssd_norm_g[layer].reshape(1, -1), attn_sinks[layer], bias,
                       attn_norm_g[layer].reshape(1, -1)).reshape(t, -1)

        wr = w_router[layer].T.astype(BF16)
        br = jnp.broadcast_to(b_router[layer][:, None], (N_EXPERTS, LANES))
        h_mid, u, slab, post, tab = _route(
            h, y_mix, w_out, layer, norm_ffn_g[layer].reshape(1, d), wr, br)

        bm = BM_EXPERT
        ntiles = t // TM_ROUTE
        nblk = (t * TOP_K + ntiles * N_EXPERTS * (SUBLANES - 1)) // bm + N_EXPERTS
        tab = tab.reshape(ntiles, N_EXPERTS, LANES)
        cnt8, off8, segst = tab[:, :, 0], tab[:, :, 1], tab[:, :, 2]
        total8 = off8[-1] + cnt8[-1]
        padded = ((total8 + bm - 1) // bm) * bm
        pend = jnp.cumsum(padded)
        segdst = (pend - padded)[None, :] + off8
        n_used = pend[-1:] // bm
        eids = jnp.arange(N_EXPERTS, dtype=jnp.int32)
        blk = jnp.arange(nblk, dtype=jnp.int32)
        blk_expert = jnp.minimum(
            jnp.sum((blk[:, None] * bm >= pend[None, :]).astype(jnp.int32), axis=1), N_EXPERTS - 1)
        e_end = jnp.sum(jnp.where(blk_expert[:, None] == eids[None, :],
                                  ((pend - padded) + total8)[None, :], 0), axis=1)
        valid = jnp.where(blk < n_used[0], jnp.clip(e_end - blk * bm, 0, bm), 0)
        blk_rows = ((valid + EXPERT_ROW_STEP - 1) // EXPERT_ROW_STEP) * EXPERT_ROW_STEP
        piece_lo = jnp.arange(bm // EXPERT_ROW_STEP, dtype=jnp.int32)[None, :] * EXPERT_ROW_STEP
        zero_flag = jnp.where(piece_lo >= valid[:, None], 2,
                              (piece_lo + EXPERT_ROW_STEP > valid[:, None]).astype(jnp.int32)).reshape(-1)
        before = jnp.concatenate([blk_expert[:1], blk_expert[:-1]])
        first_flag = ((blk == 0) | (before != blk_expert)).astype(jnp.int32)
        cand = jnp.where((eids[None, :] > eids[:, None]) & (padded[None, :] > 0), eids[None, :], N_EXPERTS)
        next_nonempty = jnp.min(cand, axis=1)
        next_nonempty = jnp.where(next_nonempty == N_EXPERTS, -1, next_nonempty)
        next_expert = jnp.sum(jnp.where(blk_expert[:, None] == eids[None, :], next_nonempty[None, :], 0), axis=1)
        pieces = _piece_tables(cnt8, segst, segdst)

        xs = _dispatch(pieces, zero_flag, u, post)
        ys = _experts(blk_expert, n_used, first_flag, next_expert, blk_rows, xs, w1[layer],
                      b1[layer], w2[layer], b2[layer])
        h = _combine(pieces, ys, h_mid, slab, norm_final_g.reshape(1, d))
    return h.reshape(bsz, seq, d)
```

```python
import math

import numpy as np
import jax
import jax.numpy as jnp
from jax import lax
from jax.experimental import pallas as pl
from jax.experimental.pallas import tpu as pltpu

F32 = jnp.float32
BF16 = jnp.bfloat16

D_MODEL = 1024
SSD_HEADS = 8
SSD_HEAD_DIM = 64
D_SSD = SSD_HEADS * SSD_HEAD_DIM
SSD_GROUPS = 2
SSD_HEADS_PER_GROUP = SSD_HEADS // SSD_GROUPS
D_STATE = 128
CONV_WIDTH = 4
CHUNK = 128
D_CONV = D_SSD + 2 * SSD_GROUPS * D_STATE
ATTN_Q_HEADS = 8
ATTN_KV_HEADS = 2
ATTN_Q_PER_KV = ATTN_Q_HEADS // ATTN_KV_HEADS
ATTN_HEAD_DIM = 64
D_ATTN = ATTN_Q_HEADS * ATTN_HEAD_DIM
D_KV = ATTN_KV_HEADS * ATTN_HEAD_DIM
WINDOW = 128
REL_BUCKETS = 32
REL_MAX_DIST = 128
N_EXPERTS = 32
TOP_K = 4
D_EXPERT = D_MODEL
SWIGLU_LIMIT = 7.0
SWIGLU_ALPHA = 1.702
RMS_EPS = 1e-5

LANES = 128
SUBLANES = 8
V7X_VMEM_BYTES = 64 * 2 ** 20
VMEM_LIMIT = V7X_VMEM_BYTES * 7 // 8
NEG = -1e30
assert WINDOW == CHUNK, "the attention kernel sees exactly the current and the previous block"

TM_ROUTE = 512
BM_EXPERT = 512
EXPERT_ROW_STEP = 128
SEQS_PER_STEP = 2
CHUNKS_PER_STEP = 4
R_SORTED = TM_ROUTE * TOP_K + N_EXPERTS * SUBLANES

D_PROJ_PACKED = D_SSD + D_CONV + D_ATTN + 2 * D_KV + LANES


def _params(n_axes):
    return pltpu.CompilerParams(dimension_semantics=("arbitrary",) * n_axes,
                                vmem_limit_bytes=VMEM_LIMIT)


def _rms(x, g):
    return x * lax.rsqrt(jnp.mean(x * x, axis=-1, keepdims=True) + RMS_EPS) * g


def _silu(x):
    return x / (1.0 + jnp.exp(-x))


def _dot(a, b):
    return jnp.dot(a, b, preferred_element_type=F32)


def _dot_f32_by_mask(mask, x, mask_on_left=False):
    hi = x.astype(BF16)
    r1 = x - hi.astype(F32)
    mid = r1.astype(BF16)
    lo = (r1 - mid.astype(F32)).astype(BF16)
    out = None
    for piece in (hi, mid, lo):
        term = _dot(mask, piece) if mask_on_left else _dot(piece, mask)
        out = term if out is None else out + term
    return out


D_PACKED = D_MODEL // 2
BF16_BITS = 16


def _pack_bf16_pairs(x):
    w = x.shape[1] // 2
    hi = lax.bitcast_convert_type(x[:, :w], jnp.int32)
    lo = lax.shift_right_logical(lax.bitcast_convert_type(x[:, w:], jnp.int32), BF16_BITS)
    return hi | lo


def _unpack_bf16_pairs(v):
    upper_half_word = jnp.int32(-(1 << BF16_BITS))
    hi = lax.bitcast_convert_type(v & upper_half_word, F32).astype(BF16)
    lo = lax.bitcast_convert_type(lax.shift_left(v, BF16_BITS), F32).astype(BF16)
    return jnp.concatenate([hi, lo], axis=1)


def _t5_bucket_table():
    dist = CHUNK + np.arange(CHUNK)[:, None] - np.arange(2 * CHUNK)[None, :]
    in_window = (dist >= 0) & (dist < WINDOW)
    d = np.clip(dist, 0, REL_MAX_DIST)
    max_exact = REL_BUCKETS // 2
    large = max_exact + (np.log(np.maximum(d, 1).astype(np.float32) / max_exact)
                         / math.log(REL_MAX_DIST / max_exact)
                         * (REL_BUCKETS - max_exact)).astype(np.int32)
    large = np.minimum(large, REL_BUCKETS - 1)
    bucket = np.where(d < max_exact, d, large)
    return np.where(in_window, bucket, -1).astype(np.int32)


def _bias_kernel(rb_ref, bucket_ref, o_ref):
    bucket = bucket_ref[...]
    col = lax.broadcasted_iota(jnp.int32, bucket.shape, 1)
    for h in range(ATTN_Q_HEADS):
        acc = jnp.full(bucket.shape, NEG, F32)
        for b in range(REL_BUCKETS):
            acc = jnp.where(bucket == b, rb_ref[b, h], acc)
        o_ref[1, h] = acc
        o_ref[0, h] = jnp.where(col >= CHUNK, acc, NEG)


def _bias_table(rel_bias):
    bucket = jnp.asarray(_t5_bucket_table())
    return pl.pallas_call(
        _bias_kernel,
        out_shape=jax.ShapeDtypeStruct((2, ATTN_Q_HEADS, CHUNK, 2 * CHUNK), F32),
        in_specs=[pl.BlockSpec(memory_space=pltpu.SMEM),
                  pl.BlockSpec(memory_space=pltpu.VMEM)],
        out_specs=pl.BlockSpec(memory_space=pltpu.VMEM),
        name="bias_table",
    )(rel_bias, bucket)


def _mixer_kernel(sink_ref, x_ref, gm_ref, w_ref, cw_ref, cb_ref, dtb_ref, alog_ref, dskip_ref, gs_ref,
                  expand_ref, bias_ref, ga_ref, o_ref, wb_ref, state_ref, xpad_ref, kvp_ref):
    j = pl.program_id(1)

    @pl.when((pl.program_id(0) == 0) & (j == 0))
    def _():
        o2 = D_SSD + D_CONV
        o3 = o2 + SSD_HEADS
        n_qkv = D_ATTN + 2 * D_KV
        for lo, src_lo, n in ((0, 0, o2), (o2, o3, n_qkv)):
            for r in range(0, n, LANES):
                wb_ref[:, lo + r:lo + r + LANES] = w_ref[0, src_lo + r:src_lo + r + LANES, :].T.astype(BF16)
        wb_ref[:, o2 + n_qkv:] = jnp.concatenate(
            [w_ref[0, o2:o3, :], jnp.zeros((LANES - SSD_HEADS, D_MODEL), F32)], axis=0).T.astype(BF16)

    @pl.when(j == 0)
    def _():
        state_ref[...] = jnp.zeros(state_ref.shape, F32)
        xpad_ref[...] = jnp.zeros(xpad_ref.shape, F32)
        kvp_ref[...] = jnp.zeros(kvp_ref.shape, F32)

    ns, rows, _ = x_ref.shape
    bounds = np.cumsum([0, D_SSD, D_CONV, D_ATTN, 2 * D_KV, LANES])
    pieces = [(c, s) for c in range(rows // CHUNK) for s in range(ns)]

    def project(c, s):
        x = x_ref[s, c * CHUNK:(c + 1) * CHUNK, :]
        return _dot(_rms(x, gm_ref[...]).astype(BF16), wb_ref[...])

    proj = project(*pieces[0])
    for k, (c, s) in enumerate(pieces):
        proj_next = project(*pieces[k + 1]) if k + 1 < len(pieces) else None
        z, xbc, q, kv, dt = [proj[:, a:b] for a, b in zip(bounds[:-1], bounds[1:])]
        has_prev = jnp.minimum(j, 1) if c == 0 else 1
        y_ssd = _ssd_chunk(z, xbc, dt, cw_ref, cb_ref, dtb_ref, alog_ref, dskip_ref, gs_ref,
                           expand_ref, state_ref.at[s], xpad_ref.at[s])
        y_attn = _swa_block(sink_ref, q, kv, kvp_ref[s], has_prev, bias_ref, ga_ref)
        kvp_ref[s] = kv
        o_ref[s, c * CHUNK:(c + 1) * CHUNK, :] = jnp.concatenate([y_ssd, y_attn], axis=1).astype(BF16)
        proj = proj_next


def _ssd_chunk(z, x_cur, dt_raw, cw_ref, cb_ref, dtb_ref, alog_ref, dskip_ref, g_ref,
               expand_ref, state_ref, xpad_ref):
    L = CHUNK
    G, R, P, N = SSD_GROUPS, SSD_HEADS_PER_GROUP, SSD_HEAD_DIM, D_STATE
    GW = R * P

    xpad = jnp.concatenate([xpad_ref[...], x_cur], axis=0)
    xpad_ref[...] = x_cur[L - SUBLANES:, :]
    acc = cb_ref[...] + cw_ref[CONV_WIDTH - 1:CONV_WIDTH, :] * x_cur
    for d in range(1, CONV_WIDTH):
        k = CONV_WIDTH - 1 - d
        acc = acc + cw_ref[k:k + 1, :] * pltpu.roll(xpad, d, axis=0)[SUBLANES:, :]
    xbc = _silu(acc)
    xs = xbc[:, :D_SSD]
    bm = xbc[:, D_SSD:D_SSD + G * N]
    cm = xbc[:, D_SSD + G * N:]

    dtr = dt_raw + dtb_ref[...]
    dt = jnp.maximum(dtr, 0.0) + jnp.log(1.0 + jnp.exp(-jnp.abs(dtr)))
    a_dt = dt * (-jnp.exp(alog_ref[...]))
    ri = lax.broadcasted_iota(jnp.int32, (L, L), 0)
    ci = lax.broadcasted_iota(jnp.int32, (L, L), 1)
    causal = ci <= ri
    a_cum = _dot_f32_by_mask(causal.astype(BF16), a_dt, mask_on_left=True)
    a_cum_t = _dot_f32_by_mask((ri <= ci).astype(BF16), a_dt.T)
    a_last = a_cum[L - 1:L, :]
    stack = jnp.concatenate(
        [dt, jnp.exp(a_cum), jnp.exp(a_last - a_cum),
         jnp.broadcast_to(jnp.exp(a_last), (SUBLANES, LANES))], axis=0)
    ex = _dot_f32_by_mask(expand_ref[...], stack)
    dt_x, ea_x, dte_x, cd_x = ex[0:L], ex[L:2 * L], ex[2 * L:3 * L], ex[3 * L:3 * L + 1]
    xdt = xs * dt_x

    ys = []
    for g in range(G):
        bm_g = bm[:, g * N:(g + 1) * N]
        cm_g = cm[:, g * N:(g + 1) * N].astype(BF16)
        cb = lax.dot_general(cm_g, bm_g.astype(BF16), (((1,), (1,)), ((), ())),
                             preferred_element_type=F32)
        xdt_g = xdt[:, g * GW:(g + 1) * GW]
        yd = []
        for r in range(R):
            h = g * R + r
            seg = a_cum[:, h:h + 1] - a_cum_t[h:h + 1, :]
            dec = jnp.exp(jnp.where(causal, seg, NEG))
            yd.append(_dot((cb * dec).astype(BF16), xdt_g[:, r * P:(r + 1) * P].astype(BF16)))
        y_diag = jnp.concatenate(yd, axis=1)
        st = state_ref[g]
        y_off = _dot(cm_g, st.astype(BF16)) * ea_x[:, g * GW:(g + 1) * GW]
        new = _dot(bm_g.T.astype(BF16), (xdt_g * dte_x[:, g * GW:(g + 1) * GW]).astype(BF16))
        state_ref[g] = st * cd_x[:, g * GW:(g + 1) * GW] + new
        ys.append(y_diag + y_off + xs[:, g * GW:(g + 1) * GW] * dskip_ref[:, g * GW:(g + 1) * GW])
    y = jnp.concatenate(ys, axis=1)
    return _rms(y * _silu(z), g_ref[...])


def _mixer(x, norm_g, w_in_t, layer, conv_w, conv_b, dt_bias, a_log, d_skip_x, ssd_norm_g, sinks, bias,
           attn_norm_g):
    bsz, seq, _ = x.shape
    rows = CHUNKS_PER_STEP * CHUNK
    nsteps = seq // rows
    expand = np.zeros((LANES, D_SSD), np.float32)
    for h in range(SSD_HEADS):
        expand[h, h * SSD_HEAD_DIM:(h + 1) * SSD_HEAD_DIM] = 1.0
    ns = SEQS_PER_STEP
    row = lambda b, j: (b, j, 0)
    fixed = lambda b, j: (0, 0)
    return pl.pallas_call(
        _mixer_kernel,
        grid=(bsz // ns, nsteps),
        in_specs=[pl.BlockSpec(memory_space=pltpu.SMEM),
                  pl.BlockSpec((ns, rows, D_MODEL), row),
                  pl.BlockSpec((1, D_MODEL), fixed),
                  pl.BlockSpec((1,) + w_in_t.shape[1:], lambda b, j: (layer, 0, 0)),
                  pl.BlockSpec((CONV_WIDTH, D_CONV), fixed),
                  pl.BlockSpec((1, D_CONV), fixed),
                  pl.BlockSpec((1, LANES), fixed),
                  pl.BlockSpec((1, LANES), fixed),
                  pl.BlockSpec((1, D_SSD), fixed),
                  pl.BlockSpec((1, D_SSD), fixed),
                  pl.BlockSpec((LANES, D_SSD), fixed),
                  pl.BlockSpec((2, ATTN_Q_HEADS, CHUNK, 2 * CHUNK), lambda b, j: (0, 0, 0, 0)),
                  pl.BlockSpec((1, D_ATTN), fixed)],
        out_specs=pl.BlockSpec((ns, rows, D_SSD + D_ATTN), row),
        out_shape=jax.ShapeDtypeStruct((bsz, seq, D_SSD + D_ATTN), BF16),
        scratch_shapes=[pltpu.VMEM((D_MODEL, D_PROJ_PACKED), BF16),
                        pltpu.VMEM((ns, SSD_GROUPS, D_STATE, SSD_HEADS_PER_GROUP * SSD_HEAD_DIM), F32),
                        pltpu.VMEM((ns, SUBLANES, D_CONV), F32),
                        pltpu.VMEM((ns, CHUNK, 2 * D_KV), F32)],
        compiler_params=_params(2),
        name="mixer",
    )(sinks, x, norm_g, w_in_t, conv_w, conv_b, dt_bias, a_log, d_skip_x, ssd_norm_g,
      jnp.asarray(expand, dtype=BF16), bias, attn_norm_g)


def _swa_block(sink_ref, q, kv, kvp, has_prev, bias_ref, g_ref):
    L, Dh = CHUNK, ATTN_HEAD_DIM
    q = q * (1.0 / math.sqrt(Dh))
    outs = []
    for hk in range(ATTN_KV_HEADS):
        ks = slice(hk * Dh, (hk + 1) * Dh)
        vs = slice(D_KV + hk * Dh, D_KV + (hk + 1) * Dh)
        kc = jnp.concatenate([kvp[:, ks], kv[:, ks]], axis=0).astype(BF16)
        vc = jnp.concatenate([kvp[:, vs], kv[:, vs]], axis=0).astype(BF16)
        for g in range(ATTN_Q_PER_KV):
            h = hk * ATTN_Q_PER_KV + g
            qh = q[:, h * Dh:(h + 1) * Dh].astype(BF16)
            s = lax.dot_general(qh, kc, (((1,), (1,)), ((), ())), preferred_element_type=F32)
            s = s + bias_ref[has_prev, h]
            sink = sink_ref[h]
            m = jnp.maximum(jnp.max(s, axis=-1, keepdims=True), sink)
            p = jnp.exp(s - m)
            denom = jnp.sum(p, axis=-1, keepdims=True) + jnp.exp(sink - m)
            outs.append(_dot(p.astype(BF16), vc) / denom)
    return _rms(jnp.concatenate(outs, axis=1), g_ref[...])


def _route_kernel(x_ref, y_ref, wo_ref, g_ref, wr_ref, br_ref, ahead_ref,
                  h_ref, u_ref, slab_ref, post_ref, tab_ref, run_ref, wob_ref):
    tm = x_ref.shape[0]

    @pl.when(pl.program_id(0) == 0)
    def _():
        run_ref[...] = jnp.zeros(run_ref.shape, F32)
        wob_ref[...] = wo_ref[0].astype(BF16)

    h = x_ref[...] + _dot(y_ref[...], wob_ref[...])
    h_ref[...] = h
    u = _rms(h, g_ref[...])
    ub = u.astype(BF16)
    u_ref[...] = ub
    E = N_EXPERTS
    logits = lax.dot_general(wr_ref[...], ub, (((1,), (1,)), ((), ())), preferred_element_type=F32)
    logits = logits + jnp.concatenate([br_ref[...]] * (tm // LANES), axis=1)

    eidx = lax.broadcasted_iota(jnp.int32, (E, tm), 0).astype(F32)
    vals, idxs = [], []
    cur = logits
    for _ in range(TOP_K):
        m = jnp.max(cur, axis=0, keepdims=True)
        ix = jnp.min(jnp.where(cur == m, eidx, float(E)), axis=0, keepdims=True)
        vals.append(m)
        idxs.append(ix)
        cur = jnp.where(eidx == ix, NEG, cur)
    es = [jnp.exp(v - vals[0]) for v in vals]
    den = sum(es[1:], es[0])

    onehot = jnp.zeros((E, tm), F32)
    for ix in idxs:
        onehot = onehot + (eidx == ix).astype(F32)
    before = _dot(onehot.astype(BF16), ahead_ref[...])
    cnt = jnp.sum(onehot, axis=1, keepdims=True)
    cnt8 = jnp.floor((cnt + (SUBLANES - 1)) * (1.0 / SUBLANES)) * SUBLANES
    el = lax.broadcasted_iota(jnp.int32, (E, E), 0)
    ec = lax.broadcasted_iota(jnp.int32, (E, E), 1)
    seg_start = _dot_f32_by_mask((ec < el).astype(BF16), jnp.broadcast_to(cnt8, (E, LANES)),
                                 mask_on_left=True)[:, 0:1]
    where_to = before + seg_start

    rows = [jnp.sum(jnp.where(eidx == idxs[k], where_to, 0.0), axis=0, keepdims=True)
            for k in range(TOP_K)]
    rows += [es[k] / den for k in range(TOP_K)]
    stack = jnp.concatenate(rows, axis=0)
    post_ref[...] = stack.astype(jnp.int32)
    slab_ref[...] = jnp.concatenate([stack, jnp.zeros((LANES - 2 * TOP_K, tm), F32)], axis=0).T

    lane = lax.broadcasted_iota(jnp.int32, (E, LANES), 1)
    tab = jnp.where(lane == 0, cnt8, jnp.where(lane == 1, run_ref[...], jnp.where(lane == 2, seg_start, 0.0)))
    tab_ref[...] = tab.astype(jnp.int32)
    run_ref[...] = run_ref[...] + cnt8


def _route(x2, y_mix, w_out, layer, g, wr, br):
    t = x2.shape[0]
    tm = TM_ROUTE
    row = lambda i: (i, 0)
    fixed = lambda i: (0, 0)
    return pl.pallas_call(
        _route_kernel,
        grid=(t // tm,),
        in_specs=[pl.BlockSpec((tm, D_MODEL), row),
                  pl.BlockSpec((tm, D_SSD + D_ATTN), row),
                  pl.BlockSpec((1, D_SSD + D_ATTN, D_MODEL), lambda i: (layer, 0, 0)),
                  pl.BlockSpec((1, D_MODEL), fixed),
                  pl.BlockSpec((N_EXPERTS, D_MODEL), fixed),
                  pl.BlockSpec((N_EXPERTS, LANES), fixed),
                  pl.BlockSpec((tm, tm), fixed)],
        out_specs=[pl.BlockSpec((tm, D_MODEL), row),
                   pl.BlockSpec((tm, D_MODEL), row),
                   pl.BlockSpec((tm, LANES), row),
                   pl.BlockSpec((SUBLANES, tm), lambda i: (0, i)),
                   pl.BlockSpec((N_EXPERTS, LANES), row)],
        out_shape=[jax.ShapeDtypeStruct((t, D_MODEL), F32),
                   jax.ShapeDtypeStruct((t, D_MODEL), BF16),
                   jax.ShapeDtypeStruct((t, LANES), F32),
                   jax.ShapeDtypeStruct((SUBLANES, t), jnp.int32),
                   jax.ShapeDtypeStruct((t // tm * N_EXPERTS, LANES), jnp.int32)],
        scratch_shapes=[pltpu.VMEM((N_EXPERTS, LANES), F32),
                        pltpu.VMEM((D_SSD + D_ATTN, D_MODEL), BF16)],
        compiler_params=_params(1),
        name="route",
    )(x2, y_mix, w_out, g, wr, br, jnp.asarray(np.triu(np.ones((tm, tm), np.float32), 1), dtype=BF16))


SEG_SIZE_BITS = (TM_ROUTE // SUBLANES).bit_length()


TILE_SIZE_BITS = (R_SORTED // SUBLANES).bit_length()


def _piece_tables(cnt8, segst, segdst):
    n = (cnt8 // SUBLANES)[:, None, :]
    b = jnp.arange(SEG_SIZE_BITS, dtype=jnp.int32)[None, :, None]
    has = (n >> b) & 1
    off = ((n >> (b + 1)) << (b + 1)) * SUBLANES
    rank = jnp.cumsum(has, axis=2) - has
    place = (has[..., None] == 1) & (rank[..., None] == jnp.arange(N_EXPERTS, dtype=jnp.int32))
    dense = lambda v: jnp.sum(jnp.where(place, v[..., None], 0), axis=2).reshape(-1)
    return (jnp.sum(has, axis=2).reshape(-1), dense(segst[:, None, :] + off),
            dense(segdst[:, None, :] + off), jnp.sum(cnt8, axis=1) // SUBLANES)


def _for_each_piece(tile, pcnt_ref, psrc_ref, pdst_ref, fn):
    for b in range(SEG_SIZE_BITS):
        base = (tile * SEG_SIZE_BITS + b) * N_EXPERTS

        def body(p, carry, b=b, base=base):
            fn(SUBLANES << b, psrc_ref[base + p], pdst_ref[base + p])
            return carry

        lax.fori_loop(0, pcnt_ref[tile * SEG_SIZE_BITS + b], body, 0)


def _wait_rows(n_tiles8, descriptor):
    for b in range(TILE_SIZE_BITS):
        @pl.when(((n_tiles8 >> b) & 1) == 1)
        def _(b=b):
            descriptor(SUBLANES << b).wait()


def _rows(ref, row, n):
    if not isinstance(row, int):
        row = pl.multiple_of(row, SUBLANES)
    return ref.at[pl.ds(row, n), :]


def _dispatch_kernel(pcnt_ref, psrc_ref, pdst_ref, ntot_ref, zflag_ref, u_ref, post_ref, xs_ref,
                     buf_ref, zero_ref, sems, zsem):
    tm = u_ref.shape[0]
    i = pl.program_id(0)
    slot = i % 2
    bm = zero_ref.shape[0]

    def fill(b, flag):
        return pltpu.make_async_copy(zero_ref, xs_ref.at[pl.ds(b * bm, bm), :], zsem.at[flag - 1])

    def for_flagged(flag, action):
        def body(b, carry):
            @pl.when(zflag_ref[b] == flag)
            def _():
                action(fill(b, flag))
            return carry

        lax.fori_loop(0, zflag_ref.shape[0], body, 0)

    @pl.when(i == 0)
    def _():
        zero_ref[...] = jnp.zeros(zero_ref.shape, jnp.int32)
        for_flagged(1, lambda c: c.start())
        for_flagged(2, lambda c: c.start())
        for_flagged(1, lambda c: c.wait())

    @pl.when(i == pl.num_programs(0) - 1)
    def _():
        for_flagged(2, lambda c: c.wait())

    pos = post_ref[...]
    j = lax.broadcasted_iota(jnp.int32, (R_SORTED, tm), 0)
    sel = jnp.zeros((R_SORTED, tm), F32)
    for k in range(TOP_K):
        sel = jnp.where(j == pos[k:k + 1, :], 1.0, sel)
    buf_ref[slot] = _pack_bf16_pairs(_dot(sel.astype(BF16), u_ref[...]))

    def write(slot_, n, src_row, dst_row):
        return pltpu.make_async_copy(_rows(buf_ref.at[slot_], src_row, n),
                                     _rows(xs_ref, dst_row, n), sems.at[slot_])

    _for_each_piece(i, pcnt_ref, psrc_ref, pdst_ref, lambda n, s, d: write(slot, n, s, d).start())

    @pl.when(i > 0)
    def _():
        _wait_rows(ntot_ref[i - 1], lambda n: write(1 - slot, n, 0, 0))

    @pl.when(i == pl.num_programs(0) - 1)
    def _():
        _wait_rows(ntot_ref[i], lambda n: write(slot, n, 0, 0))


def _dispatch(pieces, zero_flag, u, post):
    t = u.shape[0]
    tm = TM_ROUTE
    n_rows = zero_flag.shape[0] * EXPERT_ROW_STEP
    return pl.pallas_call(
        _dispatch_kernel,
        grid_spec=pltpu.PrefetchScalarGridSpec(
            num_scalar_prefetch=5,
            grid=(t // tm,),
            in_specs=[pl.BlockSpec((tm, D_MODEL), lambda i, *_: (i, 0)),
                      pl.BlockSpec((SUBLANES, tm), lambda i, *_: (0, i))],
            out_specs=pl.BlockSpec(memory_space=pl.ANY),
            scratch_shapes=[pltpu.VMEM((2, R_SORTED, D_PACKED), jnp.int32),
                            pltpu.VMEM((EXPERT_ROW_STEP, D_PACKED), jnp.int32),
                            pltpu.SemaphoreType.DMA((2,)),
                            pltpu.SemaphoreType.DMA((2,))]),
        out_shape=jax.ShapeDtypeStruct((n_rows, D_PACKED), jnp.int32),
        compiler_params=_params(1),
        name="dispatch",
    )(*pieces, zero_flag, u, post)


def _expert_kernel(be_ref, nu_ref, first_ref, next_ref, rows_ref, xs_ref, w1_ref, b1_ref, w2_ref, b2_ref,
                   ys_ref, w1f_ref, w2f_ref, w1b_ref, w2b_ref, sems):
    i = pl.program_id(0)

    def fetch(e):
        return (pltpu.make_async_copy(w1_ref.at[e], w1f_ref, sems.at[0]),
                pltpu.make_async_copy(w2_ref.at[e], w2f_ref, sems.at[1]))

    @pl.when(i >= nu_ref[0])
    def _():
        ys_ref[...] = jnp.zeros(ys_ref.shape, jnp.int32)

    @pl.when(i < nu_ref[0])
    def _():
        e = be_ref[i]

        @pl.when(i == 0)
        def _():
            for c in fetch(e):
                c.start()

        @pl.when(first_ref[i] != 0)
        def _():
            for c in fetch(e):
                c.wait()
            w1b_ref[...] = w1f_ref[...].astype(BF16)
            w2b_ref[...] = w2f_ref[...].astype(BF16)

            @pl.when(next_ref[i] >= 0)
            def _():
                for c in fetch(next_ref[i]):
                    c.start()

        bm = xs_ref.shape[0]
        for m in range(EXPERT_ROW_STEP, bm + 1, EXPERT_ROW_STEP):
            @pl.when(rows_ref[i] == m)
            def _(m=m):
                hid = _dot(_unpack_bf16_pairs(xs_ref[0:m, :]), w1b_ref[...]) + b1_ref[pl.ds(e, 1), :]
                x_glu = jnp.minimum(hid[:, :D_EXPERT], SWIGLU_LIMIT)
                x_lin = jnp.clip(hid[:, D_EXPERT:], -SWIGLU_LIMIT, SWIGLU_LIMIT)
                act = x_glu / (1.0 + jnp.exp(-SWIGLU_ALPHA * x_glu)) * (x_lin + 1.0)
                y = _dot(act.astype(BF16), w2b_ref[...]) + b2_ref[pl.ds(e, 1), :]
                ys_ref[0:m, :] = _pack_bf16_pairs(y.astype(BF16).astype(F32))
                if m < bm:
                    ys_ref[m:, :] = jnp.zeros((bm - m, D_PACKED), jnp.int32)


def _experts(blk_expert, n_used, first_flag, next_expert, blk_rows, xs, w1, b1, w2, b2):
    n_rows = xs.shape[0]
    bm = BM_EXPERT
    nblk = n_rows // bm
    last = lambda i, nu: jnp.maximum(jnp.minimum(i, nu[0] - 1), 0)
    row = lambda i, be, nu, *_: (last(i, nu), 0)
    whole = lambda i, *_: (0, 0)
    return pl.pallas_call(
        _expert_kernel,
        grid_spec=pltpu.PrefetchScalarGridSpec(
            num_scalar_prefetch=5,
            grid=(nblk,),
            in_specs=[pl.BlockSpec((bm, D_PACKED), row),
                      pl.BlockSpec(memory_space=pl.ANY),
                      pl.BlockSpec((N_EXPERTS, 2 * D_EXPERT), whole),
                      pl.BlockSpec(memory_space=pl.ANY),
                      pl.BlockSpec((N_EXPERTS, D_MODEL), whole)],
            out_specs=pl.BlockSpec((bm, D_PACKED), lambda i, *_: (i, 0)),
            scratch_shapes=[pltpu.VMEM((D_MODEL, 2 * D_EXPERT), F32),
                            pltpu.VMEM((D_EXPERT, D_MODEL), F32),
                            pltpu.VMEM((D_MODEL, 2 * D_EXPERT), BF16),
                            pltpu.VMEM((D_EXPERT, D_MODEL), BF16),
                            pltpu.SemaphoreType.DMA((2,))]),
        out_shape=jax.ShapeDtypeStruct((n_rows, D_PACKED), jnp.int32),
        compiler_params=_params(1),
        name="experts",
    )(blk_expert, n_used, first_flag, next_expert, blk_rows, xs, w1, b1, w2, b2)


def _combine_kernel(pcnt_ref, psrc_ref, pdst_ref, ntot_ref, ys_ref, h_ref, slab_ref, g_ref, o_ref,
                    buf_ref, sems):
    tm = h_ref.shape[0]
    i = pl.program_id(0)
    slot = i % 2

    def read(slot_, n, sorted_row, ys_row):
        return pltpu.make_async_copy(_rows(ys_ref, ys_row, n),
                                     _rows(buf_ref.at[slot_], sorted_row, n), sems.at[slot_])

    def gather(tile, slot_):
        _for_each_piece(tile, pcnt_ref, psrc_ref, pdst_ref, lambda n, s, d: read(slot_, n, s, d).start())

    @pl.when(i == 0)
    def _():
        buf_ref[...] = jnp.zeros(buf_ref.shape, jnp.int32)
        gather(0, 0)

    @pl.when(i + 1 < pl.num_programs(0))
    def _():
        gather(i + 1, 1 - slot)

    _wait_rows(ntot_ref[i], lambda n: read(slot, n, 0, 0))

    slab = slab_ref[...]
    lane = lax.broadcasted_iota(jnp.int32, (tm, R_SORTED), 1).astype(F32)
    wmat = jnp.zeros((tm, R_SORTED), F32)
    for k in range(TOP_K):
        wmat = jnp.where(lane == slab[:, k:k + 1], slab[:, TOP_K + k:TOP_K + k + 1], wmat)
    moe = _dot(wmat.astype(BF16), _unpack_bf16_pairs(buf_ref[slot]))
    o_ref[...] = _rms(h_ref[...] + moe, g_ref[...])


def _combine(pieces, ys, h, slab, g):
    t = h.shape[0]
    tm = TM_ROUTE
    return pl.pallas_call(
        _combine_kernel,
        grid_spec=pltpu.PrefetchScalarGridSpec(
            num_scalar_prefetch=4,
            grid=(t // tm,),
            in_specs=[pl.BlockSpec(memory_space=pl.ANY),
                      pl.BlockSpec((tm, D_MODEL), lambda i, *_: (i, 0)),
                      pl.BlockSpec((tm, LANES), lambda i, *_: (i, 0)),
                      pl.BlockSpec((1, D_MODEL), lambda i, *_: (0, 0))],
            out_specs=pl.BlockSpec((tm, D_MODEL), lambda i, *_: (i, 0)),
            scratch_shapes=[pltpu.VMEM((2, R_SORTED, D_PACKED), jnp.int32),
                            pltpu.SemaphoreType.DMA((2,))]),
        out_shape=jax.ShapeDtypeStruct((t, D_MODEL), F32),
        compiler_params=_params(1),
        name="combine",
    )(*pieces, ys, h, slab, g)


def _pad_lanes(v, fill=0.0):
    return jnp.pad(v.reshape(1, -1), ((0, 0), (0, LANES - v.shape[-1])), constant_values=fill)


def kernel(x, norm_mix_g, w_in, conv_w, conv_b, dt_bias, a_log, d_skip, ssd_norm_g, attn_sinks,
           rel_bias, attn_norm_g, w_out, norm_ffn_g, w_router, b_router, w1, b1, w2, b2,
           norm_final_g):
    bsz, seq, d = x.shape
    t = bsz * seq
    nc = seq // CHUNK
    depth = w_in.shape[0]
    bias = _bias_table(rel_bias)

    assert depth == 1, "the final norm is fused into the combine kernel: single layer only"
    h = x.reshape(t, d)
    for layer in range(depth):
        y_mix = _mixer(h.reshape(bsz, seq, d), norm_mix_g[layer].reshape(1, d), jnp.swapaxes(w_in, 1, 2), layer,
                       conv_w[layer], conv_b[layer].reshape(1, -1),
                       _pad_lanes(dt_bias[layer]), _pad_lanes(a_log[layer]),
                       jnp.repeat(d_skip[layer], SSD_HEAD_DIM).reshape(1, -1),
                       ssd_norm_g[layer].reshape(1, -1), attn_sinks[layer], bias,
                       attn_norm_g[layer].reshape(1, -1)).reshape(t, -1)

        wr = w_router[layer].T.astype(BF16)
        br = jnp.broadcast_to(b_router[layer][:, None], (N_EXPERTS, LANES))
        h_mid, u, slab, post, tab = _route(
            h, y_mix, w_out, layer, norm_ffn_g[layer].reshape(1, d), wr, br)

        bm = BM_EXPERT
        ntiles = t // TM_ROUTE
        nblk = (t * TOP_K + ntiles * N_EXPERTS * (SUBLANES - 1)) // bm + N_EXPERTS
        tab = tab.reshape(ntiles, N_EXPERTS, LANES)
        cnt8, off8, segst = tab[:, :, 0], tab[:, :, 1], tab[:, :, 2]
        total8 = off8[-1] + cnt8[-1]
        padded = ((total8 + bm - 1) // bm) * bm
        pend = jnp.cumsum(padded)
        segdst = (pend - padded)[None, :] + off8
        n_used = pend[-1:] // bm
        eids = jnp.arange(N_EXPERTS, dtype=jnp.int32)
        blk = jnp.arange(nblk, dtype=jnp.int32)
        blk_expert = jnp.minimum(
            jnp.sum((blk[:, None] * bm >= pend[None, :]).astype(jnp.int32), axis=1), N_EXPERTS - 1)
        e_end = jnp.sum(jnp.where(blk_expert[:, None] == eids[None, :],
                                  ((pend - padded) + total8)[None, :], 0), axis=1)
        valid = jnp.where(blk < n_used[0], jnp.clip(e_end - blk * bm, 0, bm), 0)
        blk_rows = ((valid + EXPERT_ROW_STEP - 1) // EXPERT_ROW_STEP) * EXPERT_ROW_STEP
        piece_lo = jnp.arange(bm // EXPERT_ROW_STEP, dtype=jnp.int32)[None, :] * EXPERT_ROW_STEP
        zero_flag = jnp.where(piece_lo >= valid[:, None], 2,
                              (piece_lo + EXPERT_ROW_STEP > valid[:, None]).astype(jnp.int32)).reshape(-1)
        before = jnp.concatenate([blk_expert[:1], blk_expert[:-1]])
        first_flag = ((blk == 0) | (before != blk_expert)).astype(jnp.int32)
        cand = jnp.where((eids[None, :] > eids[:, None]) & (padded[None, :] > 0), eids[None, :], N_EXPERTS)
        next_nonempty = jnp.min(cand, axis=1)
        next_nonempty = jnp.where(next_nonempty == N_EXPERTS, -1, next_nonempty)
        next_expert = jnp.sum(jnp.where(blk_expert[:, None] == eids[None, :], next_nonempty[None, :], 0), axis=1)
        pieces = _piece_tables(cnt8, segst, segdst)

        xs = _dispatch(pieces, zero_flag, u, post)
        ys = _experts(blk_expert, n_used, first_flag, next_expert, blk_rows, xs, w1[layer],
                      b1[layer], w2[layer], b2[layer])
        h = _combine(pieces, ys, h_mid, slab, norm_final_g.reshape(1, d))
    return h.reshape(bsz, seq, d)
```

```python
import math

import numpy as np
import jax
import jax.numpy as jnp
from jax import lax
from jax.experimental import pallas as pl
from jax.experimental.pallas import tpu as pltpu

F32 = jnp.float32
BF16 = jnp.bfloat16

D_MODEL = 1024
SSD_HEADS = 8
SSD_HEAD_DIM = 64
D_SSD = SSD_HEADS * SSD_HEAD_DIM
SSD_GROUPS = 2
SSD_HEADS_PER_GROUP = SSD_HEADS // SSD_GROUPS
D_STATE = 128
CONV_WIDTH = 4
CHUNK = 128
D_CONV = D_SSD + 2 * SSD_GROUPS * D_STATE
ATTN_Q_HEADS = 8
ATTN_KV_HEADS = 2
ATTN_Q_PER_KV = ATTN_Q_HEADS // ATTN_KV_HEADS
ATTN_HEAD_DIM = 64
D_ATTN = ATTN_Q_HEADS * ATTN_HEAD_DIM
D_KV = ATTN_KV_HEADS * ATTN_HEAD_DIM
WINDOW = 128
REL_BUCKETS = 32
REL_MAX_DIST = 128
N_EXPERTS = 32
TOP_K = 4
D_EXPERT = D_MODEL
SWIGLU_LIMIT = 7.0
SWIGLU_ALPHA = 1.702
RMS_EPS = 1e-5

LANES = 128
SUBLANES = 8
V7X_VMEM_BYTES = 64 * 2 ** 20
VMEM_LIMIT = V7X_VMEM_BYTES * 7 // 8
NEG = -1e30
assert WINDOW == CHUNK, "the attention kernel sees exactly the current and the previous block"

TM_ROUTE = 512
BM_EXPERT = 1024
EXPERT_ROW_STEP = 128
SEQS_PER_STEP = 2
CHUNKS_PER_STEP = 4
R_SORTED = TM_ROUTE * TOP_K + N_EXPERTS * SUBLANES

D_PROJ_PACKED = D_SSD + D_CONV + D_ATTN + 2 * D_KV + LANES


def _params(n_axes):
    return pltpu.CompilerParams(dimension_semantics=("arbitrary",) * n_axes,
                                vmem_limit_bytes=VMEM_LIMIT)


def _rms(x, g):
    return x * lax.rsqrt(jnp.mean(x * x, axis=-1, keepdims=True) + RMS_EPS) * g


def _silu(x):
    return x / (1.0 + jnp.exp(-x))


def _dot(a, b):
    return jnp.dot(a, b, preferred_element_type=F32)


def _dot_f32_by_mask(mask, x, mask_on_left=False):
    hi = x.astype(BF16)
    r1 = x - hi.astype(F32)
    mid = r1.astype(BF16)
    lo = (r1 - mid.astype(F32)).astype(BF16)
    out = None
    for piece in (hi, mid, lo):
        term = _dot(mask, piece) if mask_on_left else _dot(piece, mask)
        out = term if out is None else out + term
    return out


D_PACKED = D_MODEL // 2
BF16_BITS = 16


def _pack_bf16_pairs(x):
    w = x.shape[1] // 2
    hi = lax.bitcast_convert_type(x[:, :w], jnp.int32)
    lo = lax.shift_right_logical(lax.bitcast_convert_type(x[:, w:], jnp.int32), BF16_BITS)
    return hi | lo


def _unpack_bf16_pairs(v):
    upper_half_word = jnp.int32(-(1 << BF16_BITS))
    hi = lax.bitcast_convert_type(v & upper_half_word, F32).astype(BF16)
    lo = lax.bitcast_convert_type(lax.shift_left(v, BF16_BITS), F32).astype(BF16)
    return jnp.concatenate([hi, lo], axis=1)


def _t5_bucket_table():
    dist = CHUNK + np.arange(CHUNK)[:, None] - np.arange(2 * CHUNK)[None, :]
    in_window = (dist >= 0) & (dist < WINDOW)
    d = np.clip(dist, 0, REL_MAX_DIST)
    max_exact = REL_BUCKETS // 2
    large = max_exact + (np.log(np.maximum(d, 1).astype(np.float32) / max_exact)
                         / math.log(REL_MAX_DIST / max_exact)
                         * (REL_BUCKETS - max_exact)).astype(np.int32)
    large = np.minimum(large, REL_BUCKETS - 1)
    bucket = np.where(d < max_exact, d, large)
    return np.where(in_window, bucket, -1).astype(np.int32)


def _bias_kernel(rb_ref, bucket_ref, o_ref):
    bucket = bucket_ref[...]
    col = lax.broadcasted_iota(jnp.int32, bucket.shape, 1)
    for h in range(ATTN_Q_HEADS):
        acc = jnp.full(bucket.shape, NEG, F32)
        for b in range(REL_BUCKETS):
            acc = jnp.where(bucket == b, rb_ref[b, h], acc)
        o_ref[1, h] = acc
        o_ref[0, h] = jnp.where(col >= CHUNK, acc, NEG)


def _bias_table(rel_bias):
    bucket = jnp.asarray(_t5_bucket_table())
    return pl.pallas_call(
        _bias_kernel,
        out_shape=jax.ShapeDtypeStruct((2, ATTN_Q_HEADS, CHUNK, 2 * CHUNK), F32),
        in_specs=[pl.BlockSpec(memory_space=pltpu.SMEM),
                  pl.BlockSpec(memory_space=pltpu.VMEM)],
        out_specs=pl.BlockSpec(memory_space=pltpu.VMEM),
        name="bias_table",
    )(rel_bias, bucket)


def _mixer_kernel(sink_ref, x_ref, gm_ref, w_ref, cw_ref, cb_ref, dtb_ref, alog_ref, dskip_ref, gs_ref,
                  expand_ref, bias_ref, ga_ref, o_ref, wb_ref, state_ref, xpad_ref, kvp_ref):
    j = pl.program_id(1)

    @pl.when((pl.program_id(0) == 0) & (j == 0))
    def _():
        o2 = D_SSD + D_CONV
        o3 = o2 + SSD_HEADS
        n_qkv = D_ATTN + 2 * D_KV
        for lo, src_lo, n in ((0, 0, o2), (o2, o3, n_qkv)):
            for r in range(0, n, LANES):
                wb_ref[:, lo + r:lo + r + LANES] = w_ref[0, src_lo + r:src_lo + r + LANES, :].T.astype(BF16)
        wb_ref[:, o2 + n_qkv:] = jnp.concatenate(
            [w_ref[0, o2:o3, :], jnp.zeros((LANES - SSD_HEADS, D_MODEL), F32)], axis=0).T.astype(BF16)

    @pl.when(j == 0)
    def _():
        state_ref[...] = jnp.zeros(state_ref.shape, F32)
        xpad_ref[...] = jnp.zeros(xpad_ref.shape, F32)
        kvp_ref[...] = jnp.zeros(kvp_ref.shape, F32)

    ns, rows, _ = x_ref.shape
    bounds = np.cumsum([0, D_SSD, D_CONV, D_ATTN, 2 * D_KV, LANES])
    pieces = [(c, s) for c in range(rows // CHUNK) for s in range(ns)]

    def project(c, s):
        x = x_ref[s, c * CHUNK:(c + 1) * CHUNK, :]
        return _dot(_rms(x, gm_ref[...]).astype(BF16), wb_ref[...])

    proj = project(*pieces[0])
    for k, (c, s) in enumerate(pieces):
        proj_next = project(*pieces[k + 1]) if k + 1 < len(pieces) else None
        z, xbc, q, kv, dt = [proj[:, a:b] for a, b in zip(bounds[:-1], bounds[1:])]
        has_prev = jnp.minimum(j, 1) if c == 0 else 1
        y_ssd = _ssd_chunk(z, xbc, dt, cw_ref, cb_ref, dtb_ref, alog_ref, dskip_ref, gs_ref,
                           expand_ref, state_ref.at[s], xpad_ref.at[s])
        y_attn = _swa_block(sink_ref, q, kv, kvp_ref[s], has_prev, bias_ref, ga_ref)
        kvp_ref[s] = kv
        o_ref[s, c * CHUNK:(c + 1) * CHUNK, :] = jnp.concatenate([y_ssd, y_attn], axis=1).astype(BF16)
        proj = proj_next


def _ssd_chunk(z, x_cur, dt_raw, cw_ref, cb_ref, dtb_ref, alog_ref, dskip_ref, g_ref,
               expand_ref, state_ref, xpad_ref):
    L = CHUNK
    G, R, P, N = SSD_GROUPS, SSD_HEADS_PER_GROUP, SSD_HEAD_DIM, D_STATE
    GW = R * P

    xpad = jnp.concatenate([xpad_ref[...], x_cur], axis=0)
    xpad_ref[...] = x_cur[L - SUBLANES:, :]
    acc = cb_ref[...] + cw_ref[CONV_WIDTH - 1:CONV_WIDTH, :] * x_cur
    for d in range(1, CONV_WIDTH):
        k = CONV_WIDTH - 1 - d
        acc = acc + cw_ref[k:k + 1, :] * pltpu.roll(xpad, d, axis=0)[SUBLANES:, :]
    xbc = _silu(acc)
    xs = xbc[:, :D_SSD]
    bm = xbc[:, D_SSD:D_SSD + G * N]
    cm = xbc[:, D_SSD + G * N:]

    dtr = dt_raw + dtb_ref[...]
    dt = jnp.maximum(dtr, 0.0) + jnp.log(1.0 + jnp.exp(-jnp.abs(dtr)))
    a_dt = dt * (-jnp.exp(alog_ref[...]))
    ri = lax.broadcasted_iota(jnp.int32, (L, L), 0)
    ci = lax.broadcasted_iota(jnp.int32, (L, L), 1)
    causal = ci <= ri
    a_cum = _dot_f32_by_mask(causal.astype(BF16), a_dt, mask_on_left=True)
    a_cum_t = _dot_f32_by_mask((ri <= ci).astype(BF16), a_dt.T)
    a_last = a_cum[L - 1:L, :]
    stack = jnp.concatenate(
        [dt, jnp.exp(a_cum), jnp.exp(a_last - a_cum),
         jnp.broadcast_to(jnp.exp(a_last), (SUBLANES, LANES))], axis=0)
    ex = _dot_f32_by_mask(expand_ref[...], stack)
    dt_x, ea_x, dte_x, cd_x = ex[0:L], ex[L:2 * L], ex[2 * L:3 * L], ex[3 * L:3 * L + 1]
    xdt = xs * dt_x

    ys = []
    for g in range(G):
        bm_g = bm[:, g * N:(g + 1) * N]
        cm_g = cm[:, g * N:(g + 1) * N].astype(BF16)
        cb = lax.dot_general(cm_g, bm_g.astype(BF16), (((1,), (1,)), ((), ())),
                             preferred_element_type=F32)
        xdt_g = xdt[:, g * GW:(g + 1) * GW]
        yd = []
        for r in range(R):
            h = g * R + r
            seg = a_cum[:, h:h + 1] - a_cum_t[h:h + 1, :]
            dec = jnp.exp(jnp.where(causal, seg, NEG))
            yd.append(_dot((cb * dec).astype(BF16), xdt_g[:, r * P:(r + 1) * P].astype(BF16)))
        y_diag = jnp.concatenate(yd, axis=1)
        st = state_ref[g]
        y_off = _dot(cm_g, st.astype(BF16)) * ea_x[:, g * GW:(g + 1) * GW]
        new = _dot(bm_g.T.astype(BF16), (xdt_g * dte_x[:, g * GW:(g + 1) * GW]).astype(BF16))
        state_ref[g] = st * cd_x[:, g * GW:(g + 1) * GW] + new
        ys.append(y_diag + y_off + xs[:, g * GW:(g + 1) * GW] * dskip_ref[:, g * GW:(g + 1) * GW])
    y = jnp.concatenate(ys, axis=1)
    return _rms(y * _silu(z), g_ref[...])


def _mixer(x, norm_g, w_in_t, layer, conv_w, conv_b, dt_bias, a_log, d_skip_x, ssd_norm_g, sinks, bias,
           attn_norm_g):
    bsz, seq, _ = x.shape
    rows = CHUNKS_PER_STEP * CHUNK
    nsteps = seq // rows
    expand = np.zeros((LANES, D_SSD), np.float32)
    for h in range(SSD_HEADS):
        expand[h, h * SSD_HEAD_DIM:(h + 1) * SSD_HEAD_DIM] = 1.0
    ns = SEQS_PER_STEP
    row = lambda b, j: (b, j, 0)
    fixed = lambda b, j: (0, 0)
    return pl.pallas_call(
        _mixer_kernel,
        grid=(bsz // ns, nsteps),
        in_specs=[pl.BlockSpec(memory_space=pltpu.SMEM),
                  pl.BlockSpec((ns, rows, D_MODEL), row),
                  pl.BlockSpec((1, D_MODEL), fixed),
                  pl.BlockSpec((1,) + w_in_t.shape[1:], lambda b, j: (layer, 0, 0)),
                  pl.BlockSpec((CONV_WIDTH, D_CONV), fixed),
                  pl.BlockSpec((1, D_CONV), fixed),
                  pl.BlockSpec((1, LANES), fixed),
                  pl.BlockSpec((1, LANES), fixed),
                  pl.BlockSpec((1, D_SSD), fixed),
                  pl.BlockSpec((1, D_SSD), fixed),
                  pl.BlockSpec((LANES, D_SSD), fixed),
                  pl.BlockSpec((2, ATTN_Q_HEADS, CHUNK, 2 * CHUNK), lambda b, j: (0, 0, 0, 0)),
                  pl.BlockSpec((1, D_ATTN), fixed)],
        out_specs=pl.BlockSpec((ns, rows, D_SSD + D_ATTN), row),
        out_shape=jax.ShapeDtypeStruct((bsz, seq, D_SSD + D_ATTN), BF16),
        scratch_shapes=[pltpu.VMEM((D_MODEL, D_PROJ_PACKED), BF16),
                        pltpu.VMEM((ns, SSD_GROUPS, D_STATE, SSD_HEADS_PER_GROUP * SSD_HEAD_DIM), F32),
                        pltpu.VMEM((ns, SUBLANES, D_CONV), F32),
                        pltpu.VMEM((ns, CHUNK, 2 * D_KV), F32)],
        compiler_params=_params(2),
        name="mixer",
    )(sinks, x, norm_g, w_in_t, conv_w, conv_b, dt_bias, a_log, d_skip_x, ssd_norm_g,
      jnp.asarray(expand, dtype=BF16), bias, attn_norm_g)


def _swa_block(sink_ref, q, kv, kvp, has_prev, bias_ref, g_ref):
    L, Dh = CHUNK, ATTN_HEAD_DIM
    q = q * (1.0 / math.sqrt(Dh))
    outs = []
    for hk in range(ATTN_KV_HEADS):
        ks = slice(hk * Dh, (hk + 1) * Dh)
        vs = slice(D_KV + hk * Dh, D_KV + (hk + 1) * Dh)
        kc = jnp.concatenate([kvp[:, ks], kv[:, ks]], axis=0).astype(BF16)
        vc = jnp.concatenate([kvp[:, vs], kv[:, vs]], axis=0).astype(BF16)
        for g in range(ATTN_Q_PER_KV):
            h = hk * ATTN_Q_PER_KV + g
            qh = q[:, h * Dh:(h + 1) * Dh].astype(BF16)
            s = lax.dot_general(qh, kc, (((1,), (1,)), ((), ())), preferred_element_type=F32)
            s = s + bias_ref[has_prev, h]
            sink = sink_ref[h]
            m = jnp.maximum(jnp.max(s, axis=-1, keepdims=True), sink)
            p = jnp.exp(s - m)
            denom = jnp.sum(p, axis=-1, keepdims=True) + jnp.exp(sink - m)
            outs.append(_dot(p.astype(BF16), vc) / denom)
    return _rms(jnp.concatenate(outs, axis=1), g_ref[...])


def _route_kernel(x_ref, y_ref, wo_ref, g_ref, wr_ref, br_ref, ahead_ref,
                  h_ref, u_ref, slab_ref, post_ref, tab_ref, run_ref, wob_ref):
    tm = x_ref.shape[0]

    @pl.when(pl.program_id(0) == 0)
    def _():
        run_ref[...] = jnp.zeros(run_ref.shape, F32)
        wob_ref[...] = wo_ref[0].astype(BF16)

    h = x_ref[...] + _dot(y_ref[...], wob_ref[...])
    h_ref[...] = h
    u = _rms(h, g_ref[...])
    ub = u.astype(BF16)
    u_ref[...] = ub
    E = N_EXPERTS
    logits = lax.dot_general(wr_ref[...], ub, (((1,), (1,)), ((), ())), preferred_element_type=F32)
    logits = logits + jnp.concatenate([br_ref[...]] * (tm // LANES), axis=1)

    eidx = lax.broadcasted_iota(jnp.int32, (E, tm), 0).astype(F32)
    vals, idxs = [], []
    cur = logits
    for _ in range(TOP_K):
        m = jnp.max(cur, axis=0, keepdims=True)
        ix = jnp.min(jnp.where(cur == m, eidx, float(E)), axis=0, keepdims=True)
        vals.append(m)
        idxs.append(ix)
        cur = jnp.where(eidx == ix, NEG, cur)
    es = [jnp.exp(v - vals[0]) for v in vals]
    den = sum(es[1:], es[0])

    onehot = jnp.zeros((E, tm), F32)
    for ix in idxs:
        onehot = onehot + (eidx == ix).astype(F32)
    before = _dot(onehot.astype(BF16), ahead_ref[...])
    cnt = jnp.sum(onehot, axis=1, keepdims=True)
    cnt8 = jnp.floor((cnt + (SUBLANES - 1)) * (1.0 / SUBLANES)) * SUBLANES
    el = lax.broadcasted_iota(jnp.int32, (E, E), 0)
    ec = lax.broadcasted_iota(jnp.int32, (E, E), 1)
    seg_start = _dot_f32_by_mask((ec < el).astype(BF16), jnp.broadcast_to(cnt8, (E, LANES)),
                                 mask_on_left=True)[:, 0:1]
    where_to = before + seg_start

    rows = [jnp.sum(jnp.where(eidx == idxs[k], where_to, 0.0), axis=0, keepdims=True)
            for k in range(TOP_K)]
    rows += [es[k] / den for k in range(TOP_K)]
    stack = jnp.concatenate(rows, axis=0)
    post_ref[...] = stack.astype(jnp.int32)
    slab_ref[...] = jnp.concatenate([stack, jnp.zeros((LANES - 2 * TOP_K, tm), F32)], axis=0).T

    lane = lax.broadcasted_iota(jnp.int32, (E, LANES), 1)
    tab = jnp.where(lane == 0, cnt8, jnp.where(lane == 1, run_ref[...], jnp.where(lane == 2, seg_start, 0.0)))
    tab_ref[...] = tab.astype(jnp.int32)
    run_ref[...] = run_ref[...] + cnt8


def _route(x2, y_mix, w_out, layer, g, wr, br):
    t = x2.shape[0]
    tm = TM_ROUTE
    row = lambda i: (i, 0)
    fixed = lambda i: (0, 0)
    return pl.pallas_call(
        _route_kernel,
        grid=(t // tm,),
        in_specs=[pl.BlockSpec((tm, D_MODEL), row),
                  pl.BlockSpec((tm, D_SSD + D_ATTN), row),
                  pl.BlockSpec((1, D_SSD + D_ATTN, D_MODEL), lambda i: (layer, 0, 0)),
                  pl.BlockSpec((1, D_MODEL), fixed),
                  pl.BlockSpec((N_EXPERTS, D_MODEL), fixed),
                  pl.BlockSpec((N_EXPERTS, LANES), fixed),
                  pl.BlockSpec((tm, tm), fixed)],
        out_specs=[pl.BlockSpec((tm, D_MODEL), row),
                   pl.BlockSpec((tm, D_MODEL), row),
                   pl.BlockSpec((tm, LANES), row),
                   pl.BlockSpec((SUBLANES, tm), lambda i: (0, i)),
                   pl.BlockSpec((N_EXPERTS, LANES), row)],
        out_shape=[jax.ShapeDtypeStruct((t, D_MODEL), F32),
                   jax.ShapeDtypeStruct((t, D_MODEL), BF16),
                   jax.ShapeDtypeStruct((t, LANES), F32),
                   jax.ShapeDtypeStruct((SUBLANES, t), jnp.int32),
                   jax.ShapeDtypeStruct((t // tm * N_EXPERTS, LANES), jnp.int32)],
        scratch_shapes=[pltpu.VMEM((N_EXPERTS, LANES), F32),
                        pltpu.VMEM((D_SSD + D_ATTN, D_MODEL), BF16)],
        compiler_params=_params(1),
        name="route",
    )(x2, y_mix, w_out, g, wr, br, jnp.asarray(np.triu(np.ones((tm, tm), np.float32), 1), dtype=BF16))


SEG_SIZE_BITS = (TM_ROUTE // SUBLANES).bit_length()


TILE_SIZE_BITS = (R_SORTED // SUBLANES).bit_length()


def _piece_tables(cnt8, segst, segdst):
    n = (cnt8 // SUBLANES)[:, None, :]
    b = jnp.arange(SEG_SIZE_BITS, dtype=jnp.int32)[None, :, None]
    has = (n >> b) & 1
    off = ((n >> (b + 1)) << (b + 1)) * SUBLANES
    rank = jnp.cumsum(has, axis=2) - has
    place = (has[..., None] == 1) & (rank[..., None] == jnp.arange(N_EXPERTS, dtype=jnp.int32))
    dense = lambda v: jnp.sum(jnp.where(place, v[..., None], 0), axis=2).reshape(-1)
    return (jnp.sum(has, axis=2).reshape(-1), dense(segst[:, None, :] + off),
            dense(segdst[:, None, :] + off), jnp.sum(cnt8, axis=1) // SUBLANES)


def _for_each_piece(tile, pcnt_ref, psrc_ref, pdst_ref, fn):
    for b in range(SEG_SIZE_BITS):
        base = (tile * SEG_SIZE_BITS + b) * N_EXPERTS

        def body(p, carry, b=b, base=base):
            fn(SUBLANES << b, psrc_ref[base + p], pdst_ref[base + p])
            return carry

        lax.fori_loop(0, pcnt_ref[tile * SEG_SIZE_BITS + b], body, 0)


def _wait_rows(n_tiles8, descriptor):
    for b in range(TILE_SIZE_BITS):
        @pl.when(((n_tiles8 >> b) & 1) == 1)
        def _(b=b):
            descriptor(SUBLANES << b).wait()


def _rows(ref, row, n):
    if not isinstance(row, int):
        row = pl.multiple_of(row, SUBLANES)
    return ref.at[pl.ds(row, n), :]


def _dispatch_kernel(pcnt_ref, psrc_ref, pdst_ref, ntot_ref, zflag_ref, u_ref, post_ref, xs_ref,
                     buf_ref, zero_ref, sems, zsem):
    tm = u_ref.shape[0]
    i = pl.program_id(0)
    slot = i % 2
    bm = zero_ref.shape[0]

    def fill(b, flag):
        return pltpu.make_async_copy(zero_ref, xs_ref.at[pl.ds(b * bm, bm), :], zsem.at[flag - 1])

    def for_flagged(flag, action):
        def body(b, carry):
            @pl.when(zflag_ref[b] == flag)
            def _():
                action(fill(b, flag))
            return carry

        lax.fori_loop(0, zflag_ref.shape[0], body, 0)

    @pl.when(i == 0)
    def _():
        zero_ref[...] = jnp.zeros(zero_ref.shape, jnp.int32)
        for_flagged(1, lambda c: c.start())
        for_flagged(2, lambda c: c.start())
        for_flagged(1, lambda c: c.wait())

    @pl.when(i == pl.num_programs(0) - 1)
    def _():
        for_flagged(2, lambda c: c.wait())

    pos = post_ref[...]
    j = lax.broadcasted_iota(jnp.int32, (R_SORTED, tm), 0)
    sel = jnp.zeros((R_SORTED, tm), F32)
    for k in range(TOP_K):
        sel = jnp.where(j == pos[k:k + 1, :], 1.0, sel)
    buf_ref[slot] = _pack_bf16_pairs(_dot(sel.astype(BF16), u_ref[...]))

    def write(slot_, n, src_row, dst_row):
        return pltpu.make_async_copy(_rows(buf_ref.at[slot_], src_row, n),
                                     _rows(xs_ref, dst_row, n), sems.at[slot_])

    _for_each_piece(i, pcnt_ref, psrc_ref, pdst_ref, lambda n, s, d: write(slot, n, s, d).start())

    @pl.when(i > 0)
    def _():
        _wait_rows(ntot_ref[i - 1], lambda n: write(1 - slot, n, 0, 0))

    @pl.when(i == pl.num_programs(0) - 1)
    def _():
        _wait_rows(ntot_ref[i], lambda n: write(slot, n, 0, 0))


def _dispatch(pieces, zero_flag, u, post):
    t = u.shape[0]
    tm = TM_ROUTE
    n_rows = zero_flag.shape[0] * EXPERT_ROW_STEP
    return pl.pallas_call(
        _dispatch_kernel,
        grid_spec=pltpu.PrefetchScalarGridSpec(
            num_scalar_prefetch=5,
            grid=(t // tm,),
            in_specs=[pl.BlockSpec((tm, D_MODEL), lambda i, *_: (i, 0)),
                      pl.BlockSpec((SUBLANES, tm), lambda i, *_: (0, i))],
            out_specs=pl.BlockSpec(memory_space=pl.ANY),
            scratch_shapes=[pltpu.VMEM((2, R_SORTED, D_PACKED), jnp.int32),
                            pltpu.VMEM((EXPERT_ROW_STEP, D_PACKED), jnp.int32),
                            pltpu.SemaphoreType.DMA((2,)),
                            pltpu.SemaphoreType.DMA((2,))]),
        out_shape=jax.ShapeDtypeStruct((n_rows, D_PACKED), jnp.int32),
        compiler_params=_params(1),
        name="dispatch",
    )(*pieces, zero_flag, u, post)


def _expert_kernel(be_ref, nu_ref, first_ref, next_ref, rows_ref, xs_ref, w1_ref, b1_ref, w2_ref, b2_ref,
                   ys_ref, w1f_ref, w2f_ref, w1b_ref, w2b_ref, sems):
    i = pl.program_id(0)

    def fetch(e):
        return (pltpu.make_async_copy(w1_ref.at[e], w1f_ref, sems.at[0]),
                pltpu.make_async_copy(w2_ref.at[e], w2f_ref, sems.at[1]))

    @pl.when(i >= nu_ref[0])
    def _():
        ys_ref[...] = jnp.zeros(ys_ref.shape, jnp.int32)

    @pl.when(i < nu_ref[0])
    def _():
        e = be_ref[i]

        @pl.when(i == 0)
        def _():
            for c in fetch(e):
                c.start()

        @pl.when(first_ref[i] != 0)
        def _():
            for c in fetch(e):
                c.wait()
            w1b_ref[...] = w1f_ref[...].astype(BF16)
            w2b_ref[...] = w2f_ref[...].astype(BF16)

            @pl.when(next_ref[i] >= 0)
            def _():
                for c in fetch(next_ref[i]):
                    c.start()

        def ffn(off, m):
            rows = pl.ds(off, m)
            hid = _dot(_unpack_bf16_pairs(xs_ref[rows, :]), w1b_ref[...]) + b1_ref[pl.ds(e, 1), :]
            x_glu = jnp.minimum(hid[:, :D_EXPERT], SWIGLU_LIMIT)
            x_lin = jnp.clip(hid[:, D_EXPERT:], -SWIGLU_LIMIT, SWIGLU_LIMIT)
            act = x_glu / (1.0 + jnp.exp(-SWIGLU_ALPHA * x_glu)) * (x_lin + 1.0)
            y = _dot(act.astype(BF16), w2b_ref[...]) + b2_ref[pl.ds(e, 1), :]
            ys_ref[rows, :] = _pack_bf16_pairs(y.astype(BF16).astype(F32))

        bm = xs_ref.shape[0]
        half = bm // 2
        occupied = rows_ref[i]

        @pl.when(occupied == bm)
        def _():
            ffn(0, bm)

        @pl.when(occupied < bm)
        def _():
            has_half = occupied > half

            @pl.when(has_half)
            def _():
                ffn(0, half)

            off = pl.multiple_of(jnp.where(has_half, half, 0), half)
            for m in range(EXPERT_ROW_STEP, half + 1, EXPERT_ROW_STEP):
                @pl.when(occupied - off == m)
                def _(m=m):
                    ffn(off, m)

            for g in range(bm // EXPERT_ROW_STEP):
                @pl.when(g * EXPERT_ROW_STEP >= occupied)
                def _(g=g):
                    ys_ref[g * EXPERT_ROW_STEP:(g + 1) * EXPERT_ROW_STEP, :] = jnp.zeros(
                        (EXPERT_ROW_STEP, D_PACKED), jnp.int32)


def _experts(blk_expert, n_used, first_flag, next_expert, blk_rows, xs, w1, b1, w2, b2):
    n_rows = xs.shape[0]
    bm = BM_EXPERT
    nblk = n_rows // bm
    last = lambda i, nu: jnp.maximum(jnp.minimum(i, nu[0] - 1), 0)
    row = lambda i, be, nu, *_: (last(i, nu), 0)
    whole = lambda i, *_: (0, 0)
    return pl.pallas_call(
        _expert_kernel,
        grid_spec=pltpu.PrefetchScalarGridSpec(
            num_scalar_prefetch=5,
            grid=(nblk,),
            in_specs=[pl.BlockSpec((bm, D_PACKED), row),
                      pl.BlockSpec(memory_space=pl.ANY),
                      pl.BlockSpec((N_EXPERTS, 2 * D_EXPERT), whole),
                      pl.BlockSpec(memory_space=pl.ANY),
                      pl.BlockSpec((N_EXPERTS, D_MODEL), whole)],
            out_specs=pl.BlockSpec((bm, D_PACKED), lambda i, *_: (i, 0)),
            scratch_shapes=[pltpu.VMEM((D_MODEL, 2 * D_EXPERT), F32),
                            pltpu.VMEM((D_EXPERT, D_MODEL), F32),
                            pltpu.VMEM((D_MODEL, 2 * D_EXPERT), BF16),
                            pltpu.VMEM((D_EXPERT, D_MODEL), BF16),
                            pltpu.SemaphoreType.DMA((2,))]),
        out_shape=jax.ShapeDtypeStruct((n_rows, D_PACKED), jnp.int32),
        compiler_params=_params(1),
        name="experts",
    )(blk_expert, n_used, first_flag, next_expert, blk_rows, xs, w1, b1, w2, b2)


def _combine_kernel(pcnt_ref, psrc_ref, pdst_ref, ntot_ref, ys_ref, h_ref, slab_ref, g_ref, o_ref,
                    buf_ref, sems):
    tm = h_ref.shape[0]
    i = pl.program_id(0)
    slot = i % 2

    def read(slot_, n, sorted_row, ys_row):
        return pltpu.make_async_copy(_rows(ys_ref, ys_row, n),
                                     _rows(buf_ref.at[slot_], sorted_row, n), sems.at[slot_])

    def gather(tile, slot_):
        _for_each_piece(tile, pcnt_ref, psrc_ref, pdst_ref, lambda n, s, d: read(slot_, n, s, d).start())

    @pl.when(i == 0)
    def _():
        buf_ref[...] = jnp.zeros(buf_ref.shape, jnp.int32)
        gather(0, 0)

    @pl.when(i + 1 < pl.num_programs(0))
    def _():
        gather(i + 1, 1 - slot)

    _wait_rows(ntot_ref[i], lambda n: read(slot, n, 0, 0))

    slab = slab_ref[...]
    lane = lax.broadcasted_iota(jnp.int32, (tm, R_SORTED), 1).astype(F32)
    wmat = jnp.zeros((tm, R_SORTED), F32)
    for k in range(TOP_K):
        wmat = jnp.where(lane == slab[:, k:k + 1], slab[:, TOP_K + k:TOP_K + k + 1], wmat)
    moe = _dot(wmat.astype(BF16), _unpack_bf16_pairs(buf_ref[slot]))
    o_ref[...] = _rms(h_ref[...] + moe, g_ref[...])


def _combine(pieces, ys, h, slab, g):
    t = h.shape[0]
    tm = TM_ROUTE
    return pl.pallas_call(
        _combine_kernel,
        grid_spec=pltpu.PrefetchScalarGridSpec(
            num_scalar_prefetch=4,
            grid=(t // tm,),
            in_specs=[pl.BlockSpec(memory_space=pl.ANY),
                      pl.BlockSpec((tm, D_MODEL), lambda i, *_: (i, 0)),
                      pl.BlockSpec((tm, LANES), lambda i, *_: (i, 0)),
                      pl.BlockSpec((1, D_MODEL), lambda i, *_: (0, 0))],
            out_specs=pl.BlockSpec((tm, D_MODEL), lambda i, *_: (i, 0)),
            scratch_shapes=[pltpu.VMEM((2, R_SORTED, D_PACKED), jnp.int32),
                            pltpu.SemaphoreType.DMA((2,))]),
        out_shape=jax.ShapeDtypeStruct((t, D_MODEL), F32),
        compiler_params=_params(1),
        name="combine",
    )(*pieces, ys, h, slab, g)


def _pad_lanes(v, fill=0.0):
    return jnp.pad(v.reshape(1, -1), ((0, 0), (0, LANES - v.shape[-1])), constant_values=fill)


def kernel(x, norm_mix_g, w_in, conv_w, conv_b, dt_bias, a_log, d_skip, ssd_norm_g, attn_sinks,
           rel_bias, attn_norm_g, w_out, norm_ffn_g, w_router, b_router, w1, b1, w2, b2,
           norm_final_g):
    bsz, seq, d = x.shape
    t = bsz * seq
    nc = seq // CHUNK
    depth = w_in.shape[0]
    bias = _bias_table(rel_bias)

    assert depth == 1, "the final norm is fused into the combine kernel: single layer only"
    h = x.reshape(t, d)
    for layer in range(depth):
        y_mix = _mixer(h.reshape(bsz, seq, d), norm_mix_g[layer].reshape(1, d), jnp.swapaxes(w_in, 1, 2), layer,
                       conv_w[layer], conv_b[layer].reshape(1, -1),
                       _pad_lanes(dt_bias[layer]), _pad_lanes(a_log[layer]),
                       jnp.repeat(d_skip[layer], SSD_HEAD_DIM).reshape(1, -1),
                       ssd_norm_g[layer].reshape(1, -1), attn_sinks[layer], bias,
                       attn_norm_g[layer].reshape(1, -1)).reshape(t, -1)

        wr = w_router[layer].T.astype(BF16)
        br = jnp.broadcast_to(b_router[layer][:, None], (N_EXPERTS, LANES))
        h_mid, u, slab, post, tab = _route(
            h, y_mix, w_out, layer, norm_ffn_g[layer].reshape(1, d), wr, br)

        bm = BM_EXPERT
        ntiles = t // TM_ROUTE
        nblk = (t * TOP_K + ntiles * N_EXPERTS * (SUBLANES - 1)) // bm + N_EXPERTS
        tab = tab.reshape(ntiles, N_EXPERTS, LANES)
        cnt8, off8, segst = tab[:, :, 0], tab[:, :, 1], tab[:, :, 2]
        total8 = off8[-1] + cnt8[-1]
        padded = ((total8 + bm - 1) // bm) * bm
        pend = jnp.cumsum(padded)
        segdst = (pend - padded)[None, :] + off8
        n_used = pend[-1:] // bm
        eids = jnp.arange(N_EXPERTS, dtype=jnp.int32)
        blk = jnp.arange(nblk, dtype=jnp.int32)
        blk_expert = jnp.minimum(
            jnp.sum((blk[:, None] * bm >= pend[None, :]).astype(jnp.int32), axis=1), N_EXPERTS - 1)
        e_end = jnp.sum(jnp.where(blk_expert[:, None] == eids[None, :],
                                  ((pend - padded) + total8)[None, :], 0), axis=1)
        valid = jnp.where(blk < n_used[0], jnp.clip(e_end - blk * bm, 0, bm), 0)
        blk_rows = ((valid + EXPERT_ROW_STEP - 1) // EXPERT_ROW_STEP) * EXPERT_ROW_STEP
        piece_lo = jnp.arange(bm // EXPERT_ROW_STEP, dtype=jnp.int32)[None, :] * EXPERT_ROW_STEP
        zero_flag = jnp.where(piece_lo >= valid[:, None], 2,
                              (piece_lo + EXPERT_ROW_STEP > valid[:, None]).astype(jnp.int32)).reshape(-1)
        before = jnp.concatenate([blk_expert[:1], blk_expert[:-1]])
        first_flag = ((blk == 0) | (before != blk_expert)).astype(jnp.int32)
        cand = jnp.where((eids[None, :] > eids[:, None]) & (padded[None, :] > 0), eids[None, :], N_EXPERTS)
        next_nonempty = jnp.min(cand, axis=1)
        next_nonempty = jnp.where(next_nonempty == N_EXPERTS, -1, next_nonempty)
        next_expert = jnp.sum(jnp.where(blk_expert[:, None] == eids[None, :], next_nonempty[None, :], 0), axis=1)
        pieces = _piece_tables(cnt8, segst, segdst)

        xs = _dispatch(pieces, zero_flag, u, post)
        ys = _experts(blk_expert, n_used, first_flag, next_expert, blk_rows, xs, w1[layer],
                      b1[layer], w2[layer], b2[layer])
        h = _combine(pieces, ys, h_mid, slab, norm_final_g.reshape(1, d))
    return h.reshape(bsz, seq, d)
```

```python
import math

import numpy as np
import jax
import jax.numpy as jnp
from jax import lax
from jax.experimental import pallas as pl
from jax.experimental.pallas import tpu as pltpu

F32 = jnp.float32
BF16 = jnp.bfloat16

D_MODEL = 1024
SSD_HEADS = 8
SSD_HEAD_DIM = 64
D_SSD = SSD_HEADS * SSD_HEAD_DIM
SSD_GROUPS = 2
SSD_HEADS_PER_GROUP = SSD_HEADS // SSD_GROUPS
D_STATE = 128
CONV_WIDTH = 4
CHUNK = 128
D_CONV = D_SSD + 2 * SSD_GROUPS * D_STATE
ATTN_Q_HEADS = 8
ATTN_KV_HEADS = 2
ATTN_Q_PER_KV = ATTN_Q_HEADS // ATTN_KV_HEADS
ATTN_HEAD_DIM = 64
D_ATTN = ATTN_Q_HEADS * ATTN_HEAD_DIM
D_KV = ATTN_KV_HEADS * ATTN_HEAD_DIM
WINDOW = 128
REL_BUCKETS = 32
REL_MAX_DIST = 128
N_EXPERTS = 32
TOP_K = 4
D_EXPERT = D_MODEL
SWIGLU_LIMIT = 7.0
SWIGLU_ALPHA = 1.702
RMS_EPS = 1e-5

LANES = 128
SUBLANES = 8
V7X_VMEM_BYTES = 64 * 2 ** 20
VMEM_LIMIT = V7X_VMEM_BYTES * 7 // 8
NEG = -1e30
assert WINDOW == CHUNK, "the attention kernel sees exactly the current and the previous block"

TM_ROUTE = 512
BM_EXPERT = 1024
EXPERT_ROW_STEP = 128
SEQS_PER_STEP = 2
CHUNKS_PER_STEP = 4
R_SORTED = TM_ROUTE * TOP_K + N_EXPERTS * SUBLANES

D_PROJ_PACKED = D_SSD + D_CONV + D_ATTN + 2 * D_KV + LANES


def _params(n_axes):
    return pltpu.CompilerParams(dimension_semantics=("arbitrary",) * n_axes,
                                vmem_limit_bytes=VMEM_LIMIT)


def _rms(x, g):
    return x * lax.rsqrt(jnp.mean(x * x, axis=-1, keepdims=True) + RMS_EPS) * g


def _silu(x):
    return x / (1.0 + jnp.exp(-x))


def _dot(a, b):
    return jnp.dot(a, b, preferred_element_type=F32)


def _dot_f32_by_mask(mask, x, mask_on_left=False):
    hi = x.astype(BF16)
    r1 = x - hi.astype(F32)
    mid = r1.astype(BF16)
    lo = (r1 - mid.astype(F32)).astype(BF16)
    out = None
    for piece in (hi, mid, lo):
        term = _dot(mask, piece) if mask_on_left else _dot(piece, mask)
        out = term if out is None else out + term
    return out


D_PACKED = D_MODEL // 2
BF16_BITS = 16


def _pack_bf16_pairs(x):
    w = x.shape[1] // 2
    hi = lax.bitcast_convert_type(x[:, :w], jnp.int32)
    lo = lax.shift_right_logical(lax.bitcast_convert_type(x[:, w:], jnp.int32), BF16_BITS)
    return hi | lo


def _unpack_bf16_pairs(v):
    upper_half_word = jnp.int32(-(1 << BF16_BITS))
    hi = lax.bitcast_convert_type(v & upper_half_word, F32).astype(BF16)
    lo = lax.bitcast_convert_type(lax.shift_left(v, BF16_BITS), F32).astype(BF16)
    return jnp.concatenate([hi, lo], axis=1)


def _t5_bucket_table():
    dist = CHUNK + np.arange(CHUNK)[:, None] - np.arange(2 * CHUNK)[None, :]
    in_window = (dist >= 0) & (dist < WINDOW)
    d = np.clip(dist, 0, REL_MAX_DIST)
    max_exact = REL_BUCKETS // 2
    large = max_exact + (np.log(np.maximum(d, 1).astype(np.float32) / max_exact)
                         / math.log(REL_MAX_DIST / max_exact)
                         * (REL_BUCKETS - max_exact)).astype(np.int32)
    large = np.minimum(large, REL_BUCKETS - 1)
    bucket = np.where(d < max_exact, d, large)
    return np.where(in_window, bucket, -1).astype(np.int32)


def _bias_kernel(rb_ref, bucket_ref, o_ref):
    bucket = bucket_ref[...]
    col = lax.broadcasted_iota(jnp.int32, bucket.shape, 1)
    for h in range(ATTN_Q_HEADS):
        acc = jnp.full(bucket.shape, NEG, F32)
        for b in range(REL_BUCKETS):
            acc = jnp.where(bucket == b, rb_ref[b, h], acc)
        o_ref[1, h] = acc
        o_ref[0, h] = jnp.where(col >= CHUNK, acc, NEG)


def _bias_table(rel_bias):
    bucket = jnp.asarray(_t5_bucket_table())
    return pl.pallas_call(
        _bias_kernel,
        out_shape=jax.ShapeDtypeStruct((2, ATTN_Q_HEADS, CHUNK, 2 * CHUNK), F32),
        in_specs=[pl.BlockSpec(memory_space=pltpu.SMEM),
                  pl.BlockSpec(memory_space=pltpu.VMEM)],
        out_specs=pl.BlockSpec(memory_space=pltpu.VMEM),
        name="bias_table",
    )(rel_bias, bucket)


def _mixer_kernel(sink_ref, x_ref, gm_ref, w_ref, cw_ref, cb_ref, dtb_ref, alog_ref, dskip_ref, gs_ref,
                  expand_ref, bias_ref, ga_ref, o_ref, wb_ref, state_ref, xpad_ref, kvp_ref):
    j = pl.program_id(1)

    @pl.when((pl.program_id(0) == 0) & (j == 0))
    def _():
        o2 = D_SSD + D_CONV
        o3 = o2 + SSD_HEADS
        n_qkv = D_ATTN + 2 * D_KV
        for lo, src_lo, n in ((0, 0, o2), (o2, o3, n_qkv)):
            for r in range(0, n, LANES):
                wb_ref[:, lo + r:lo + r + LANES] = w_ref[0, src_lo + r:src_lo + r + LANES, :].T.astype(BF16)
        wb_ref[:, o2 + n_qkv:] = jnp.concatenate(
            [w_ref[0, o2:o3, :], jnp.zeros((LANES - SSD_HEADS, D_MODEL), F32)], axis=0).T.astype(BF16)

    @pl.when(j == 0)
    def _():
        state_ref[...] = jnp.zeros(state_ref.shape, F32)
        xpad_ref[...] = jnp.zeros(xpad_ref.shape, F32)
        kvp_ref[...] = jnp.zeros(kvp_ref.shape, F32)

    ns, rows, _ = x_ref.shape
    bounds = np.cumsum([0, D_SSD, D_CONV, D_ATTN, 2 * D_KV, LANES])
    pieces = [(c, s) for c in range(rows // CHUNK) for s in range(ns)]

    def project(c, s):
        x = x_ref[s, c * CHUNK:(c + 1) * CHUNK, :]
        return _dot(_rms(x, gm_ref[...]).astype(BF16), wb_ref[...])

    proj = project(*pieces[0])
    for k, (c, s) in enumerate(pieces):
        proj_next = project(*pieces[k + 1]) if k + 1 < len(pieces) else None
        z, xbc, q, kv, dt = [proj[:, a:b] for a, b in zip(bounds[:-1], bounds[1:])]
        has_prev = jnp.minimum(j, 1) if c == 0 else 1
        y_ssd = _ssd_chunk(z, xbc, dt, cw_ref, cb_ref, dtb_ref, alog_ref, dskip_ref, gs_ref,
                           expand_ref, state_ref.at[s], xpad_ref.at[s])
        y_attn = _swa_block(sink_ref, q, kv, kvp_ref[s], has_prev, bias_ref, ga_ref)
        kvp_ref[s] = kv
        o_ref[s, c * CHUNK:(c + 1) * CHUNK, :] = jnp.concatenate([y_ssd, y_attn], axis=1).astype(BF16)
        proj = proj_next


def _ssd_chunk(z, x_cur, dt_raw, cw_ref, cb_ref, dtb_ref, alog_ref, dskip_ref, g_ref,
               expand_ref, state_ref, xpad_ref):
    L = CHUNK
    G, R, P, N = SSD_GROUPS, SSD_HEADS_PER_GROUP, SSD_HEAD_DIM, D_STATE
    GW = R * P

    xpad = jnp.concatenate([xpad_ref[...], x_cur], axis=0)
    xpad_ref[...] = x_cur[L - SUBLANES:, :]
    acc = cb_ref[...] + cw_ref[CONV_WIDTH - 1:CONV_WIDTH, :] * x_cur
    for d in range(1, CONV_WIDTH):
        k = CONV_WIDTH - 1 - d
        acc = acc + cw_ref[k:k + 1, :] * pltpu.roll(xpad, d, axis=0)[SUBLANES:, :]
    xbc = _silu(acc)
    xs = xbc[:, :D_SSD]
    bm = xbc[:, D_SSD:D_SSD + G * N]
    cm = xbc[:, D_SSD + G * N:]

    dtr = dt_raw + dtb_ref[...]
    dt = jnp.maximum(dtr, 0.0) + jnp.log(1.0 + jnp.exp(-jnp.abs(dtr)))
    a_dt = dt * (-jnp.exp(alog_ref[...]))
    ri = lax.broadcasted_iota(jnp.int32, (L, L), 0)
    ci = lax.broadcasted_iota(jnp.int32, (L, L), 1)
    causal = ci <= ri
    a_cum = _dot_f32_by_mask(causal.astype(BF16), a_dt, mask_on_left=True)
    a_cum_t = _dot_f32_by_mask((ri <= ci).astype(BF16), a_dt.T)
    a_last = a_cum[L - 1:L, :]
    stack = jnp.concatenate(
        [dt, jnp.exp(a_cum), jnp.exp(a_last - a_cum),
         jnp.broadcast_to(jnp.exp(a_last), (SUBLANES, LANES))], axis=0)
    ex = _dot_f32_by_mask(expand_ref[...], stack)
    dt_x, ea_x, dte_x, cd_x = ex[0:L], ex[L:2 * L], ex[2 * L:3 * L], ex[3 * L:3 * L + 1]
    xdt = xs * dt_x

    ys = []
    for g in range(G):
        bm_g = bm[:, g * N:(g + 1) * N]
        cm_g = cm[:, g * N:(g + 1) * N].astype(BF16)
        cb = lax.dot_general(cm_g, bm_g.astype(BF16), (((1,), (1,)), ((), ())),
                             preferred_element_type=F32)
        xdt_g = xdt[:, g * GW:(g + 1) * GW]
        yd = []
        for r in range(R):
            h = g * R + r
            seg = a_cum[:, h:h + 1] - a_cum_t[h:h + 1, :]
            dec = jnp.exp(jnp.where(causal, seg, NEG))
            yd.append(_dot((cb * dec).astype(BF16), xdt_g[:, r * P:(r + 1) * P].astype(BF16)))
        y_diag = jnp.concatenate(yd, axis=1)
        st = state_ref[g]
        y_off = _dot(cm_g, st.astype(BF16)) * ea_x[:, g * GW:(g + 1) * GW]
        new = _dot(bm_g.T.astype(BF16), (xdt_g * dte_x[:, g * GW:(g + 1) * GW]).astype(BF16))
        state_ref[g] = st * cd_x[:, g * GW:(g + 1) * GW] + new
        ys.append(y_diag + y_off + xs[:, g * GW:(g + 1) * GW] * dskip_ref[:, g * GW:(g + 1) * GW])
    y = jnp.concatenate(ys, axis=1)
    return _rms(y * _silu(z), g_ref[...])


def _mixer(x, norm_g, w_in_t, layer, conv_w, conv_b, dt_bias, a_log, d_skip_x, ssd_norm_g, sinks, bias,
           attn_norm_g):
    bsz, seq, _ = x.shape
    rows = CHUNKS_PER_STEP * CHUNK
    nsteps = seq // rows
    expand = np.zeros((LANES, D_SSD), np.float32)
    for h in range(SSD_HEADS):
        expand[h, h * SSD_HEAD_DIM:(h + 1) * SSD_HEAD_DIM] = 1.0
    ns = SEQS_PER_STEP
    row = lambda b, j: (b, j, 0)
    fixed = lambda b, j: (0, 0)
    return pl.pallas_call(
        _mixer_kernel,
        grid=(bsz // ns, nsteps),
        in_specs=[pl.BlockSpec(memory_space=pltpu.SMEM),
                  pl.BlockSpec((ns, rows, D_MODEL), row),
                  pl.BlockSpec((1, D_MODEL), fixed),
                  pl.BlockSpec((1,) + w_in_t.shape[1:], lambda b, j: (layer, 0, 0)),
                  pl.BlockSpec((CONV_WIDTH, D_CONV), fixed),
                  pl.BlockSpec((1, D_CONV), fixed),
                  pl.BlockSpec((1, LANES), fixed),
                  pl.BlockSpec((1, LANES), fixed),
                  pl.BlockSpec((1, D_SSD), fixed),
                  pl.BlockSpec((1, D_SSD), fixed),
                  pl.BlockSpec((LANES, D_SSD), fixed),
                  pl.BlockSpec((2, ATTN_Q_HEADS, CHUNK, 2 * CHUNK), lambda b, j: (0, 0, 0, 0)),
                  pl.BlockSpec((1, D_ATTN), fixed)],
        out_specs=pl.BlockSpec((ns, rows, D_SSD + D_ATTN), row),
        out_shape=jax.ShapeDtypeStruct((bsz, seq, D_SSD + D_ATTN), BF16),
        scratch_shapes=[pltpu.VMEM((D_MODEL, D_PROJ_PACKED), BF16),
                        pltpu.VMEM((ns, SSD_GROUPS, D_STATE, SSD_HEADS_PER_GROUP * SSD_HEAD_DIM), F32),
                        pltpu.VMEM((ns, SUBLANES, D_CONV), F32),
                        pltpu.VMEM((ns, CHUNK, 2 * D_KV), F32)],
        compiler_params=_params(2),
        name="mixer",
    )(sinks, x, norm_g, w_in_t, conv_w, conv_b, dt_bias, a_log, d_skip_x, ssd_norm_g,
      jnp.asarray(expand, dtype=BF16), bias, attn_norm_g)


def _swa_block(sink_ref, q, kv, kvp, has_prev, bias_ref, g_ref):
    L, Dh = CHUNK, ATTN_HEAD_DIM
    q = q * (1.0 / math.sqrt(Dh))
    outs = []
    for hk in range(ATTN_KV_HEADS):
        ks = slice(hk * Dh, (hk + 1) * Dh)
        vs = slice(D_KV + hk * Dh, D_KV + (hk + 1) * Dh)
        kc = jnp.concatenate([kvp[:, ks], kv[:, ks]], axis=0).astype(BF16)
        vc = jnp.concatenate([kvp[:, vs], kv[:, vs]], axis=0).astype(BF16)
        for g in range(ATTN_Q_PER_KV):
            h = hk * ATTN_Q_PER_KV + g
            qh = q[:, h * Dh:(h + 1) * Dh].astype(BF16)
            s = lax.dot_general(qh, kc, (((1,), (1,)), ((), ())), preferred_element_type=F32)
            s = s + bias_ref[has_prev, h]
            sink = sink_ref[h]
            m = jnp.maximum(jnp.max(s, axis=-1, keepdims=True), sink)
            p = jnp.exp(s - m)
            denom = jnp.sum(p, axis=-1, keepdims=True) + jnp.exp(sink - m)
            outs.append(_dot(p.astype(BF16), vc) / denom)
    return _rms(jnp.concatenate(outs, axis=1), g_ref[...])


def _route_kernel(x_ref, y_ref, wo_ref, g_ref, wr_ref, br_ref, ahead_ref,
                  h_ref, u_ref, slab_ref, post_ref, tab_ref, run_ref, wob_ref):
    tm = x_ref.shape[0]

    @pl.when(pl.program_id(0) == 0)
    def _():
        run_ref[...] = jnp.zeros(run_ref.shape, F32)
        wob_ref[...] = wo_ref[0].astype(BF16)

    h = x_ref[...] + _dot(y_ref[...], wob_ref[...])
    h_ref[...] = h
    u = _rms(h, g_ref[...])
    ub = u.astype(BF16)
    u_ref[...] = ub
    E = N_EXPERTS
    logits = lax.dot_general(wr_ref[...], ub, (((1,), (1,)), ((), ())), preferred_element_type=F32)
    logits = logits + jnp.concatenate([br_ref[...]] * (tm // LANES), axis=1)

    eidx = lax.broadcasted_iota(jnp.int32, (E, tm), 0).astype(F32)
    vals, idxs = [], []
    cur = logits
    for _ in range(TOP_K):
        m = jnp.max(cur, axis=0, keepdims=True)
        ix = jnp.min(jnp.where(cur == m, eidx, float(E)), axis=0, keepdims=True)
        vals.append(m)
        idxs.append(ix)
        cur = jnp.where(eidx == ix, NEG, cur)
    es = [jnp.exp(v - vals[0]) for v in vals]
    den = sum(es[1:], es[0])

    onehot = jnp.zeros((E, tm), F32)
    for ix in idxs:
        onehot = onehot + (eidx == ix).astype(F32)
    before = _dot(onehot.astype(BF16), ahead_ref[...])
    cnt = jnp.sum(onehot, axis=1, keepdims=True)
    cnt8 = jnp.floor((cnt + (SUBLANES - 1)) * (1.0 / SUBLANES)) * SUBLANES
    el = lax.broadcasted_iota(jnp.int32, (E, E), 0)
    ec = lax.broadcasted_iota(jnp.int32, (E, E), 1)
    seg_start = _dot_f32_by_mask((ec < el).astype(BF16), jnp.broadcast_to(cnt8, (E, LANES)),
                                 mask_on_left=True)[:, 0:1]
    where_to = before + seg_start

    rows = [jnp.sum(jnp.where(eidx == idxs[k], where_to, 0.0), axis=0, keepdims=True)
            for k in range(TOP_K)]
    rows += [es[k] / den for k in range(TOP_K)]
    stack = jnp.concatenate(rows, axis=0)
    post_ref[...] = stack.astype(jnp.int32)
    slab_ref[...] = jnp.concatenate([stack, jnp.zeros((LANES - 2 * TOP_K, tm), F32)], axis=0).T

    lane = lax.broadcasted_iota(jnp.int32, (E, LANES), 1)
    tab = jnp.where(lane == 0, cnt8, jnp.where(lane == 1, run_ref[...], jnp.where(lane == 2, seg_start, 0.0)))
    tab_ref[...] = tab.astype(jnp.int32)
    run_ref[...] = run_ref[...] + cnt8


def _route(x2, y_mix, w_out, layer, g, wr, br):
    t = x2.shape[0]
    tm = TM_ROUTE
    row = lambda i: (i, 0)
    fixed = lambda i: (0, 0)
    return pl.pallas_call(
        _route_kernel,
        grid=(t // tm,),
        in_specs=[pl.BlockSpec((tm, D_MODEL), row),
                  pl.BlockSpec((tm, D_SSD + D_ATTN), row),
                  pl.BlockSpec((1, D_SSD + D_ATTN, D_MODEL), lambda i: (layer, 0, 0)),
                  pl.BlockSpec((1, D_MODEL), fixed),
                  pl.BlockSpec((N_EXPERTS, D_MODEL), fixed),
                  pl.BlockSpec((N_EXPERTS, LANES), fixed),
                  pl.BlockSpec((tm, tm), fixed)],
        out_specs=[pl.BlockSpec((tm, D_MODEL), row),
                   pl.BlockSpec((tm, D_MODEL), row),
                   pl.BlockSpec((tm, LANES), row),
                   pl.BlockSpec((SUBLANES, tm), lambda i: (0, i)),
                   pl.BlockSpec((N_EXPERTS, LANES), row)],
        out_shape=[jax.ShapeDtypeStruct((t, D_MODEL), F32),
                   jax.ShapeDtypeStruct((t, D_MODEL), BF16),
                   jax.ShapeDtypeStruct((t, LANES), F32),
                   jax.ShapeDtypeStruct((SUBLANES, t), jnp.int32),
                   jax.ShapeDtypeStruct((t // tm * N_EXPERTS, LANES), jnp.int32)],
        scratch_shapes=[pltpu.VMEM((N_EXPERTS, LANES), F32),
                        pltpu.VMEM((D_SSD + D_ATTN, D_MODEL), BF16)],
        compiler_params=_params(1),
        name="route",
    )(x2, y_mix, w_out, g, wr, br, jnp.asarray(np.triu(np.ones((tm, tm), np.float32), 1), dtype=BF16))


SEG_SIZE_BITS = (TM_ROUTE // SUBLANES).bit_length()


TILE_SIZE_BITS = (R_SORTED // SUBLANES).bit_length()


def _piece_tables(cnt8, segst, segdst):
    n = (cnt8 // SUBLANES)[:, None, :]
    b = jnp.arange(SEG_SIZE_BITS, dtype=jnp.int32)[None, :, None]
    has = (n >> b) & 1
    off = ((n >> (b + 1)) << (b + 1)) * SUBLANES
    rank = jnp.cumsum(has, axis=2) - has
    place = (has[..., None] == 1) & (rank[..., None] == jnp.arange(N_EXPERTS, dtype=jnp.int32))
    dense = lambda v: jnp.sum(jnp.where(place, v[..., None], 0), axis=2).reshape(-1)
    return (jnp.sum(has, axis=2).reshape(-1), dense(segst[:, None, :] + off),
            dense(segdst[:, None, :] + off), jnp.sum(cnt8, axis=1) // SUBLANES)


def _for_each_piece(tile, pcnt_ref, psrc_ref, pdst_ref, fn):
    for b in range(SEG_SIZE_BITS):
        base = (tile * SEG_SIZE_BITS + b) * N_EXPERTS

        def body(p, carry, b=b, base=base):
            fn(SUBLANES << b, psrc_ref[base + p], pdst_ref[base + p])
            return carry

        lax.fori_loop(0, pcnt_ref[tile * SEG_SIZE_BITS + b], body, 0)


def _wait_rows(n_tiles8, descriptor):
    for b in range(TILE_SIZE_BITS):
        @pl.when(((n_tiles8 >> b) & 1) == 1)
        def _(b=b):
            descriptor(SUBLANES << b).wait()


def _rows(ref, row, n):
    if not isinstance(row, int):
        row = pl.multiple_of(row, SUBLANES)
    return ref.at[pl.ds(row, n), :]


def _dispatch_kernel(pcnt_ref, psrc_ref, pdst_ref, ntot_ref, zflag_ref, u_ref, post_ref, xs_ref,
                     buf_ref, zero_ref, sems, zsem):
    tm = u_ref.shape[0]
    i = pl.program_id(0)
    slot = i % 2
    bm = zero_ref.shape[0]

    def fill(b, flag):
        return pltpu.make_async_copy(zero_ref, xs_ref.at[pl.ds(b * bm, bm), :], zsem.at[flag - 1])

    n_groups = zflag_ref.shape[0]
    n_steps = pl.num_programs(0)

    def for_flagged(flag, action, first=0, stride=1):
        def body(k, carry):
            b = first + k * stride

            @pl.when(zflag_ref[b] == flag)
            def _():
                action(fill(b, flag))
            return carry

        lax.fori_loop(0, (n_groups - first + stride - 1) // stride, body, 0)

    @pl.when(i == 0)
    def _():
        zero_ref[...] = jnp.zeros(zero_ref.shape, jnp.int32)
        for_flagged(1, lambda c: c.start())
        for_flagged(1, lambda c: c.wait())

    for_flagged(2, lambda c: c.start(), first=i, stride=n_steps)

    @pl.when(i == n_steps - 1)
    def _():
        for_flagged(2, lambda c: c.wait())

    pos = post_ref[...]
    j = lax.broadcasted_iota(jnp.int32, (R_SORTED, tm), 0)
    sel = jnp.zeros((R_SORTED, tm), F32)
    for k in range(TOP_K):
        sel = jnp.where(j == pos[k:k + 1, :], 1.0, sel)
    buf_ref[slot] = _pack_bf16_pairs(_dot(sel.astype(BF16), u_ref[...]))

    def write(slot_, n, src_row, dst_row):
        return pltpu.make_async_copy(_rows(buf_ref.at[slot_], src_row, n),
                                     _rows(xs_ref, dst_row, n), sems.at[slot_])

    _for_each_piece(i, pcnt_ref, psrc_ref, pdst_ref, lambda n, s, d: write(slot, n, s, d).start())

    @pl.when(i > 0)
    def _():
        _wait_rows(ntot_ref[i - 1], lambda n: write(1 - slot, n, 0, 0))

    @pl.when(i == pl.num_programs(0) - 1)
    def _():
        _wait_rows(ntot_ref[i], lambda n: write(slot, n, 0, 0))


def _dispatch(pieces, zero_flag, u, post):
    t = u.shape[0]
    tm = TM_ROUTE
    n_rows = zero_flag.shape[0] * EXPERT_ROW_STEP
    return pl.pallas_call(
        _dispatch_kernel,
        grid_spec=pltpu.PrefetchScalarGridSpec(
            num_scalar_prefetch=5,
            grid=(t // tm,),
            in_specs=[pl.BlockSpec((tm, D_MODEL), lambda i, *_: (i, 0)),
                      pl.BlockSpec((SUBLANES, tm), lambda i, *_: (0, i))],
            out_specs=pl.BlockSpec(memory_space=pl.ANY),
            scratch_shapes=[pltpu.VMEM((2, R_SORTED, D_PACKED), jnp.int32),
                            pltpu.VMEM((EXPERT_ROW_STEP, D_PACKED), jnp.int32),
                            pltpu.SemaphoreType.DMA((2,)),
                            pltpu.SemaphoreType.DMA((2,))]),
        out_shape=jax.ShapeDtypeStruct((n_rows, D_PACKED), jnp.int32),
        compiler_params=_params(1),
        name="dispatch",
    )(*pieces, zero_flag, u, post)


def _expert_kernel(be_ref, nu_ref, first_ref, next_ref, rows_ref, xs_ref, w1_ref, b1_ref, w2_ref, b2_ref,
                   ys_ref, w1f_ref, w2f_ref, w1b_ref, w2b_ref, sems):
    i = pl.program_id(0)

    def fetch(e):
        return (pltpu.make_async_copy(w1_ref.at[e], w1f_ref, sems.at[0]),
                pltpu.make_async_copy(w2_ref.at[e], w2f_ref, sems.at[1]))

    @pl.when(i >= nu_ref[0])
    def _():
        ys_ref[...] = jnp.zeros(ys_ref.shape, jnp.int32)

    @pl.when(i < nu_ref[0])
    def _():
        e = be_ref[i]

        @pl.when(i == 0)
        def _():
            for c in fetch(e):
                c.start()

        @pl.when(first_ref[i] != 0)
        def _():
            for c in fetch(e):
                c.wait()
            w1b_ref[...] = w1f_ref[...].astype(BF16)
            w2b_ref[...] = w2f_ref[...].astype(BF16)

            @pl.when(next_ref[i] >= 0)
            def _():
                for c in fetch(next_ref[i]):
                    c.start()

        def ffn(off, m):
            rows = pl.ds(off, m)
            hid = _dot(_unpack_bf16_pairs(xs_ref[rows, :]), w1b_ref[...]) + b1_ref[pl.ds(e, 1), :]
            x_glu = jnp.minimum(hid[:, :D_EXPERT], SWIGLU_LIMIT)
            x_lin = jnp.clip(hid[:, D_EXPERT:], -SWIGLU_LIMIT, SWIGLU_LIMIT)
            act = x_glu / (1.0 + jnp.exp(-SWIGLU_ALPHA * x_glu)) * (x_lin + 1.0)
            y = _dot(act.astype(BF16), w2b_ref[...]) + b2_ref[pl.ds(e, 1), :]
            ys_ref[rows, :] = _pack_bf16_pairs(y.astype(BF16).astype(F32))

        bm = xs_ref.shape[0]
        half = bm // 2
        occupied = rows_ref[i]

        @pl.when(occupied == bm)
        def _():
            ffn(0, bm)

        @pl.when(occupied < bm)
        def _():
            has_half = occupied > half

            @pl.when(has_half)
            def _():
                ffn(0, half)

            off = pl.multiple_of(jnp.where(has_half, half, 0), half)
            for m in range(EXPERT_ROW_STEP, half + 1, EXPERT_ROW_STEP):
                @pl.when(occupied - off == m)
                def _(m=m):
                    ffn(off, m)

            for g in range(bm // EXPERT_ROW_STEP):
                @pl.when(g * EXPERT_ROW_STEP >= occupied)
                def _(g=g):
                    ys_ref[g * EXPERT_ROW_STEP:(g + 1) * EXPERT_ROW_STEP, :] = jnp.zeros(
                        (EXPERT_ROW_STEP, D_PACKED), jnp.int32)


def _experts(blk_expert, n_used, first_flag, next_expert, blk_rows, xs, w1, b1, w2, b2):
    n_rows = xs.shape[0]
    bm = BM_EXPERT
    nblk = n_rows // bm
    last = lambda i, nu: jnp.maximum(jnp.minimum(i, nu[0] - 1), 0)
    row = lambda i, be, nu, *_: (last(i, nu), 0)
    whole = lambda i, *_: (0, 0)
    return pl.pallas_call(
        _expert_kernel,
        grid_spec=pltpu.PrefetchScalarGridSpec(
            num_scalar_prefetch=5,
            grid=(nblk,),
            in_specs=[pl.BlockSpec((bm, D_PACKED), row),
                      pl.BlockSpec(memory_space=pl.ANY),
                      pl.BlockSpec((N_EXPERTS, 2 * D_EXPERT), whole),
                      pl.BlockSpec(memory_space=pl.ANY),
                      pl.BlockSpec((N_EXPERTS, D_MODEL), whole)],
            out_specs=pl.BlockSpec((bm, D_PACKED), lambda i, *_: (i, 0)),
            scratch_shapes=[pltpu.VMEM((D_MODEL, 2 * D_EXPERT), F32),
                            pltpu.VMEM((D_EXPERT, D_MODEL), F32),
                            pltpu.VMEM((D_MODEL, 2 * D_EXPERT), BF16),
                            pltpu.VMEM((D_EXPERT, D_MODEL), BF16),
                            pltpu.SemaphoreType.DMA((2,))]),
        out_shape=jax.ShapeDtypeStruct((n_rows, D_PACKED), jnp.int32),
        compiler_params=_params(1),
        name="experts",
    )(blk_expert, n_used, first_flag, next_expert, blk_rows, xs, w1, b1, w2, b2)


def _combine_kernel(pcnt_ref, psrc_ref, pdst_ref, ntot_ref, ys_ref, h_ref, slab_ref, g_ref, o_ref,
                    buf_ref, sems):
    tm = h_ref.shape[0]
    i = pl.program_id(0)
    slot = i % 2

    def read(slot_, n, sorted_row, ys_row):
        return pltpu.make_async_copy(_rows(ys_ref, ys_row, n),
                                     _rows(buf_ref.at[slot_], sorted_row, n), sems.at[slot_])

    def gather(tile, slot_):
        _for_each_piece(tile, pcnt_ref, psrc_ref, pdst_ref, lambda n, s, d: read(slot_, n, s, d).start())

    @pl.when(i == 0)
    def _():
        buf_ref[...] = jnp.zeros(buf_ref.shape, jnp.int32)
        gather(0, 0)

    @pl.when(i + 1 < pl.num_programs(0))
    def _():
        gather(i + 1, 1 - slot)

    _wait_rows(ntot_ref[i], lambda n: read(slot, n, 0, 0))

    slab = slab_ref[...]
    lane = lax.broadcasted_iota(jnp.int32, (tm, R_SORTED), 1).astype(F32)
    wmat = jnp.zeros((tm, R_SORTED), F32)
    for k in range(TOP_K):
        wmat = jnp.where(lane == slab[:, k:k + 1], slab[:, TOP_K + k:TOP_K + k + 1], wmat)
    moe = _dot(wmat.astype(BF16), _unpack_bf16_pairs(buf_ref[slot]))
    o_ref[...] = _rms(h_ref[...] + moe, g_ref[...])


def _combine(pieces, ys, h, slab, g):
    t = h.shape[0]
    tm = TM_ROUTE
    return pl.pallas_call(
        _combine_kernel,
        grid_spec=pltpu.PrefetchScalarGridSpec(
            num_scalar_prefetch=4,
            grid=(t // tm,),
            in_specs=[pl.BlockSpec(memory_space=pl.ANY),
                      pl.BlockSpec((tm, D_MODEL), lambda i, *_: (i, 0)),
                      pl.BlockSpec((tm, LANES), lambda i, *_: (i, 0)),
                      pl.BlockSpec((1, D_MODEL), lambda i, *_: (0, 0))],
            out_specs=pl.BlockSpec((tm, D_MODEL), lambda i, *_: (i, 0)),
            scratch_shapes=[pltpu.VMEM((2, R_SORTED, D_PACKED), jnp.int32),
                            pltpu.SemaphoreType.DMA((2,))]),
        out_shape=jax.ShapeDtypeStruct((t, D_MODEL), F32),
        compiler_params=_params(1),
        name="combine",
    )(*pieces, ys, h, slab, g)


def _pad_lanes(v, fill=0.0):
    return jnp.pad(v.reshape(1, -1), ((0, 0), (0, LANES - v.shape[-1])), constant_values=fill)


def kernel(x, norm_mix_g, w_in, conv_w, conv_b, dt_bias, a_log, d_skip, ssd_norm_g, attn_sinks,
           rel_bias, attn_norm_g, w_out, norm_ffn_g, w_router, b_router, w1, b1, w2, b2,
           norm_final_g):
    bsz, seq, d = x.shape
    t = bsz * seq
    nc = seq // CHUNK
    depth = w_in.shape[0]
    bias = _bias_table(rel_bias)

    assert depth == 1, "the final norm is fused into the combine kernel: single layer only"
    h = x.reshape(t, d)
    for layer in range(depth):
        y_mix = _mixer(h.reshape(bsz, seq, d), norm_mix_g[layer].reshape(1, d), jnp.swapaxes(w_in, 1, 2), layer,
                       conv_w[layer], conv_b[layer].reshape(1, -1),
                       _pad_lanes(dt_bias[layer]), _pad_lanes(a_log[layer]),
                       jnp.repeat(d_skip[layer], SSD_HEAD_DIM).reshape(1, -1),
                       ssd_norm_g[layer].reshape(1, -1), attn_sinks[layer], bias,
                       attn_norm_g[layer].reshape(1, -1)).reshape(t, -1)

        wr = w_router[layer].T.astype(BF16)
        br = jnp.broadcast_to(b_router[layer][:, None], (N_EXPERTS, LANES))
        h_mid, u, slab, post, tab = _route(
            h, y_mix, w_out, layer, norm_ffn_g[layer].reshape(1, d), wr, br)

        bm = BM_EXPERT
        ntiles = t // TM_ROUTE
        nblk = (t * TOP_K + ntiles * N_EXPERTS * (SUBLANES - 1)) // bm + N_EXPERTS
        tab = tab.reshape(ntiles, N_EXPERTS, LANES)
        cnt8, off8, segst = tab[:, :, 0], tab[:, :, 1], tab[:, :, 2]
        total8 = off8[-1] + cnt8[-1]
        padded = ((total8 + bm - 1) // bm) * bm
        pend = jnp.cumsum(padded)
        segdst = (pend - padded)[None, :] + off8
        n_used = pend[-1:] // bm
        eids = jnp.arange(N_EXPERTS, dtype=jnp.int32)
        blk = jnp.arange(nblk, dtype=jnp.int32)
        blk_expert = jnp.minimum(
            jnp.sum((blk[:, None] * bm >= pend[None, :]).astype(jnp.int32), axis=1), N_EXPERTS - 1)
        e_end = jnp.sum(jnp.where(blk_expert[:, None] == eids[None, :],
                                  ((pend - padded) + total8)[None, :], 0), axis=1)
        valid = jnp.where(blk < n_used[0], jnp.clip(e_end - blk * bm, 0, bm), 0)
        blk_rows = ((valid + EXPERT_ROW_STEP - 1) // EXPERT_ROW_STEP) * EXPERT_ROW_STEP
        piece_lo = jnp.arange(bm // EXPERT_ROW_STEP, dtype=jnp.int32)[None, :] * EXPERT_ROW_STEP
        zero_flag = jnp.where(piece_lo >= valid[:, None], 2,
                              (piece_lo + EXPERT_ROW_STEP > valid[:, None]).astype(jnp.int32)).reshape(-1)
        before = jnp.concatenate([blk_expert[:1], blk_expert[:-1]])
        first_flag = ((blk == 0) | (before != blk_expert)).astype(jnp.int32)
        cand = jnp.where((eids[None, :] > eids[:, None]) & (padded[None, :] > 0), eids[None, :], N_EXPERTS)
        next_nonempty = jnp.min(cand, axis=1)
        next_nonempty = jnp.where(next_nonempty == N_EXPERTS, -1, next_nonempty)
        next_expert = jnp.sum(jnp.where(blk_expert[:, None] == eids[None, :], next_nonempty[None, :], 0), axis=1)
        pieces = _piece_tables(cnt8, segst, segdst)

        xs = _dispatch(pieces, zero_flag, u, post)
        ys = _experts(blk_expert, n_used, first_flag, next_expert, blk_rows, xs, w1[layer],
                      b1[layer], w2[layer], b2[layer])
        h = _combine(pieces, ys, h_mid, slab, norm_final_g.reshape(1, d))
    return h.reshape(bsz, seq, d)
```

```python
import math

import numpy as np
import jax
import jax.numpy as jnp
from jax import lax
from jax.experimental import pallas as pl
from jax.experimental.pallas import tpu as pltpu

F32 = jnp.float32
BF16 = jnp.bfloat16

D_MODEL = 1024
SSD_HEADS = 8
SSD_HEAD_DIM = 64
D_SSD = SSD_HEADS * SSD_HEAD_DIM
SSD_GROUPS = 2
SSD_HEADS_PER_GROUP = SSD_HEADS // SSD_GROUPS
D_STATE = 128
CONV_WIDTH = 4
CHUNK = 128
D_CONV = D_SSD + 2 * SSD_GROUPS * D_STATE
ATTN_Q_HEADS = 8
ATTN_KV_HEADS = 2
ATTN_Q_PER_KV = ATTN_Q_HEADS // ATTN_KV_HEADS
ATTN_HEAD_DIM = 64
D_ATTN = ATTN_Q_HEADS * ATTN_HEAD_DIM
D_KV = ATTN_KV_HEADS * ATTN_HEAD_DIM
WINDOW = 128
REL_BUCKETS = 32
REL_MAX_DIST = 128
N_EXPERTS = 32
TOP_K = 4
D_EXPERT = D_MODEL
SWIGLU_LIMIT = 7.0
SWIGLU_ALPHA = 1.702
RMS_EPS = 1e-5

LANES = 128
SUBLANES = 8
V7X_VMEM_BYTES = 64 * 2 ** 20
VMEM_LIMIT = V7X_VMEM_BYTES * 7 // 8
NEG = -1e30
assert WINDOW == CHUNK, "the attention kernel sees exactly the current and the previous block"

TM_ROUTE = 512
BM_EXPERT = 1024
EXPERT_ROW_STEP = 128
SEQS_PER_STEP = 2
CHUNKS_PER_STEP = 4
R_SORTED = TM_ROUTE * TOP_K + N_EXPERTS * SUBLANES

D_PROJ_PACKED = D_SSD + D_CONV + D_ATTN + 2 * D_KV + LANES


def _params(n_axes):
    return pltpu.CompilerParams(dimension_semantics=("arbitrary",) * n_axes,
                                vmem_limit_bytes=VMEM_LIMIT)


def _rms(x, g):
    return x * lax.rsqrt(jnp.mean(x * x, axis=-1, keepdims=True) + RMS_EPS) * g


def _silu(x):
    return x / (1.0 + jnp.exp(-x))


def _dot(a, b):
    return jnp.dot(a, b, preferred_element_type=F32)


def _dot_f32_by_mask(mask, x, mask_on_left=False):
    hi = x.astype(BF16)
    r1 = x - hi.astype(F32)
    mid = r1.astype(BF16)
    lo = (r1 - mid.astype(F32)).astype(BF16)
    out = None
    for piece in (hi, mid, lo):
        term = _dot(mask, piece) if mask_on_left else _dot(piece, mask)
        out = term if out is None else out + term
    return out


D_PACKED = D_MODEL // 2
BF16_BITS = 16


def _pack_bf16_pairs(x):
    w = x.shape[1] // 2
    hi = lax.bitcast_convert_type(x[:, :w], jnp.int32)
    lo = lax.shift_right_logical(lax.bitcast_convert_type(x[:, w:], jnp.int32), BF16_BITS)
    return hi | lo


def _unpack_bf16_pairs(v):
    upper_half_word = jnp.int32(-(1 << BF16_BITS))
    hi = lax.bitcast_convert_type(v & upper_half_word, F32).astype(BF16)
    lo = lax.bitcast_convert_type(lax.shift_left(v, BF16_BITS), F32).astype(BF16)
    return jnp.concatenate([hi, lo], axis=1)


def _t5_bucket_table():
    dist = CHUNK + np.arange(CHUNK)[:, None] - np.arange(2 * CHUNK)[None, :]
    in_window = (dist >= 0) & (dist < WINDOW)
    d = np.clip(dist, 0, REL_MAX_DIST)
    max_exact = REL_BUCKETS // 2
    large = max_exact + (np.log(np.maximum(d, 1).astype(np.float32) / max_exact)
                         / math.log(REL_MAX_DIST / max_exact)
                         * (REL_BUCKETS - max_exact)).astype(np.int32)
    large = np.minimum(large, REL_BUCKETS - 1)
    bucket = np.where(d < max_exact, d, large)
    return np.where(in_window, bucket, -1).astype(np.int32)


def _bias_kernel(rb_ref, bucket_ref, o_ref):
    bucket = bucket_ref[...]
    col = lax.broadcasted_iota(jnp.int32, bucket.shape, 1)
    for h in range(ATTN_Q_HEADS):
        acc = jnp.full(bucket.shape, NEG, F32)
        for b in range(REL_BUCKETS):
            acc = jnp.where(bucket == b, rb_ref[b, h], acc)
        o_ref[1, h] = acc
        o_ref[0, h] = jnp.where(col >= CHUNK, acc, NEG)


def _bias_table(rel_bias):
    bucket = jnp.asarray(_t5_bucket_table())
    return pl.pallas_call(
        _bias_kernel,
        out_shape=jax.ShapeDtypeStruct((2, ATTN_Q_HEADS, CHUNK, 2 * CHUNK), F32),
        in_specs=[pl.BlockSpec(memory_space=pltpu.SMEM),
                  pl.BlockSpec(memory_space=pltpu.VMEM)],
        out_specs=pl.BlockSpec(memory_space=pltpu.VMEM),
        name="bias_table",
    )(rel_bias, bucket)


def _mixer_kernel(sink_ref, x_ref, gm_ref, w_ref, cw_ref, cb_ref, dtb_ref, alog_ref, dskip_ref, gs_ref,
                  expand_ref, bias_ref, ga_ref, o_ref, wb_ref, state_ref, xpad_ref, kvp_ref):
    j = pl.program_id(1)

    @pl.when((pl.program_id(0) == 0) & (j == 0))
    def _():
        o2 = D_SSD + D_CONV
        o3 = o2 + SSD_HEADS
        n_qkv = D_ATTN + 2 * D_KV
        for lo, src_lo, n in ((0, 0, o2), (o2, o3, n_qkv)):
            for r in range(0, n, LANES):
                wb_ref[:, lo + r:lo + r + LANES] = w_ref[0, src_lo + r:src_lo + r + LANES, :].T.astype(BF16)
        wb_ref[:, o2 + n_qkv:] = jnp.concatenate(
            [w_ref[0, o2:o3, :], jnp.zeros((LANES - SSD_HEADS, D_MODEL), F32)], axis=0).T.astype(BF16)

    @pl.when(j == 0)
    def _():
        state_ref[...] = jnp.zeros(state_ref.shape, F32)
        xpad_ref[...] = jnp.zeros(xpad_ref.shape, F32)
        kvp_ref[...] = jnp.zeros(kvp_ref.shape, F32)

    ns, rows, _ = x_ref.shape
    bounds = np.cumsum([0, D_SSD, D_CONV, D_ATTN, 2 * D_KV, LANES])
    pieces = [(c, s) for c in range(rows // CHUNK) for s in range(ns)]

    def project(c, s):
        x = x_ref[s, c * CHUNK:(c + 1) * CHUNK, :]
        return _dot(_rms(x, gm_ref[...]).astype(BF16), wb_ref[...])

    proj = project(*pieces[0])
    for k, (c, s) in enumerate(pieces):
        proj_next = project(*pieces[k + 1]) if k + 1 < len(pieces) else None
        z, xbc, q, kv, dt = [proj[:, a:b] for a, b in zip(bounds[:-1], bounds[1:])]
        has_prev = jnp.minimum(j, 1) if c == 0 else 1
        y_ssd = _ssd_chunk(z, xbc, dt, cw_ref, cb_ref, dtb_ref, alog_ref, dskip_ref, gs_ref,
                           expand_ref, state_ref.at[s], xpad_ref.at[s])
        y_attn = _swa_block(sink_ref, q, kv, kvp_ref[s], has_prev, bias_ref, ga_ref)
        kvp_ref[s] = kv
        o_ref[s, c * CHUNK:(c + 1) * CHUNK, :] = jnp.concatenate([y_ssd, y_attn], axis=1).astype(BF16)
        proj = proj_next


def _ssd_chunk(z, x_cur, dt_raw, cw_ref, cb_ref, dtb_ref, alog_ref, dskip_ref, g_ref,
               expand_ref, state_ref, xpad_ref):
    L = CHUNK
    G, R, P, N = SSD_GROUPS, SSD_HEADS_PER_GROUP, SSD_HEAD_DIM, D_STATE
    GW = R * P

    xpad = jnp.concatenate([xpad_ref[...], x_cur], axis=0)
    xpad_ref[...] = x_cur[L - SUBLANES:, :]
    acc = cb_ref[...] + cw_ref[CONV_WIDTH - 1:CONV_WIDTH, :] * x_cur
    for d in range(1, CONV_WIDTH):
        k = CONV_WIDTH - 1 - d
        acc = acc + cw_ref[k:k + 1, :] * pltpu.roll(xpad, d, axis=0)[SUBLANES:, :]
    xbc = _silu(acc)
    xs = xbc[:, :D_SSD]
    bm = xbc[:, D_SSD:D_SSD + G * N]
    cm = xbc[:, D_SSD + G * N:]

    dtr = dt_raw + dtb_ref[...]
    dt = jnp.maximum(dtr, 0.0) + jnp.log(1.0 + jnp.exp(-jnp.abs(dtr)))
    a_dt = dt * (-jnp.exp(alog_ref[...]))
    ri = lax.broadcasted_iota(jnp.int32, (L, L), 0)
    ci = lax.broadcasted_iota(jnp.int32, (L, L), 1)
    causal = ci <= ri
    a_cum = _dot_f32_by_mask(causal.astype(BF16), a_dt, mask_on_left=True)
    a_cum_t = _dot_f32_by_mask((ri <= ci).astype(BF16), a_dt.T)
    a_last = a_cum[L - 1:L, :]
    stack = jnp.concatenate(
        [dt, jnp.exp(a_cum), jnp.exp(a_last - a_cum),
         jnp.broadcast_to(jnp.exp(a_last), (SUBLANES, LANES))], axis=0)
    ex = _dot_f32_by_mask(expand_ref[...], stack)
    dt_x, ea_x, dte_x, cd_x = ex[0:L], ex[L:2 * L], ex[2 * L:3 * L], ex[3 * L:3 * L + 1]
    xdt = xs * dt_x

    ys = []
    for g in range(G):
        bm_g = bm[:, g * N:(g + 1) * N]
        cm_g = cm[:, g * N:(g + 1) * N].astype(BF16)
        cb = lax.dot_general(cm_g, bm_g.astype(BF16), (((1,), (1,)), ((), ())),
                             preferred_element_type=F32)
        xdt_g = xdt[:, g * GW:(g + 1) * GW]
        yd = []
        for r in range(R):
            h = g * R + r
            seg = a_cum[:, h:h + 1] - a_cum_t[h:h + 1, :]
            dec = jnp.exp(jnp.where(causal, seg, NEG))
            yd.append(_dot((cb * dec).astype(BF16), xdt_g[:, r * P:(r + 1) * P].astype(BF16)))
        y_diag = jnp.concatenate(yd, axis=1)
        st = state_ref[g]
        y_off = _dot(cm_g, st.astype(BF16)) * ea_x[:, g * GW:(g + 1) * GW]
        new = _dot(bm_g.T.astype(BF16), (xdt_g * dte_x[:, g * GW:(g + 1) * GW]).astype(BF16))
        state_ref[g] = st * cd_x[:, g * GW:(g + 1) * GW] + new
        ys.append(y_diag + y_off + xs[:, g * GW:(g + 1) * GW] * dskip_ref[:, g * GW:(g + 1) * GW])
    y = jnp.concatenate(ys, axis=1)
    return _rms(y * _silu(z), g_ref[...])


def _mixer(x, norm_g, w_in_t, layer, conv_w, conv_b, dt_bias, a_log, d_skip_x, ssd_norm_g, sinks, bias,
           attn_norm_g):
    bsz, seq, _ = x.shape
    rows = CHUNKS_PER_STEP * CHUNK
    nsteps = seq // rows
    expand = np.zeros((LANES, D_SSD), np.float32)
    for h in range(SSD_HEADS):
        expand[h, h * SSD_HEAD_DIM:(h + 1) * SSD_HEAD_DIM] = 1.0
    ns = SEQS_PER_STEP
    row = lambda b, j: (b, j, 0)
    fixed = lambda b, j: (0, 0)
    return pl.pallas_call(
        _mixer_kernel,
        grid=(bsz // ns, nsteps),
        in_specs=[pl.BlockSpec(memory_space=pltpu.SMEM),
                  pl.BlockSpec((ns, rows, D_MODEL), row),
                  pl.BlockSpec((1, D_MODEL), fixed),
                  pl.BlockSpec((1,) + w_in_t.shape[1:], lambda b, j: (layer, 0, 0)),
                  pl.BlockSpec((CONV_WIDTH, D_CONV), fixed),
                  pl.BlockSpec((1, D_CONV), fixed),
                  pl.BlockSpec((1, LANES), fixed),
                  pl.BlockSpec((1, LANES), fixed),
                  pl.BlockSpec((1, D_SSD), fixed),
                  pl.BlockSpec((1, D_SSD), fixed),
                  pl.BlockSpec((LANES, D_SSD), fixed),
                  pl.BlockSpec((2, ATTN_Q_HEADS, CHUNK, 2 * CHUNK), lambda b, j: (0, 0, 0, 0)),
                  pl.BlockSpec((1, D_ATTN), fixed)],
        out_specs=pl.BlockSpec((ns, rows, D_SSD + D_ATTN), row),
        out_shape=jax.ShapeDtypeStruct((bsz, seq, D_SSD + D_ATTN), BF16),
        scratch_shapes=[pltpu.VMEM((D_MODEL, D_PROJ_PACKED), BF16),
                        pltpu.VMEM((ns, SSD_GROUPS, D_STATE, SSD_HEADS_PER_GROUP * SSD_HEAD_DIM), F32),
                        pltpu.VMEM((ns, SUBLANES, D_CONV), F32),
                        pltpu.VMEM((ns, CHUNK, 2 * D_KV), F32)],
        compiler_params=_params(2),
        name="mixer",
    )(sinks, x, norm_g, w_in_t, conv_w, conv_b, dt_bias, a_log, d_skip_x, ssd_norm_g,
      jnp.asarray(expand, dtype=BF16), bias, attn_norm_g)


def _swa_block(sink_ref, q, kv, kvp, has_prev, bias_ref, g_ref):
    L, Dh = CHUNK, ATTN_HEAD_DIM
    q = q * (1.0 / math.sqrt(Dh))
    outs = []
    for hk in range(ATTN_KV_HEADS):
        ks = slice(hk * Dh, (hk + 1) * Dh)
        vs = slice(D_KV + hk * Dh, D_KV + (hk + 1) * Dh)
        kc = jnp.concatenate([kvp[:, ks], kv[:, ks]], axis=0).astype(BF16)
        vc = jnp.concatenate([kvp[:, vs], kv[:, vs]], axis=0).astype(BF16)
        for g in range(ATTN_Q_PER_KV):
            h = hk * ATTN_Q_PER_KV + g
            qh = q[:, h * Dh:(h + 1) * Dh].astype(BF16)
            s = lax.dot_general(qh, kc, (((1,), (1,)), ((), ())), preferred_element_type=F32)
            s = s + bias_ref[has_prev, h]
            sink = sink_ref[h]
            m = jnp.maximum(jnp.max(s, axis=-1, keepdims=True), sink)
            p = jnp.exp(s - m)
            denom = jnp.sum(p, axis=-1, keepdims=True) + jnp.exp(sink - m)
            outs.append(_dot(p.astype(BF16), vc) / denom)
    return _rms(jnp.concatenate(outs, axis=1), g_ref[...])


def _route_kernel(x_ref, y_ref, wo_ref, g_ref, wr_ref, br_ref, ahead_ref,
                  h_ref, u_ref, slab_ref, post_ref, tab_ref, run_ref, wob_ref):
    tm = x_ref.shape[0]

    @pl.when(pl.program_id(0) == 0)
    def _():
        run_ref[...] = jnp.zeros(run_ref.shape, F32)
        wob_ref[...] = wo_ref[0].astype(BF16)

    h = x_ref[...] + _dot(y_ref[...], wob_ref[...])
    h_ref[...] = h
    u = _rms(h, g_ref[...])
    ub = u.astype(BF16)
    u_ref[...] = ub
    E = N_EXPERTS
    logits = lax.dot_general(wr_ref[...], ub, (((1,), (1,)), ((), ())), preferred_element_type=F32)
    logits = logits + jnp.concatenate([br_ref[...]] * (tm // LANES), axis=1)

    eidx = lax.broadcasted_iota(jnp.int32, (E, tm), 0).astype(F32)
    vals, idxs = [], []
    cur = logits
    for _ in range(TOP_K):
        m = jnp.max(cur, axis=0, keepdims=True)
        ix = jnp.min(jnp.where(cur == m, eidx, float(E)), axis=0, keepdims=True)
        vals.append(m)
        idxs.append(ix)
        cur = jnp.where(eidx == ix, NEG, cur)
    es = [jnp.exp(v - vals[0]) for v in vals]
    den = sum(es[1:], es[0])

    onehot = jnp.zeros((E, tm), F32)
    for ix in idxs:
        onehot = onehot + (eidx == ix).astype(F32)
    before = _dot(onehot.astype(BF16), ahead_ref[...])
    cnt = jnp.sum(onehot, axis=1, keepdims=True)
    cnt8 = jnp.floor((cnt + (SUBLANES - 1)) * (1.0 / SUBLANES)) * SUBLANES
    el = lax.broadcasted_iota(jnp.int32, (E, E), 0)
    ec = lax.broadcasted_iota(jnp.int32, (E, E), 1)
    seg_start = _dot_f32_by_mask((ec < el).astype(BF16), jnp.broadcast_to(cnt8, (E, LANES)),
                                 mask_on_left=True)[:, 0:1]
    where_to = before + seg_start

    rows = [jnp.sum(jnp.where(eidx == idxs[k], where_to, 0.0), axis=0, keepdims=True)
            for k in range(TOP_K)]
    rows += [es[k] / den for k in range(TOP_K)]
    stack = jnp.concatenate(rows, axis=0)
    post_ref[...] = stack.astype(jnp.int32)
    slab_ref[...] = jnp.concatenate([stack, jnp.zeros((LANES - 2 * TOP_K, tm), F32)], axis=0).T

    lane = lax.broadcasted_iota(jnp.int32, (E, LANES), 1)
    tab = jnp.where(lane == 0, cnt8, jnp.where(lane == 1, run_ref[...], jnp.where(lane == 2, seg_start, 0.0)))
    tab_ref[...] = tab.astype(jnp.int32)
    run_ref[...] = run_ref[...] + cnt8


def _route(x2, y_mix, w_out, layer, g, wr, br):
    t = x2.shape[0]
    tm = TM_ROUTE
    row = lambda i: (i, 0)
    fixed = lambda i: (0, 0)
    return pl.pallas_call(
        _route_kernel,
        grid=(t // tm,),
        in_specs=[pl.BlockSpec((tm, D_MODEL), row),
                  pl.BlockSpec((tm, D_SSD + D_ATTN), row),
                  pl.BlockSpec((1, D_SSD + D_ATTN, D_MODEL), lambda i: (layer, 0, 0)),
                  pl.BlockSpec((1, D_MODEL), fixed),
                  pl.BlockSpec((N_EXPERTS, D_MODEL), fixed),
                  pl.BlockSpec((N_EXPERTS, LANES), fixed),
                  pl.BlockSpec((tm, tm), fixed)],
        out_specs=[pl.BlockSpec((tm, D_MODEL), row),
                   pl.BlockSpec((tm, D_MODEL), row),
                   pl.BlockSpec((tm, LANES), row),
                   pl.BlockSpec((SUBLANES, tm), lambda i: (0, i)),
                   pl.BlockSpec((N_EXPERTS, LANES), row)],
        out_shape=[jax.ShapeDtypeStruct((t, D_MODEL), F32),
                   jax.ShapeDtypeStruct((t, D_MODEL), BF16),
                   jax.ShapeDtypeStruct((t, LANES), F32),
                   jax.ShapeDtypeStruct((SUBLANES, t), jnp.int32),
                   jax.ShapeDtypeStruct((t // tm * N_EXPERTS, LANES), jnp.int32)],
        scratch_shapes=[pltpu.VMEM((N_EXPERTS, LANES), F32),
                        pltpu.VMEM((D_SSD + D_ATTN, D_MODEL), BF16)],
        compiler_params=_params(1),
        name="route",
    )(x2, y_mix, w_out, g, wr, br, jnp.asarray(np.triu(np.ones((tm, tm), np.float32), 1), dtype=BF16))


SEG_SIZE_BITS = (TM_ROUTE // SUBLANES).bit_length()


TILE_SIZE_BITS = (R_SORTED // SUBLANES).bit_length()


def _piece_tables(cnt8, segst, segdst):
    n = (cnt8 // SUBLANES)[:, None, :]
    b = jnp.arange(SEG_SIZE_BITS, dtype=jnp.int32)[None, :, None]
    has = (n >> b) & 1
    off = ((n >> (b + 1)) << (b + 1)) * SUBLANES
    rank = jnp.cumsum(has, axis=2) - has
    place = (has[..., None] == 1) & (rank[..., None] == jnp.arange(N_EXPERTS, dtype=jnp.int32))
    dense = lambda v: jnp.sum(jnp.where(place, v[..., None], 0), axis=2).reshape(-1)
    return (jnp.sum(has, axis=2).reshape(-1), dense(segst[:, None, :] + off),
            dense(segdst[:, None, :] + off), jnp.sum(cnt8, axis=1) // SUBLANES)


def _for_each_piece(tile, pcnt_ref, psrc_ref, pdst_ref, fn):
    for b in range(SEG_SIZE_BITS):
        base = (tile * SEG_SIZE_BITS + b) * N_EXPERTS

        def body(p, carry, b=b, base=base):
            fn(SUBLANES << b, psrc_ref[base + p], pdst_ref[base + p])
            return carry

        lax.fori_loop(0, pcnt_ref[tile * SEG_SIZE_BITS + b], body, 0)


def _wait_rows(n_tiles8, descriptor):
    for b in range(TILE_SIZE_BITS):
        @pl.when(((n_tiles8 >> b) & 1) == 1)
        def _(b=b):
            descriptor(SUBLANES << b).wait()


def _rows(ref, row, n):
    if not isinstance(row, int):
        row = pl.multiple_of(row, SUBLANES)
    return ref.at[pl.ds(row, n), :]


def _dispatch_kernel(pcnt_ref, psrc_ref, pdst_ref, ntot_ref, zflag_ref, u_ref, post_ref, xs_ref,
                     buf_ref, zero_ref, sems, zsem):
    tm = u_ref.shape[0]
    i = pl.program_id(0)
    slot = i % 2
    bm = zero_ref.shape[0]

    def fill(b, flag):
        return pltpu.make_async_copy(zero_ref, xs_ref.at[pl.ds(b * bm, bm), :], zsem.at[flag - 1])

    def for_flagged(flag, action):
        def body(b, carry):
            @pl.when(zflag_ref[b] == flag)
            def _():
                action(fill(b, flag))
            return carry

        lax.fori_loop(0, zflag_ref.shape[0], body, 0)

    @pl.when(i == 0)
    def _():
        zero_ref[...] = jnp.zeros(zero_ref.shape, jnp.int32)
        for_flagged(1, lambda c: c.start())
        for_flagged(2, lambda c: c.start())
        for_flagged(1, lambda c: c.wait())

    @pl.when(i == pl.num_programs(0) - 1)
    def _():
        for_flagged(2, lambda c: c.wait())

    pos = post_ref[...]
    j = lax.broadcasted_iota(jnp.int32, (R_SORTED, tm), 0)
    sel = jnp.zeros((R_SORTED, tm), F32)
    for k in range(TOP_K):
        sel = jnp.where(j == pos[k:k + 1, :], 1.0, sel)
    buf_ref[slot] = _pack_bf16_pairs(_dot(sel.astype(BF16), u_ref[...]))

    def write(slot_, n, src_row, dst_row):
        return pltpu.make_async_copy(_rows(buf_ref.at[slot_], src_row, n),
                                     _rows(xs_ref, dst_row, n), sems.at[slot_])

    _for_each_piece(i, pcnt_ref, psrc_ref, pdst_ref, lambda n, s, d: write(slot, n, s, d).start())

    @pl.when(i > 0)
    def _():
        _wait_rows(ntot_ref[i - 1], lambda n: write(1 - slot, n, 0, 0))

    @pl.when(i == pl.num_programs(0) - 1)
    def _():
        _wait_rows(ntot_ref[i], lambda n: write(slot, n, 0, 0))


def _dispatch(pieces, zero_flag, u, post):
    t = u.shape[0]
    tm = TM_ROUTE
    n_rows = zero_flag.shape[0] * EXPERT_ROW_STEP
    return pl.pallas_call(
        _dispatch_kernel,
        grid_spec=pltpu.PrefetchScalarGridSpec(
            num_scalar_prefetch=5,
            grid=(t // tm,),
            in_specs=[pl.BlockSpec((tm, D_MODEL), lambda i, *_: (i, 0)),
                      pl.BlockSpec((SUBLANES, tm), lambda i, *_: (0, i))],
            out_specs=pl.BlockSpec(memory_space=pl.ANY),
            scratch_shapes=[pltpu.VMEM((2, R_SORTED, D_PACKED), jnp.int32),
                            pltpu.VMEM((EXPERT_ROW_STEP, D_PACKED), jnp.int32),
                            pltpu.SemaphoreType.DMA((2,)),
                            pltpu.SemaphoreType.DMA((2,))]),
        out_shape=jax.ShapeDtypeStruct((n_rows, D_PACKED), jnp.int32),
        compiler_params=_params(1),
        name="dispatch",
    )(*pieces, zero_flag, u, post)


def _expert_kernel(be_ref, nu_ref, first_ref, next_ref, rows_ref, xs_ref, w1_ref, b1_ref, w2_ref, b2_ref,
                   ys_ref, w1f_ref, w2f_ref, w1b_ref, w2b_ref, sems):
    i = pl.program_id(0)

    def fetch(e):
        return (pltpu.make_async_copy(w1_ref.at[e], w1f_ref, sems.at[0]),
                pltpu.make_async_copy(w2_ref.at[e], w2f_ref, sems.at[1]))

    @pl.when(i < nu_ref[0])
    def _():
        e = be_ref[i]

        @pl.when(i == 0)
        def _():
            for c in fetch(e):
                c.start()

        @pl.when(first_ref[i] != 0)
        def _():
            for c in fetch(e):
                c.wait()
            w1b_ref[...] = w1f_ref[...].astype(BF16)
            w2b_ref[...] = w2f_ref[...].astype(BF16)

            @pl.when(next_ref[i] >= 0)
            def _():
                for c in fetch(next_ref[i]):
                    c.start()

        def ffn(off, m):
            rows = pl.ds(off, m)
            hid = _dot(_unpack_bf16_pairs(xs_ref[rows, :]), w1b_ref[...]) + b1_ref[pl.ds(e, 1), :]
            x_glu = jnp.minimum(hid[:, :D_EXPERT], SWIGLU_LIMIT)
            x_lin = jnp.clip(hid[:, D_EXPERT:], -SWIGLU_LIMIT, SWIGLU_LIMIT)
            act = x_glu / (1.0 + jnp.exp(-SWIGLU_ALPHA * x_glu)) * (x_lin + 1.0)
            y = _dot(act.astype(BF16), w2b_ref[...]) + b2_ref[pl.ds(e, 1), :]
            ys_ref[rows, :] = _pack_bf16_pairs(y.astype(BF16).astype(F32))

        bm = xs_ref.shape[0]
        half = bm // 2
        occupied = rows_ref[i]

        @pl.when(occupied == bm)
        def _():
            ffn(0, bm)

        @pl.when(occupied < bm)
        def _():
            has_half = occupied > half

            @pl.when(has_half)
            def _():
                ffn(0, half)

            off = pl.multiple_of(jnp.where(has_half, half, 0), half)
            for m in range(EXPERT_ROW_STEP, half + 1, EXPERT_ROW_STEP):
                @pl.when(occupied - off == m)
                def _(m=m):
                    ffn(off, m)

            for g in range(bm // EXPERT_ROW_STEP):
                @pl.when(g * EXPERT_ROW_STEP >= occupied)
                def _(g=g):
                    ys_ref[g * EXPERT_ROW_STEP:(g + 1) * EXPERT_ROW_STEP, :] = jnp.zeros(
                        (EXPERT_ROW_STEP, D_PACKED), jnp.int32)


def _experts(blk_expert, n_used, first_flag, next_expert, blk_rows, xs, w1, b1, w2, b2):
    n_rows = xs.shape[0]
    bm = BM_EXPERT
    nblk = n_rows // bm
    last = lambda i, nu: jnp.maximum(jnp.minimum(i, nu[0] - 1), 0)
    row = lambda i, be, nu, *_: (last(i, nu), 0)
    whole = lambda i, *_: (0, 0)
    return pl.pallas_call(
        _expert_kernel,
        grid_spec=pltpu.PrefetchScalarGridSpec(
            num_scalar_prefetch=5,
            grid=(nblk,),
            in_specs=[pl.BlockSpec((bm, D_PACKED), row),
                      pl.BlockSpec(memory_space=pl.ANY),
                      pl.BlockSpec((N_EXPERTS, 2 * D_EXPERT), whole),
                      pl.BlockSpec(memory_space=pl.ANY),
                      pl.BlockSpec((N_EXPERTS, D_MODEL), whole)],
            out_specs=pl.BlockSpec((bm, D_PACKED), row),
            scratch_shapes=[pltpu.VMEM((D_MODEL, 2 * D_EXPERT), F32),
                            pltpu.VMEM((D_EXPERT, D_MODEL), F32),
                            pltpu.VMEM((D_MODEL, 2 * D_EXPERT), BF16),
                            pltpu.VMEM((D_EXPERT, D_MODEL), BF16),
                            pltpu.SemaphoreType.DMA((2,))]),
        out_shape=jax.ShapeDtypeStruct((n_rows, D_PACKED), jnp.int32),
        input_output_aliases={5: 0},
        compiler_params=_params(1),
        name="experts",
    )(blk_expert, n_used, first_flag, next_expert, blk_rows, xs, w1, b1, w2, b2)


def _combine_kernel(pcnt_ref, psrc_ref, pdst_ref, ntot_ref, ys_ref, h_ref, slab_ref, g_ref, o_ref,
                    buf_ref, sems):
    tm = h_ref.shape[0]
    i = pl.program_id(0)
    slot = i % 2

    def read(slot_, n, sorted_row, ys_row):
        return pltpu.make_async_copy(_rows(ys_ref, ys_row, n),
                                     _rows(buf_ref.at[slot_], sorted_row, n), sems.at[slot_])

    def gather(tile, slot_):
        _for_each_piece(tile, pcnt_ref, psrc_ref, pdst_ref, lambda n, s, d: read(slot_, n, s, d).start())

    @pl.when(i == 0)
    def _():
        buf_ref[...] = jnp.zeros(buf_ref.shape, jnp.int32)
        gather(0, 0)

    @pl.when(i + 1 < pl.num_programs(0))
    def _():
        gather(i + 1, 1 - slot)

    _wait_rows(ntot_ref[i], lambda n: read(slot, n, 0, 0))

    slab = slab_ref[...]
    lane = lax.broadcasted_iota(jnp.int32, (tm, R_SORTED), 1).astype(F32)
    wmat = jnp.zeros((tm, R_SORTED), F32)
    for k in range(TOP_K):
        wmat = jnp.where(lane == slab[:, k:k + 1], slab[:, TOP_K + k:TOP_K + k + 1], wmat)
    moe = _dot(wmat.astype(BF16), _unpack_bf16_pairs(buf_ref[slot]))
    o_ref[...] = _rms(h_ref[...] + moe, g_ref[...])


def _combine(pieces, ys, h, slab, g):
    t = h.shape[0]
    tm = TM_ROUTE
    return pl.pallas_call(
        _combine_kernel,
        grid_spec=pltpu.PrefetchScalarGridSpec(
            num_scalar_prefetch=4,
            grid=(t // tm,),
            in_specs=[pl.BlockSpec(memory_space=pl.ANY),
                      pl.BlockSpec((tm, D_MODEL), lambda i, *_: (i, 0)),
                      pl.BlockSpec((tm, LANES), lambda i, *_: (i, 0)),
                      pl.BlockSpec((1, D_MODEL), lambda i, *_: (0, 0))],
            out_specs=pl.BlockSpec((tm, D_MODEL), lambda i, *_: (i, 0)),
            scratch_shapes=[pltpu.VMEM((2, R_SORTED, D_PACKED), jnp.int32),
                            pltpu.SemaphoreType.DMA((2,))]),
        out_shape=jax.ShapeDtypeStruct((t, D_MODEL), F32),
        compiler_params=_params(1),
        name="combine",
    )(*pieces, ys, h, slab, g)


def _pad_lanes(v, fill=0.0):
    return jnp.pad(v.reshape(1, -1), ((0, 0), (0, LANES - v.shape[-1])), constant_values=fill)


def kernel(x, norm_mix_g, w_in, conv_w, conv_b, dt_bias, a_log, d_skip, ssd_norm_g, attn_sinks,
           rel_bias, attn_norm_g, w_out, norm_ffn_g, w_router, b_router, w1, b1, w2, b2,
           norm_final_g):
    bsz, seq, d = x.shape
    t = bsz * seq
    nc = seq // CHUNK
    depth = w_in.shape[0]
    bias = _bias_table(rel_bias)

    assert depth == 1, "the final norm is fused into the combine kernel: single layer only"
    h = x.reshape(t, d)
    for layer in range(depth):
        y_mix = _mixer(h.reshape(bsz, seq, d), norm_mix_g[layer].reshape(1, d), jnp.swapaxes(w_in, 1, 2), layer,
                       conv_w[layer], conv_b[layer].reshape(1, -1),
                       _pad_lanes(dt_bias[layer]), _pad_lanes(a_log[layer]),
                       jnp.repeat(d_skip[layer], SSD_HEAD_DIM).reshape(1, -1),
                       ssd_norm_g[layer].reshape(1, -1), attn_sinks[layer], bias,
                       attn_norm_g[layer].reshape(1, -1)).reshape(t, -1)

        wr = w_router[layer].T.astype(BF16)
        br = jnp.broadcast_to(b_router[layer][:, None], (N_EXPERTS, LANES))
        h_mid, u, slab, post, tab = _route(
            h, y_mix, w_out, layer, norm_ffn_g[layer].reshape(1, d), wr, br)

        bm = BM_EXPERT
        ntiles = t // TM_ROUTE
        nblk = (t * TOP_K + ntiles * N_EXPERTS * (SUBLANES - 1)) // bm + N_EXPERTS
        tab = tab.reshape(ntiles, N_EXPERTS, LANES)
        cnt8, off8, segst = tab[:, :, 0], tab[:, :, 1], tab[:, :, 2]
        total8 = off8[-1] + cnt8[-1]
        padded = ((total8 + bm - 1) // bm) * bm
        pend = jnp.cumsum(padded)
        segdst = (pend - padded)[None, :] + off8
        n_used = pend[-1:] // bm
        eids = jnp.arange(N_EXPERTS, dtype=jnp.int32)
        blk = jnp.arange(nblk, dtype=jnp.int32)
        blk_expert = jnp.minimum(
            jnp.sum((blk[:, None] * bm >= pend[None, :]).astype(jnp.int32), axis=1), N_EXPERTS - 1)
        e_end = jnp.sum(jnp.where(blk_expert[:, None] == eids[None, :],
                                  ((pend - padded) + total8)[None, :], 0), axis=1)
        valid = jnp.where(blk < n_used[0], jnp.clip(e_end - blk * bm, 0, bm), 0)
        blk_rows = ((valid + EXPERT_ROW_STEP - 1) // EXPERT_ROW_STEP) * EXPERT_ROW_STEP
        piece_lo = jnp.arange(bm // EXPERT_ROW_STEP, dtype=jnp.int32)[None, :] * EXPERT_ROW_STEP
        zero_flag = jnp.where(piece_lo >= valid[:, None], 2,
                              (piece_lo + EXPERT_ROW_STEP > valid[:, None]).astype(jnp.int32)).reshape(-1)
        before = jnp.concatenate([blk_expert[:1], blk_expert[:-1]])
        first_flag = ((blk == 0) | (before != blk_expert)).astype(jnp.int32)
        cand = jnp.where((eids[None, :] > eids[:, None]) & (padded[None, :] > 0), eids[None, :], N_EXPERTS)
        next_nonempty = jnp.min(cand, axis=1)
        next_nonempty = jnp.where(next_nonempty == N_EXPERTS, -1, next_nonempty)
        next_expert = jnp.sum(jnp.where(blk_expert[:, None] == eids[None, :], next_nonempty[None, :], 0), axis=1)
        pieces = _piece_tables(cnt8, segst, segdst)

        xs = _dispatch(pieces, zero_flag, u, post)
        ys = _experts(blk_expert, n_used, first_flag, next_expert, blk_rows, xs, w1[layer],
                      b1[layer], w2[layer], b2[layer])
        h = _combine(pieces, ys, h_mid, slab, norm_final_g.reshape(1, d))
    return h.reshape(bsz, seq, d)
```
